```python
import math
import jax, jax.numpy as jnp
from jax import lax
import numpy as np

D_MODEL = 2048
BATCH = 4
SEQ = 2048
DEPTH = 2

HEAD_DIM = 64
N_HEADS = D_MODEL // 128
N_KV = 4
HPG = N_HEADS // N_KV
NSA_W = N_HEADS * HEAD_DIM
KV_W = N_KV * HEAD_DIM
CMP_BLOCK = 32
CMP_STRIDE = 16
CMP_HIDDEN = 256
SEL_BLOCK = 64
N_SEL = 16
WINDOW = 512
Q_BLOCK = 128
SEL_Q_BLOCK = 64
GM_W = D_MODEL // 2
GM_CHUNK = 128
GM_GW = 128
GM_GROUPS = GM_W // GM_GW
N_BUCKETS = 32
MAX_DISTANCE = 128
D_FF = 5504
EPS = 1e-6
NEG = -1e30
FORCE = 1e4
OFF_KV = NSA_W
OFF_NG = OFF_KV + 6 * KV_W
OFF_UV = OFF_NG + 3 * N_HEADS
OFF_MG = OFF_UV + 2 * GM_W
IN_W = OFF_MG + 2 * D_MODEL

kernel_name = "hybrid_nsa_gmlp_macaron_block"


def rms_norm(x, g):
    x32 = x.astype(jnp.float32)
    y = x32 * lax.rsqrt(jnp.mean(x32 * x32, axis=-1, keepdims=True) + EPS)
    return (y * g.astype(jnp.float32)).astype(x.dtype)


def layer_norm(x, g, b):
    x32 = x.astype(jnp.float32)
    mu = jnp.mean(x32, axis=-1, keepdims=True)
    xc = x32 - mu
    y = xc * lax.rsqrt(jnp.mean(xc * xc, axis=-1, keepdims=True) + EPS)
    return (y * g.astype(jnp.float32) + b.astype(jnp.float32)).astype(x.dtype)


def swiglu(x, w_gate, w_up, w_down):
    return (jax.nn.silu(x @ w_gate) * (x @ w_up)) @ w_down


def t5_bucket(dist):
    n = jnp.maximum(dist, 0)
    max_exact = N_BUCKETS // 2
    nf = jnp.maximum(n, 1).astype(jnp.float32)
    large = max_exact + (jnp.log(nf / max_exact) / math.log(MAX_DISTANCE / max_exact)
                         * (N_BUCKETS - max_exact)).astype(jnp.int32)
    large = jnp.minimum(large, N_BUCKETS - 1)
    return jnp.where(n < max_exact, n, large)


def masked_softmax(logits, bias, mask):
    s = jnp.where(mask, logits.astype(jnp.float32) + bias.astype(jnp.float32), NEG)
    return jax.nn.softmax(s, axis=-1) * mask


def compress(kv, pos, w1, w2):
    B, S, G, D = kv.shape
    n_c = (S - CMP_BLOCK) // CMP_STRIDE + 1
    idx = jnp.arange(n_c)[:, None] * CMP_STRIDE + jnp.arange(CMP_BLOCK)[None, :]
    blocks = kv[:, idx] + pos[:, None, :]
    flat = blocks.transpose(0, 1, 3, 2, 4).reshape(B, n_c, G, CMP_BLOCK * D)
    return jax.nn.silu(flat @ w1) @ w2


def nsa_attention(q, k_c, v_c, k_s, v_s, k_w, v_w, gates, k_norm,
                  cmp_pos_k, cmp_pos_v, cmp_k_w1, cmp_k_w2, cmp_v_w1, cmp_v_w2, rel_bias):
    B, S = q.shape[:2]
    scale = HEAD_DIM ** -0.5
    t = jnp.arange(S)

    kc = rms_norm(compress(k_c, cmp_pos_k, cmp_k_w1, cmp_k_w2), k_norm[0])
    vc = compress(v_c, cmp_pos_v, cmp_v_w1, cmp_v_w2)
    n_c = kc.shape[1]
    c_end = jnp.arange(n_c) * CMP_STRIDE + CMP_BLOCK - 1
    dist_c = t[:, None] - c_end[None, :]
    bias_c = rel_bias[t5_bucket(dist_c)].reshape(S, n_c, N_KV, HPG).transpose(2, 3, 0, 1)
    logits_c = jnp.einsum('bsghd,bcgd->bghsc', q, kc) * scale
    p_c = masked_softmax(logits_c, bias_c, dist_c >= 0)
    o_c = jnp.einsum('bghsc,bcgd->bsghd', p_c.astype(vc.dtype), vc)

    n_s = S // SEL_BLOCK
    ci = jnp.arange(n_c)[:, None] * CMP_STRIDE
    sj = jnp.arange(n_s)[None, :] * SEL_BLOCK
    overlap = jnp.clip(jnp.minimum(ci + CMP_BLOCK, sj + SEL_BLOCK) - jnp.maximum(ci, sj), 0, None)
    overlap = overlap.astype(jnp.float32) / CMP_BLOCK
    imp = jnp.einsum('bghsc,cj->bgsj', p_c, overlap)
    blk = jnp.arange(n_s)[None, :]
    cur = (t // SEL_BLOCK)[:, None]
    forced = (blk == 0) | (blk == cur) | (blk == cur - 1)
    score = jnp.where(blk <= cur, jnp.where(forced, FORCE, imp), NEG)
    n_sel = min(N_SEL, n_s)
    _, sel_idx = lax.top_k(score, n_sel)
    sel_ok = sel_idx <= cur

    k_s_t = k_s.transpose(0, 2, 1, 3)
    v_s_t = v_s.transpose(0, 2, 1, 3)
    tbl = rel_bias.reshape(N_BUCKETS, N_KV, HPG).transpose(1, 0, 2)
    n_qb = S // SEL_Q_BLOCK
    bi = jnp.arange(B)[:, None, None, None]
    gi = jnp.arange(N_KV)[None, :, None, None]
    offs = jnp.arange(SEL_BLOCK)

    def sel_block(args):
        qb, ib, okb, tb = args
        tok = (ib[..., None] * SEL_BLOCK + offs).reshape(B, N_KV, SEL_Q_BLOCK, n_sel * SEL_BLOCK)
        kg = k_s_t[bi, gi, tok]
        vg = v_s_t[bi, gi, tok]
        dist = tb[None, None, :, None] - tok
        bias = tbl[gi, t5_bucket(dist)].transpose(0, 1, 4, 2, 3)
        mask = (dist >= 0) & jnp.repeat(okb, SEL_BLOCK, axis=-1)
        logits = jnp.einsum('bqghd,bgqkd->bghqk', qb, kg) * scale
        p = masked_softmax(logits, bias, mask[:, :, None])
        return jnp.einsum('bghqk,bgqkd->bqghd', p.astype(vg.dtype), vg)

    qs = q.reshape(B, n_qb, SEL_Q_BLOCK, N_KV, HPG, HEAD_DIM).swapaxes(0, 1)
    is_ = sel_idx.reshape(B, N_KV, n_qb, SEL_Q_BLOCK, n_sel).transpose(2, 0, 1, 3, 4)
    oks = sel_ok.reshape(B, N_KV, n_qb, SEL_Q_BLOCK, n_sel).transpose(2, 0, 1, 3, 4)
    ts = t.reshape(n_qb, SEL_Q_BLOCK)
    o_s = lax.map(sel_block, (qs, is_, oks, ts))
    o_s = o_s.swapaxes(0, 1).reshape(B, S, N_KV, HPG, HEAD_DIM)

    nb = S // Q_BLOCK
    span = WINDOW + Q_BLOCK
    pad = ((0, 0), (WINDOW, 0), (0, 0), (0, 0))
    widx = jnp.arange(nb)[:, None] * Q_BLOCK + jnp.arange(span)[None, :]
    kb = jnp.pad(k_w, pad)[:, widx]
    vb = jnp.pad(v_w, pad)[:, widx]
    qpos = t.reshape(nb, Q_BLOCK)
    kpos = widx - WINDOW
    dist_w = qpos[:, :, None] - kpos[:, None, :]
    mask_w = (dist_w >= 0) & (dist_w < WINDOW) & (kpos[:, None, :] >= 0)
    bias_w = rel_bias[t5_bucket(dist_w)].reshape(nb, Q_BLOCK, span, N_KV, HPG).transpose(0, 3, 4, 1, 2)
    logits_w = jnp.einsum('bnqghd,bnkgd->bnghqk',
                          q.reshape(B, nb, Q_BLOCK, N_KV, HPG, HEAD_DIM), kb) * scale
    p_w = masked_softmax(logits_w, bias_w, mask_w[:, None, None])
    o_w = jnp.einsum('bnghqk,bnkgd->bnqghd', p_w.astype(vb.dtype), vb).reshape(B, S, N_KV, HPG, HEAD_DIM)

    return gates[..., 0:1] * o_c + gates[..., 1:2] * o_s + gates[..., 2:3] * o_w


def spatial_gating(uv, g, b, w_s, b_s):
    B, S, _ = uv.shape
    u, v = uv[..., :GM_W], uv[..., GM_W:]
    v = layer_norm(v, g, b).reshape(B, S // GM_CHUNK, GM_CHUNK, GM_GROUPS, GM_GW)
    causal = jnp.tril(jnp.ones((GM_CHUNK, GM_CHUNK), w_s.dtype))
    mixed = jnp.einsum('gts,bcsgd->bctgd', w_s * causal, v) + b_s.T[:, :, None]
    return u * mixed.reshape(B, S, GM_W)


def mixer_block(x, mix_norm, w_in, q_norm, k_norm, cmp_pos_k, cmp_pos_v, cmp_k_w1, cmp_k_w2,
                cmp_v_w1, cmp_v_w2, sgu_norm_g, sgu_norm_b, sgu_w, sgu_b,
                w_proj_nsa, w_proj_sgu, w_out, rel_bias):
    B, S, _ = x.shape
    h = rms_norm(x, mix_norm)
    z = h @ w_in
    q = rms_norm(z[..., :OFF_KV].reshape(B, S, N_KV, HPG, HEAD_DIM), q_norm)
    kv = z[..., OFF_KV:OFF_NG].reshape(B, S, 6, N_KV, HEAD_DIM)
    k_s = rms_norm(kv[:, :, 2], k_norm[1])
    k_w = rms_norm(kv[:, :, 4], k_norm[2])
    gates = jax.nn.sigmoid(z[..., OFF_NG:OFF_UV]).reshape(B, S, N_KV, HPG, 3)
    a = nsa_attention(q, kv[:, :, 0], kv[:, :, 1], k_s, kv[:, :, 3], k_w, kv[:, :, 5], gates, k_norm,
                      cmp_pos_k, cmp_pos_v, cmp_k_w1, cmp_k_w2, cmp_v_w1, cmp_v_w2,
                      rel_bias).reshape(B, S, NSA_W)
    sg = spatial_gating(jax.nn.gelu(z[..., OFF_UV:OFF_MG]), sgu_norm_g, sgu_norm_b, sgu_w, sgu_b)
    mg = jax.nn.sigmoid(z[..., OFF_MG:])
    merged = mg[..., :D_MODEL] * (a @ w_proj_nsa) + mg[..., D_MODEL:] * (sg @ w_proj_sgu)
    return merged @ w_out


def setup_inputs(seed: int = 0) -> dict:
    key = jax.random.key(seed)
    ks = jax.random.split(key, 32)
    f32 = jnp.float32
    L = DEPTH

    def nrm(k, shape, fan_in):
        return jax.random.normal(k, shape, f32) * (fan_in ** -0.5)

    def gain(k, shape):
        return 1.0 + 0.05 * jax.random.normal(k, shape, f32)

    return {
        "x": jax.random.normal(ks[0], (BATCH, SEQ, D_MODEL), f32),
        "rel_bias": 0.5 * jax.random.normal(ks[1], (N_BUCKETS, N_HEADS), f32),
        "ffn1_norm": gain(ks[2], (L, D_MODEL)),
        "ffn1_w_gate": nrm(ks[3], (L, D_MODEL, D_FF), D_MODEL),
        "ffn1_w_up": nrm(ks[4], (L, D_MODEL, D_FF), D_MODEL),
        "ffn1_w_down": nrm(ks[5], (L, D_FF, D_MODEL), D_FF),
        "mix_norm": gain(ks[6], (L, D_MODEL)),
        "w_in": nrm(ks[7], (L, D_MODEL, IN_W), D_MODEL),
        "q_norm": gain(ks[8], (L, HEAD_DIM)),
        "k_norm": gain(ks[9], (L, 3, HEAD_DIM)),
        "cmp_pos_k": 0.5 * jax.random.normal(ks[10], (L, CMP_BLOCK, HEAD_DIM), f32),
        "cmp_pos_v": 0.5 * jax.random.normal(ks[11], (L, CMP_BLOCK, HEAD_DIM), f32),
        "cmp_k_w1": nrm(ks[12], (L, CMP_BLOCK * HEAD_DIM, CMP_HIDDEN), CMP_BLOCK * HEAD_DIM),
        "cmp_k_w2": nrm(ks[13], (L, CMP_HIDDEN, HEAD_DIM), CMP_HIDDEN),
        "cmp_v_w1": nrm(ks[14], (L, CMP_BLOCK * HEAD_DIM, CMP_HIDDEN), CMP_BLOCK * HEAD_DIM),
        "cmp_v_w2": nrm(ks[15], (L, CMP_HIDDEN, HEAD_DIM), CMP_HIDDEN),
        "sgu_norm_g": gain(ks[16], (L, GM_W)),
        "sgu_norm_b": 0.02 * jax.random.normal(ks[17], (L, GM_W), f32),
        "sgu_w": nrm(ks[18], (L, GM_GROUPS, GM_CHUNK, GM_CHUNK), GM_CHUNK),
        "sgu_b": 1.0 + 0.1 * jax.random.normal(ks[19], (L, GM_GROUPS, GM_CHUNK), f32),
        "w_proj_nsa": nrm(ks[20], (L, NSA_W, D_MODEL), NSA_W),
        "w_proj_sgu": nrm(ks[21], (L, GM_W, D_MODEL), GM_W),
        "w_out": nrm(ks[22], (L, D_MODEL, D_MODEL), D_MODEL),
        "ffn2_norm": gain(ks[23], (L, D_MODEL)),
        "ffn2_w_gate": nrm(ks[24], (L, D_MODEL, D_FF), D_MODEL),
        "ffn2_w_up": nrm(ks[25], (L, D_MODEL, D_FF), D_MODEL),
        "ffn2_w_down": nrm(ks[26], (L, D_FF, D_MODEL), D_FF),
    }


def reference(x, rel_bias, ffn1_norm, ffn1_w_gate, ffn1_w_up, ffn1_w_down, mix_norm, w_in,
              q_norm, k_norm, cmp_pos_k, cmp_pos_v, cmp_k_w1, cmp_k_w2, cmp_v_w1, cmp_v_w2,
              sgu_norm_g, sgu_norm_b, sgu_w, sgu_b, w_proj_nsa, w_proj_sgu, w_out,
              ffn2_norm, ffn2_w_gate, ffn2_w_up, ffn2_w_down):
    for l in range(DEPTH):
        x = x + 0.5 * swiglu(rms_norm(x, ffn1_norm[l]), ffn1_w_gate[l], ffn1_w_up[l], ffn1_w_down[l])
        x = x + mixer_block(x, mix_norm[l], w_in[l], q_norm[l], k_norm[l], cmp_pos_k[l], cmp_pos_v[l],
                            cmp_k_w1[l], cmp_k_w2[l], cmp_v_w1[l], cmp_v_w2[l],
                            sgu_norm_g[l], sgu_norm_b[l], sgu_w[l], sgu_b[l],
                            w_proj_nsa[l], w_proj_sgu[l], w_out[l], rel_bias)
        x = x + 0.5 * swiglu(rms_norm(x, ffn2_norm[l]), ffn2_w_gate[l], ffn2_w_up[l], ffn2_w_down[l])
    return x
```

```python
import functools
import math

import numpy as np
import jax
import jax.numpy as jnp
from jax import lax
from jax.experimental import pallas as pl
from jax.experimental.pallas import tpu as pltpu

F32 = jnp.float32
BF16 = jnp.bfloat16

D_MODEL = 2048
BATCH = 4
SEQ = 2048
DEPTH = 2
TOK = BATCH * SEQ
HEAD_DIM = 64
N_HEADS = 16
N_KV = 4
HPG = 4
NSA_W = 1024
KV_W = 256
CMP_BLOCK = 32
CMP_STRIDE = 16
CMP_HIDDEN = 256
SEL_BLOCK = 64
N_SEL = 16
WINDOW = 512
GM_W = 1024
GM_CHUNK = 128
GM_GW = 128
GM_GROUPS = 8
N_BUCKETS = 32
MAX_DISTANCE = 128
D_FF = 5504
EPS = 1e-6
NEG = -1e30
FORCE = 1e4
OFF_KV = NSA_W
OFF_NG = OFF_KV + 6 * KV_W
OFF_UV = OFF_NG + 3 * N_HEADS
OFF_MG = OFF_UV + 2 * GM_W

LANE = 128
VMEM_LIMIT = 56 * 1024 * 1024

F_PAD = 5632
FFN_TM = 512
FFN_TF = 512
PIN_TM = 1024
PIN_TN = 512
Z_MG = 0
Z_U = Z_MG + 2 * D_MODEL
Z_V = Z_U + GM_W
Z_Q = Z_V + GM_W
Z_KV = Z_Q + NSA_W
Z_NG = Z_KV + 6 * KV_W
Z_W = Z_NG + PIN_TN
TQ = 128
NQ = SEQ // TQ
QL = HPG * TQ
N_CMP = 128
N_SBLK = SEQ // SEL_BLOCK
SGU_TM = 512
OUT_TM = 256


def _cparams(sem):
    return pltpu.CompilerParams(dimension_semantics=sem, vmem_limit_bytes=VMEM_LIMIT)


def _ffn_body(x_ref, g_ref, wg_ref, wu_ref, wd_ref, o_ref, h_ref):
    f = pl.program_id(1)

    @pl.when(f == 0)
    def _():
        x = x_ref[...]
        ms = jnp.mean(x * x, axis=-1, keepdims=True)
        h_ref[...] = (x * lax.rsqrt(ms + EPS) * g_ref[...]).astype(BF16)
        o_ref[...] = jnp.zeros_like(o_ref)

    h = h_ref[...]
    g = jnp.dot(h, wg_ref[...], preferred_element_type=F32)
    u = jnp.dot(h, wu_ref[...], preferred_element_type=F32)
    a = (g * jax.nn.sigmoid(g) * u).astype(BF16)
    o_ref[...] += jnp.dot(a, wd_ref[...], preferred_element_type=F32)

    @pl.when(f == pl.num_programs(1) - 1)
    def _():
        o_ref[...] = x_ref[...] + 0.5 * o_ref[...]


def _ffn(x, gain, wg, wu, wd):
    nf = F_PAD // FFN_TF
    return pl.pallas_call(
        _ffn_body,
        grid=(TOK // FFN_TM, nf),
        in_specs=[
            pl.BlockSpec((FFN_TM, D_MODEL), lambda i, f: (i, 0)),
            pl.BlockSpec((1, D_MODEL), lambda i, f: (0, 0)),
            pl.BlockSpec((D_MODEL, FFN_TF), lambda i, f: (0, f)),
            pl.BlockSpec((D_MODEL, FFN_TF), lambda i, f: (0, f)),
            pl.BlockSpec((FFN_TF, D_MODEL), lambda i, f: (f, 0)),
        ],
        out_specs=pl.BlockSpec((FFN_TM, D_MODEL), lambda i, f: (i, 0)),
        out_shape=jax.ShapeDtypeStruct((TOK, D_MODEL), F32),
        scratch_shapes=[pltpu.VMEM((FFN_TM, D_MODEL), BF16)],
        compiler_params=_cparams(("parallel", "arbitrary")),
        name="ffn",
    )(x, gain, wg, wu, wd)


def _head_rms(y, bd):
    y2 = y * y
    hi = y2.astype(BF16)
    lo = (y2 - hi.astype(F32)).astype(BF16)
    parts = []
    for k in range(y.shape[1] // LANE):
        sl = slice(k * LANE, (k + 1) * LANE)
        parts.append(jnp.dot(hi[:, sl], bd, preferred_element_type=F32)
                     + jnp.dot(lo[:, sl], bd, preferred_element_type=F32))
    ss = jnp.concatenate(parts, axis=1)
    return y * lax.rsqrt(ss * (1.0 / HEAD_DIM) + EPS)


def _pin_body(x_ref, g_ref, w_ref, bd_ref, qg_ref, k1g_ref, k2g_ref, o_ref, h_ref):
    j = pl.program_id(1)

    @pl.when(j == 0)
    def _():
        x = x_ref[...]
        ms = jnp.mean(x * x, axis=-1, keepdims=True)
        h_ref[...] = (x * lax.rsqrt(ms + EPS) * g_ref[...]).astype(BF16)

    y = jnp.dot(h_ref[...], w_ref[...], preferred_element_type=F32)
    t_u = Z_U // PIN_TN
    t_q = Z_Q // PIN_TN
    t_kv = Z_KV // PIN_TN
    t_ng = Z_NG // PIN_TN

    @pl.when(j < t_u)
    def _():
        o_ref[...] = jax.nn.sigmoid(y)

    @pl.when((j >= t_u) & (j < t_q))
    def _():
        o_ref[...] = jax.nn.gelu(y)

    @pl.when((j >= t_q) & (j < t_kv))
    def _():
        o_ref[...] = _head_rms(y, bd_ref[...]) * (qg_ref[...] * (HEAD_DIM ** -0.5))

    @pl.when(j == t_kv)
    def _():
        o_ref[...] = y

    def kv_tile(kg_ref):
        half = PIN_TN // 2
        o_ref[:, :half] = _head_rms(y[:, :half], bd_ref[...]) * kg_ref[...]
        o_ref[:, half:] = y[:, half:]

    @pl.when(j == t_kv + 1)
    def _():
        kv_tile(k1g_ref)

    @pl.when(j == t_kv + 2)
    def _():
        kv_tile(k2g_ref)

    @pl.when(j == t_ng)
    def _():
        o_ref[...] = jax.nn.sigmoid(y)


def _proj_in(x, gain, w, bd, qg, k1g, k2g):
    half = PIN_TN // 2
    return pl.pallas_call(
        _pin_body,
        grid=(TOK // PIN_TM, Z_W // PIN_TN),
        in_specs=[
            pl.BlockSpec((PIN_TM, D_MODEL), lambda i, j: (i, 0)),
            pl.BlockSpec((1, D_MODEL), lambda i, j: (0, 0)),
            pl.BlockSpec((D_MODEL, PIN_TN), lambda i, j: (0, j)),
            pl.BlockSpec((LANE, LANE), lambda i, j: (0, 0)),
            pl.BlockSpec((1, PIN_TN), lambda i, j: (0, 0)),
            pl.BlockSpec((1, half), lambda i, j: (0, 0)),
            pl.BlockSpec((1, half), lambda i, j: (0, 0)),
        ],
        out_specs=pl.BlockSpec((PIN_TM, PIN_TN), lambda i, j: (i, j)),
        out_shape=jax.ShapeDtypeStruct((TOK, Z_W), F32),
        scratch_shapes=[pltpu.VMEM((PIN_TM, D_MODEL), BF16)],
        compiler_params=_cparams(("parallel", "arbitrary")),
        name="proj_in",
    )(x, gain, w, bd, qg, k1g, k2g)


def _cmp_body(x_ref, pos_ref, w1_ref, w2_ref, kg_ref, o_ref):
    kind = pl.program_id(0)
    half = CMP_STRIDE * HEAD_DIM
    x = x_ref[0]
    xa = (x + pos_ref[0, :, :half]).astype(BF16)
    xb = (x + pos_ref[0, :, half:]).astype(BF16)
    a = jnp.dot(xa, w1_ref[0, :half, :], preferred_element_type=F32)
    b = jnp.dot(xb, w1_ref[0, half:, :], preferred_element_type=F32)
    rows = x.shape[0]
    hid = a + pltpu.roll(b, rows - 1, 0)
    hid = hid * jax.nn.sigmoid(hid)
    y = jnp.dot(hid.astype(BF16), w2_ref[0], preferred_element_type=F32)

    @pl.when(kind == 0)
    def _():
        ms = jnp.mean(y * y, axis=-1, keepdims=True)
        o_ref[0] = y * lax.rsqrt(ms + EPS) * kg_ref[...]

    @pl.when(kind != 0)
    def _():
        o_ref[0] = y


def _compress(x2, pos, w1, w2, kg):
    rows = x2.shape[1]
    return pl.pallas_call(
        _cmp_body,
        grid=(2,),
        in_specs=[
            pl.BlockSpec((1, rows, CMP_STRIDE * HEAD_DIM), lambda k: (k, 0, 0)),
            pl.BlockSpec((1, 1, CMP_BLOCK * HEAD_DIM), lambda k: (k, 0, 0)),
            pl.BlockSpec((1, CMP_BLOCK * HEAD_DIM, CMP_HIDDEN), lambda k: (k, 0, 0)),
            pl.BlockSpec((1, CMP_HIDDEN, HEAD_DIM), lambda k: (k, 0, 0)),
            pl.BlockSpec((1, HEAD_DIM), lambda k: (0, 0)),
        ],
        out_specs=pl.BlockSpec((1, rows, HEAD_DIM), lambda k: (k, 0, 0)),
        out_shape=jax.ShapeDtypeStruct((2, rows, HEAD_DIM), F32),
        compiler_params=_cparams(("arbitrary",)),
        name="compress",
    )(x2, pos, w1, w2, kg)


def _np_bucket(dist):
    n = np.maximum(dist, 0)
    max_exact = N_BUCKETS // 2
    nf = np.maximum(n, 1).astype(np.float32)
    large = max_exact + (np.log(nf / np.float32(max_exact)) / np.float32(math.log(MAX_DISTANCE / max_exact))
                         * np.float32(N_BUCKETS - max_exact)).astype(np.int32)
    large = np.minimum(large, N_BUCKETS - 1)
    return np.where(n < max_exact, n, large).astype(np.int32)


def _bucket_tables():
    kj = np.arange(TQ)[:, None]
    qi = np.arange(TQ)[None, :]
    near = np.stack([_np_bucket(qi - kj), _np_bucket(TQ + qi - kj)])
    c = np.arange(N_CMP)[:, None]
    t = (np.arange(NQ)[:, None, None] * TQ + qi[None])
    cmp_b = _np_bucket(t - (c[None] * CMP_STRIDE + CMP_BLOCK - 1))
    return near, cmp_b.astype(np.int32)


def _bias_body(rb_ref, near_ref, cmpb_ref, t0_ref, t1_ref, tc_ref, cst_ref):
    g = pl.program_id(0)

    def lut(bk, h):
        val = jnp.zeros(bk.shape, F32)
        for b in range(N_BUCKETS):
            val = jnp.where(bk == b, rb_ref[b, h], val)
        return val

    for hh in range(HPG):
        h = g * HPG + hh
        sl = slice(hh * TQ, (hh + 1) * TQ)
        t0_ref[0, :, sl] = lut(near_ref[0], h)
        t1_ref[0, :, sl] = lut(near_ref[1], h)
        cst_ref[0, :, sl] = jnp.full((8, TQ), rb_ref[N_BUCKETS - 1, h], F32)

        def body(i, carry):
            tc_ref[0, i, :, sl] = lut(cmpb_ref[i], h)
            return carry

        lax.fori_loop(0, NQ, body, 0)


def _bias_tables(rel_bias, near, cmpb):
    return pl.pallas_call(
        _bias_body,
        grid=(N_KV,),
        in_specs=[
            pl.BlockSpec(memory_space=pltpu.SMEM),
            pl.BlockSpec((2, TQ, TQ), lambda g: (0, 0, 0)),
            pl.BlockSpec((NQ, N_CMP, TQ), lambda g: (0, 0, 0)),
        ],
        out_specs=[
            pl.BlockSpec((1, TQ, QL), lambda g: (g, 0, 0)),
            pl.BlockSpec((1, TQ, QL), lambda g: (g, 0, 0)),
            pl.BlockSpec((1, NQ, N_CMP, QL), lambda g: (g, 0, 0, 0)),
            pl.BlockSpec((1, 8, QL), lambda g: (g, 0, 0)),
        ],
        out_shape=[
            jax.ShapeDtypeStruct((N_KV, TQ, QL), F32),
            jax.ShapeDtypeStruct((N_KV, TQ, QL), F32),
            jax.ShapeDtypeStruct((N_KV, NQ, N_CMP, QL), F32),
            jax.ShapeDtypeStruct((N_KV, 8, QL), F32),
        ],
        compiler_params=_cparams(("arbitrary",)),
        name="bias_tables",
    )(rel_bias, near, cmpb)


def _attn_body(qT_ref, ks_ref, vsT_ref, kw_ref, vwT_ref, kc_ref, vcT_ref, gT_ref,
               t0_ref, t1_ref, tc_ref, cst_ref, ov_ref, o_ref, sel_ref):
    i = pl.program_id(2)
    qT = qT_ref[0, 0, 0]
    key = lax.broadcasted_iota(jnp.int32, (TQ, QL), 0)
    qi = lax.broadcasted_iota(jnp.int32, (TQ, QL), 1) & (TQ - 1)
    iv = jnp.full((TQ, QL), i, jnp.int32)

    s = jnp.dot(kc_ref[0, 0], qT, preferred_element_type=F32) + tc_ref[0, 0]
    mask_c = (iv * TQ + qi) >= (key * CMP_STRIDE + (CMP_BLOCK - 1))
    s = jnp.where(mask_c, s, NEG)
    m = jnp.max(s, axis=0, keepdims=True)
    e = jnp.where(mask_c, jnp.exp(s - m), 0.0)
    l = jnp.sum(e, axis=0, keepdims=True)
    p = e * jnp.where(l > 0.0, 1.0 / l, 0.0)
    ocT = jnp.dot(vcT_ref[0, 0], p.astype(BF16), preferred_element_type=F32)

    psum = p[:, 0:TQ] + p[:, TQ:2 * TQ] + p[:, 2 * TQ:3 * TQ] + p[:, 3 * TQ:4 * TQ]
    imp = jnp.dot(ov_ref[...], psum, preferred_element_type=F32,
                  precision=lax.Precision.HIGHEST)
    blk = lax.broadcasted_iota(jnp.int32, (N_SBLK, TQ), 0)
    ql = lax.broadcasted_iota(jnp.int32, (N_SBLK, TQ), 1)
    cur = jnp.full((N_SBLK, TQ), i, jnp.int32) * (TQ // SEL_BLOCK) + ql // SEL_BLOCK
    forced = (blk == 0) | (blk == cur) | (blk == cur - 1)
    score = jnp.where(blk <= cur, jnp.where(forced, FORCE, imp), NEG)
    rank = jnp.zeros((N_SBLK, TQ), jnp.int32)
    for jj in range(N_SBLK):
        row = score[jj:jj + 1, :]
        beats = (row > score) | ((row == score) & (blk > jj))
        rank = rank + beats.astype(jnp.int32)
    sel = (rank < N_SEL) & (blk <= cur)
    sel_ref[...] = sel.astype(F32)

    cst = cst_ref[0, 0:1, :]

    def sel_mask(n):
        r0 = sel_ref[pl.ds(2 * n, 1), :]
        r1 = sel_ref[pl.ds(2 * n + 1, 1), :]
        sm = jnp.where(key[:, :TQ] < SEL_BLOCK, r0, r1)
        return jnp.concatenate([sm] * HPG, axis=1) > 0.5

    def first(k_ref, vT_ref, n, bias, mask):
        sc = jnp.dot(k_ref[0, 0, n], qT, preferred_element_type=F32) + bias
        sc = jnp.where(mask, sc, NEG)
        mx = jnp.max(sc, axis=0, keepdims=True)
        pe = jnp.exp(sc - mx)
        ls = jnp.sum(pe, axis=0, keepdims=True)
        acc = jnp.dot(vT_ref[0, 0, n], pe.astype(BF16), preferred_element_type=F32)
        return mx, ls, acc

    def step(k_ref, vT_ref, n, bias, mask, state):
        m_old, l_old, acc = state
        sc = jnp.dot(k_ref[0, 0, n], qT, preferred_element_type=F32) + bias
        if mask is not None:
            sc = jnp.where(mask, sc, NEG)
        m_new = jnp.maximum(m_old, jnp.max(sc, axis=0, keepdims=True))
        alpha = jnp.exp(m_old - m_new)
        pe = jnp.exp(sc - m_new)
        l_new = alpha * l_old + jnp.sum(pe, axis=0, keepdims=True)
        acc = alpha * acc + jnp.dot(vT_ref[0, 0, n], pe.astype(BF16), preferred_element_type=F32)
        return m_new, l_new, acc

    causal = key <= qi

    st = first(ks_ref, vsT_ref, i, t0_ref[0], causal & sel_mask(i))
    n1 = jnp.maximum(i - 1, 0)
    st = step(ks_ref, vsT_ref, n1, t1_ref[0], sel_mask(n1) & (iv >= 1), st)
    st = lax.fori_loop(
        0, jnp.maximum(i - 1, 0),
        lambda n, c: step(ks_ref, vsT_ref, n, cst, sel_mask(n), c), st)
    osT = st[2] * (1.0 / st[1])

    sw = first(kw_ref, vwT_ref, i, t0_ref[0], causal)
    sw = step(kw_ref, vwT_ref, n1, t1_ref[0], iv >= 1, sw)
    for d in (2, 3):
        sw = step(kw_ref, vwT_ref, jnp.maximum(i - d, 0), cst, iv >= d, sw)
    nlast = WINDOW // TQ
    sw = step(kw_ref, vwT_ref, jnp.maximum(i - nlast, 0), cst, (qi < key) & (iv >= nlast), sw)
    owT = sw[2] * (1.0 / sw[1])

    gts = gT_ref[0, 0, 0]
    o_ref[0, 0, 0] = gts[0:1, :] * ocT + gts[1:2, :] * osT + gts[2:3, :] * owT


def _attention(qT, ks, vsT, kw, vwT, kc, vcT, gT, t0, t1, tc, cst, ov):
    nk = SEQ // TQ
    kspec = pl.BlockSpec((1, 1, nk, TQ, HEAD_DIM), lambda b, g, i: (b, g, 0, 0, 0))
    vspec = pl.BlockSpec((1, 1, nk, HEAD_DIM, TQ), lambda b, g, i: (b, g, 0, 0, 0))
    return pl.pallas_call(
        _attn_body,
        grid=(BATCH, N_KV, NQ),
        in_specs=[
            pl.BlockSpec((1, 1, 1, HEAD_DIM, QL), lambda b, g, i: (b, g, i, 0, 0)),
            kspec, vspec, kspec, vspec,
            pl.BlockSpec((1, 1, N_CMP, HEAD_DIM), lambda b, g, i: (b, g, 0, 0)),
            pl.BlockSpec((1, 1, HEAD_DIM, N_CMP), lambda b, g, i: (b, g, 0, 0)),
            pl.BlockSpec((1, 1, 1, 8, QL), lambda b, g, i: (b, g, i, 0, 0)),
            pl.BlockSpec((1, TQ, QL), lambda b, g, i: (g, 0, 0)),
            pl.BlockSpec((1, TQ, QL), lambda b, g, i: (g, 0, 0)),
            pl.BlockSpec((1, 1, N_CMP, QL), lambda b, g, i: (g, i, 0, 0)),
            pl.BlockSpec((1, 8, QL), lambda b, g, i: (g, 0, 0)),
            pl.BlockSpec((N_SBLK, N_CMP), lambda b, g, i: (0, 0)),
        ],
        out_specs=pl.BlockSpec((1, 1, 1, HEAD_DIM, QL), lambda b, g, i: (b, g, i, 0, 0)),
        out_shape=jax.ShapeDtypeStruct((BATCH, N_KV, NQ, HEAD_DIM, QL), F32),
        scratch_shapes=[pltpu.VMEM((N_SBLK, TQ), F32)],
        compiler_params=_cparams(("parallel", "parallel", "arbitrary")),
        name="nsa_attention",
    )(qT, ks, vsT, kw, vwT, kc, vcT, gT, t0, t1, tc, cst, ov)


def _sgu_body(u_ref, v_ref, g_ref, b_ref, w_ref, bs_ref, o_ref):
    r = lax.broadcasted_iota(jnp.int32, (GM_CHUNK, GM_CHUNK), 0)
    c = lax.broadcasted_iota(jnp.int32, (GM_CHUNK, GM_CHUNK), 1)
    tril = r >= c
    ws = [jnp.where(tril, w_ref[gi], 0.0).astype(BF16) for gi in range(GM_GROUPS)]
    for ch in range(SGU_TM // GM_CHUNK):
        rows = slice(ch * GM_CHUNK, (ch + 1) * GM_CHUNK)
        v = v_ref[rows, :]
        mu = jnp.mean(v, axis=-1, keepdims=True)
        vc = v - mu
        var = jnp.mean(vc * vc, axis=-1, keepdims=True)
        vn = (vc * lax.rsqrt(var + EPS) * g_ref[...] + b_ref[...]).astype(BF16)
        for gi in range(GM_GROUPS):
            cols = slice(gi * GM_GW, (gi + 1) * GM_GW)
            mixed = jnp.dot(ws[gi], vn[:, cols], preferred_element_type=F32) + bs_ref[:, cols]
            o_ref[rows, cols] = (u_ref[rows, cols] * mixed).astype(o_ref.dtype)


def _sgu(z, ln_g, ln_b, w_s, bs_exp):
    return pl.pallas_call(
        _sgu_body,
        grid=(TOK // SGU_TM,),
        in_specs=[
            pl.BlockSpec((SGU_TM, GM_W), lambda i: (i, Z_U // GM_W)),
            pl.BlockSpec((SGU_TM, GM_W), lambda i: (i, Z_V // GM_W)),
            pl.BlockSpec((1, GM_W), lambda i: (0, 0)),
            pl.BlockSpec((1, GM_W), lambda i: (0, 0)),
            pl.BlockSpec((GM_GROUPS, GM_CHUNK, GM_CHUNK), lambda i: (0, 0, 0)),
            pl.BlockSpec((GM_CHUNK, GM_W), lambda i: (0, 0)),
        ],
        out_specs=pl.BlockSpec((SGU_TM, GM_W), lambda i: (i, 0)),
        out_shape=jax.ShapeDtypeStruct((TOK, GM_W), BF16),
        compiler_params=_cparams(("parallel",)),
        name="sgu",
    )(z, z, ln_g, ln_b, w_s, bs_exp)


def _out_body(a_ref, sg_ref, m1_ref, m2_ref, x_ref, wn_ref, ws_ref, wo_ref, o_ref):
    pa = jnp.dot(a_ref[...], wn_ref[...], preferred_element_type=F32)
    ps = jnp.dot(sg_ref[...], ws_ref[...], preferred_element_type=F32)
    merged = (m1_ref[...] * pa + m2_ref[...] * ps).astype(BF16)
    o_ref[...] = x_ref[...] + jnp.dot(merged, wo_ref[...], preferred_element_type=F32)


def _merge_out(a, sg, z, x, wn, ws, wo):
    const = lambda i: (0, 0)
    return pl.pallas_call(
        _out_body,
        grid=(TOK // OUT_TM,),
        in_specs=[
            pl.BlockSpec((OUT_TM, NSA_W), lambda i: (i, 0)),
            pl.BlockSpec((OUT_TM, GM_W), lambda i: (i, 0)),
            pl.BlockSpec((OUT_TM, D_MODEL), lambda i: (i, 0)),
            pl.BlockSpec((OUT_TM, D_MODEL), lambda i: (i, 1)),
            pl.BlockSpec((OUT_TM, D_MODEL), lambda i: (i, 0)),
            pl.BlockSpec((NSA_W, D_MODEL), const),
            pl.BlockSpec((GM_W, D_MODEL), const),
            pl.BlockSpec((D_MODEL, D_MODEL), const),
        ],
        out_specs=pl.BlockSpec((OUT_TM, D_MODEL), lambda i: (i, 0)),
        out_shape=jax.ShapeDtypeStruct((TOK, D_MODEL), F32),
        compiler_params=_cparams(("parallel",)),
        name="merge_out",
    )(a, sg, z, z, x, wn, ws, wo)


def _overlap_T():
    ci = np.arange(N_CMP)[None, :] * CMP_STRIDE
    sj = np.arange(N_SBLK)[:, None] * SEL_BLOCK
    ov = np.clip(np.minimum(ci + CMP_BLOCK, sj + SEL_BLOCK) - np.maximum(ci, sj), 0, None)
    ov = ov.astype(np.float32) / CMP_BLOCK
    ov[:, N_CMP - 1] = 0.0
    return ov


def _block_diag_ones():
    r = np.arange(LANE)
    return (r[:, None] // HEAD_DIM == r[None, :] // HEAD_DIM).astype(np.float32)


def _prep_ffn(wg, wu, wd):
    pad = F_PAD - D_FF
    return (jnp.pad(wg, ((0, 0), (0, pad))).astype(BF16),
            jnp.pad(wu, ((0, 0), (0, pad))).astype(BF16),
            jnp.pad(wd, ((0, pad), (0, 0))).astype(BF16))


def _prep_w_in(w):
    ng = jnp.pad(w[:, OFF_NG:OFF_UV], ((0, 0), (0, PIN_TN - 3 * N_HEADS)))
    return jnp.concatenate(
        [w[:, OFF_MG:], w[:, OFF_UV:OFF_MG], w[:, :OFF_KV], w[:, OFF_KV:OFF_NG], ng], axis=1).astype(BF16)


def _mixer(x, z, l, p, tables):
    t0, t1, tc, cst, ov = tables
    B, S, G = BATCH, SEQ, N_KV
    nk = S // TQ
    q = z[:, Z_Q:Z_Q + NSA_W].reshape(B, NQ, TQ, G, HPG, HEAD_DIM)
    qT = q.transpose(0, 3, 1, 5, 4, 2).reshape(B, G, NQ, HEAD_DIM, QL).astype(BF16)
    kv = z[:, Z_KV:Z_KV + 6 * KV_W].reshape(B, S, 6, G, HEAD_DIM)

    def rows(j):
        return kv[:, :, j].transpose(0, 2, 1, 3).reshape(B, G, nk, TQ, HEAD_DIM)

    def cols(j):
        return kv[:, :, j].reshape(B, nk, TQ, G, HEAD_DIM).transpose(0, 3, 1, 4, 2)

    x2 = jnp.stack([rows(0), rows(1)]).reshape(2, B * G * N_CMP, CMP_STRIDE * HEAD_DIM)
    pos = jnp.stack([p["cmp_pos_k"][l], p["cmp_pos_v"][l]]).reshape(2, 1, CMP_BLOCK * HEAD_DIM)
    w1 = jnp.stack([p["cmp_k_w1"][l], p["cmp_v_w1"][l]]).astype(BF16)
    w2 = jnp.stack([p["cmp_k_w2"][l], p["cmp_v_w2"][l]]).astype(BF16)
    cmp = _compress(x2, pos, w1, w2, p["k_norm"][l, 0].reshape(1, HEAD_DIM))
    kc = cmp[0].reshape(B, G, N_CMP, HEAD_DIM).astype(BF16)
    vcT = cmp[1].reshape(B, G, N_CMP, HEAD_DIM).transpose(0, 1, 3, 2).astype(BF16)

    gts = z[:, Z_NG:Z_NG + 3 * N_HEADS].reshape(B, NQ, TQ, G, HPG, 3)
    gT = gts.transpose(0, 3, 1, 5, 4, 2).reshape(B, G, NQ, 3, QL)
    gT = jnp.pad(gT, ((0, 0), (0, 0), (0, 0), (0, 5), (0, 0)))

    aT = _attention(qT, rows(2).astype(BF16), cols(3).astype(BF16),
                    rows(4).astype(BF16), cols(5).astype(BF16), kc, vcT, gT, t0, t1, tc, cst, ov)
    a = aT.reshape(B, G, NQ, HEAD_DIM, HPG, TQ).transpose(0, 2, 5, 1, 4, 3).reshape(TOK, NSA_W).astype(BF16)

    bs_exp = jnp.repeat(p["sgu_b"][l].T, GM_GW, axis=1)
    sg = _sgu(z, p["sgu_norm_g"][l].reshape(1, GM_W), p["sgu_norm_b"][l].reshape(1, GM_W),
              p["sgu_w"][l], bs_exp)
    return _merge_out(a, sg, z, x, p["w_proj_nsa"][l].astype(BF16), p["w_proj_sgu"][l].astype(BF16),
                      p["w_out"][l].astype(BF16))


def kernel(x, rel_bias, ffn1_norm, ffn1_w_gate, ffn1_w_up, ffn1_w_down, mix_norm, w_in, q_norm, k_norm, cmp_pos_k, cmp_pos_v, cmp_k_w1, cmp_k_w2, cmp_v_w1, cmp_v_w2, sgu_norm_g, sgu_norm_b, sgu_w, sgu_b, w_proj_nsa, w_proj_sgu, w_out, ffn2_norm, ffn2_w_gate, ffn2_w_up, ffn2_w_down):
    p = dict(cmp_pos_k=cmp_pos_k, cmp_pos_v=cmp_pos_v, cmp_k_w1=cmp_k_w1, cmp_k_w2=cmp_k_w2,
             cmp_v_w1=cmp_v_w1, cmp_v_w2=cmp_v_w2, k_norm=k_norm, sgu_norm_g=sgu_norm_g,
             sgu_norm_b=sgu_norm_b, sgu_w=sgu_w, sgu_b=sgu_b, w_proj_nsa=w_proj_nsa,
             w_proj_sgu=w_proj_sgu, w_out=w_out)
    near, cmpb = _bucket_tables()
    t0, t1, tc, cst = _bias_tables(rel_bias, jnp.asarray(near), jnp.asarray(cmpb))
    tables = (t0, t1, tc, cst, jnp.asarray(_overlap_T()))
    bd = jnp.asarray(_block_diag_ones()).astype(BF16)

    h = x.reshape(TOK, D_MODEL)
    for l in range(DEPTH):
        h = _ffn(h, ffn1_norm[l].reshape(1, D_MODEL), *_prep_ffn(ffn1_w_gate[l], ffn1_w_up[l], ffn1_w_down[l]))
        z = _proj_in(h, mix_norm[l].reshape(1, D_MODEL), _prep_w_in(w_in[l]), bd,
                     jnp.tile(q_norm[l], PIN_TN // HEAD_DIM).reshape(1, PIN_TN),
                     jnp.tile(k_norm[l, 1], KV_W // HEAD_DIM).reshape(1, KV_W),
                     jnp.tile(k_norm[l, 2], KV_W // HEAD_DIM).reshape(1, KV_W))
        h = _mixer(h, z, l, p, tables)
        h = _ffn(h, ffn2_norm[l].reshape(1, D_MODEL), *_prep_ffn(ffn2_w_gate[l], ffn2_w_up[l], ffn2_w_down[l]))
    return h.reshape(BATCH, SEQ, D_MODEL)
```

```python
import functools
import math

import numpy as np
import jax
import jax.numpy as jnp
from jax import lax
from jax.experimental import pallas as pl
from jax.experimental.pallas import tpu as pltpu

F32 = jnp.float32
BF16 = jnp.bfloat16

D_MODEL = 2048
BATCH = 4
SEQ = 2048
DEPTH = 2
TOK = BATCH * SEQ
HEAD_DIM = 64
N_HEADS = 16
N_KV = 4
HPG = 4
NSA_W = 1024
KV_W = 256
CMP_BLOCK = 32
CMP_STRIDE = 16
CMP_HIDDEN = 256
SEL_BLOCK = 64
N_SEL = 16
WINDOW = 512
GM_W = 1024
GM_CHUNK = 128
GM_GW = 128
GM_GROUPS = 8
N_BUCKETS = 32
MAX_DISTANCE = 128
D_FF = 5504
EPS = 1e-6
NEG = -1e30
FORCE = 1e4
OFF_KV = NSA_W
OFF_NG = OFF_KV + 6 * KV_W
OFF_UV = OFF_NG + 3 * N_HEADS
OFF_MG = OFF_UV + 2 * GM_W

LANE = 128
VMEM_LIMIT = 56 * 1024 * 1024

FFN_TM = 512
FFN_TF = 512
PIN_TM = 1024
PIN_TN = 512
Z_MG = 0
Z_U = Z_MG + 2 * D_MODEL
Z_V = Z_U + GM_W
Z_Q = Z_V + GM_W
Z_KV = Z_Q + NSA_W
Z_NG = Z_KV + 6 * KV_W
Z_W = Z_NG + PIN_TN
TQ = 128
NQ = SEQ // TQ
QL = HPG * TQ
N_CMP = 128
N_SBLK = SEQ // SEL_BLOCK
V_ROWS = HEAD_DIM + 16
SGU_TM = 512
OUT_TM = 256


def _cparams(sem):
    return pltpu.CompilerParams(dimension_semantics=sem, vmem_limit_bytes=VMEM_LIMIT)


def _ffn_body(x_ref, g_ref, wg_ref, wu_ref, wd_ref, o_ref, h_ref):
    f = pl.program_id(1)

    @pl.when(f == 0)
    def _():
        x = x_ref[...]
        ms = jnp.mean(x * x, axis=-1, keepdims=True)
        h_ref[...] = (x * lax.rsqrt(ms + EPS) * g_ref[...]).astype(BF16)
        o_ref[...] = jnp.zeros_like(o_ref)

    valid = D_FF - f * FFN_TF
    col = lax.broadcasted_iota(jnp.int32, (1, FFN_TF), 1)
    row = lax.broadcasted_iota(jnp.int32, (FFN_TF, 1), 0)
    h = h_ref[...]
    g = jnp.dot(h, wg_ref[...], preferred_element_type=F32)
    u = jnp.dot(h, wu_ref[...], preferred_element_type=F32)
    a = jnp.where(col < valid, g * jax.nn.sigmoid(g) * u, 0.0).astype(BF16)
    wd = jnp.where(row < valid, wd_ref[...], jnp.zeros((), BF16))
    o_ref[...] += jnp.dot(a, wd, preferred_element_type=F32)

    @pl.when(f == pl.num_programs(1) - 1)
    def _():
        o_ref[...] = x_ref[...] + 0.5 * o_ref[...]


def _ffn(x, gain, wg, wu, wd, l):
    nf = pl.cdiv(D_FF, FFN_TF)
    return pl.pallas_call(
        _ffn_body,
        grid=(TOK // FFN_TM, nf),
        in_specs=[
            pl.BlockSpec((FFN_TM, D_MODEL), lambda i, f: (i, 0)),
            pl.BlockSpec((1, D_MODEL), lambda i, f: (0, 0)),
            pl.BlockSpec((None, D_MODEL, FFN_TF), lambda i, f: (l, 0, f)),
            pl.BlockSpec((None, D_MODEL, FFN_TF), lambda i, f: (l, 0, f)),
            pl.BlockSpec((None, FFN_TF, D_MODEL), lambda i, f: (l, f, 0)),
        ],
        out_specs=pl.BlockSpec((FFN_TM, D_MODEL), lambda i, f: (i, 0)),
        out_shape=jax.ShapeDtypeStruct((TOK, D_MODEL), F32),
        scratch_shapes=[pltpu.VMEM((FFN_TM, D_MODEL), BF16)],
        compiler_params=_cparams(("parallel", "arbitrary")),
        name="ffn",
    )(x, gain, wg, wu, wd)


def _head_rms(y, bd):
    y2 = y * y
    hi = y2.astype(BF16)
    lo = (y2 - hi.astype(F32)).astype(BF16)
    parts = []
    for k in range(y.shape[1] // LANE):
        sl = slice(k * LANE, (k + 1) * LANE)
        parts.append(jnp.dot(hi[:, sl], bd, preferred_element_type=F32)
                     + jnp.dot(lo[:, sl], bd, preferred_element_type=F32))
    ss = jnp.concatenate(parts, axis=1)
    return y * lax.rsqrt(ss * (1.0 / HEAD_DIM) + EPS)


def _pin_body(x_ref, g_ref, w_ref, bd_ref, qg_ref, k1g_ref, k2g_ref, o_ref, h_ref):
    j = pl.program_id(1)

    @pl.when(j == 0)
    def _():
        x = x_ref[...]
        ms = jnp.mean(x * x, axis=-1, keepdims=True)
        h_ref[...] = (x * lax.rsqrt(ms + EPS) * g_ref[...]).astype(BF16)

    y = jnp.dot(h_ref[...], w_ref[...], preferred_element_type=F32)
    t_u = Z_U // PIN_TN
    t_q = Z_Q // PIN_TN
    t_kv = Z_KV // PIN_TN
    t_ng = Z_NG // PIN_TN

    @pl.when(j < t_u)
    def _():
        o_ref[...] = jax.nn.sigmoid(y)

    @pl.when((j >= t_u) & (j < t_q))
    def _():
        o_ref[...] = jax.nn.gelu(y)

    @pl.when((j >= t_q) & (j < t_kv))
    def _():
        o_ref[...] = _head_rms(y, bd_ref[...]) * (qg_ref[...] * (HEAD_DIM ** -0.5))

    @pl.when(j == t_kv)
    def _():
        o_ref[...] = y

    def kv_tile(kg_ref):
        half = PIN_TN // 2
        o_ref[:, :half] = _head_rms(y[:, :half], bd_ref[...]) * kg_ref[...]
        o_ref[:, half:] = y[:, half:]

    @pl.when(j == t_kv + 1)
    def _():
        kv_tile(k1g_ref)

    @pl.when(j == t_kv + 2)
    def _():
        kv_tile(k2g_ref)

    @pl.when(j == t_ng)
    def _():
        o_ref[...] = jax.nn.sigmoid(y)


def _proj_in(x, gain, w, bd, qg, k1g, k2g):
    half = PIN_TN // 2
    return pl.pallas_call(
        _pin_body,
        grid=(TOK // PIN_TM, Z_W // PIN_TN),
        in_specs=[
            pl.BlockSpec((PIN_TM, D_MODEL), lambda i, j: (i, 0)),
            pl.BlockSpec((1, D_MODEL), lambda i, j: (0, 0)),
            pl.BlockSpec((D_MODEL, PIN_TN), lambda i, j: (0, j)),
            pl.BlockSpec((LANE, LANE), lambda i, j: (0, 0)),
            pl.BlockSpec((1, PIN_TN), lambda i, j: (0, 0)),
            pl.BlockSpec((1, half), lambda i, j: (0, 0)),
            pl.BlockSpec((1, half), lambda i, j: (0, 0)),
        ],
        out_specs=pl.BlockSpec((PIN_TM, PIN_TN), lambda i, j: (i, j)),
        out_shape=jax.ShapeDtypeStruct((TOK, Z_W), F32),
        scratch_shapes=[pltpu.VMEM((PIN_TM, D_MODEL), BF16)],
        compiler_params=_cparams(("parallel", "arbitrary")),
        name="proj_in",
    )(x, gain, w, bd, qg, k1g, k2g)


def _cmp_body(x_ref, pos_ref, w1_ref, w2_ref, kg_ref, o_ref):
    kind = pl.program_id(0)
    half = CMP_STRIDE * HEAD_DIM
    x = x_ref[0]
    xa = (x + pos_ref[0, :, :half]).astype(BF16)
    xb = (x + pos_ref[0, :, half:]).astype(BF16)
    a = jnp.dot(xa, w1_ref[0, :half, :], preferred_element_type=F32)
    b = jnp.dot(xb, w1_ref[0, half:, :], preferred_element_type=F32)
    rows = x.shape[0]
    hid = a + pltpu.roll(b, rows - 1, 0)
    hid = hid * jax.nn.sigmoid(hid)
    y = jnp.dot(hid.astype(BF16), w2_ref[0], preferred_element_type=F32)

    @pl.when(kind == 0)
    def _():
        ms = jnp.mean(y * y, axis=-1, keepdims=True)
        o_ref[0] = y * lax.rsqrt(ms + EPS) * kg_ref[...]

    @pl.when(kind != 0)
    def _():
        o_ref[0] = y


def _compress(x2, pos, w1, w2, kg):
    rows = x2.shape[1]
    return pl.pallas_call(
        _cmp_body,
        grid=(2,),
        in_specs=[
            pl.BlockSpec((1, rows, CMP_STRIDE * HEAD_DIM), lambda k: (k, 0, 0)),
            pl.BlockSpec((1, 1, CMP_BLOCK * HEAD_DIM), lambda k: (k, 0, 0)),
            pl.BlockSpec((1, CMP_BLOCK * HEAD_DIM, CMP_HIDDEN), lambda k: (k, 0, 0)),
            pl.BlockSpec((1, CMP_HIDDEN, HEAD_DIM), lambda k: (k, 0, 0)),
            pl.BlockSpec((1, HEAD_DIM), lambda k: (0, 0)),
        ],
        out_specs=pl.BlockSpec((1, rows, HEAD_DIM), lambda k: (k, 0, 0)),
        out_shape=jax.ShapeDtypeStruct((2, rows, HEAD_DIM), F32),
        compiler_params=_cparams(("arbitrary",)),
        name="compress",
    )(x2, pos, w1, w2, kg)


def _np_bucket(dist):
    n = np.maximum(dist, 0)
    max_exact = N_BUCKETS // 2
    nf = np.maximum(n, 1).astype(np.float32)
    large = max_exact + (np.log(nf / np.float32(max_exact)) / np.float32(math.log(MAX_DISTANCE / max_exact))
                         * np.float32(N_BUCKETS - max_exact)).astype(np.int32)
    large = np.minimum(large, N_BUCKETS - 1)
    return np.where(n < max_exact, n, large).astype(np.int32)


def _bucket_tables():
    kj = np.arange(TQ)[:, None]
    qi = np.arange(TQ)[None, :]
    near = np.stack([_np_bucket(qi - kj), _np_bucket(TQ + qi - kj)])
    c = np.arange(N_CMP)[:, None]
    t = (np.arange(NQ)[:, None, None] * TQ + qi[None])
    cmp_b = _np_bucket(t - (c[None] * CMP_STRIDE + CMP_BLOCK - 1))
    return near, cmp_b.astype(np.int32)


def _bias_body(rb_ref, near_ref, cmpb_ref, t0_ref, t1_ref, tc_ref, cst_ref):
    g = pl.program_id(0)

    def lut(bk, h):
        val = jnp.zeros(bk.shape, F32)
        for b in range(N_BUCKETS):
            val = jnp.where(bk == b, rb_ref[b, h], val)
        return val

    for hh in range(HPG):
        h = g * HPG + hh
        sl = slice(hh * TQ, (hh + 1) * TQ)
        t0_ref[0, :, sl] = lut(near_ref[0], h)
        t1_ref[0, :, sl] = lut(near_ref[1], h)
        cst_ref[0, :, sl] = jnp.full((8, TQ), rb_ref[N_BUCKETS - 1, h], F32)

        def body(i, carry):
            tc_ref[0, i, :, sl] = lut(cmpb_ref[i], h)
            return carry

        lax.fori_loop(0, NQ, body, 0)


def _bias_tables(rel_bias, near, cmpb):
    return pl.pallas_call(
        _bias_body,
        grid=(N_KV,),
        in_specs=[
            pl.BlockSpec(memory_space=pltpu.SMEM),
            pl.BlockSpec((2, TQ, TQ), lambda g: (0, 0, 0)),
            pl.BlockSpec((NQ, N_CMP, TQ), lambda g: (0, 0, 0)),
        ],
        out_specs=[
            pl.BlockSpec((1, TQ, QL), lambda g: (g, 0, 0)),
            pl.BlockSpec((1, TQ, QL), lambda g: (g, 0, 0)),
            pl.BlockSpec((1, NQ, N_CMP, QL), lambda g: (g, 0, 0, 0)),
            pl.BlockSpec((1, 8, QL), lambda g: (g, 0, 0)),
        ],
        out_shape=[
            jax.ShapeDtypeStruct((N_KV, TQ, QL), F32),
            jax.ShapeDtypeStruct((N_KV, TQ, QL), F32),
            jax.ShapeDtypeStruct((N_KV, NQ, N_CMP, QL), F32),
            jax.ShapeDtypeStruct((N_KV, 8, QL), F32),
        ],
        compiler_params=_cparams(("arbitrary",)),
        name="bias_tables",
    )(rel_bias, near, cmpb)


def _attn_body(qT_ref, ks_ref, vsT_ref, kw_ref, vwT_ref, kc_ref, vcT_ref, gT_ref,
               t0_ref, t1_ref, tc_ref, cst_ref, ov_ref, o_ref, sel_ref, sfar_ref):
    i = pl.program_id(2)
    qT = qT_ref[0, 0, 0]
    key = lax.broadcasted_iota(jnp.int32, (TQ, QL), 0)
    qi = lax.broadcasted_iota(jnp.int32, (TQ, QL), 1) & (TQ - 1)
    iv = jnp.full((TQ, QL), i, jnp.int32)

    s = jnp.dot(kc_ref[0, 0], qT, preferred_element_type=F32) + tc_ref[0, 0]
    mask_c = (iv * TQ + qi) >= (key * CMP_STRIDE + (CMP_BLOCK - 1))
    s = jnp.where(mask_c, s, NEG)
    m = jnp.max(s, axis=0, keepdims=True)
    e = jnp.where(mask_c, jnp.exp(s - m), 0.0)
    l = jnp.sum(e, axis=0, keepdims=True)
    p = e * jnp.where(l > 0.0, 1.0 / l, 0.0)
    ocT = jnp.dot(vcT_ref[0, 0], p.astype(BF16), preferred_element_type=F32)

    psum = p[:, 0:TQ] + p[:, TQ:2 * TQ] + p[:, 2 * TQ:3 * TQ] + p[:, 3 * TQ:4 * TQ]
    imp = jnp.dot(ov_ref[...], psum, preferred_element_type=F32,
                  precision=lax.Precision.HIGHEST)
    blk = lax.broadcasted_iota(jnp.int32, (N_SBLK, TQ), 0)
    ql = lax.broadcasted_iota(jnp.int32, (N_SBLK, TQ), 1)
    cur = jnp.full((N_SBLK, TQ), i, jnp.int32) * (TQ // SEL_BLOCK) + ql // SEL_BLOCK
    forced = (blk == 0) | (blk == cur) | (blk == cur - 1)
    score = jnp.where(blk <= cur, jnp.where(forced, FORCE, imp), NEG)
    rank = jnp.zeros((N_SBLK, TQ), jnp.int32)
    for jj in range(N_SBLK):
        row = score[jj:jj + 1, :]
        beats = (row > score) | ((row == score) & (blk > jj))
        rank = rank + beats.astype(jnp.int32)
    sel = (rank < N_SEL) & (blk <= cur)
    sel_ref[...] = sel.astype(F32)

    cst = cst_ref[0, 0:1, :]

    key1 = lax.broadcasted_iota(jnp.int32, (TQ, TQ), 0)
    q1 = lax.broadcasted_iota(jnp.int32, (TQ, TQ), 1)
    i1 = jnp.full((TQ, TQ), i, jnp.int32)
    causal = key1 <= q1

    def sel_mask(n):
        r0 = sel_ref[pl.ds(2 * n, 1), :]
        r1 = sel_ref[pl.ds(2 * n + 1, 1), :]
        return jnp.where(key1 < SEL_BLOCK, r0, r1) > 0.5

    def scores(k_ref, n, bias, mask):
        sc = jnp.dot(k_ref[0, 0, n], qT, preferred_element_type=F32) + bias
        return jnp.concatenate(
            [jnp.where(mask, sc[:, h * TQ:(h + 1) * TQ], NEG) for h in range(HPG)], axis=1)

    def colmax8(sc):
        return jnp.max(sc.reshape(TQ // 8, 8, QL), axis=0)

    def probs(sc, mrow):
        return jnp.exp(sc - mrow).astype(BF16)

    def pv(vT_ref, ns, ps):
        vv = vT_ref[0, 0, ns[0]] if len(ns) == 1 else jnp.concatenate([vT_ref[0, 0, n] for n in ns], axis=1)
        pp = ps[0] if len(ps) == 1 else jnp.concatenate(ps, axis=0)
        return jnp.dot(vv, pp, preferred_element_type=F32)

    def finish(acc):
        return acc[:HEAD_DIM] * (1.0 / acc[HEAD_DIM:HEAD_DIM + 1])

    n1 = jnp.maximum(i - 1, 0)

    s_d = scores(ks_ref, i, t0_ref[0], causal & sel_mask(i))
    s_1 = scores(ks_ref, n1, t1_ref[0], sel_mask(n1) & (i1 >= 1))
    nfar = jnp.maximum(i - 1, 0)
    npair = lax.shift_right_logical(nfar + 1, 1)

    def far_scores(pr, mx):
        a = 2 * pr
        b = a + 1
        sa = scores(ks_ref, a, cst, sel_mask(a))
        sb = scores(ks_ref, b, cst, sel_mask(b) & (jnp.full((TQ, TQ), b, jnp.int32) < nfar))
        sfar_ref[a] = sa
        sfar_ref[b] = sb
        return jnp.maximum(mx, jnp.maximum(colmax8(sa), colmax8(sb)))

    mx = lax.fori_loop(0, npair, far_scores, jnp.maximum(colmax8(s_d), colmax8(s_1)))
    m_s = jnp.max(mx, axis=0, keepdims=True)
    acc = pv(vsT_ref, [i, n1], [probs(s_d, m_s), probs(s_1, m_s)])

    def far_pv(pr, acc_):
        a = 2 * pr
        b = a + 1
        return acc_ + pv(vsT_ref, [a, b], [probs(sfar_ref[a], m_s), probs(sfar_ref[b], m_s)])

    osT = finish(lax.fori_loop(0, npair, far_pv, acc))

    nlast = WINDOW // TQ
    n2, n3, n4 = jnp.maximum(i - 2, 0), jnp.maximum(i - 3, 0), jnp.maximum(i - nlast, 0)
    w_d = scores(kw_ref, i, t0_ref[0], causal)
    w_1 = scores(kw_ref, n1, t1_ref[0], i1 >= 1)
    w_2 = scores(kw_ref, n2, cst, i1 >= 2)
    w_3 = scores(kw_ref, n3, cst, i1 >= 3)
    w_4 = scores(kw_ref, n4, cst, (q1 < key1) & (i1 >= nlast))
    mxw = jnp.maximum(jnp.maximum(jnp.maximum(colmax8(w_d), colmax8(w_1)),
                                  jnp.maximum(colmax8(w_2), colmax8(w_3))), colmax8(w_4))
    m_w = jnp.max(mxw, axis=0, keepdims=True)
    accw = (pv(vwT_ref, [i], [probs(w_d, m_w)])
            + pv(vwT_ref, [n1, n2], [probs(w_1, m_w), probs(w_2, m_w)])
            + pv(vwT_ref, [n3, n4], [probs(w_3, m_w), probs(w_4, m_w)]))
    owT = finish(accw)

    gts = gT_ref[0, 0, 0]
    o_ref[0, 0, 0] = gts[0:1, :] * ocT + gts[1:2, :] * osT + gts[2:3, :] * owT


def _attention(qT, ks, vsT, kw, vwT, kc, vcT, gT, t0, t1, tc, cst, ov):
    nk = SEQ // TQ
    kspec = pl.BlockSpec((1, 1, nk, TQ, HEAD_DIM), lambda b, g, i: (b, g, 0, 0, 0))
    vspec = pl.BlockSpec((1, 1, nk, V_ROWS, TQ), lambda b, g, i: (b, g, 0, 0, 0))
    return pl.pallas_call(
        _attn_body,
        grid=(BATCH, N_KV, NQ),
        in_specs=[
            pl.BlockSpec((1, 1, 1, HEAD_DIM, QL), lambda b, g, i: (b, g, i, 0, 0)),
            kspec, vspec, kspec, vspec,
            pl.BlockSpec((1, 1, N_CMP, HEAD_DIM), lambda b, g, i: (b, g, 0, 0)),
            pl.BlockSpec((1, 1, HEAD_DIM, N_CMP), lambda b, g, i: (b, g, 0, 0)),
            pl.BlockSpec((1, 1, 1, 8, QL), lambda b, g, i: (b, g, i, 0, 0)),
            pl.BlockSpec((1, TQ, QL), lambda b, g, i: (g, 0, 0)),
            pl.BlockSpec((1, TQ, QL), lambda b, g, i: (g, 0, 0)),
            pl.BlockSpec((1, 1, N_CMP, QL), lambda b, g, i: (g, i, 0, 0)),
            pl.BlockSpec((1, 8, QL), lambda b, g, i: (g, 0, 0)),
            pl.BlockSpec((N_SBLK, N_CMP), lambda b, g, i: (0, 0)),
        ],
        out_specs=pl.BlockSpec((1, 1, 1, HEAD_DIM, QL), lambda b, g, i: (b, g, i, 0, 0)),
        out_shape=jax.ShapeDtypeStruct((BATCH, N_KV, NQ, HEAD_DIM, QL), F32),
        scratch_shapes=[pltpu.VMEM((N_SBLK, TQ), F32), pltpu.VMEM((nk, TQ, QL), F32)],
        compiler_params=_cparams(("parallel", "parallel", "arbitrary")),
        name="nsa_attention",
    )(qT, ks, vsT, kw, vwT, kc, vcT, gT, t0, t1, tc, cst, ov)


def _sgu_body(u_ref, v_ref, g_ref, b_ref, w_ref, bs_ref, o_ref):
    r = lax.broadcasted_iota(jnp.int32, (GM_CHUNK, GM_CHUNK), 0)
    c = lax.broadcasted_iota(jnp.int32, (GM_CHUNK, GM_CHUNK), 1)
    tril = r >= c
    ws = [jnp.where(tril, w_ref[gi], 0.0).astype(BF16) for gi in range(GM_GROUPS)]
    for ch in range(SGU_TM // GM_CHUNK):
        rows = slice(ch * GM_CHUNK, (ch + 1) * GM_CHUNK)
        v = v_ref[rows, :]
        mu = jnp.mean(v, axis=-1, keepdims=True)
        vc = v - mu
        var = jnp.mean(vc * vc, axis=-1, keepdims=True)
        vn = (vc * lax.rsqrt(var + EPS) * g_ref[...] + b_ref[...]).astype(BF16)
        for gi in range(GM_GROUPS):
            cols = slice(gi * GM_GW, (gi + 1) * GM_GW)
            mixed = jnp.dot(ws[gi], vn[:, cols], preferred_element_type=F32) + bs_ref[:, cols]
            o_ref[rows, cols] = (u_ref[rows, cols] * mixed).astype(o_ref.dtype)


def _sgu(z, ln_g, ln_b, w_s, bs_exp):
    return pl.pallas_call(
        _sgu_body,
        grid=(TOK // SGU_TM,),
        in_specs=[
            pl.BlockSpec((SGU_TM, GM_W), lambda i: (i, Z_U // GM_W)),
            pl.BlockSpec((SGU_TM, GM_W), lambda i: (i, Z_V // GM_W)),
            pl.BlockSpec((1, GM_W), lambda i: (0, 0)),
            pl.BlockSpec((1, GM_W), lambda i: (0, 0)),
            pl.BlockSpec((GM_GROUPS, GM_CHUNK, GM_CHUNK), lambda i: (0, 0, 0)),
            pl.BlockSpec((GM_CHUNK, GM_W), lambda i: (0, 0)),
        ],
        out_specs=pl.BlockSpec((SGU_TM, GM_W), lambda i: (i, 0)),
        out_shape=jax.ShapeDtypeStruct((TOK, GM_W), BF16),
        compiler_params=_cparams(("parallel",)),
        name="sgu",
    )(z, z, ln_g, ln_b, w_s, bs_exp)


def _out_body(a_ref, sg_ref, m1_ref, m2_ref, x_ref, wn_ref, ws_ref, wo_ref, o_ref):
    pa = jnp.dot(a_ref[...], wn_ref[...], preferred_element_type=F32)
    ps = jnp.dot(sg_ref[...], ws_ref[...], preferred_element_type=F32)
    merged = (m1_ref[...] * pa + m2_ref[...] * ps).astype(BF16)
    o_ref[...] = x_ref[...] + jnp.dot(merged, wo_ref[...], preferred_element_type=F32)


def _merge_out(a, sg, z, x, wn, ws, wo):
    const = lambda i: (0, 0)
    return pl.pallas_call(
        _out_body,
        grid=(TOK // OUT_TM,),
        in_specs=[
            pl.BlockSpec((OUT_TM, NSA_W), lambda i: (i, 0)),
            pl.BlockSpec((OUT_TM, GM_W), lambda i: (i, 0)),
            pl.BlockSpec((OUT_TM, D_MODEL), lambda i: (i, 0)),
            pl.BlockSpec((OUT_TM, D_MODEL), lambda i: (i, 1)),
            pl.BlockSpec((OUT_TM, D_MODEL), lambda i: (i, 0)),
            pl.BlockSpec((NSA_W, D_MODEL), const),
            pl.BlockSpec((GM_W, D_MODEL), const),
            pl.BlockSpec((D_MODEL, D_MODEL), const),
        ],
        out_specs=pl.BlockSpec((OUT_TM, D_MODEL), lambda i: (i, 0)),
        out_shape=jax.ShapeDtypeStruct((TOK, D_MODEL), F32),
        compiler_params=_cparams(("parallel",)),
        name="merge_out",
    )(a, sg, z, z, x, wn, ws, wo)


def _overlap_T():
    ci = np.arange(N_CMP)[None, :] * CMP_STRIDE
    sj = np.arange(N_SBLK)[:, None] * SEL_BLOCK
    ov = np.clip(np.minimum(ci + CMP_BLOCK, sj + SEL_BLOCK) - np.maximum(ci, sj), 0, None)
    ov = ov.astype(np.float32) / CMP_BLOCK
    ov[:, N_CMP - 1] = 0.0
    return ov


def _block_diag_ones():
    r = np.arange(LANE)
    return (r[:, None] // HEAD_DIM == r[None, :] // HEAD_DIM).astype(np.float32)


def _prep_w_in(w):
    ng = jnp.pad(w[:, OFF_NG:OFF_UV], ((0, 0), (0, PIN_TN - 3 * N_HEADS)))
    return jnp.concatenate(
        [w[:, OFF_MG:], w[:, OFF_UV:OFF_MG], w[:, :OFF_KV], w[:, OFF_KV:OFF_NG], ng], axis=1).astype(BF16)


def _mixer(x, z, l, p, tables):
    t0, t1, tc, cst, ov = tables
    B, S, G = BATCH, SEQ, N_KV
    nk = S // TQ
    q = z[:, Z_Q:Z_Q + NSA_W].reshape(B, NQ, TQ, G, HPG, HEAD_DIM)
    qT = q.transpose(0, 3, 1, 5, 4, 2).reshape(B, G, NQ, HEAD_DIM, QL).astype(BF16)
    kv = z[:, Z_KV:Z_KV + 6 * KV_W].reshape(B, S, 6, G, HEAD_DIM)

    def rows(j):
        return kv[:, :, j].transpose(0, 2, 1, 3).reshape(B, G, nk, TQ, HEAD_DIM)

    def cols(j):
        vt = kv[:, :, j].reshape(B, nk, TQ, G, HEAD_DIM).transpose(0, 3, 1, 4, 2)
        return jnp.pad(vt, ((0, 0), (0, 0), (0, 0), (0, V_ROWS - HEAD_DIM), (0, 0)), constant_values=1.0)

    x2 = jnp.stack([rows(0), rows(1)]).reshape(2, B * G * N_CMP, CMP_STRIDE * HEAD_DIM)
    pos = jnp.stack([p["cmp_pos_k"][l], p["cmp_pos_v"][l]]).reshape(2, 1, CMP_BLOCK * HEAD_DIM)
    w1 = jnp.stack([p["cmp_k_w1"][l], p["cmp_v_w1"][l]]).astype(BF16)
    w2 = jnp.stack([p["cmp_k_w2"][l], p["cmp_v_w2"][l]]).astype(BF16)
    cmp = _compress(x2, pos, w1, w2, p["k_norm"][l, 0].reshape(1, HEAD_DIM))
    kc = cmp[0].reshape(B, G, N_CMP, HEAD_DIM).astype(BF16)
    vcT = cmp[1].reshape(B, G, N_CMP, HEAD_DIM).transpose(0, 1, 3, 2).astype(BF16)

    gts = z[:, Z_NG:Z_NG + 3 * N_HEADS].reshape(B, NQ, TQ, G, HPG, 3)
    gT = gts.transpose(0, 3, 1, 5, 4, 2).reshape(B, G, NQ, 3, QL)
    gT = jnp.pad(gT, ((0, 0), (0, 0), (0, 0), (0, 5), (0, 0)))

    aT = _attention(qT, rows(2).astype(BF16), cols(3).astype(BF16),
                    rows(4).astype(BF16), cols(5).astype(BF16), kc, vcT, gT, t0, t1, tc, cst, ov)
    a = aT.reshape(B, G, NQ, HEAD_DIM, HPG, TQ).transpose(0, 2, 5, 1, 4, 3).reshape(TOK, NSA_W).astype(BF16)

    bs_exp = jnp.repeat(p["sgu_b"][l].T, GM_GW, axis=1)
    sg = _sgu(z, p["sgu_norm_g"][l].reshape(1, GM_W), p["sgu_norm_b"][l].reshape(1, GM_W),
              p["sgu_w"][l], bs_exp)
    return _merge_out(a, sg, z, x, p["w_proj_nsa"][l].astype(BF16), p["w_proj_sgu"][l].astype(BF16),
                      p["w_out"][l].astype(BF16))


def kernel(x, rel_bias, ffn1_norm, ffn1_w_gate, ffn1_w_up, ffn1_w_down, mix_norm, w_in, q_norm, k_norm, cmp_pos_k, cmp_pos_v, cmp_k_w1, cmp_k_w2, cmp_v_w1, cmp_v_w2, sgu_norm_g, sgu_norm_b, sgu_w, sgu_b, w_proj_nsa, w_proj_sgu, w_out, ffn2_norm, ffn2_w_gate, ffn2_w_up, ffn2_w_down):
    p = dict(cmp_pos_k=cmp_pos_k, cmp_pos_v=cmp_pos_v, cmp_k_w1=cmp_k_w1, cmp_k_w2=cmp_k_w2,
             cmp_v_w1=cmp_v_w1, cmp_v_w2=cmp_v_w2, k_norm=k_norm, sgu_norm_g=sgu_norm_g,
             sgu_norm_b=sgu_norm_b, sgu_w=sgu_w, sgu_b=sgu_b, w_proj_nsa=w_proj_nsa,
             w_proj_sgu=w_proj_sgu, w_out=w_out)
    near, cmpb = _bucket_tables()
    t0, t1, tc, cst = _bias_tables(rel_bias, jnp.asarray(near), jnp.asarray(cmpb))
    tables = (t0, t1, tc, cst, jnp.asarray(_overlap_T()))
    bd = jnp.asarray(_block_diag_ones()).astype(BF16)

    f1 = (ffn1_w_gate.astype(BF16), ffn1_w_up.astype(BF16), ffn1_w_down.astype(BF16))
    f2 = (ffn2_w_gate.astype(BF16), ffn2_w_up.astype(BF16), ffn2_w_down.astype(BF16))
    h = x.reshape(TOK, D_MODEL)
    for l in range(DEPTH):
        h = _ffn(h, ffn1_norm[l].reshape(1, D_MODEL), *f1, l)
        z = _proj_in(h, mix_norm[l].reshape(1, D_MODEL), _prep_w_in(w_in[l]), bd,
                     jnp.tile(q_norm[l], PIN_TN // HEAD_DIM).reshape(1, PIN_TN),
                     jnp.tile(k_norm[l, 1], KV_W // HEAD_DIM).reshape(1, KV_W),
                     jnp.tile(k_norm[l, 2], KV_W // HEAD_DIM).reshape(1, KV_W))
        h = _mixer(h, z, l, p, tables)
        h = _ffn(h, ffn2_norm[l].reshape(1, D_MODEL), *f2, l)
    return h.reshape(BATCH, SEQ, D_MODEL)
```

```python
import math

import numpy as np
import jax
import jax.numpy as jnp
from jax import lax
from jax.experimental import pallas as pl
from jax.experimental.pallas import tpu as pltpu

F32 = jnp.float32
BF16 = jnp.bfloat16

D_MODEL = 2048
BATCH = 4
SEQ = 2048
DEPTH = 2
TOK = BATCH * SEQ
HEAD_DIM = 64
N_HEADS = 16
N_KV = 4
HPG = 4
NSA_W = 1024
KV_W = 256
CMP_BLOCK = 32
CMP_STRIDE = 16
CMP_HIDDEN = 256
SEL_BLOCK = 64
N_SEL = 16
WINDOW = 512
GM_W = 1024
GM_CHUNK = 128
GM_GW = 128
GM_GROUPS = 8
N_BUCKETS = 32
MAX_DISTANCE = 128
D_FF = 5504
EPS = 1e-6
NEG = -1e30
FORCE = 1e4
OFF_KV = NSA_W
OFF_NG = OFF_KV + 6 * KV_W
OFF_UV = OFF_NG + 3 * N_HEADS
OFF_MG = OFF_UV + 2 * GM_W
LOG2E = 1.4426950408889634

LANE = 128
VMEM_LIMIT = 56 * 1024 * 1024

FFN_TM = 512
FFN_TF = 512
PIN_TM = 1024
PIN_TN = 512
Z_MG = 0
Z_U = Z_MG + 2 * D_MODEL
Z_V = Z_U + GM_W
Z_Q = Z_V + GM_W
Z_KV = Z_Q + NSA_W
Z_NG = Z_KV + 6 * KV_W
Z_W = Z_NG + PIN_TN
KV_GW = 2 * HEAD_DIM
GATE_GW = 16
TQ = 128
NQ = SEQ // TQ
NK = SEQ // TQ
QL = HPG * TQ
N_CMP = 128
N_SBLK = SEQ // SEL_BLOCK
V_ROWS = HEAD_DIM + 16
FAR_GROUP = 4
SGU_TM = 512
OUT_TM = 256


def _cparams(sem):
    return pltpu.CompilerParams(dimension_semantics=sem, vmem_limit_bytes=VMEM_LIMIT)


def _ffn_body(x_ref, g_ref, wg_ref, wu_ref, wd_ref, o_ref, h_ref):
    f = pl.program_id(1)

    @pl.when(f == 0)
    def _():
        x = x_ref[...]
        ms = jnp.mean(x * x, axis=-1, keepdims=True)
        h_ref[...] = (x * lax.rsqrt(ms + EPS) * g_ref[...]).astype(BF16)
        o_ref[...] = jnp.zeros_like(o_ref)

    valid = D_FF - f * FFN_TF
    col = lax.broadcasted_iota(jnp.int32, (1, FFN_TF), 1)
    row = lax.broadcasted_iota(jnp.int32, (FFN_TF, 1), 0)
    h = h_ref[...]
    g = jnp.dot(h, wg_ref[...], preferred_element_type=F32)
    u = jnp.dot(h, wu_ref[...], preferred_element_type=F32)
    a = jnp.where(col < valid, g * jax.nn.sigmoid(g) * u, 0.0).astype(BF16)
    wd = jnp.where(row < valid, wd_ref[...], jnp.zeros((), BF16))
    o_ref[...] += jnp.dot(a, wd, preferred_element_type=F32)

    @pl.when(f == pl.num_programs(1) - 1)
    def _():
        o_ref[...] = x_ref[...] + 0.5 * o_ref[...]


def _ffn(x, gain, wg, wu, wd, l):
    nf = pl.cdiv(D_FF, FFN_TF)
    return pl.pallas_call(
        _ffn_body,
        grid=(TOK // FFN_TM, nf),
        in_specs=[
            pl.BlockSpec((FFN_TM, D_MODEL), lambda i, f: (i, 0)),
            pl.BlockSpec((1, D_MODEL), lambda i, f: (0, 0)),
            pl.BlockSpec((None, D_MODEL, FFN_TF), lambda i, f: (l, 0, f)),
            pl.BlockSpec((None, D_MODEL, FFN_TF), lambda i, f: (l, 0, f)),
            pl.BlockSpec((None, FFN_TF, D_MODEL), lambda i, f: (l, f, 0)),
        ],
        out_specs=pl.BlockSpec((FFN_TM, D_MODEL), lambda i, f: (i, 0)),
        out_shape=jax.ShapeDtypeStruct((TOK, D_MODEL), F32),
        scratch_shapes=[pltpu.VMEM((FFN_TM, D_MODEL), BF16)],
        compiler_params=_cparams(("parallel", "arbitrary")),
        name="ffn",
    )(x, gain, wg, wu, wd)


def _head_rms(y, bd):
    y2 = y * y
    hi = y2.astype(BF16)
    lo = (y2 - hi.astype(F32)).astype(BF16)
    parts = []
    for k in range(y.shape[1] // LANE):
        sl = slice(k * LANE, (k + 1) * LANE)
        parts.append(jnp.dot(hi[:, sl], bd, preferred_element_type=F32)
                     + jnp.dot(lo[:, sl], bd, preferred_element_type=F32))
    ss = jnp.concatenate(parts, axis=1)
    return y * lax.rsqrt(ss * (1.0 / HEAD_DIM) + EPS)


def _pin_body(x_ref, g_ref, w_ref, bd_ref, qg_ref, k1g_ref, k2g_ref, o_ref, h_ref):
    j = pl.program_id(1)

    @pl.when(j == 0)
    def _():
        x = x_ref[...]
        ms = jnp.mean(x * x, axis=-1, keepdims=True)
        h_ref[...] = (x * lax.rsqrt(ms + EPS) * g_ref[...]).astype(BF16)

    y = jnp.dot(h_ref[...], w_ref[...], preferred_element_type=F32)
    t_u = Z_U // PIN_TN
    t_q = Z_Q // PIN_TN
    t_kv = Z_KV // PIN_TN
    t_ng = Z_NG // PIN_TN

    @pl.when(j < t_u)
    def _():
        o_ref[...] = jax.nn.sigmoid(y)

    @pl.when((j >= t_u) & (j < t_q))
    def _():
        o_ref[...] = jax.nn.gelu(y)

    @pl.when((j >= t_q) & (j < t_kv))
    def _():
        o_ref[...] = _head_rms(y, bd_ref[...]) * (qg_ref[...] * (HEAD_DIM ** -0.5 * LOG2E))

    @pl.when(j == t_kv)
    def _():
        o_ref[...] = y

    def kv_tile(kg_ref):
        is_k = (lax.broadcasted_iota(jnp.int32, (1, PIN_TN), 1) & (KV_GW - 1)) < HEAD_DIM
        o_ref[...] = jnp.where(is_k, _head_rms(y, bd_ref[...]) * kg_ref[...], y)

    @pl.when(j == t_kv + 1)
    def _():
        kv_tile(k1g_ref)

    @pl.when(j == t_kv + 2)
    def _():
        kv_tile(k2g_ref)

    @pl.when(j == t_ng)
    def _():
        o_ref[...] = jax.nn.sigmoid(y)


def _proj_in(x, gain, w, l, bd, qg, k1g, k2g):
    return pl.pallas_call(
        _pin_body,
        grid=(TOK // PIN_TM, Z_W // PIN_TN),
        in_specs=[
            pl.BlockSpec((PIN_TM, D_MODEL), lambda i, j: (i, 0)),
            pl.BlockSpec((1, D_MODEL), lambda i, j: (0, 0)),
            pl.BlockSpec((None, D_MODEL, PIN_TN), lambda i, j: (l, 0, j)),
            pl.BlockSpec((LANE, LANE), lambda i, j: (0, 0)),
            pl.BlockSpec((1, PIN_TN), lambda i, j: (0, 0)),
            pl.BlockSpec((1, PIN_TN), lambda i, j: (0, 0)),
            pl.BlockSpec((1, PIN_TN), lambda i, j: (0, 0)),
        ],
        out_specs=pl.BlockSpec((PIN_TM, PIN_TN), lambda i, j: (i, j)),
        out_shape=jax.ShapeDtypeStruct((TOK, Z_W), F32),
        scratch_shapes=[pltpu.VMEM((PIN_TM, D_MODEL), BF16)],
        compiler_params=_cparams(("parallel", "arbitrary")),
        name="proj_in",
    )(x, gain, w, bd, qg, k1g, k2g)


def _cmp_body(z_ref, pos_ref, w1_ref, w2k_ref, w2vT_ref, kg_ref, kc_ref, vcT_ref):
    half = CMP_STRIDE * HEAD_DIM
    lane = lax.broadcasted_iota(jnp.int32, (N_CMP, LANE), 1)
    first = lane < HEAD_DIM
    xk, xv = [], []
    for j in range(CMP_STRIDE // 2):
        ev = z_ref[pl.ds(2 * j, N_CMP, stride=CMP_STRIDE), :]
        od = z_ref[pl.ds(2 * j + 1, N_CMP, stride=CMP_STRIDE), :]
        xk.append(jnp.where(first, ev, pltpu.roll(od, HEAD_DIM, 1)))
        xv.append(jnp.where(first, pltpu.roll(ev, HEAD_DIM, 1), od))

    def hidden(x, kind):
        xa = (x + pos_ref[kind, :, :half]).astype(BF16)
        xb = (x + pos_ref[kind, :, half:]).astype(BF16)
        a = jnp.dot(xa, w1_ref[kind, :half, :], preferred_element_type=F32)
        b = jnp.dot(xb, w1_ref[kind, half:, :], preferred_element_type=F32)
        hid = a + pltpu.roll(b, N_CMP - 1, 0)
        return (hid * jax.nn.sigmoid(hid)).astype(BF16)

    yk = jnp.dot(hidden(jnp.concatenate(xk, axis=1), 0), w2k_ref[...], preferred_element_type=F32)
    ms = jnp.mean(yk * yk, axis=-1, keepdims=True)
    kc_ref[0, 0] = (yk * lax.rsqrt(ms + EPS) * kg_ref[...]).astype(BF16)
    hv = hidden(jnp.concatenate(xv, axis=1), 1)
    vcT_ref[0, 0] = lax.dot_general(w2vT_ref[...], hv, (((1,), (1,)), ((), ())),
                                    preferred_element_type=F32).astype(BF16)


def _compress(z, pos, w1, w2k, w2vT, kg):
    return pl.pallas_call(
        _cmp_body,
        grid=(BATCH, N_KV),
        in_specs=[
            pl.BlockSpec((SEQ, KV_GW), lambda b, g: (b, Z_KV // KV_GW + g)),
            pl.BlockSpec((2, 1, CMP_BLOCK * HEAD_DIM), lambda b, g: (0, 0, 0)),
            pl.BlockSpec((2, CMP_BLOCK * HEAD_DIM, CMP_HIDDEN), lambda b, g: (0, 0, 0)),
            pl.BlockSpec((CMP_HIDDEN, HEAD_DIM), lambda b, g: (0, 0)),
            pl.BlockSpec((HEAD_DIM, CMP_HIDDEN), lambda b, g: (0, 0)),
            pl.BlockSpec((1, HEAD_DIM), lambda b, g: (0, 0)),
        ],
        out_specs=[
            pl.BlockSpec((1, 1, N_CMP, HEAD_DIM), lambda b, g: (b, g, 0, 0)),
            pl.BlockSpec((1, 1, HEAD_DIM, N_CMP), lambda b, g: (b, g, 0, 0)),
        ],
        out_shape=[
            jax.ShapeDtypeStruct((BATCH, N_KV, N_CMP, HEAD_DIM), BF16),
            jax.ShapeDtypeStruct((BATCH, N_KV, HEAD_DIM, N_CMP), BF16),
        ],
        compiler_params=_cparams(("parallel", "parallel")),
        name="compress",
    )(z, pos, w1, w2k, w2vT, kg)


def _np_bucket(dist):
    n = np.maximum(dist, 0)
    max_exact = N_BUCKETS // 2
    nf = np.maximum(n, 1).astype(np.float32)
    large = max_exact + (np.log(nf / np.float32(max_exact)) / np.float32(math.log(MAX_DISTANCE / max_exact))
                         * np.float32(N_BUCKETS - max_exact)).astype(np.int32)
    large = np.minimum(large, N_BUCKETS - 1)
    return np.where(n < max_exact, n, large).astype(np.int32)


def _bucket_tables():
    kj = np.arange(TQ)[:, None]
    qi = np.arange(TQ)[None, :]
    near = np.stack([_np_bucket(qi - kj), _np_bucket(TQ + qi - kj)])
    c = np.arange(N_CMP)[:, None]
    t = (np.arange(NQ)[:, None, None] * TQ + qi[None])
    cmp_b = _np_bucket(t - (c[None] * CMP_STRIDE + CMP_BLOCK - 1))
    return near, cmp_b.astype(np.int32)


def _bias_body(rb_ref, near_ref, cmpb_ref, t0_ref, t1_ref, tc_ref):
    g = pl.program_id(0)

    def lut(bk, h):
        val = jnp.zeros(bk.shape, F32)
        for b in range(N_BUCKETS):
            val = jnp.where(bk == b, rb_ref[b, h], val)
        return val

    for hh in range(HPG):
        h = g * HPG + hh
        sl = slice(hh * TQ, (hh + 1) * TQ)
        far = rb_ref[N_BUCKETS - 1, h]
        t0_ref[0, :, sl] = (lut(near_ref[0], h) - far) * LOG2E
        t1_ref[0, :, sl] = (lut(near_ref[1], h) - far) * LOG2E

        def body(i, carry):
            tc_ref[0, i, :, sl] = lut(cmpb_ref[i], h) * LOG2E
            return carry

        lax.fori_loop(0, NQ, body, 0)


def _bias_tables(rel_bias, near, cmpb):
    return pl.pallas_call(
        _bias_body,
        grid=(N_KV,),
        in_specs=[
            pl.BlockSpec(memory_space=pltpu.SMEM),
            pl.BlockSpec((2, TQ, TQ), lambda g: (0, 0, 0)),
            pl.BlockSpec((NQ, N_CMP, TQ), lambda g: (0, 0, 0)),
        ],
        out_specs=[
            pl.BlockSpec((1, TQ, QL), lambda g: (g, 0, 0)),
            pl.BlockSpec((1, TQ, QL), lambda g: (g, 0, 0)),
            pl.BlockSpec((1, NQ, N_CMP, QL), lambda g: (g, 0, 0, 0)),
        ],
        out_shape=[
            jax.ShapeDtypeStruct((N_KV, TQ, QL), F32),
            jax.ShapeDtypeStruct((N_KV, TQ, QL), F32),
            jax.ShapeDtypeStruct((N_KV, NQ, N_CMP, QL), F32),
        ],
        compiler_params=_cparams(("arbitrary",)),
        name="bias_tables",
    )(rel_bias, near, cmpb)


def _attn_body(zq_ref, zs_ref, zw_ref, zg_ref, kc_ref, vcT_ref, t0_ref, t1_ref, tc_ref, ov_ref, o_ref,
               ks_ref, vsT_ref, kw_ref, vwT_ref, sel_ref, sfar_ref, gsc_ref):
    g = pl.program_id(1)
    i = pl.program_id(2)

    @pl.when(i == 0)
    def _():
        ones = jnp.ones((V_ROWS - HEAD_DIM, TQ), BF16)

        def fill(n, carry):
            for z_ref, k_ref, vT_ref in ((zs_ref, ks_ref, vsT_ref), (zw_ref, kw_ref, vwT_ref)):
                blk = z_ref[pl.ds(pl.multiple_of(n * TQ, TQ), TQ), :]
                k_ref[n] = blk[:, :HEAD_DIM].astype(BF16)
                vT_ref[n, :HEAD_DIM, :] = blk.T[HEAD_DIM:, :].astype(BF16)
                vT_ref[n, HEAD_DIM:, :] = ones
            return carry

        lax.fori_loop(0, NK, fill, 0)

    qt = zq_ref[...].T
    qT = jnp.concatenate([qt[h * HEAD_DIM:(h + 1) * HEAD_DIM] for h in range(HPG)], axis=1).astype(BF16)
    gsc_ref[...] = zg_ref[...].T

    key = lax.broadcasted_iota(jnp.int32, (TQ, QL), 0)
    qi = lax.broadcasted_iota(jnp.int32, (TQ, QL), 1) & (TQ - 1)
    iv = jnp.full((TQ, QL), i, jnp.int32)

    s = jnp.dot(kc_ref[0, 0], qT, preferred_element_type=F32) + tc_ref[0, 0]
    mask_c = (iv * TQ + qi) >= (key * CMP_STRIDE + (CMP_BLOCK - 1))
    s = jnp.where(mask_c, s, NEG)
    m = jnp.max(s, axis=0, keepdims=True)
    e = jnp.where(mask_c, jnp.exp2(s - m), 0.0)
    l = jnp.sum(e, axis=0, keepdims=True)
    p = e * jnp.where(l > 0.0, 1.0 / l, 0.0)
    ocT = jnp.dot(vcT_ref[0, 0], p.astype(BF16), preferred_element_type=F32)

    psum = p[:, 0:TQ] + p[:, TQ:2 * TQ] + p[:, 2 * TQ:3 * TQ] + p[:, 3 * TQ:4 * TQ]
    imp = jnp.dot(ov_ref[...], psum, preferred_element_type=F32,
                  precision=lax.Precision.HIGHEST)
    blk = lax.broadcasted_iota(jnp.int32, (N_SBLK, TQ), 0)
    ql = lax.broadcasted_iota(jnp.int32, (N_SBLK, TQ), 1)
    cur = jnp.full((N_SBLK, TQ), i, jnp.int32) * (TQ // SEL_BLOCK) + ql // SEL_BLOCK
    forced = (blk == 0) | (blk == cur) | (blk == cur - 1)
    score = jnp.where(blk <= cur, jnp.where(forced, FORCE, imp), NEG)
    rank = jnp.zeros((N_SBLK, TQ), jnp.int32)
    for jj in range(N_SBLK):
        row = score[jj:jj + 1, :]
        beats = (row > score) | ((row == score) & (blk > jj))
        rank = rank + beats.astype(jnp.int32)
    sel = (rank < N_SEL) & (blk <= cur)
    sel_ref[...] = sel.astype(F32)

    key1 = lax.broadcasted_iota(jnp.int32, (TQ, TQ), 0)
    q1 = lax.broadcasted_iota(jnp.int32, (TQ, TQ), 1)
    i1 = jnp.full((TQ, TQ), i, jnp.int32)
    causal = key1 <= q1

    def sel_mask(n):
        r0 = sel_ref[pl.ds(2 * n, 1), :]
        r1 = sel_ref[pl.ds(2 * n + 1, 1), :]
        return jnp.where(key1 < SEL_BLOCK, r0, r1) > 0.5

    def scores(k_ref, n, bias, mask):
        sc = jnp.dot(k_ref[n], qT, preferred_element_type=F32)
        if bias is not None:
            sc = sc + bias
        return jnp.concatenate(
            [jnp.where(mask, sc[:, h * TQ:(h + 1) * TQ], NEG) for h in range(HPG)], axis=1)

    def colmax8(sc):
        return jnp.max(sc.reshape(TQ // 8, 8, QL), axis=0)

    def probs(sc, mrow):
        return jnp.exp2(sc - mrow).astype(BF16)

    def pv(vT_ref, ns, ps):
        vv = vT_ref[ns[0]] if len(ns) == 1 else jnp.concatenate([vT_ref[n] for n in ns], axis=1)
        pp = ps[0] if len(ps) == 1 else jnp.concatenate(ps, axis=0)
        return jnp.dot(vv, pp, preferred_element_type=F32)

    def finish(acc):
        return acc[:HEAD_DIM] * (1.0 / acc[HEAD_DIM:HEAD_DIM + 1])

    n1 = jnp.maximum(i - 1, 0)

    nlast = WINDOW // TQ
    n2, n3, n4 = jnp.maximum(i - 2, 0), jnp.maximum(i - 3, 0), jnp.maximum(i - nlast, 0)
    w_d = scores(kw_ref, i, t0_ref[0], causal)
    w_1 = scores(kw_ref, n1, t1_ref[0], i1 >= 1)
    w_2 = scores(kw_ref, n2, None, i1 >= 2)
    w_3 = scores(kw_ref, n3, None, i1 >= 3)
    w_4 = scores(kw_ref, n4, None, (q1 < key1) & (i1 >= nlast))
    mxw = jnp.maximum(jnp.maximum(jnp.maximum(colmax8(w_d), colmax8(w_1)),
                                  jnp.maximum(colmax8(w_2), colmax8(w_3))), colmax8(w_4))
    m_w = jnp.max(mxw, axis=0, keepdims=True)
    owT = finish(pv(vwT_ref, [i], [probs(w_d, m_w)])
                 + pv(vwT_ref, [n1, n2], [probs(w_1, m_w), probs(w_2, m_w)])
                 + pv(vwT_ref, [n3, n4], [probs(w_3, m_w), probs(w_4, m_w)]))

    s_d = scores(ks_ref, i, t0_ref[0], causal & sel_mask(i))
    s_1 = scores(ks_ref, n1, t1_ref[0], sel_mask(n1) & (i1 >= 1))
    nfar = jnp.maximum(i - 1, 0)
    ntrip = lax.shift_right_logical(nfar + (FAR_GROUP - 1), FAR_GROUP.bit_length() - 1)

    def far_scores(tr, mx):
        for c in range(FAR_GROUP):
            n = FAR_GROUP * tr + c
            sc = scores(ks_ref, n, None, sel_mask(n) & (jnp.full((TQ, TQ), n, jnp.int32) < nfar))
            sfar_ref[n] = sc
            mx = jnp.maximum(mx, colmax8(sc))
        return mx

    mx = lax.fori_loop(0, ntrip, far_scores, jnp.maximum(colmax8(s_d), colmax8(s_1)))
    m_s = jnp.max(mx, axis=0, keepdims=True)
    acc = pv(vsT_ref, [i, n1], [probs(s_d, m_s), probs(s_1, m_s)])

    def far_pv(tr, acc_):
        ns = [FAR_GROUP * tr + c for c in range(FAR_GROUP)]
        return acc_ + pv(vsT_ref, ns, [probs(sfar_ref[n], m_s) for n in ns])

    osT = finish(lax.fori_loop(0, ntrip, far_pv, acc))

    def gate_row(br):
        base = g * GATE_GW + br * HPG
        return jnp.concatenate([gsc_ref[pl.ds(base + h, 1), :] for h in range(HPG)], axis=1)

    aT = gate_row(0) * ocT + gate_row(1) * osT + gate_row(2) * owT
    a4 = jnp.concatenate([aT[:, h * TQ:(h + 1) * TQ] for h in range(HPG)], axis=0)
    o_ref[...] = a4.T.astype(o_ref.dtype)


def _attention(z, kc, vcT, t0, t1, tc, ov):
    qw = HPG * HEAD_DIM
    return pl.pallas_call(
        _attn_body,
        grid=(BATCH, N_KV, NQ),
        in_specs=[
            pl.BlockSpec((TQ, qw), lambda b, g, i: (b * NQ + i, Z_Q // qw + g)),
            pl.BlockSpec((SEQ, KV_GW), lambda b, g, i: (b, (Z_KV + 2 * KV_W) // KV_GW + g)),
            pl.BlockSpec((SEQ, KV_GW), lambda b, g, i: (b, (Z_KV + 4 * KV_W) // KV_GW + g)),
            pl.BlockSpec((TQ, LANE), lambda b, g, i: (b * NQ + i, Z_NG // LANE)),
            pl.BlockSpec((1, 1, N_CMP, HEAD_DIM), lambda b, g, i: (b, g, 0, 0)),
            pl.BlockSpec((1, 1, HEAD_DIM, N_CMP), lambda b, g, i: (b, g, 0, 0)),
            pl.BlockSpec((1, TQ, QL), lambda b, g, i: (g, 0, 0)),
            pl.BlockSpec((1, TQ, QL), lambda b, g, i: (g, 0, 0)),
            pl.BlockSpec((1, 1, N_CMP, QL), lambda b, g, i: (g, i, 0, 0)),
            pl.BlockSpec((N_SBLK, N_CMP), lambda b, g, i: (0, 0)),
        ],
        out_specs=pl.BlockSpec((TQ, qw), lambda b, g, i: (b * NQ + i, g)),
        out_shape=jax.ShapeDtypeStruct((TOK, NSA_W), BF16),
        scratch_shapes=[
            pltpu.VMEM((NK, TQ, HEAD_DIM), BF16),
            pltpu.VMEM((NK, V_ROWS, TQ), BF16),
            pltpu.VMEM((NK, TQ, HEAD_DIM), BF16),
            pltpu.VMEM((NK, V_ROWS, TQ), BF16),
            pltpu.VMEM((N_SBLK, TQ), F32),
            pltpu.VMEM((NK, TQ, QL), F32),
            pltpu.VMEM((LANE, TQ), F32),
        ],
        compiler_params=_cparams(("parallel", "parallel", "arbitrary")),
        name="nsa_attention",
    )(z, z, z, z, kc, vcT, t0, t1, tc, ov)


def _sgu_body(u_ref, v_ref, g_ref, b_ref, w_ref, bs_ref, o_ref):
    r = lax.broadcasted_iota(jnp.int32, (GM_CHUNK, GM_CHUNK), 0)
    c = lax.broadcasted_iota(jnp.int32, (GM_CHUNK, GM_CHUNK), 1)
    tril = r >= c
    ws = [jnp.where(tril, w_ref[gi], 0.0).astype(BF16) for gi in range(GM_GROUPS)]
    for ch in range(SGU_TM // GM_CHUNK):
        rows = slice(ch * GM_CHUNK, (ch + 1) * GM_CHUNK)
        v = v_ref[rows, :]
        mu = jnp.mean(v, axis=-1, keepdims=True)
        vc = v - mu
        var = jnp.mean(vc * vc, axis=-1, keepdims=True)
        vn = (vc * lax.rsqrt(var + EPS) * g_ref[...] + b_ref[...]).astype(BF16)
        for gi in range(GM_GROUPS):
            cols = slice(gi * GM_GW, (gi + 1) * GM_GW)
            mixed = jnp.dot(ws[gi], vn[:, cols], preferred_element_type=F32) + bs_ref[:, cols]
            o_ref[rows, cols] = (u_ref[rows, cols] * mixed).astype(o_ref.dtype)


def _sgu(z, ln_g, ln_b, w_s, bs_exp):
    return pl.pallas_call(
        _sgu_body,
        grid=(TOK // SGU_TM,),
        in_specs=[
            pl.BlockSpec((SGU_TM, GM_W), lambda i: (i, Z_U // GM_W)),
            pl.BlockSpec((SGU_TM, GM_W), lambda i: (i, Z_V // GM_W)),
            pl.BlockSpec((1, GM_W), lambda i: (0, 0)),
            pl.BlockSpec((1, GM_W), lambda i: (0, 0)),
            pl.BlockSpec((GM_GROUPS, GM_CHUNK, GM_CHUNK), lambda i: (0, 0, 0)),
            pl.BlockSpec((GM_CHUNK, GM_W), lambda i: (0, 0)),
        ],
        out_specs=pl.BlockSpec((SGU_TM, GM_W), lambda i: (i, 0)),
        out_shape=jax.ShapeDtypeStruct((TOK, GM_W), BF16),
        compiler_params=_cparams(("parallel",)),
        name="sgu",
    )(z, z, ln_g, ln_b, w_s, bs_exp)


def _out_body(a_ref, sg_ref, m1_ref, m2_ref, x_ref, wn_ref, ws_ref, wo_ref, o_ref):
    pa = jnp.dot(a_ref[...], wn_ref[...], preferred_element_type=F32)
    ps = jnp.dot(sg_ref[...], ws_ref[...], preferred_element_type=F32)
    merged = (m1_ref[...] * pa + m2_ref[...] * ps).astype(BF16)
    o_ref[...] = x_ref[...] + jnp.dot(merged, wo_ref[...], preferred_element_type=F32)


def _merge_out(a, sg, z, x, wn, ws, wo, l):
    const = lambda i: (l, 0, 0)
    return pl.pallas_call(
        _out_body,
        grid=(TOK // OUT_TM,),
        in_specs=[
            pl.BlockSpec((OUT_TM, NSA_W), lambda i: (i, 0)),
            pl.BlockSpec((OUT_TM, GM_W), lambda i: (i, 0)),
            pl.BlockSpec((OUT_TM, D_MODEL), lambda i: (i, 0)),
            pl.BlockSpec((OUT_TM, D_MODEL), lambda i: (i, 1)),
            pl.BlockSpec((OUT_TM, D_MODEL), lambda i: (i, 0)),
            pl.BlockSpec((None, NSA_W, D_MODEL), const),
            pl.BlockSpec((None, GM_W, D_MODEL), const),
            pl.BlockSpec((None, D_MODEL, D_MODEL), const),
        ],
        out_specs=pl.BlockSpec((OUT_TM, D_MODEL), lambda i: (i, 0)),
        out_shape=jax.ShapeDtypeStruct((TOK, D_MODEL), F32),
        compiler_params=_cparams(("parallel",)),
        name="merge_out",
    )(a, sg, z, z, x, wn, ws, wo)


def _overlap_T():
    ci = np.arange(N_CMP)[None, :] * CMP_STRIDE
    sj = np.arange(N_SBLK)[:, None] * SEL_BLOCK
    ov = np.clip(np.minimum(ci + CMP_BLOCK, sj + SEL_BLOCK) - np.maximum(ci, sj), 0, None)
    ov = ov.astype(np.float32) / CMP_BLOCK
    ov[:, N_CMP - 1] = 0.0
    return ov


def _block_diag_ones():
    r = np.arange(LANE)
    return (r[:, None] // HEAD_DIM == r[None, :] // HEAD_DIM).astype(np.float32)


def _prep_w_in(w):
    L = w.shape[0]
    kv = w[:, :, OFF_KV:OFF_NG].reshape(L, D_MODEL, 3, 2, N_KV, HEAD_DIM)
    kv = kv.transpose(0, 1, 2, 4, 3, 5).reshape(L, D_MODEL, 6 * KV_W)
    ng = w[:, :, OFF_NG:OFF_UV].reshape(L, D_MODEL, N_KV, HPG, 3).transpose(0, 1, 2, 4, 3)
    ng = ng.reshape(L, D_MODEL, N_KV, 3 * HPG)
    ng = jnp.pad(ng, ((0, 0), (0, 0), (0, 0), (0, GATE_GW - 3 * HPG))).reshape(L, D_MODEL, N_KV * GATE_GW)
    ng = jnp.pad(ng, ((0, 0), (0, 0), (0, PIN_TN - N_KV * GATE_GW)))
    return jnp.concatenate(
        [w[:, :, OFF_MG:], w[:, :, OFF_UV:OFF_MG], w[:, :, :OFF_KV], kv, ng], axis=2).astype(BF16)


def _mixer(x, z, l, p, tables):
    t0, t1, tc, ov = tables
    pos = jnp.stack([p["cmp_pos_k"][l], p["cmp_pos_v"][l]]).reshape(2, 1, CMP_BLOCK * HEAD_DIM)
    w1 = jnp.stack([p["cmp_k_w1"][l], p["cmp_v_w1"][l]]).astype(BF16)
    kc, vcT = _compress(z, pos, w1, p["cmp_k_w2"][l].astype(BF16), p["cmp_v_w2"][l].T.astype(BF16),
                        p["k_norm"][l, 0].reshape(1, HEAD_DIM))
    a = _attention(z, kc, vcT, t0, t1, tc, ov)
    bs_exp = jnp.repeat(p["sgu_b"][l].T, GM_GW, axis=1)
    sg = _sgu(z, p["sgu_norm_g"][l].reshape(1, GM_W), p["sgu_norm_b"][l].reshape(1, GM_W),
              p["sgu_w"][l], bs_exp)
    return _merge_out(a, sg, z, x, p["wn"], p["ws"], p["wo"], l)


def kernel(x, rel_bias, ffn1_norm, ffn1_w_gate, ffn1_w_up, ffn1_w_down, mix_norm, w_in, q_norm, k_norm, cmp_pos_k, cmp_pos_v, cmp_k_w1, cmp_k_w2, cmp_v_w1, cmp_v_w2, sgu_norm_g, sgu_norm_b, sgu_w, sgu_b, w_proj_nsa, w_proj_sgu, w_out, ffn2_norm, ffn2_w_gate, ffn2_w_up, ffn2_w_down):
    p = dict(cmp_pos_k=cmp_pos_k, cmp_pos_v=cmp_pos_v, cmp_k_w1=cmp_k_w1, cmp_k_w2=cmp_k_w2,
             cmp_v_w1=cmp_v_w1, cmp_v_w2=cmp_v_w2, k_norm=k_norm, sgu_norm_g=sgu_norm_g,
             sgu_norm_b=sgu_norm_b, sgu_w=sgu_w, sgu_b=sgu_b, wn=w_proj_nsa.astype(BF16),
             ws=w_proj_sgu.astype(BF16), wo=w_out.astype(BF16))
    near, cmpb = _bucket_tables()
    t0, t1, tc = _bias_tables(rel_bias, jnp.asarray(near), jnp.asarray(cmpb))
    tables = (t0, t1, tc, jnp.asarray(_overlap_T()))
    bd = jnp.asarray(_block_diag_ones()).astype(BF16)
    w_in_r = _prep_w_in(w_in)
    f1 = (ffn1_w_gate.astype(BF16), ffn1_w_up.astype(BF16), ffn1_w_down.astype(BF16))
    f2 = (ffn2_w_gate.astype(BF16), ffn2_w_up.astype(BF16), ffn2_w_down.astype(BF16))

    h = x.reshape(TOK, D_MODEL)
    for l in range(DEPTH):
        h = _ffn(h, ffn1_norm[l].reshape(1, D_MODEL), *f1, l)
        z = _proj_in(h, mix_norm[l].reshape(1, D_MODEL), w_in_r, l, bd,
                     jnp.tile(q_norm[l], PIN_TN // HEAD_DIM).reshape(1, PIN_TN),
                     jnp.tile(k_norm[l, 1], PIN_TN // HEAD_DIM).reshape(1, PIN_TN),
                     jnp.tile(k_norm[l, 2], PIN_TN // HEAD_DIM).reshape(1, PIN_TN))
        h = _mixer(h, z, l, p, tables)
        h = _ffn(h, ffn2_norm[l].reshape(1, D_MODEL), *f2, l)
    return h.reshape(BATCH, SEQ, D_MODEL)
```

```python
import math

import numpy as np
import jax
import jax.numpy as jnp
from jax import lax
from jax.experimental import pallas as pl
from jax.experimental.pallas import tpu as pltpu

F32 = jnp.float32
BF16 = jnp.bfloat16

D_MODEL = 2048
BATCH = 4
SEQ = 2048
DEPTH = 2
TOK = BATCH * SEQ
HEAD_DIM = 64
N_HEADS = 16
N_KV = 4
HPG = 4
NSA_W = 1024
KV_W = 256
CMP_BLOCK = 32
CMP_STRIDE = 16
CMP_HIDDEN = 256
SEL_BLOCK = 64
N_SEL = 16
WINDOW = 512
GM_W = 1024
GM_CHUNK = 128
GM_GW = 128
GM_GROUPS = 8
N_BUCKETS = 32
MAX_DISTANCE = 128
D_FF = 5504
EPS = 1e-6
NEG = -1e30
FORCE = 1e4
OFF_KV = NSA_W
OFF_NG = OFF_KV + 6 * KV_W
OFF_UV = OFF_NG + 3 * N_HEADS
OFF_MG = OFF_UV + 2 * GM_W
LOG2E = 1.4426950408889634

LANE = 128
VMEM_LIMIT = 56 * 1024 * 1024

FFN_TF = 512
FFN_UP_TM = 1024
FFN_DN_TM = 512
FFN_DN_TN = 512
PIN_TM = 1024
PIN_TN = 512
PIN_RC = 256
Z_MG = 0
Z_U = Z_MG + 2 * D_MODEL
Z_V = Z_U + GM_W
Z_Q = Z_V + GM_W
Z_KV = Z_Q + NSA_W
Z_NG = Z_KV + 6 * KV_W
Z_W = Z_NG + PIN_TN
KV_GW = 2 * HEAD_DIM
GATE_GW = 16
TQ = 128
NQ = SEQ // TQ
NK = SEQ // TQ
QL = HPG * TQ
N_CMP = 128
N_SBLK = SEQ // SEL_BLOCK
V_ROWS = HEAD_DIM + 16
FAR_GROUP = 4
SGU_TM = 512
OUT_TM = 256


def _cparams(sem):
    return pltpu.CompilerParams(dimension_semantics=sem, vmem_limit_bytes=VMEM_LIMIT)


def _cast_rows(src_ref, dst_ref, chunk):
    def body(r, carry):
        rows = pl.ds(pl.multiple_of(r * chunk, chunk), chunk)
        dst_ref[rows, :] = src_ref[rows, :].astype(BF16)
        return carry

    lax.fori_loop(0, src_ref.shape[0] // chunk, body, 0)


def _ffn_up_body(x_ref, g_ref, wg_ref, wu_ref, a_ref, wgb_ref, wub_ref):
    @pl.when(pl.program_id(1) == 0)
    def _():
        _cast_rows(wg_ref, wgb_ref, LANE)
        _cast_rows(wu_ref, wub_ref, LANE)

    x = x_ref[...]
    ms = jnp.mean(x * x, axis=-1, keepdims=True)
    h = (x * lax.rsqrt(ms + EPS) * g_ref[...]).astype(BF16)
    g = jnp.dot(h, wgb_ref[...], preferred_element_type=F32)
    u = jnp.dot(h, wub_ref[...], preferred_element_type=F32)
    a_ref[...] = (g * jax.nn.sigmoid(g) * u).astype(BF16)


def _ffn_down_body(a_ref, wd_ref, x_ref, o_ref, wdb_ref):
    @pl.when(pl.program_id(1) == 0)
    def _():
        _cast_rows(wd_ref, wdb_ref, LANE)

    o_ref[...] = x_ref[...] + 0.5 * jnp.dot(a_ref[...], wdb_ref[...], preferred_element_type=F32)


def _ffn(x, gain, wg, wu, wd, l):
    nf = pl.cdiv(D_FF, FFN_TF)
    a = pl.pallas_call(
        _ffn_up_body,
        grid=(nf, TOK // FFN_UP_TM),
        in_specs=[
            pl.BlockSpec((FFN_UP_TM, D_MODEL), lambda f, m: (m, 0)),
            pl.BlockSpec((1, D_MODEL), lambda f, m: (0, 0)),
            pl.BlockSpec((None, D_MODEL, FFN_TF), lambda f, m: (l, 0, f)),
            pl.BlockSpec((None, D_MODEL, FFN_TF), lambda f, m: (l, 0, f)),
        ],
        out_specs=pl.BlockSpec((FFN_UP_TM, FFN_TF), lambda f, m: (m, f)),
        out_shape=jax.ShapeDtypeStruct((TOK, D_FF), BF16),
        scratch_shapes=[pltpu.VMEM((D_MODEL, FFN_TF), BF16), pltpu.VMEM((D_MODEL, FFN_TF), BF16)],
        compiler_params=_cparams(("arbitrary", "arbitrary")),
        name="ffn_up",
    )(x, gain, wg, wu)
    return pl.pallas_call(
        _ffn_down_body,
        grid=(D_MODEL // FFN_DN_TN, TOK // FFN_DN_TM),
        in_specs=[
            pl.BlockSpec((FFN_DN_TM, D_FF), lambda n, m: (m, 0)),
            pl.BlockSpec((None, D_FF, FFN_DN_TN), lambda n, m: (l, 0, n)),
            pl.BlockSpec((FFN_DN_TM, FFN_DN_TN), lambda n, m: (m, n)),
        ],
        out_specs=pl.BlockSpec((FFN_DN_TM, FFN_DN_TN), lambda n, m: (m, n)),
        out_shape=jax.ShapeDtypeStruct((TOK, D_MODEL), F32),
        scratch_shapes=[pltpu.VMEM((D_FF, FFN_DN_TN), BF16)],
        compiler_params=_cparams(("arbitrary", "arbitrary")),
        name="ffn_down",
    )(a, wd, x)


def _head_rms(y, bd):
    y2 = (y * y).astype(BF16)
    ss = jnp.concatenate(
        [jnp.dot(y2[:, k * LANE:(k + 1) * LANE], bd, preferred_element_type=F32)
         for k in range(y.shape[1] // LANE)], axis=1)
    return y * lax.rsqrt(ss * (1.0 / HEAD_DIM) + EPS)


def _pin_body(x_ref, g_ref, w_ref, bd_ref, qg_ref, k1g_ref, k2g_ref, o_ref, h_ref):
    j = pl.program_id(1)

    @pl.when(j == 0)
    def _():
        x = x_ref[...]
        ms = jnp.mean(x * x, axis=-1, keepdims=True)
        h_ref[...] = (x * lax.rsqrt(ms + EPS) * g_ref[...]).astype(BF16)

    t_u = Z_U // PIN_TN
    t_q = Z_Q // PIN_TN
    t_kv = Z_KV // PIN_TN
    t_ng = Z_NG // PIN_TN

    def emit(act):
        for r in range(PIN_TM // PIN_RC):
            rows = slice(r * PIN_RC, (r + 1) * PIN_RC)
            o_ref[rows, :] = act(jnp.dot(h_ref[rows, :], w_ref[...], preferred_element_type=F32))

    @pl.when((j < t_u) | (j == t_ng))
    def _():
        emit(jax.nn.sigmoid)

    @pl.when((j >= t_u) & (j < t_q))
    def _():
        emit(jax.nn.gelu)

    @pl.when((j >= t_q) & (j < t_kv))
    def _():
        emit(lambda y: _head_rms(y, bd_ref[...]) * (qg_ref[...] * (HEAD_DIM ** -0.5 * LOG2E)))

    @pl.when(j == t_kv)
    def _():
        emit(lambda y: y)

    def kv_act(kg_ref):
        is_k = (lax.broadcasted_iota(jnp.int32, (1, PIN_TN), 1) & (KV_GW - 1)) < HEAD_DIM
        return lambda y: jnp.where(is_k, _head_rms(y, bd_ref[...]) * kg_ref[...], y)

    @pl.when(j == t_kv + 1)
    def _():
        emit(kv_act(k1g_ref))

    @pl.when(j == t_kv + 2)
    def _():
        emit(kv_act(k2g_ref))


def _proj_in(x, gain, w, l, bd, qg, k1g, k2g):
    return pl.pallas_call(
        _pin_body,
        grid=(TOK // PIN_TM, Z_W // PIN_TN),
        in_specs=[
            pl.BlockSpec((PIN_TM, D_MODEL), lambda i, j: (i, 0)),
            pl.BlockSpec((1, D_MODEL), lambda i, j: (0, 0)),
            pl.BlockSpec((None, D_MODEL, PIN_TN), lambda i, j: (l, 0, j)),
            pl.BlockSpec((LANE, LANE), lambda i, j: (0, 0)),
            pl.BlockSpec((1, PIN_TN), lambda i, j: (0, 0)),
            pl.BlockSpec((1, PIN_TN), lambda i, j: (0, 0)),
            pl.BlockSpec((1, PIN_TN), lambda i, j: (0, 0)),
        ],
        out_specs=pl.BlockSpec((PIN_TM, PIN_TN), lambda i, j: (i, j)),
        out_shape=jax.ShapeDtypeStruct((TOK, Z_W), F32),
        scratch_shapes=[pltpu.VMEM((PIN_TM, D_MODEL), BF16)],
        compiler_params=_cparams(("parallel", "arbitrary")),
        name="proj_in",
    )(x, gain, w, bd, qg, k1g, k2g)


def _cmp_body(z_ref, pos_ref, w1_ref, w2k_ref, w2vT_ref, kg_ref, kc_ref, vcT_ref):
    half = CMP_STRIDE * HEAD_DIM
    lane = lax.broadcasted_iota(jnp.int32, (N_CMP, LANE), 1)
    first = lane < HEAD_DIM
    xk, xv = [], []
    for j in range(CMP_STRIDE // 2):
        ev = z_ref[pl.ds(2 * j, N_CMP, stride=CMP_STRIDE), :]
        od = z_ref[pl.ds(2 * j + 1, N_CMP, stride=CMP_STRIDE), :]
        xk.append(jnp.where(first, ev, pltpu.roll(od, HEAD_DIM, 1)))
        xv.append(jnp.where(first, pltpu.roll(ev, HEAD_DIM, 1), od))

    def hidden(x, kind):
        xa = (x + pos_ref[kind, :, :half]).astype(BF16)
        xb = (x + pos_ref[kind, :, half:]).astype(BF16)
        a = jnp.dot(xa, w1_ref[kind, :half, :], preferred_element_type=F32)
        b = jnp.dot(xb, w1_ref[kind, half:, :], preferred_element_type=F32)
        hid = a + pltpu.roll(b, N_CMP - 1, 0)
        return (hid * jax.nn.sigmoid(hid)).astype(BF16)

    yk = jnp.dot(hidden(jnp.concatenate(xk, axis=1), 0), w2k_ref[...], preferred_element_type=F32)
    ms = jnp.mean(yk * yk, axis=-1, keepdims=True)
    kc_ref[0, 0] = (yk * lax.rsqrt(ms + EPS) * kg_ref[...]).astype(BF16)
    hv = hidden(jnp.concatenate(xv, axis=1), 1)
    vcT_ref[0, 0] = lax.dot_general(w2vT_ref[...], hv, (((1,), (1,)), ((), ())),
                                    preferred_element_type=F32).astype(BF16)


def _compress(z, pos, w1, w2k, w2vT, kg):
    return pl.pallas_call(
        _cmp_body,
        grid=(BATCH, N_KV),
        in_specs=[
            pl.BlockSpec((SEQ, KV_GW), lambda b, g: (b, Z_KV // KV_GW + g)),
            pl.BlockSpec((2, 1, CMP_BLOCK * HEAD_DIM), lambda b, g: (0, 0, 0)),
            pl.BlockSpec((2, CMP_BLOCK * HEAD_DIM, CMP_HIDDEN), lambda b, g: (0, 0, 0)),
            pl.BlockSpec((CMP_HIDDEN, HEAD_DIM), lambda b, g: (0, 0)),
            pl.BlockSpec((HEAD_DIM, CMP_HIDDEN), lambda b, g: (0, 0)),
            pl.BlockSpec((1, HEAD_DIM), lambda b, g: (0, 0)),
        ],
        out_specs=[
            pl.BlockSpec((1, 1, N_CMP, HEAD_DIM), lambda b, g: (b, g, 0, 0)),
            pl.BlockSpec((1, 1, HEAD_DIM, N_CMP), lambda b, g: (b, g, 0, 0)),
        ],
        out_shape=[
            jax.ShapeDtypeStruct((BATCH, N_KV, N_CMP, HEAD_DIM), BF16),
            jax.ShapeDtypeStruct((BATCH, N_KV, HEAD_DIM, N_CMP), BF16),
        ],
        compiler_params=_cparams(("parallel", "parallel")),
        name="compress",
    )(z, pos, w1, w2k, w2vT, kg)


def _np_bucket(dist):
    n = np.maximum(dist, 0)
    max_exact = N_BUCKETS // 2
    nf = np.maximum(n, 1).astype(np.float32)
    large = max_exact + (np.log(nf / np.float32(max_exact)) / np.float32(math.log(MAX_DISTANCE / max_exact))
                         * np.float32(N_BUCKETS - max_exact)).astype(np.int32)
    large = np.minimum(large, N_BUCKETS - 1)
    return np.where(n < max_exact, n, large).astype(np.int32)


def _bucket_tables():
    kj = np.arange(TQ)[:, None]
    qi = np.arange(TQ)[None, :]
    near = np.stack([_np_bucket(qi - kj), _np_bucket(TQ + qi - kj)])
    c = np.arange(N_CMP)[:, None]
    t = (np.arange(NQ)[:, None, None] * TQ + qi[None])
    cmp_b = _np_bucket(t - (c[None] * CMP_STRIDE + CMP_BLOCK - 1))
    return near, cmp_b.astype(np.int32)


def _bias_body(rb_ref, near_ref, cmpb_ref, t0_ref, t1_ref, tc_ref):
    g = pl.program_id(0)

    def lut(bk, h):
        val = jnp.zeros(bk.shape, F32)
        for b in range(N_BUCKETS):
            val = jnp.where(bk == b, rb_ref[b, h], val)
        return val

    for hh in range(HPG):
        h = g * HPG + hh
        sl = slice(hh * TQ, (hh + 1) * TQ)
        far = rb_ref[N_BUCKETS - 1, h]
        t0_ref[0, :, sl] = (lut(near_ref[0], h) - far) * LOG2E
        t1_ref[0, :, sl] = (lut(near_ref[1], h) - far) * LOG2E

        def body(i, carry):
            tc_ref[0, i, :, sl] = lut(cmpb_ref[i], h) * LOG2E
            return carry

        lax.fori_loop(0, NQ, body, 0)


def _bias_tables(rel_bias, near, cmpb):
    return pl.pallas_call(
        _bias_body,
        grid=(N_KV,),
        in_specs=[
            pl.BlockSpec(memory_space=pltpu.SMEM),
            pl.BlockSpec((2, TQ, TQ), lambda g: (0, 0, 0)),
            pl.BlockSpec((NQ, N_CMP, TQ), lambda g: (0, 0, 0)),
        ],
        out_specs=[
            pl.BlockSpec((1, TQ, QL), lambda g: (g, 0, 0)),
            pl.BlockSpec((1, TQ, QL), lambda g: (g, 0, 0)),
            pl.BlockSpec((1, NQ, N_CMP, QL), lambda g: (g, 0, 0, 0)),
        ],
        out_shape=[
            jax.ShapeDtypeStruct((N_KV, TQ, QL), F32),
            jax.ShapeDtypeStruct((N_KV, TQ, QL), F32),
            jax.ShapeDtypeStruct((N_KV, NQ, N_CMP, QL), F32),
        ],
        compiler_params=_cparams(("arbitrary",)),
        name="bias_tables",
    )(rel_bias, near, cmpb)


def _attn_body(zq_ref, zs_ref, zw_ref, zg_ref, kc_ref, vcT_ref, t0_ref, t1_ref, tc_ref, ov_ref, o_ref,
               ks_ref, vsT_ref, kw_ref, vwT_ref, sel_ref, sfar_ref, gsc_ref):
    g = pl.program_id(1)
    i = pl.program_id(2)

    @pl.when(i == 0)
    def _():
        ones = jnp.ones((V_ROWS - HEAD_DIM, TQ), BF16)

        def fill(n, carry):
            for z_ref, k_ref, vT_ref in ((zs_ref, ks_ref, vsT_ref), (zw_ref, kw_ref, vwT_ref)):
                blk = z_ref[pl.ds(pl.multiple_of(n * TQ, TQ), TQ), :]
                k_ref[n] = blk[:, :HEAD_DIM].astype(BF16)
                vT_ref[n, :HEAD_DIM, :] = blk.T[HEAD_DIM:, :].astype(BF16)
                vT_ref[n, HEAD_DIM:, :] = ones
            return carry

        lax.fori_loop(0, NK, fill, 0)

    qt = zq_ref[...].T
    qT = jnp.concatenate([qt[h * HEAD_DIM:(h + 1) * HEAD_DIM] for h in range(HPG)], axis=1).astype(BF16)
    gsc_ref[...] = zg_ref[...].T

    key = lax.broadcasted_iota(jnp.int32, (TQ, QL), 0)
    qi = lax.broadcasted_iota(jnp.int32, (TQ, QL), 1) & (TQ - 1)
    iv = jnp.full((TQ, QL), i, jnp.int32)

    s = jnp.dot(kc_ref[0, 0], qT, preferred_element_type=F32) + tc_ref[0, 0]
    mask_c = (iv * TQ + qi) >= (key * CMP_STRIDE + (CMP_BLOCK - 1))
    s = jnp.where(mask_c, s, NEG)
    m = jnp.max(s, axis=0, keepdims=True)
    e = jnp.where(mask_c, jnp.exp2(s - m), 0.0)
    l = jnp.sum(e, axis=0, keepdims=True)
    p = e * jnp.where(l > 0.0, 1.0 / l, 0.0)
    ocT = jnp.dot(vcT_ref[0, 0], p.astype(BF16), preferred_element_type=F32)

    psum = p[:, 0:TQ] + p[:, TQ:2 * TQ] + p[:, 2 * TQ:3 * TQ] + p[:, 3 * TQ:4 * TQ]
    imp = jnp.dot(ov_ref[...], psum, preferred_element_type=F32,
                  precision=lax.Precision.HIGHEST)
    blk = lax.broadcasted_iota(jnp.int32, (N_SBLK, TQ), 0)
    ql = lax.broadcasted_iota(jnp.int32, (N_SBLK, TQ), 1)
    cur = jnp.full((N_SBLK, TQ), i, jnp.int32) * (TQ // SEL_BLOCK) + ql // SEL_BLOCK
    forced = (blk == 0) | (blk == cur) | (blk == cur - 1)
    score = jnp.where(blk <= cur, jnp.where(forced, FORCE, imp), NEG)
    rank = jnp.zeros((N_SBLK, TQ), jnp.int32)
    for jj in range(N_SBLK):
        row = score[jj:jj + 1, :]
        beats = (row > score) | ((row == score) & (blk > jj))
        rank = rank + beats.astype(jnp.int32)
    sel = (rank < N_SEL) & (blk <= cur)
    sel_ref[...] = sel.astype(F32)

    key1 = lax.broadcasted_iota(jnp.int32, (TQ, TQ), 0)
    q1 = lax.broadcasted_iota(jnp.int32, (TQ, TQ), 1)
    i1 = jnp.full((TQ, TQ), i, jnp.int32)
    causal = key1 <= q1

    def sel_mask(n):
        r0 = sel_ref[pl.ds(2 * n, 1), :]
        r1 = sel_ref[pl.ds(2 * n + 1, 1), :]
        return jnp.where(key1 < SEL_BLOCK, r0, r1) > 0.5

    def scores(k_ref, n, bias, mask):
        sc = jnp.dot(k_ref[n], qT, preferred_element_type=F32)
        if bias is not None:
            sc = sc + bias
        return jnp.concatenate(
            [jnp.where(mask, sc[:, h * TQ:(h + 1) * TQ], NEG) for h in range(HPG)], axis=1)

    def colmax8(sc):
        return jnp.max(sc.reshape(TQ // 8, 8, QL), axis=0)

    def probs(sc, mrow):
        return jnp.exp2(sc - mrow).astype(BF16)

    def pv(vT_ref, ns, ps):
        vv = vT_ref[ns[0]] if len(ns) == 1 else jnp.concatenate([vT_ref[n] for n in ns], axis=1)
        pp = ps[0] if len(ps) == 1 else jnp.concatenate(ps, axis=0)
        return jnp.dot(vv, pp, preferred_element_type=F32)

    def finish(acc):
        return acc[:HEAD_DIM] * (1.0 / acc[HEAD_DIM:HEAD_DIM + 1])

    n1 = jnp.maximum(i - 1, 0)

    nlast = WINDOW // TQ
    n2, n3, n4 = jnp.maximum(i - 2, 0), jnp.maximum(i - 3, 0), jnp.maximum(i - nlast, 0)
    w_d = scores(kw_ref, i, t0_ref[0], causal)
    w_1 = scores(kw_ref, n1, t1_ref[0], i1 >= 1)
    w_2 = scores(kw_ref, n2, None, i1 >= 2)
    w_3 = scores(kw_ref, n3, None, i1 >= 3)
    w_4 = scores(kw_ref, n4, None, (q1 < key1) & (i1 >= nlast))
    mxw = jnp.maximum(jnp.maximum(jnp.maximum(colmax8(w_d), colmax8(w_1)),
                                  jnp.maximum(colmax8(w_2), colmax8(w_3))), colmax8(w_4))
    m_w = jnp.max(mxw, axis=0, keepdims=True)
    owT = finish(pv(vwT_ref, [i], [probs(w_d, m_w)])
                 + pv(vwT_ref, [n1, n2], [probs(w_1, m_w), probs(w_2, m_w)])
                 + pv(vwT_ref, [n3, n4], [probs(w_3, m_w), probs(w_4, m_w)]))

    s_d = scores(ks_ref, i, t0_ref[0], causal & sel_mask(i))
    s_1 = scores(ks_ref, n1, t1_ref[0], sel_mask(n1) & (i1 >= 1))
    nfar = jnp.maximum(i - 1, 0)
    ntrip = lax.shift_right_logical(nfar + (FAR_GROUP - 1), FAR_GROUP.bit_length() - 1)

    def far_scores(tr, mx):
        for c in range(FAR_GROUP):
            n = FAR_GROUP * tr + c
            sc = scores(ks_ref, n, None, sel_mask(n) & (jnp.full((TQ, TQ), n, jnp.int32) < nfar))
            sfar_ref[n] = sc
            mx = jnp.maximum(mx, colmax8(sc))
        return mx

    mx = lax.fori_loop(0, ntrip, far_scores, jnp.maximum(colmax8(s_d), colmax8(s_1)))
    m_s = jnp.max(mx, axis=0, keepdims=True)
    acc = pv(vsT_ref, [i, n1], [probs(s_d, m_s), probs(s_1, m_s)])

    def far_pv(tr, acc_):
        ns = [FAR_GROUP * tr + c for c in range(FAR_GROUP)]
        return acc_ + pv(vsT_ref, ns, [probs(sfar_ref[n], m_s) for n in ns])

    osT = finish(lax.fori_loop(0, ntrip, far_pv, acc))

    def gate_row(br):
        base = g * GATE_GW + br * HPG
        return jnp.concatenate([gsc_ref[pl.ds(base + h, 1), :] for h in range(HPG)], axis=1)

    aT = gate_row(0) * ocT + gate_row(1) * osT + gate_row(2) * owT
    a4 = jnp.concatenate([aT[:, h * TQ:(h + 1) * TQ] for h in range(HPG)], axis=0)
    o_ref[...] = a4.T.astype(o_ref.dtype)


def _attention(z, kc, vcT, t0, t1, tc, ov):
    qw = HPG * HEAD_DIM
    return pl.pallas_call(
        _attn_body,
        grid=(BATCH, N_KV, NQ),
        in_specs=[
            pl.BlockSpec((TQ, qw), lambda b, g, i: (b * NQ + i, Z_Q // qw + g)),
            pl.BlockSpec((SEQ, KV_GW), lambda b, g, i: (b, (Z_KV + 2 * KV_W) // KV_GW + g)),
            pl.BlockSpec((SEQ, KV_GW), lambda b, g, i: (b, (Z_KV + 4 * KV_W) // KV_GW + g)),
            pl.BlockSpec((TQ, LANE), lambda b, g, i: (b * NQ + i, Z_NG // LANE)),
            pl.BlockSpec((1, 1, N_CMP, HEAD_DIM), lambda b, g, i: (b, g, 0, 0)),
            pl.BlockSpec((1, 1, HEAD_DIM, N_CMP), lambda b, g, i: (b, g, 0, 0)),
            pl.BlockSpec((1, TQ, QL), lambda b, g, i: (g, 0, 0)),
            pl.BlockSpec((1, TQ, QL), lambda b, g, i: (g, 0, 0)),
            pl.BlockSpec((1, 1, N_CMP, QL), lambda b, g, i: (g, i, 0, 0)),
            pl.BlockSpec((N_SBLK, N_CMP), lambda b, g, i: (0, 0)),
        ],
        out_specs=pl.BlockSpec((TQ, qw), lambda b, g, i: (b * NQ + i, g)),
        out_shape=jax.ShapeDtypeStruct((TOK, NSA_W), BF16),
        scratch_shapes=[
            pltpu.VMEM((NK, TQ, HEAD_DIM), BF16),
            pltpu.VMEM((NK, V_ROWS, TQ), BF16),
            pltpu.VMEM((NK, TQ, HEAD_DIM), BF16),
            pltpu.VMEM((NK, V_ROWS, TQ), BF16),
            pltpu.VMEM((N_SBLK, TQ), F32),
            pltpu.VMEM((NK, TQ, QL), F32),
            pltpu.VMEM((LANE, TQ), F32),
        ],
        compiler_params=_cparams(("parallel", "parallel", "arbitrary")),
        name="nsa_attention",
    )(z, z, z, z, kc, vcT, t0, t1, tc, ov)


def _sgu_body(u_ref, v_ref, g_ref, b_ref, w_ref, bs_ref, o_ref):
    r = lax.broadcasted_iota(jnp.int32, (GM_CHUNK, GM_CHUNK), 0)
    c = lax.broadcasted_iota(jnp.int32, (GM_CHUNK, GM_CHUNK), 1)
    tril = r >= c
    ws = [jnp.where(tril, w_ref[gi], 0.0).astype(BF16) for gi in range(GM_GROUPS)]
    for ch in range(SGU_TM // GM_CHUNK):
        rows = slice(ch * GM_CHUNK, (ch + 1) * GM_CHUNK)
        v = v_ref[rows, :]
        mu = jnp.mean(v, axis=-1, keepdims=True)
        vc = v - mu
        var = jnp.mean(vc * vc, axis=-1, keepdims=True)
        vn = (vc * lax.rsqrt(var + EPS) * g_ref[...] + b_ref[...]).astype(BF16)
        for gi in range(GM_GROUPS):
            cols = slice(gi * GM_GW, (gi + 1) * GM_GW)
            mixed = jnp.dot(ws[gi], vn[:, cols], preferred_element_type=F32) + bs_ref[:, cols]
            o_ref[rows, cols] = (u_ref[rows, cols] * mixed).astype(o_ref.dtype)


def _sgu(z, ln_g, ln_b, w_s, bs_exp):
    return pl.pallas_call(
        _sgu_body,
        grid=(TOK // SGU_TM,),
        in_specs=[
            pl.BlockSpec((SGU_TM, GM_W), lambda i: (i, Z_U // GM_W)),
            pl.BlockSpec((SGU_TM, GM_W), lambda i: (i, Z_V // GM_W)),
            pl.BlockSpec((1, GM_W), lambda i: (0, 0)),
            pl.BlockSpec((1, GM_W), lambda i: (0, 0)),
            pl.BlockSpec((GM_GROUPS, GM_CHUNK, GM_CHUNK), lambda i: (0, 0, 0)),
            pl.BlockSpec((GM_CHUNK, GM_W), lambda i: (0, 0)),
        ],
        out_specs=pl.BlockSpec((SGU_TM, GM_W), lambda i: (i, 0)),
        out_shape=jax.ShapeDtypeStruct((TOK, GM_W), BF16),
        compiler_params=_cparams(("parallel",)),
        name="sgu",
    )(z, z, ln_g, ln_b, w_s, bs_exp)


def _out_body(a_ref, sg_ref, m1_ref, m2_ref, x_ref, wn_ref, ws_ref, wo_ref, o_ref):
    pa = jnp.dot(a_ref[...], wn_ref[...], preferred_element_type=F32)
    ps = jnp.dot(sg_ref[...], ws_ref[...], preferred_element_type=F32)
    merged = (m1_ref[...] * pa + m2_ref[...] * ps).astype(BF16)
    o_ref[...] = x_ref[...] + jnp.dot(merged, wo_ref[...], preferred_element_type=F32)


def _merge_out(a, sg, z, x, wn, ws, wo, l):
    const = lambda i: (l, 0, 0)
    return pl.pallas_call(
        _out_body,
        grid=(TOK // OUT_TM,),
        in_specs=[
            pl.BlockSpec((OUT_TM, NSA_W), lambda i: (i, 0)),
            pl.BlockSpec((OUT_TM, GM_W), lambda i: (i, 0)),
            pl.BlockSpec((OUT_TM, D_MODEL), lambda i: (i, 0)),
            pl.BlockSpec((OUT_TM, D_MODEL), lambda i: (i, 1)),
            pl.BlockSpec((OUT_TM, D_MODEL), lambda i: (i, 0)),
            pl.BlockSpec((None, NSA_W, D_MODEL), const),
            pl.BlockSpec((None, GM_W, D_MODEL), const),
            pl.BlockSpec((None, D_MODEL, D_MODEL), const),
        ],
        out_specs=pl.BlockSpec((OUT_TM, D_MODEL), lambda i: (i, 0)),
        out_shape=jax.ShapeDtypeStruct((TOK, D_MODEL), F32),
        compiler_params=_cparams(("parallel",)),
        name="merge_out",
    )(a, sg, z, z, x, wn, ws, wo)


def _overlap_T():
    ci = np.arange(N_CMP)[None, :] * CMP_STRIDE
    sj = np.arange(N_SBLK)[:, None] * SEL_BLOCK
    ov = np.clip(np.minimum(ci + CMP_BLOCK, sj + SEL_BLOCK) - np.maximum(ci, sj), 0, None)
    ov = ov.astype(np.float32) / CMP_BLOCK
    ov[:, N_CMP - 1] = 0.0
    return ov


def _block_diag_ones():
    r = np.arange(LANE)
    return (r[:, None] // HEAD_DIM == r[None, :] // HEAD_DIM).astype(np.float32)


def _prep_w_in(w):
    L = w.shape[0]
    kv = w[:, :, OFF_KV:OFF_NG].reshape(L, D_MODEL, 3, 2, N_KV, HEAD_DIM)
    kv = kv.transpose(0, 1, 2, 4, 3, 5).reshape(L, D_MODEL, 6 * KV_W)
    ng = w[:, :, OFF_NG:OFF_UV].reshape(L, D_MODEL, N_KV, HPG, 3).transpose(0, 1, 2, 4, 3)
    ng = ng.reshape(L, D_MODEL, N_KV, 3 * HPG)
    ng = jnp.pad(ng, ((0, 0), (0, 0), (0, 0), (0, GATE_GW - 3 * HPG))).reshape(L, D_MODEL, N_KV * GATE_GW)
    ng = jnp.pad(ng, ((0, 0), (0, 0), (0, PIN_TN - N_KV * GATE_GW)))
    return jnp.concatenate(
        [w[:, :, OFF_MG:], w[:, :, OFF_UV:OFF_MG], w[:, :, :OFF_KV], kv, ng], axis=2).astype(BF16)


def _mixer(x, z, l, p, tables):
    t0, t1, tc, ov = tables
    pos = jnp.stack([p["cmp_pos_k"][l], p["cmp_pos_v"][l]]).reshape(2, 1, CMP_BLOCK * HEAD_DIM)
    w1 = jnp.stack([p["cmp_k_w1"][l], p["cmp_v_w1"][l]]).astype(BF16)
    kc, vcT = _compress(z, pos, w1, p["cmp_k_w2"][l].astype(BF16), p["cmp_v_w2"][l].T.astype(BF16),
                        p["k_norm"][l, 0].reshape(1, HEAD_DIM))
    a = _attention(z, kc, vcT, t0, t1, tc, ov)
    bs_exp = jnp.repeat(p["sgu_b"][l].T, GM_GW, axis=1)
    sg = _sgu(z, p["sgu_norm_g"][l].reshape(1, GM_W), p["sgu_norm_b"][l].reshape(1, GM_W),
              p["sgu_w"][l], bs_exp)
    return _merge_out(a, sg, z, x, p["wn"], p["ws"], p["wo"], l)


def kernel(x, rel_bias, ffn1_norm, ffn1_w_gate, ffn1_w_up, ffn1_w_down, mix_norm, w_in, q_norm, k_norm, cmp_pos_k, cmp_pos_v, cmp_k_w1, cmp_k_w2, cmp_v_w1, cmp_v_w2, sgu_norm_g, sgu_norm_b, sgu_w, sgu_b, w_proj_nsa, w_proj_sgu, w_out, ffn2_norm, ffn2_w_gate, ffn2_w_up, ffn2_w_down):
    p = dict(cmp_pos_k=cmp_pos_k, cmp_pos_v=cmp_pos_v, cmp_k_w1=cmp_k_w1, cmp_k_w2=cmp_k_w2,
             cmp_v_w1=cmp_v_w1, cmp_v_w2=cmp_v_w2, k_norm=k_norm, sgu_norm_g=sgu_norm_g,
             sgu_norm_b=sgu_norm_b, sgu_w=sgu_w, sgu_b=sgu_b, wn=w_proj_nsa.astype(BF16),
             ws=w_proj_sgu.astype(BF16), wo=w_out.astype(BF16))
    near, cmpb = _bucket_tables()
    t0, t1, tc = _bias_tables(rel_bias, jnp.asarray(near), jnp.asarray(cmpb))
    tables = (t0, t1, tc, jnp.asarray(_overlap_T()))
    bd = jnp.asarray(_block_diag_ones()).astype(BF16)
    w_in_r = _prep_w_in(w_in)
    f1 = (ffn1_w_gate, ffn1_w_up, ffn1_w_down)
    f2 = (ffn2_w_gate, ffn2_w_up, ffn2_w_down)

    h = x.reshape(TOK, D_MODEL)
    for l in range(DEPTH):
        h = _ffn(h, ffn1_norm[l].reshape(1, D_MODEL), *f1, l)
        z = _proj_in(h, mix_norm[l].reshape(1, D_MODEL), w_in_r, l, bd,
                     jnp.tile(q_norm[l], PIN_TN // HEAD_DIM).reshape(1, PIN_TN),
                     jnp.tile(k_norm[l, 1], PIN_TN // HEAD_DIM).reshape(1, PIN_TN),
                     jnp.tile(k_norm[l, 2], PIN_TN // HEAD_DIM).reshape(1, PIN_TN))
        h = _mixer(h, z, l, p, tables)
        h = _ffn(h, ffn2_norm[l].reshape(1, D_MODEL), *f2, l)
    return h.reshape(BATCH, SEQ, D_MODEL)
```

```python
import math

import numpy as np
import jax
import jax.numpy as jnp
from jax import lax
from jax.experimental import pallas as pl
from jax.experimental.pallas import tpu as pltpu

F32 = jnp.float32
BF16 = jnp.bfloat16

D_MODEL = 2048
BATCH = 4
SEQ = 2048
DEPTH = 2
TOK = BATCH * SEQ
HEAD_DIM = 64
N_HEADS = 16
N_KV = 4
HPG = 4
NSA_W = 1024
KV_W = 256
CMP_BLOCK = 32
CMP_STRIDE = 16
CMP_HIDDEN = 256
SEL_BLOCK = 64
N_SEL = 16
WINDOW = 512
GM_W = 1024
GM_CHUNK = 128
GM_GW = 128
GM_GROUPS = 8
N_BUCKETS = 32
MAX_DISTANCE = 128
D_FF = 5504
EPS = 1e-6
NEG = -1e30
FORCE = 1e4
OFF_KV = NSA_W
OFF_NG = OFF_KV + 6 * KV_W
OFF_UV = OFF_NG + 3 * N_HEADS
OFF_MG = OFF_UV + 2 * GM_W
LOG2E = 1.4426950408889634

LANE = 128
VMEM_LIMIT = 56 * 1024 * 1024

FFN_TF = 512
FFN_UP_TM = 1024
FFN_DN_TM = 512
FFN_DN_TN = 512
PIN_TM = 1024
PIN_TN = 512
PIN_RC = 256
Z_MG = 0
Z_U = Z_MG + 2 * D_MODEL
Z_V = Z_U + GM_W
Z_Q = Z_V + GM_W
Z_KV = Z_Q + NSA_W
Z_NG = Z_KV + 6 * KV_W
Z_W = Z_NG + PIN_TN
KV_GW = 2 * HEAD_DIM
GATE_GW = 16
TQ = 128
NQ = SEQ // TQ
NK = SEQ // TQ
QL = HPG * TQ
N_CMP = 128
N_SBLK = SEQ // SEL_BLOCK
V_ROWS = HEAD_DIM + 16
FAR_GROUP = 4
Q_PAIR = 2
SGU_TM = 512
OUT_TM = 256


def _cparams(sem):
    return pltpu.CompilerParams(dimension_semantics=sem, vmem_limit_bytes=VMEM_LIMIT)


def _cast_rows(src_ref, dst_ref, chunk):
    def body(r, carry):
        rows = pl.ds(pl.multiple_of(r * chunk, chunk), chunk)
        dst_ref[rows, :] = src_ref[rows, :].astype(BF16)
        return carry

    lax.fori_loop(0, src_ref.shape[0] // chunk, body, 0)


def _ffn_up_body(x_ref, g_ref, wg_ref, wu_ref, a_ref, wgb_ref, wub_ref):
    @pl.when(pl.program_id(1) == 0)
    def _():
        _cast_rows(wg_ref, wgb_ref, LANE)
        _cast_rows(wu_ref, wub_ref, LANE)

    x = x_ref[...]
    ms = jnp.mean(x * x, axis=-1, keepdims=True)
    h = (x * lax.rsqrt(ms + EPS) * g_ref[...]).astype(BF16)
    g = jnp.dot(h, wgb_ref[...], preferred_element_type=F32)
    u = jnp.dot(h, wub_ref[...], preferred_element_type=F32)
    a_ref[...] = (g * jax.nn.sigmoid(g) * u).astype(BF16)


def _ffn_down_body(a_ref, wd_ref, x_ref, o_ref, wdb_ref):
    @pl.when(pl.program_id(1) == 0)
    def _():
        _cast_rows(wd_ref, wdb_ref, LANE)

    o_ref[...] = x_ref[...] + 0.5 * jnp.dot(a_ref[...], wdb_ref[...], preferred_element_type=F32)


def _ffn(x, gain, wg, wu, wd, l):
    nf = pl.cdiv(D_FF, FFN_TF)
    a = pl.pallas_call(
        _ffn_up_body,
        grid=(nf, TOK // FFN_UP_TM),
        in_specs=[
            pl.BlockSpec((FFN_UP_TM, D_MODEL), lambda f, m: (m, 0)),
            pl.BlockSpec((1, D_MODEL), lambda f, m: (0, 0)),
            pl.BlockSpec((None, D_MODEL, FFN_TF), lambda f, m: (l, 0, f)),
            pl.BlockSpec((None, D_MODEL, FFN_TF), lambda f, m: (l, 0, f)),
        ],
        out_specs=pl.BlockSpec((FFN_UP_TM, FFN_TF), lambda f, m: (m, f)),
        out_shape=jax.ShapeDtypeStruct((TOK, D_FF), BF16),
        scratch_shapes=[pltpu.VMEM((D_MODEL, FFN_TF), BF16), pltpu.VMEM((D_MODEL, FFN_TF), BF16)],
        compiler_params=_cparams(("arbitrary", "arbitrary")),
        name="ffn_up",
    )(x, gain, wg, wu)
    return pl.pallas_call(
        _ffn_down_body,
        grid=(D_MODEL // FFN_DN_TN, TOK // FFN_DN_TM),
        in_specs=[
            pl.BlockSpec((FFN_DN_TM, D_FF), lambda n, m: (m, 0)),
            pl.BlockSpec((None, D_FF, FFN_DN_TN), lambda n, m: (l, 0, n)),
            pl.BlockSpec((FFN_DN_TM, FFN_DN_TN), lambda n, m: (m, n)),
        ],
        out_specs=pl.BlockSpec((FFN_DN_TM, FFN_DN_TN), lambda n, m: (m, n)),
        out_shape=jax.ShapeDtypeStruct((TOK, D_MODEL), F32),
        scratch_shapes=[pltpu.VMEM((D_FF, FFN_DN_TN), BF16)],
        compiler_params=_cparams(("arbitrary", "arbitrary")),
        name="ffn_down",
    )(a, wd, x)


def _head_rms(y, bd):
    y2 = (y * y).astype(BF16)
    ss = jnp.concatenate(
        [jnp.dot(y2[:, k * LANE:(k + 1) * LANE], bd, preferred_element_type=F32)
         for k in range(y.shape[1] // LANE)], axis=1)
    return y * lax.rsqrt(ss * (1.0 / HEAD_DIM) + EPS)


def _pin_body(x_ref, g_ref, w_ref, bd_ref, qg_ref, k1g_ref, k2g_ref, o_ref, h_ref):
    j = pl.program_id(1)

    @pl.when(j == 0)
    def _():
        x = x_ref[...]
        ms = jnp.mean(x * x, axis=-1, keepdims=True)
        h_ref[...] = (x * lax.rsqrt(ms + EPS) * g_ref[...]).astype(BF16)

    t_u = Z_U // PIN_TN
    t_q = Z_Q // PIN_TN
    t_kv = Z_KV // PIN_TN
    t_ng = Z_NG // PIN_TN

    def emit(act):
        for r in range(PIN_TM // PIN_RC):
            rows = slice(r * PIN_RC, (r + 1) * PIN_RC)
            o_ref[rows, :] = act(jnp.dot(h_ref[rows, :], w_ref[...], preferred_element_type=F32))

    @pl.when((j < t_u) | (j == t_ng))
    def _():
        emit(jax.nn.sigmoid)

    @pl.when((j >= t_u) & (j < t_q))
    def _():
        emit(jax.nn.gelu)

    @pl.when((j >= t_q) & (j < t_kv))
    def _():
        emit(lambda y: _head_rms(y, bd_ref[...]) * (qg_ref[...] * (HEAD_DIM ** -0.5 * LOG2E)))

    @pl.when(j == t_kv)
    def _():
        emit(lambda y: y)

    def kv_act(kg_ref):
        is_k = (lax.broadcasted_iota(jnp.int32, (1, PIN_TN), 1) & (KV_GW - 1)) < HEAD_DIM
        return lambda y: jnp.where(is_k, _head_rms(y, bd_ref[...]) * kg_ref[...], y)

    @pl.when(j == t_kv + 1)
    def _():
        emit(kv_act(k1g_ref))

    @pl.when(j == t_kv + 2)
    def _():
        emit(kv_act(k2g_ref))


def _proj_in(x, gain, w, l, bd, qg, k1g, k2g):
    return pl.pallas_call(
        _pin_body,
        grid=(TOK // PIN_TM, Z_W // PIN_TN),
        in_specs=[
            pl.BlockSpec((PIN_TM, D_MODEL), lambda i, j: (i, 0)),
            pl.BlockSpec((1, D_MODEL), lambda i, j: (0, 0)),
            pl.BlockSpec((None, D_MODEL, PIN_TN), lambda i, j: (l, 0, j)),
            pl.BlockSpec((LANE, LANE), lambda i, j: (0, 0)),
            pl.BlockSpec((1, PIN_TN), lambda i, j: (0, 0)),
            pl.BlockSpec((1, PIN_TN), lambda i, j: (0, 0)),
            pl.BlockSpec((1, PIN_TN), lambda i, j: (0, 0)),
        ],
        out_specs=pl.BlockSpec((PIN_TM, PIN_TN), lambda i, j: (i, j)),
        out_shape=jax.ShapeDtypeStruct((TOK, Z_W), F32),
        scratch_shapes=[pltpu.VMEM((PIN_TM, D_MODEL), BF16)],
        compiler_params=_cparams(("parallel", "arbitrary")),
        name="proj_in",
    )(x, gain, w, bd, qg, k1g, k2g)


def _cmp_body(z_ref, pos_ref, w1_ref, w2k_ref, w2vT_ref, kg_ref, kc_ref, vcT_ref):
    half = CMP_STRIDE * HEAD_DIM
    lane = lax.broadcasted_iota(jnp.int32, (N_CMP, LANE), 1)
    first = lane < HEAD_DIM
    xk, xv = [], []
    for j in range(CMP_STRIDE // 2):
        ev = z_ref[pl.ds(2 * j, N_CMP, stride=CMP_STRIDE), :]
        od = z_ref[pl.ds(2 * j + 1, N_CMP, stride=CMP_STRIDE), :]
        xk.append(jnp.where(first, ev, pltpu.roll(od, HEAD_DIM, 1)))
        xv.append(jnp.where(first, pltpu.roll(ev, HEAD_DIM, 1), od))

    def hidden(x, kind):
        xa = (x + pos_ref[kind, :, :half]).astype(BF16)
        xb = (x + pos_ref[kind, :, half:]).astype(BF16)
        a = jnp.dot(xa, w1_ref[kind, :half, :], preferred_element_type=F32)
        b = jnp.dot(xb, w1_ref[kind, half:, :], preferred_element_type=F32)
        hid = a + pltpu.roll(b, N_CMP - 1, 0)
        return (hid * jax.nn.sigmoid(hid)).astype(BF16)

    yk = jnp.dot(hidden(jnp.concatenate(xk, axis=1), 0), w2k_ref[...], preferred_element_type=F32)
    ms = jnp.mean(yk * yk, axis=-1, keepdims=True)
    kc_ref[0, 0] = (yk * lax.rsqrt(ms + EPS) * kg_ref[...]).astype(BF16)
    hv = hidden(jnp.concatenate(xv, axis=1), 1)
    vcT_ref[0, 0] = lax.dot_general(w2vT_ref[...], hv, (((1,), (1,)), ((), ())),
                                    preferred_element_type=F32).astype(BF16)


def _compress(z, pos, w1, w2k, w2vT, kg):
    return pl.pallas_call(
        _cmp_body,
        grid=(BATCH, N_KV),
        in_specs=[
            pl.BlockSpec((SEQ, KV_GW), lambda b, g: (b, Z_KV // KV_GW + g)),
            pl.BlockSpec((2, 1, CMP_BLOCK * HEAD_DIM), lambda b, g: (0, 0, 0)),
            pl.BlockSpec((2, CMP_BLOCK * HEAD_DIM, CMP_HIDDEN), lambda b, g: (0, 0, 0)),
            pl.BlockSpec((CMP_HIDDEN, HEAD_DIM), lambda b, g: (0, 0)),
            pl.BlockSpec((HEAD_DIM, CMP_HIDDEN), lambda b, g: (0, 0)),
            pl.BlockSpec((1, HEAD_DIM), lambda b, g: (0, 0)),
        ],
        out_specs=[
            pl.BlockSpec((1, 1, N_CMP, HEAD_DIM), lambda b, g: (b, g, 0, 0)),
            pl.BlockSpec((1, 1, HEAD_DIM, N_CMP), lambda b, g: (b, g, 0, 0)),
        ],
        out_shape=[
            jax.ShapeDtypeStruct((BATCH, N_KV, N_CMP, HEAD_DIM), BF16),
            jax.ShapeDtypeStruct((BATCH, N_KV, HEAD_DIM, N_CMP), BF16),
        ],
        compiler_params=_cparams(("parallel", "parallel")),
        name="compress",
    )(z, pos, w1, w2k, w2vT, kg)


def _np_bucket(dist):
    n = np.maximum(dist, 0)
    max_exact = N_BUCKETS // 2
    nf = np.maximum(n, 1).astype(np.float32)
    large = max_exact + (np.log(nf / np.float32(max_exact)) / np.float32(math.log(MAX_DISTANCE / max_exact))
                         * np.float32(N_BUCKETS - max_exact)).astype(np.int32)
    large = np.minimum(large, N_BUCKETS - 1)
    return np.where(n < max_exact, n, large).astype(np.int32)


def _bucket_tables():
    kj = np.arange(TQ)[:, None]
    qi = np.arange(TQ)[None, :]
    near = np.stack([_np_bucket(qi - kj), _np_bucket(TQ + qi - kj)])
    c = np.arange(N_CMP)[:, None]
    t = (np.arange(NQ)[:, None, None] * TQ + qi[None])
    cmp_b = _np_bucket(t - (c[None] * CMP_STRIDE + CMP_BLOCK - 1))
    return near, cmp_b.astype(np.int32)


def _bias_body(rb_ref, near_ref, cmpb_ref, t0_ref, t1_ref, tc_ref):
    g = pl.program_id(0)

    def lut(bk, h):
        val = jnp.zeros(bk.shape, F32)
        for b in range(N_BUCKETS):
            val = jnp.where(bk == b, rb_ref[b, h], val)
        return val

    for hh in range(HPG):
        h = g * HPG + hh
        sl = slice(hh * TQ, (hh + 1) * TQ)
        far = rb_ref[N_BUCKETS - 1, h]
        t0_ref[0, :, sl] = (lut(near_ref[0], h) - far) * LOG2E
        t1_ref[0, :, sl] = (lut(near_ref[1], h) - far) * LOG2E

        def body(i, carry):
            tc_ref[0, i, :, sl] = lut(cmpb_ref[i], h) * LOG2E
            return carry

        lax.fori_loop(0, NQ, body, 0)


def _bias_tables(rel_bias, near, cmpb):
    return pl.pallas_call(
        _bias_body,
        grid=(N_KV,),
        in_specs=[
            pl.BlockSpec(memory_space=pltpu.SMEM),
            pl.BlockSpec((2, TQ, TQ), lambda g: (0, 0, 0)),
            pl.BlockSpec((NQ, N_CMP, TQ), lambda g: (0, 0, 0)),
        ],
        out_specs=[
            pl.BlockSpec((1, TQ, QL), lambda g: (g, 0, 0)),
            pl.BlockSpec((1, TQ, QL), lambda g: (g, 0, 0)),
            pl.BlockSpec((1, NQ, N_CMP, QL), lambda g: (g, 0, 0, 0)),
        ],
        out_shape=[
            jax.ShapeDtypeStruct((N_KV, TQ, QL), F32),
            jax.ShapeDtypeStruct((N_KV, TQ, QL), F32),
            jax.ShapeDtypeStruct((N_KV, NQ, N_CMP, QL), F32),
        ],
        compiler_params=_cparams(("arbitrary",)),
        name="bias_tables",
    )(rel_bias, near, cmpb)


def _attn_body(zq_ref, zs_ref, zw_ref, zg_ref, kc_ref, vcT_ref, t0_ref, t1_ref, tc_ref, ov_ref, o_ref,
               ks_ref, vsT_ref, kw_ref, vwT_ref, sel_ref, sfar_ref, gsc_ref):
    g = pl.program_id(1)
    ip = pl.program_id(2)

    @pl.when(ip == 0)
    def _():
        ones = jnp.ones((V_ROWS - HEAD_DIM, TQ), BF16)

        def fill(n, carry):
            for z_ref, k_ref, vT_ref in ((zs_ref, ks_ref, vsT_ref), (zw_ref, kw_ref, vwT_ref)):
                blk = z_ref[pl.ds(pl.multiple_of(n * TQ, TQ), TQ), :]
                k_ref[n] = blk[:, :HEAD_DIM].astype(BF16)
                vT_ref[n, :HEAD_DIM, :] = blk.T[HEAD_DIM:, :].astype(BF16)
                vT_ref[n, HEAD_DIM:, :] = ones
            return carry

        lax.fori_loop(0, NK, fill, 0)

    key = lax.broadcasted_iota(jnp.int32, (TQ, QL), 0)
    qi = lax.broadcasted_iota(jnp.int32, (TQ, QL), 1) & (TQ - 1)
    blk = lax.broadcasted_iota(jnp.int32, (N_SBLK, TQ), 0)
    ql = lax.broadcasted_iota(jnp.int32, (N_SBLK, TQ), 1)
    key1 = lax.broadcasted_iota(jnp.int32, (TQ, TQ), 0)
    q1 = lax.broadcasted_iota(jnp.int32, (TQ, TQ), 1)
    causal = key1 <= q1
    nlast = WINDOW // TQ

    def sel_mask(t, n):
        r0 = sel_ref[t, pl.ds(2 * n, 1), :]
        r1 = sel_ref[t, pl.ds(2 * n + 1, 1), :]
        return jnp.where(key1 < SEL_BLOCK, r0, r1) > 0.5

    def scores(qT, k_ref, n, bias, mask):
        sc = jnp.dot(k_ref[n], qT, preferred_element_type=F32)
        if bias is not None:
            sc = sc + bias
        return jnp.concatenate(
            [jnp.where(mask, sc[:, h * TQ:(h + 1) * TQ], NEG) for h in range(HPG)], axis=1)

    def colmax8(sc):
        return jnp.max(sc.reshape(TQ // 8, 8, QL), axis=0)

    def probs(sc, mrow):
        return jnp.exp2(sc - mrow).astype(BF16)

    def pv(vT_ref, ns, ps):
        vv = vT_ref[ns[0]] if len(ns) == 1 else jnp.concatenate([vT_ref[n] for n in ns], axis=1)
        pp = ps[0] if len(ps) == 1 else jnp.concatenate(ps, axis=0)
        return jnp.dot(vv, pp, preferred_element_type=F32)

    def finish(acc):
        return acc[:HEAD_DIM] * (1.0 / acc[HEAD_DIM:HEAD_DIM + 1])

    def front(t):
        i = Q_PAIR * ip + t
        rows = slice(t * TQ, (t + 1) * TQ)
        qt = zq_ref[rows, :].T
        qT = jnp.concatenate([qt[h * HEAD_DIM:(h + 1) * HEAD_DIM] for h in range(HPG)], axis=1).astype(BF16)
        gsc_ref[t] = zg_ref[rows, :].T
        iv = jnp.full((TQ, QL), i, jnp.int32)
        i1 = jnp.full((TQ, TQ), i, jnp.int32)

        s = jnp.dot(kc_ref[0, 0], qT, preferred_element_type=F32) + tc_ref[0, t]
        mask_c = (iv * TQ + qi) >= (key * CMP_STRIDE + (CMP_BLOCK - 1))
        s = jnp.where(mask_c, s, NEG)
        m = jnp.max(s, axis=0, keepdims=True)
        e = jnp.where(mask_c, jnp.exp2(s - m), 0.0)
        l = jnp.sum(e, axis=0, keepdims=True)
        p = e * jnp.where(l > 0.0, 1.0 / l, 0.0)
        ocT = jnp.dot(vcT_ref[0, 0], p.astype(BF16), preferred_element_type=F32)

        psum = p[:, 0:TQ] + p[:, TQ:2 * TQ] + p[:, 2 * TQ:3 * TQ] + p[:, 3 * TQ:4 * TQ]
        imp = jnp.dot(ov_ref[...], psum, preferred_element_type=F32,
                      precision=lax.Precision.HIGHEST)
        cur = jnp.full((N_SBLK, TQ), i, jnp.int32) * (TQ // SEL_BLOCK) + ql // SEL_BLOCK
        forced = (blk == 0) | (blk == cur) | (blk == cur - 1)
        score = jnp.where(blk <= cur, jnp.where(forced, FORCE, imp), NEG)
        rank = jnp.zeros((N_SBLK, TQ), jnp.int32)
        for jj in range(N_SBLK):
            row = score[jj:jj + 1, :]
            beats = (row > score) | ((row == score) & (blk > jj))
            rank = rank + beats.astype(jnp.int32)
        sel_ref[t] = ((rank < N_SEL) & (blk <= cur)).astype(F32)

        n1 = jnp.maximum(i - 1, 0)
        n2, n3, n4 = jnp.maximum(i - 2, 0), jnp.maximum(i - 3, 0), jnp.maximum(i - nlast, 0)
        w_d = scores(qT, kw_ref, i, t0_ref[0], causal)
        w_1 = scores(qT, kw_ref, n1, t1_ref[0], i1 >= 1)
        w_2 = scores(qT, kw_ref, n2, None, i1 >= 2)
        w_3 = scores(qT, kw_ref, n3, None, i1 >= 3)
        w_4 = scores(qT, kw_ref, n4, None, (q1 < key1) & (i1 >= nlast))
        mxw = jnp.maximum(jnp.maximum(jnp.maximum(colmax8(w_d), colmax8(w_1)),
                                      jnp.maximum(colmax8(w_2), colmax8(w_3))), colmax8(w_4))
        m_w = jnp.max(mxw, axis=0, keepdims=True)
        owT = finish(pv(vwT_ref, [i], [probs(w_d, m_w)])
                     + pv(vwT_ref, [n1, n2], [probs(w_1, m_w), probs(w_2, m_w)])
                     + pv(vwT_ref, [n3, n4], [probs(w_3, m_w), probs(w_4, m_w)]))

        s_d = scores(qT, ks_ref, i, t0_ref[0], causal & sel_mask(t, i))
        s_1 = scores(qT, ks_ref, n1, t1_ref[0], sel_mask(t, n1) & (i1 >= 1))
        return dict(i=i, n1=n1, qT=qT, ocT=ocT, owT=owT, s_d=s_d, s_1=s_1,
                    mx=jnp.maximum(colmax8(s_d), colmax8(s_1)),
                    nfar=jnp.maximum(i - 1, 0))

    tiles = [front(t) for t in range(Q_PAIR)]
    ntrip = lax.shift_right_logical(tiles[-1]["nfar"] + (FAR_GROUP - 1), FAR_GROUP.bit_length() - 1)

    def far_scores(tr, mxs):
        out = []
        for t, tl in enumerate(tiles):
            mx = mxs[t]
            for c in range(FAR_GROUP):
                n = FAR_GROUP * tr + c
                sc = scores(tl["qT"], ks_ref, n, None,
                            sel_mask(t, n) & (jnp.full((TQ, TQ), n, jnp.int32) < tl["nfar"]))
                sfar_ref[t, n] = sc
                mx = jnp.maximum(mx, colmax8(sc))
            out.append(mx)
        return tuple(out)

    mxs = lax.fori_loop(0, ntrip, far_scores, tuple(tl["mx"] for tl in tiles))
    m_s = [jnp.max(mx, axis=0, keepdims=True) for mx in mxs]
    accs = tuple(pv(vsT_ref, [tl["i"], tl["n1"]], [probs(tl["s_d"], m_s[t]), probs(tl["s_1"], m_s[t])])
                 for t, tl in enumerate(tiles))

    def far_pv(tr, accs_):
        ns = [FAR_GROUP * tr + c for c in range(FAR_GROUP)]
        return tuple(accs_[t] + pv(vsT_ref, ns, [probs(sfar_ref[t, n], m_s[t]) for n in ns])
                     for t in range(Q_PAIR))

    accs = lax.fori_loop(0, ntrip, far_pv, accs)

    for t, tl in enumerate(tiles):
        base = g * GATE_GW
        gate = [jnp.concatenate([gsc_ref[t, pl.ds(base + br * HPG + h, 1), :] for h in range(HPG)], axis=1)
                for br in range(3)]
        aT = gate[0] * tl["ocT"] + gate[1] * finish(accs[t]) + gate[2] * tl["owT"]
        a4 = jnp.concatenate([aT[:, h * TQ:(h + 1) * TQ] for h in range(HPG)], axis=0)
        o_ref[t * TQ:(t + 1) * TQ, :] = a4.T.astype(o_ref.dtype)


def _attention(z, kc, vcT, t0, t1, tc, ov):
    qw = HPG * HEAD_DIM
    rows = Q_PAIR * TQ
    npair = NQ // Q_PAIR
    return pl.pallas_call(
        _attn_body,
        grid=(BATCH, N_KV, npair),
        in_specs=[
            pl.BlockSpec((rows, qw), lambda b, g, i: (b * npair + i, Z_Q // qw + g)),
            pl.BlockSpec((SEQ, KV_GW), lambda b, g, i: (b, (Z_KV + 2 * KV_W) // KV_GW + g)),
            pl.BlockSpec((SEQ, KV_GW), lambda b, g, i: (b, (Z_KV + 4 * KV_W) // KV_GW + g)),
            pl.BlockSpec((rows, LANE), lambda b, g, i: (b * npair + i, Z_NG // LANE)),
            pl.BlockSpec((1, 1, N_CMP, HEAD_DIM), lambda b, g, i: (b, g, 0, 0)),
            pl.BlockSpec((1, 1, HEAD_DIM, N_CMP), lambda b, g, i: (b, g, 0, 0)),
            pl.BlockSpec((1, TQ, QL), lambda b, g, i: (g, 0, 0)),
            pl.BlockSpec((1, TQ, QL), lambda b, g, i: (g, 0, 0)),
            pl.BlockSpec((1, Q_PAIR, N_CMP, QL), lambda b, g, i: (g, i, 0, 0)),
            pl.BlockSpec((N_SBLK, N_CMP), lambda b, g, i: (0, 0)),
        ],
        out_specs=pl.BlockSpec((rows, qw), lambda b, g, i: (b * npair + i, g)),
        out_shape=jax.ShapeDtypeStruct((TOK, NSA_W), BF16),
        scratch_shapes=[
            pltpu.VMEM((NK, TQ, HEAD_DIM), BF16),
            pltpu.VMEM((NK, V_ROWS, TQ), BF16),
            pltpu.VMEM((NK, TQ, HEAD_DIM), BF16),
            pltpu.VMEM((NK, V_ROWS, TQ), BF16),
            pltpu.VMEM((Q_PAIR, N_SBLK, TQ), F32),
            pltpu.VMEM((Q_PAIR, NK, TQ, QL), F32),
            pltpu.VMEM((Q_PAIR, LANE, TQ), F32),
        ],
        compiler_params=_cparams(("parallel", "parallel", "arbitrary")),
        name="nsa_attention",
    )(z, z, z, z, kc, vcT, t0, t1, tc, ov)


def _sgu_body(u_ref, v_ref, g_ref, b_ref, w_ref, bs_ref, o_ref):
    r = lax.broadcasted_iota(jnp.int32, (GM_CHUNK, GM_CHUNK), 0)
    c = lax.broadcasted_iota(jnp.int32, (GM_CHUNK, GM_CHUNK), 1)
    tril = r >= c
    ws = [jnp.where(tril, w_ref[gi], 0.0).astype(BF16) for gi in range(GM_GROUPS)]
    for ch in range(SGU_TM // GM_CHUNK):
        rows = slice(ch * GM_CHUNK, (ch + 1) * GM_CHUNK)
        v = v_ref[rows, :]
        mu = jnp.mean(v, axis=-1, keepdims=True)
        vc = v - mu
        var = jnp.mean(vc * vc, axis=-1, keepdims=True)
        vn = (vc * lax.rsqrt(var + EPS) * g_ref[...] + b_ref[...]).astype(BF16)
        for gi in range(GM_GROUPS):
            cols = slice(gi * GM_GW, (gi + 1) * GM_GW)
            mixed = jnp.dot(ws[gi], vn[:, cols], preferred_element_type=F32) + bs_ref[:, cols]
            o_ref[rows, cols] = (u_ref[rows, cols] * mixed).astype(o_ref.dtype)


def _sgu(z, ln_g, ln_b, w_s, bs_exp):
    return pl.pallas_call(
        _sgu_body,
        grid=(TOK // SGU_TM,),
        in_specs=[
            pl.BlockSpec((SGU_TM, GM_W), lambda i: (i, Z_U // GM_W)),
            pl.BlockSpec((SGU_TM, GM_W), lambda i: (i, Z_V // GM_W)),
            pl.BlockSpec((1, GM_W), lambda i: (0, 0)),
            pl.BlockSpec((1, GM_W), lambda i: (0, 0)),
            pl.BlockSpec((GM_GROUPS, GM_CHUNK, GM_CHUNK), lambda i: (0, 0, 0)),
            pl.BlockSpec((GM_CHUNK, GM_W), lambda i: (0, 0)),
        ],
        out_specs=pl.BlockSpec((SGU_TM, GM_W), lambda i: (i, 0)),
        out_shape=jax.ShapeDtypeStruct((TOK, GM_W), BF16),
        compiler_params=_cparams(("parallel",)),
        name="sgu",
    )(z, z, ln_g, ln_b, w_s, bs_exp)


def _out_body(a_ref, sg_ref, m1_ref, m2_ref, x_ref, wn_ref, ws_ref, wo_ref, o_ref):
    pa = jnp.dot(a_ref[...], wn_ref[...], preferred_element_type=F32)
    ps = jnp.dot(sg_ref[...], ws_ref[...], preferred_element_type=F32)
    merged = (m1_ref[...] * pa + m2_ref[...] * ps).astype(BF16)
    o_ref[...] = x_ref[...] + jnp.dot(merged, wo_ref[...], preferred_element_type=F32)


def _merge_out(a, sg, z, x, wn, ws, wo, l):
    const = lambda i: (l, 0, 0)
    return pl.pallas_call(
        _out_body,
        grid=(TOK // OUT_TM,),
        in_specs=[
            pl.BlockSpec((OUT_TM, NSA_W), lambda i: (i, 0)),
            pl.BlockSpec((OUT_TM, GM_W), lambda i: (i, 0)),
            pl.BlockSpec((OUT_TM, D_MODEL), lambda i: (i, 0)),
            pl.BlockSpec((OUT_TM, D_MODEL), lambda i: (i, 1)),
            pl.BlockSpec((OUT_TM, D_MODEL), lambda i: (i, 0)),
            pl.BlockSpec((None, NSA_W, D_MODEL), const),
            pl.BlockSpec((None, GM_W, D_MODEL), const),
            pl.BlockSpec((None, D_MODEL, D_MODEL), const),
        ],
        out_specs=pl.BlockSpec((OUT_TM, D_MODEL), lambda i: (i, 0)),
        out_shape=jax.ShapeDtypeStruct((TOK, D_MODEL), F32),
        compiler_params=_cparams(("parallel",)),
        name="merge_out",
    )(a, sg, z, z, x, wn, ws, wo)


def _overlap_T():
    ci = np.arange(N_CMP)[None, :] * CMP_STRIDE
    sj = np.arange(N_SBLK)[:, None] * SEL_BLOCK
    ov = np.clip(np.minimum(ci + CMP_BLOCK, sj + SEL_BLOCK) - np.maximum(ci, sj), 0, None)
    ov = ov.astype(np.float32) / CMP_BLOCK
    ov[:, N_CMP - 1] = 0.0
    return ov


def _block_diag_ones():
    r = np.arange(LANE)
    return (r[:, None] // HEAD_DIM == r[None, :] // HEAD_DIM).astype(np.float32)


def _prep_w_in(w):
    L = w.shape[0]
    kv = w[:, :, OFF_KV:OFF_NG].reshape(L, D_MODEL, 3, 2, N_KV, HEAD_DIM)
    kv = kv.transpose(0, 1, 2, 4, 3, 5).reshape(L, D_MODEL, 6 * KV_W)
    ng = w[:, :, OFF_NG:OFF_UV].reshape(L, D_MODEL, N_KV, HPG, 3).transpose(0, 1, 2, 4, 3)
    ng = ng.reshape(L, D_MODEL, N_KV, 3 * HPG)
    ng = jnp.pad(ng, ((0, 0), (0, 0), (0, 0), (0, GATE_GW - 3 * HPG))).reshape(L, D_MODEL, N_KV * GATE_GW)
    ng = jnp.pad(ng, ((0, 0), (0, 0), (0, PIN_TN - N_KV * GATE_GW)))
    return jnp.concatenate(
        [w[:, :, OFF_MG:], w[:, :, OFF_UV:OFF_MG], w[:, :, :OFF_KV], kv, ng], axis=2).astype(BF16)


def _mixer(x, z, l, p, tables):
    t0, t1, tc, ov = tables
    pos = jnp.stack([p["cmp_pos_k"][l], p["cmp_pos_v"][l]]).reshape(2, 1, CMP_BLOCK * HEAD_DIM)
    w1 = jnp.stack([p["cmp_k_w1"][l], p["cmp_v_w1"][l]]).astype(BF16)
    kc, vcT = _compress(z, pos, w1, p["cmp_k_w2"][l].astype(BF16), p["cmp_v_w2"][l].T.astype(BF16),
                        p["k_norm"][l, 0].reshape(1, HEAD_DIM))
    a = _attention(z, kc, vcT, t0, t1, tc, ov)
    bs_exp = jnp.repeat(p["sgu_b"][l].T, GM_GW, axis=1)
    sg = _sgu(z, p["sgu_norm_g"][l].reshape(1, GM_W), p["sgu_norm_b"][l].reshape(1, GM_W),
              p["sgu_w"][l], bs_exp)
    return _merge_out(a, sg, z, x, p["wn"], p["ws"], p["wo"], l)


def kernel(x, rel_bias, ffn1_norm, ffn1_w_gate, ffn1_w_up, ffn1_w_down, mix_norm, w_in, q_norm, k_norm, cmp_pos_k, cmp_pos_v, cmp_k_w1, cmp_k_w2, cmp_v_w1, cmp_v_w2, sgu_norm_g, sgu_norm_b, sgu_w, sgu_b, w_proj_nsa, w_proj_sgu, w_out, ffn2_norm, ffn2_w_gate, ffn2_w_up, ffn2_w_down):
    p = dict(cmp_pos_k=cmp_pos_k, cmp_pos_v=cmp_pos_v, cmp_k_w1=cmp_k_w1, cmp_k_w2=cmp_k_w2,
             cmp_v_w1=cmp_v_w1, cmp_v_w2=cmp_v_w2, k_norm=k_norm, sgu_norm_g=sgu_norm_g,
             sgu_norm_b=sgu_norm_b, sgu_w=sgu_w, sgu_b=sgu_b, wn=w_proj_nsa.astype(BF16),
             ws=w_proj_sgu.astype(BF16), wo=w_out.astype(BF16))
    near, cmpb = _bucket_tables()
    t0, t1, tc = _bias_tables(rel_bias, jnp.asarray(near), jnp.asarray(cmpb))
    tables = (t0, t1, tc, jnp.asarray(_overlap_T()))
    bd = jnp.asarray(_block_diag_ones()).astype(BF16)
    w_in_r = _prep_w_in(w_in)
    f1 = (ffn1_w_gate, ffn1_w_up, ffn1_w_down)
    f2 = (ffn2_w_gate, ffn2_w_up, ffn2_w_down)

    h = x.reshape(TOK, D_MODEL)
    for l in range(DEPTH):
        h = _ffn(h, ffn1_norm[l].reshape(1, D_MODEL), *f1, l)
        z = _proj_in(h, mix_norm[l].reshape(1, D_MODEL), w_in_r, l, bd,
                     jnp.tile(q_norm[l], PIN_TN // HEAD_DIM).reshape(1, PIN_TN),
                     jnp.tile(k_norm[l, 1], PIN_TN // HEAD_DIM).reshape(1, PIN_TN),
                     jnp.tile(k_norm[l, 2], PIN_TN // HEAD_DIM).reshape(1, PIN_TN))
        h = _mixer(h, z, l, p, tables)
        h = _ffn(h, ffn2_norm[l].reshape(1, D_MODEL), *f2, l)
    return h.reshape(BATCH, SEQ, D_MODEL)
```

```python
import math

import numpy as np
import jax
import jax.numpy as jnp
from jax import lax
from jax.experimental import pallas as pl
from jax.experimental.pallas import tpu as pltpu

F32 = jnp.float32
BF16 = jnp.bfloat16

D_MODEL = 2048
BATCH = 4
SEQ = 2048
DEPTH = 2
TOK = BATCH * SEQ
HEAD_DIM = 64
N_HEADS = 16
N_KV = 4
HPG = 4
NSA_W = 1024
KV_W = 256
CMP_BLOCK = 32
CMP_STRIDE = 16
CMP_HIDDEN = 256
SEL_BLOCK = 64
N_SEL = 16
WINDOW = 512
GM_W = 1024
GM_CHUNK = 128
GM_GW = 128
GM_GROUPS = 8
N_BUCKETS = 32
MAX_DISTANCE = 128
D_FF = 5504
EPS = 1e-6
NEG = -1e30
FORCE = 1e4
OFF_KV = NSA_W
OFF_NG = OFF_KV + 6 * KV_W
OFF_UV = OFF_NG + 3 * N_HEADS
OFF_MG = OFF_UV + 2 * GM_W
LOG2E = 1.4426950408889634

LANE = 128
VMEM_LIMIT = 56 * 1024 * 1024

FFN_TF = 512
FFN_UP_TM = 1024
NORM_TM = 512
FFN_DN_TM = 512
FFN_DN_TN = 512
PIN_TM = 1024
PIN_TN = 512
PIN_RC = 256
Z_MG = 0
Z_U = Z_MG + 2 * D_MODEL
Z_V = Z_U + GM_W
Z_Q = Z_V + GM_W
Z_KV = Z_Q + NSA_W
Z_NG = Z_KV + 6 * KV_W
Z_W = Z_NG + PIN_TN
KV_GW = 2 * HEAD_DIM
GATE_GW = 16
TQ = 128
NQ = SEQ // TQ
NK = SEQ // TQ
QL = HPG * TQ
N_CMP = 128
CMP_BAND = 16
N_SBLK = SEQ // SEL_BLOCK
V_ROWS = HEAD_DIM + 16
FAR_GROUP = 4
Q_PAIR = 2
SGU_TM = 512
OUT_TM = 256


def _cparams(sem):
    return pltpu.CompilerParams(dimension_semantics=sem, vmem_limit_bytes=VMEM_LIMIT)


def _cast_rows(src_ref, dst_ref, chunk):
    def body(r, carry):
        rows = pl.ds(pl.multiple_of(r * chunk, chunk), chunk)
        dst_ref[rows, :] = src_ref[rows, :].astype(BF16)
        return carry

    lax.fori_loop(0, src_ref.shape[0] // chunk, body, 0)


def _rms_bf16(x, gain):
    ms = jnp.mean(x * x, axis=-1, keepdims=True)
    return (x * lax.rsqrt(ms + EPS) * gain).astype(BF16)


def _norm_body(x_ref, g_ref, h_ref):
    h_ref[...] = _rms_bf16(x_ref[...], g_ref[...])


def _rmsnorm(x, gain):
    return pl.pallas_call(
        _norm_body,
        grid=(TOK // NORM_TM,),
        in_specs=[pl.BlockSpec((NORM_TM, D_MODEL), lambda i: (i, 0)),
                  pl.BlockSpec((1, D_MODEL), lambda i: (0, 0))],
        out_specs=pl.BlockSpec((NORM_TM, D_MODEL), lambda i: (i, 0)),
        out_shape=jax.ShapeDtypeStruct((TOK, D_MODEL), BF16),
        compiler_params=_cparams(("parallel",)),
        name="rmsnorm",
    )(x, gain)


def _ffn_up_body(h_ref, wg_ref, wu_ref, a_ref, wgb_ref, wub_ref):
    @pl.when(pl.program_id(1) == 0)
    def _():
        _cast_rows(wg_ref, wgb_ref, LANE)
        _cast_rows(wu_ref, wub_ref, LANE)

    h = h_ref[...]
    g = jnp.dot(h, wgb_ref[...], preferred_element_type=F32)
    u = jnp.dot(h, wub_ref[...], preferred_element_type=F32)
    a_ref[...] = (g * jax.nn.sigmoid(g) * u).astype(BF16)


def _ffn_down_body(a_ref, wd_ref, x_ref, o_ref, wdb_ref):
    @pl.when(pl.program_id(1) == 0)
    def _():
        _cast_rows(wd_ref, wdb_ref, LANE)

    o_ref[...] = x_ref[...] + 0.5 * jnp.dot(a_ref[...], wdb_ref[...], preferred_element_type=F32)


def _ffn(x, h, wg, wu, wd, l):
    nf = pl.cdiv(D_FF, FFN_TF)
    a = pl.pallas_call(
        _ffn_up_body,
        grid=(nf, TOK // FFN_UP_TM),
        in_specs=[
            pl.BlockSpec((FFN_UP_TM, D_MODEL), lambda f, m: (m, 0)),
            pl.BlockSpec((None, D_MODEL, FFN_TF), lambda f, m: (l, 0, f)),
            pl.BlockSpec((None, D_MODEL, FFN_TF), lambda f, m: (l, 0, f)),
        ],
        out_specs=pl.BlockSpec((FFN_UP_TM, FFN_TF), lambda f, m: (m, f)),
        out_shape=jax.ShapeDtypeStruct((TOK, D_FF), BF16),
        scratch_shapes=[pltpu.VMEM((D_MODEL, FFN_TF), BF16), pltpu.VMEM((D_MODEL, FFN_TF), BF16)],
        compiler_params=_cparams(("arbitrary", "arbitrary")),
        name="ffn_up",
    )(h, wg, wu)
    return pl.pallas_call(
        _ffn_down_body,
        grid=(D_MODEL // FFN_DN_TN, TOK // FFN_DN_TM),
        in_specs=[
            pl.BlockSpec((FFN_DN_TM, D_FF), lambda n, m: (m, 0)),
            pl.BlockSpec((None, D_FF, FFN_DN_TN), lambda n, m: (l, 0, n)),
            pl.BlockSpec((FFN_DN_TM, FFN_DN_TN), lambda n, m: (m, n)),
        ],
        out_specs=pl.BlockSpec((FFN_DN_TM, FFN_DN_TN), lambda n, m: (m, n)),
        out_shape=jax.ShapeDtypeStruct((TOK, D_MODEL), F32),
        scratch_shapes=[pltpu.VMEM((D_FF, FFN_DN_TN), BF16)],
        compiler_params=_cparams(("arbitrary", "arbitrary")),
        name="ffn_down",
    )(a, wd, x)


def _head_rms(y, bd):
    y2 = (y * y).astype(BF16)
    ss = jnp.concatenate(
        [jnp.dot(y2[:, k * LANE:(k + 1) * LANE], bd, preferred_element_type=F32)
         for k in range(y.shape[1] // LANE)], axis=1)
    return y * lax.rsqrt(ss * (1.0 / HEAD_DIM) + EPS)


def _pin_body(x_ref, g_ref, w_ref, bd_ref, qg_ref, k1g_ref, k2g_ref, o_ref, h_ref):
    j = pl.program_id(1)

    @pl.when(j == 0)
    def _():
        x = x_ref[...]
        ms = jnp.mean(x * x, axis=-1, keepdims=True)
        h_ref[...] = (x * lax.rsqrt(ms + EPS) * g_ref[...]).astype(BF16)

    t_u = Z_U // PIN_TN
    t_q = Z_Q // PIN_TN
    t_kv = Z_KV // PIN_TN
    t_ng = Z_NG // PIN_TN

    def emit(act):
        for r in range(PIN_TM // PIN_RC):
            rows = slice(r * PIN_RC, (r + 1) * PIN_RC)
            o_ref[rows, :] = act(jnp.dot(h_ref[rows, :], w_ref[...], preferred_element_type=F32))

    @pl.when((j < t_u) | (j == t_ng))
    def _():
        emit(jax.nn.sigmoid)

    @pl.when((j >= t_u) & (j < t_q))
    def _():
        emit(jax.nn.gelu)

    @pl.when((j >= t_q) & (j < t_kv))
    def _():
        emit(lambda y: _head_rms(y, bd_ref[...]) * (qg_ref[...] * (HEAD_DIM ** -0.5 * LOG2E)))

    @pl.when(j == t_kv)
    def _():
        emit(lambda y: y)

    def kv_act(kg_ref):
        is_k = (lax.broadcasted_iota(jnp.int32, (1, PIN_TN), 1) & (KV_GW - 1)) < HEAD_DIM
        return lambda y: jnp.where(is_k, _head_rms(y, bd_ref[...]) * kg_ref[...], y)

    @pl.when(j == t_kv + 1)
    def _():
        emit(kv_act(k1g_ref))

    @pl.when(j == t_kv + 2)
    def _():
        emit(kv_act(k2g_ref))


def _proj_in(x, gain, w, l, bd, qg, k1g, k2g):
    return pl.pallas_call(
        _pin_body,
        grid=(TOK // PIN_TM, Z_W // PIN_TN),
        in_specs=[
            pl.BlockSpec((PIN_TM, D_MODEL), lambda i, j: (i, 0)),
            pl.BlockSpec((1, D_MODEL), lambda i, j: (0, 0)),
            pl.BlockSpec((None, D_MODEL, PIN_TN), lambda i, j: (l, 0, j)),
            pl.BlockSpec((LANE, LANE), lambda i, j: (0, 0)),
            pl.BlockSpec((1, PIN_TN), lambda i, j: (0, 0)),
            pl.BlockSpec((1, PIN_TN), lambda i, j: (0, 0)),
            pl.BlockSpec((1, PIN_TN), lambda i, j: (0, 0)),
        ],
        out_specs=pl.BlockSpec((PIN_TM, PIN_TN), lambda i, j: (i, j)),
        out_shape=jax.ShapeDtypeStruct((TOK, Z_W), F32),
        scratch_shapes=[pltpu.VMEM((PIN_TM, D_MODEL), BF16)],
        compiler_params=_cparams(("parallel", "arbitrary")),
        name="proj_in",
    )(x, gain, w, bd, qg, k1g, k2g)


def _cmp_body(z_ref, pos_ref, w1_ref, w2k_ref, w2vT_ref, kg_ref, kc_ref, vcT_ref):
    half = CMP_STRIDE * HEAD_DIM
    lane = lax.broadcasted_iota(jnp.int32, (N_CMP, LANE), 1)
    first = lane < HEAD_DIM
    xk, xv = [], []
    for j in range(CMP_STRIDE // 2):
        ev = z_ref[pl.ds(2 * j, N_CMP, stride=CMP_STRIDE), :]
        od = z_ref[pl.ds(2 * j + 1, N_CMP, stride=CMP_STRIDE), :]
        xk.append(jnp.where(first, ev, pltpu.roll(od, HEAD_DIM, 1)))
        xv.append(jnp.where(first, pltpu.roll(ev, HEAD_DIM, 1), od))

    def hidden(x, kind):
        xa = (x + pos_ref[kind, :, :half]).astype(BF16)
        xb = (x + pos_ref[kind, :, half:]).astype(BF16)
        a = jnp.dot(xa, w1_ref[kind, :half, :], preferred_element_type=F32)
        b = jnp.dot(xb, w1_ref[kind, half:, :], preferred_element_type=F32)
        hid = a + pltpu.roll(b, N_CMP - 1, 0)
        return (hid * jax.nn.sigmoid(hid)).astype(BF16)

    yk = jnp.dot(hidden(jnp.concatenate(xk, axis=1), 0), w2k_ref[...], preferred_element_type=F32)
    ms = jnp.mean(yk * yk, axis=-1, keepdims=True)
    kc_ref[0, 0] = (yk * lax.rsqrt(ms + EPS) * kg_ref[...]).astype(BF16)
    hv = hidden(jnp.concatenate(xv, axis=1), 1)
    vcT_ref[0, 0] = lax.dot_general(w2vT_ref[...], hv, (((1,), (1,)), ((), ())),
                                    preferred_element_type=F32).astype(BF16)


def _compress(z, pos, w1, w2k, w2vT, kg):
    return pl.pallas_call(
        _cmp_body,
        grid=(BATCH, N_KV),
        in_specs=[
            pl.BlockSpec((SEQ, KV_GW), lambda b, g: (b, Z_KV // KV_GW + g)),
            pl.BlockSpec((2, 1, CMP_BLOCK * HEAD_DIM), lambda b, g: (0, 0, 0)),
            pl.BlockSpec((2, CMP_BLOCK * HEAD_DIM, CMP_HIDDEN), lambda b, g: (0, 0, 0)),
            pl.BlockSpec((CMP_HIDDEN, HEAD_DIM), lambda b, g: (0, 0)),
            pl.BlockSpec((HEAD_DIM, CMP_HIDDEN), lambda b, g: (0, 0)),
            pl.BlockSpec((1, HEAD_DIM), lambda b, g: (0, 0)),
        ],
        out_specs=[
            pl.BlockSpec((1, 1, N_CMP, HEAD_DIM), lambda b, g: (b, g, 0, 0)),
            pl.BlockSpec((1, 1, HEAD_DIM, N_CMP), lambda b, g: (b, g, 0, 0)),
        ],
        out_shape=[
            jax.ShapeDtypeStruct((BATCH, N_KV, N_CMP, HEAD_DIM), BF16),
            jax.ShapeDtypeStruct((BATCH, N_KV, HEAD_DIM, N_CMP), BF16),
        ],
        compiler_params=_cparams(("parallel", "parallel")),
        name="compress",
    )(z, pos, w1, w2k, w2vT, kg)


def _np_bucket(dist):
    n = np.maximum(dist, 0)
    max_exact = N_BUCKETS // 2
    nf = np.maximum(n, 1).astype(np.float32)
    large = max_exact + (np.log(nf / np.float32(max_exact)) / np.float32(math.log(MAX_DISTANCE / max_exact))
                         * np.float32(N_BUCKETS - max_exact)).astype(np.int32)
    large = np.minimum(large, N_BUCKETS - 1)
    return np.where(n < max_exact, n, large).astype(np.int32)


def _bucket_tables():
    kj = np.arange(TQ)[:, None]
    qi = np.arange(TQ)[None, :]
    near = np.stack([_np_bucket(qi - kj), _np_bucket(TQ + qi - kj)])
    start = np.array([_cmp_band_start(i) for i in range(NQ)])[:, None, None]
    c = start + np.arange(CMP_BAND)[None, :, None]
    t = (np.arange(NQ)[:, None, None] * TQ + qi[None])
    cmp_b = _np_bucket(t - (c * CMP_STRIDE + CMP_BLOCK - 1))
    full = _np_bucket(t - (np.arange(N_CMP)[None, :, None] * CMP_STRIDE + CMP_BLOCK - 1))
    for i in range(NQ):
        s0 = _cmp_band_start(i)
        rest = np.ones(N_CMP, bool)
        rest[s0:s0 + CMP_BAND] = False
        dist = t[i] - (np.arange(N_CMP)[:, None] * CMP_STRIDE + CMP_BLOCK - 1)
        assert np.all((dist[rest] < 0) | (full[i][rest] == N_BUCKETS - 1))
    return near, cmp_b.astype(np.int32)


def _cmp_band_start(i):
    return max((TQ // CMP_STRIDE) * i - 8, 0)


def _bias_body(rb_ref, near_ref, cmpb_ref, t0_ref, t1_ref, tc_ref):
    g = pl.program_id(0)

    def lut(bk, h):
        val = jnp.zeros(bk.shape, F32)
        for b in range(N_BUCKETS):
            val = jnp.where(bk == b, rb_ref[b, h], val)
        return val

    for hh in range(HPG):
        h = g * HPG + hh
        sl = slice(hh * TQ, (hh + 1) * TQ)
        far = rb_ref[N_BUCKETS - 1, h]
        t0_ref[0, :, sl] = (lut(near_ref[0], h) - far) * LOG2E
        t1_ref[0, :, sl] = (lut(near_ref[1], h) - far) * LOG2E

        def body(i, carry):
            start = pl.multiple_of(jnp.maximum((TQ // CMP_STRIDE) * i - 8, 0), 8)
            tc_ref[0, i, :, sl] = jnp.full((N_CMP, TQ), far * LOG2E, F32)
            tc_ref[0, i, pl.ds(start, CMP_BAND), sl] = lut(cmpb_ref[i], h) * LOG2E
            return carry

        lax.fori_loop(0, NQ, body, 0)


def _bias_tables(rel_bias, near, cmpb):
    return pl.pallas_call(
        _bias_body,
        grid=(N_KV,),
        in_specs=[
            pl.BlockSpec(memory_space=pltpu.SMEM),
            pl.BlockSpec((2, TQ, TQ), lambda g: (0, 0, 0)),
            pl.BlockSpec((NQ, CMP_BAND, TQ), lambda g: (0, 0, 0)),
        ],
        out_specs=[
            pl.BlockSpec((1, TQ, QL), lambda g: (g, 0, 0)),
            pl.BlockSpec((1, TQ, QL), lambda g: (g, 0, 0)),
            pl.BlockSpec((1, NQ, N_CMP, QL), lambda g: (g, 0, 0, 0)),
        ],
        out_shape=[
            jax.ShapeDtypeStruct((N_KV, TQ, QL), F32),
            jax.ShapeDtypeStruct((N_KV, TQ, QL), F32),
            jax.ShapeDtypeStruct((N_KV, NQ, N_CMP, QL), F32),
        ],
        compiler_params=_cparams(("arbitrary",)),
        name="bias_tables",
    )(rel_bias, near, cmpb)


def _attn_body(zq_ref, zs_ref, zw_ref, zg_ref, kc_ref, vcT_ref, t0_ref, t1_ref, tc_ref, ov_ref, o_ref,
               ks_ref, vsT_ref, kw_ref, vwT_ref, sel_ref, sfar_ref, gsc_ref):
    g = pl.program_id(1)
    ip = pl.program_id(2)

    @pl.when(ip == 0)
    def _():
        ones = jnp.ones((V_ROWS - HEAD_DIM, TQ), BF16)

        def fill(n, carry):
            for z_ref, k_ref, vT_ref in ((zs_ref, ks_ref, vsT_ref), (zw_ref, kw_ref, vwT_ref)):
                blk = z_ref[pl.ds(pl.multiple_of(n * TQ, TQ), TQ), :]
                k_ref[n] = blk[:, :HEAD_DIM].astype(BF16)
                vT_ref[n, :HEAD_DIM, :] = blk.T[HEAD_DIM:, :].astype(BF16)
                vT_ref[n, HEAD_DIM:, :] = ones
            return carry

        lax.fori_loop(0, NK, fill, 0)

    key = lax.broadcasted_iota(jnp.int32, (TQ, QL), 0)
    qi = lax.broadcasted_iota(jnp.int32, (TQ, QL), 1) & (TQ - 1)
    blk = lax.broadcasted_iota(jnp.int32, (N_SBLK, TQ), 0)
    ql = lax.broadcasted_iota(jnp.int32, (N_SBLK, TQ), 1)
    key1 = lax.broadcasted_iota(jnp.int32, (TQ, TQ), 0)
    q1 = lax.broadcasted_iota(jnp.int32, (TQ, TQ), 1)
    causal = key1 <= q1
    nlast = WINDOW // TQ

    def sel_mask(t, n):
        r0 = sel_ref[t, pl.ds(2 * n, 1), :]
        r1 = sel_ref[t, pl.ds(2 * n + 1, 1), :]
        return jnp.where(key1 < SEL_BLOCK, r0, r1) > 0.5

    def scores(qT, k_ref, n, bias, mask):
        sc = jnp.dot(k_ref[n], qT, preferred_element_type=F32)
        if bias is not None:
            sc = sc + bias
        return jnp.concatenate(
            [jnp.where(mask, sc[:, h * TQ:(h + 1) * TQ], NEG) for h in range(HPG)], axis=1)

    def colmax8(sc):
        return jnp.max(sc.reshape(TQ // 8, 8, QL), axis=0)

    def probs(sc, mrow):
        return jnp.exp2(sc - mrow).astype(BF16)

    def pv(vT_ref, ns, ps):
        vv = vT_ref[ns[0]] if len(ns) == 1 else jnp.concatenate([vT_ref[n] for n in ns], axis=1)
        pp = ps[0] if len(ps) == 1 else jnp.concatenate(ps, axis=0)
        return jnp.dot(vv, pp, preferred_element_type=F32)

    def finish(acc):
        return acc[:HEAD_DIM] * (1.0 / acc[HEAD_DIM:HEAD_DIM + 1])

    def front(t):
        i = Q_PAIR * ip + t
        rows = slice(t * TQ, (t + 1) * TQ)
        qt = zq_ref[rows, :].T
        qT = jnp.concatenate([qt[h * HEAD_DIM:(h + 1) * HEAD_DIM] for h in range(HPG)], axis=1).astype(BF16)
        gsc_ref[t] = zg_ref[rows, :].T
        iv = jnp.full((TQ, QL), i, jnp.int32)
        i1 = jnp.full((TQ, TQ), i, jnp.int32)

        s = jnp.dot(kc_ref[0, 0], qT, preferred_element_type=F32) + tc_ref[0, t]
        mask_c = (iv * TQ + qi) >= (key * CMP_STRIDE + (CMP_BLOCK - 1))
        s = jnp.where(mask_c, s, NEG)
        m = jnp.max(s, axis=0, keepdims=True)
        e = jnp.where(mask_c, jnp.exp2(s - m), 0.0)
        l = jnp.sum(e, axis=0, keepdims=True)
        p = e * jnp.where(l > 0.0, 1.0 / l, 0.0)
        ocT = jnp.dot(vcT_ref[0, 0], p.astype(BF16), preferred_element_type=F32)

        psum = p[:, 0:TQ] + p[:, TQ:2 * TQ] + p[:, 2 * TQ:3 * TQ] + p[:, 3 * TQ:4 * TQ]
        imp = jnp.dot(ov_ref[...], psum, preferred_element_type=F32,
                      precision=lax.Precision.HIGHEST)
        cur = jnp.full((N_SBLK, TQ), i, jnp.int32) * (TQ // SEL_BLOCK) + ql // SEL_BLOCK
        forced = (blk == 0) | (blk == cur) | (blk == cur - 1)
        score = jnp.where(blk <= cur, jnp.where(forced, FORCE, imp), NEG)
        rank = jnp.zeros((N_SBLK, TQ), jnp.int32)
        for jj in range(N_SBLK):
            row = score[jj:jj + 1, :]
            beats = (row > score) | ((row == score) & (blk > jj))
            rank = rank + beats.astype(jnp.int32)
        sel_ref[t] = ((rank < N_SEL) & (blk <= cur)).astype(F32)

        n1 = jnp.maximum(i - 1, 0)
        n2, n3, n4 = jnp.maximum(i - 2, 0), jnp.maximum(i - 3, 0), jnp.maximum(i - nlast, 0)
        w_d = scores(qT, kw_ref, i, t0_ref[0], causal)
        w_1 = scores(qT, kw_ref, n1, t1_ref[0], i1 >= 1)
        w_2 = scores(qT, kw_ref, n2, None, i1 >= 2)
        w_3 = scores(qT, kw_ref, n3, None, i1 >= 3)
        w_4 = scores(qT, kw_ref, n4, None, (q1 < key1) & (i1 >= nlast))
        mxw = jnp.maximum(jnp.maximum(jnp.maximum(colmax8(w_d), colmax8(w_1)),
                                      jnp.maximum(colmax8(w_2), colmax8(w_3))), colmax8(w_4))
        m_w = jnp.max(mxw, axis=0, keepdims=True)
        owT = finish(pv(vwT_ref, [i], [probs(w_d, m_w)])
                     + pv(vwT_ref, [n1, n2], [probs(w_1, m_w), probs(w_2, m_w)])
                     + pv(vwT_ref, [n3, n4], [probs(w_3, m_w), probs(w_4, m_w)]))

        s_d = scores(qT, ks_ref, i, t0_ref[0], causal & sel_mask(t, i))
        s_1 = scores(qT, ks_ref, n1, t1_ref[0], sel_mask(t, n1) & (i1 >= 1))
        return dict(i=i, n1=n1, qT=qT, ocT=ocT, owT=owT, s_d=s_d, s_1=s_1,
                    mx=jnp.maximum(colmax8(s_d), colmax8(s_1)),
                    nfar=jnp.maximum(i - 1, 0))

    tiles = [front(t) for t in range(Q_PAIR)]
    ntrip = lax.shift_right_logical(tiles[-1]["nfar"] + (FAR_GROUP - 1), FAR_GROUP.bit_length() - 1)

    def far_scores(tr, mxs):
        out = []
        for t, tl in enumerate(tiles):
            mx = mxs[t]
            for c in range(FAR_GROUP):
                n = FAR_GROUP * tr + c
                sc = scores(tl["qT"], ks_ref, n, None,
                            sel_mask(t, n) & (jnp.full((TQ, TQ), n, jnp.int32) < tl["nfar"]))
                sfar_ref[t, n] = sc
                mx = jnp.maximum(mx, colmax8(sc))
            out.append(mx)
        return tuple(out)

    mxs = lax.fori_loop(0, ntrip, far_scores, tuple(tl["mx"] for tl in tiles))
    m_s = [jnp.max(mx, axis=0, keepdims=True) for mx in mxs]
    accs = tuple(pv(vsT_ref, [tl["i"], tl["n1"]], [probs(tl["s_d"], m_s[t]), probs(tl["s_1"], m_s[t])])
                 for t, tl in enumerate(tiles))

    def far_pv(tr, accs_):
        ns = [FAR_GROUP * tr + c for c in range(FAR_GROUP)]
        return tuple(accs_[t] + pv(vsT_ref, ns, [probs(sfar_ref[t, n], m_s[t]) for n in ns])
                     for t in range(Q_PAIR))

    accs = lax.fori_loop(0, ntrip, far_pv, accs)

    for t, tl in enumerate(tiles):
        base = g * GATE_GW
        gate = [jnp.concatenate([gsc_ref[t, pl.ds(base + br * HPG + h, 1), :] for h in range(HPG)], axis=1)
                for br in range(3)]
        aT = gate[0] * tl["ocT"] + gate[1] * finish(accs[t]) + gate[2] * tl["owT"]
        a4 = jnp.concatenate([aT[:, h * TQ:(h + 1) * TQ] for h in range(HPG)], axis=0)
        o_ref[t * TQ:(t + 1) * TQ, :] = a4.T.astype(o_ref.dtype)


def _attention(z, kc, vcT, t0, t1, tc, ov):
    qw = HPG * HEAD_DIM
    rows = Q_PAIR * TQ
    npair = NQ // Q_PAIR
    return pl.pallas_call(
        _attn_body,
        grid=(BATCH, N_KV, npair),
        in_specs=[
            pl.BlockSpec((rows, qw), lambda b, g, i: (b * npair + i, Z_Q // qw + g)),
            pl.BlockSpec((SEQ, KV_GW), lambda b, g, i: (b, (Z_KV + 2 * KV_W) // KV_GW + g)),
            pl.BlockSpec((SEQ, KV_GW), lambda b, g, i: (b, (Z_KV + 4 * KV_W) // KV_GW + g)),
            pl.BlockSpec((rows, LANE), lambda b, g, i: (b * npair + i, Z_NG // LANE)),
            pl.BlockSpec((1, 1, N_CMP, HEAD_DIM), lambda b, g, i: (b, g, 0, 0)),
            pl.BlockSpec((1, 1, HEAD_DIM, N_CMP), lambda b, g, i: (b, g, 0, 0)),
            pl.BlockSpec((1, TQ, QL), lambda b, g, i: (g, 0, 0)),
            pl.BlockSpec((1, TQ, QL), lambda b, g, i: (g, 0, 0)),
            pl.BlockSpec((1, Q_PAIR, N_CMP, QL), lambda b, g, i: (g, i, 0, 0)),
            pl.BlockSpec((N_SBLK, N_CMP), lambda b, g, i: (0, 0)),
        ],
        out_specs=pl.BlockSpec((rows, qw), lambda b, g, i: (b * npair + i, g)),
        out_shape=jax.ShapeDtypeStruct((TOK, NSA_W), BF16),
        scratch_shapes=[
            pltpu.VMEM((NK, TQ, HEAD_DIM), BF16),
            pltpu.VMEM((NK, V_ROWS, TQ), BF16),
            pltpu.VMEM((NK, TQ, HEAD_DIM), BF16),
            pltpu.VMEM((NK, V_ROWS, TQ), BF16),
            pltpu.VMEM((Q_PAIR, N_SBLK, TQ), F32),
            pltpu.VMEM((Q_PAIR, NK, TQ, QL), F32),
            pltpu.VMEM((Q_PAIR, LANE, TQ), F32),
        ],
        compiler_params=_cparams(("parallel", "parallel", "arbitrary")),
        name="nsa_attention",
    )(z, z, z, z, kc, vcT, t0, t1, tc, ov)


def _sgu_body(u_ref, v_ref, g_ref, b_ref, w_ref, bs_ref, o_ref):
    r = lax.broadcasted_iota(jnp.int32, (GM_CHUNK, GM_CHUNK), 0)
    c = lax.broadcasted_iota(jnp.int32, (GM_CHUNK, GM_CHUNK), 1)
    tril = r >= c
    ws = [jnp.where(tril, w_ref[gi], 0.0).astype(BF16) for gi in range(GM_GROUPS)]
    for ch in range(SGU_TM // GM_CHUNK):
        rows = slice(ch * GM_CHUNK, (ch + 1) * GM_CHUNK)
        v = v_ref[rows, :]
        mu = jnp.mean(v, axis=-1, keepdims=True)
        vc = v - mu
        var = jnp.mean(vc * vc, axis=-1, keepdims=True)
        vn = (vc * lax.rsqrt(var + EPS) * g_ref[...] + b_ref[...]).astype(BF16)
        for gi in range(GM_GROUPS):
            cols = slice(gi * GM_GW, (gi + 1) * GM_GW)
            mixed = jnp.dot(ws[gi], vn[:, cols], preferred_element_type=F32) + bs_ref[:, cols]
            o_ref[rows, cols] = (u_ref[rows, cols] * mixed).astype(o_ref.dtype)


def _sgu(z, ln_g, ln_b, w_s, bs_exp):
    return pl.pallas_call(
        _sgu_body,
        grid=(TOK // SGU_TM,),
        in_specs=[
            pl.BlockSpec((SGU_TM, GM_W), lambda i: (i, Z_U // GM_W)),
            pl.BlockSpec((SGU_TM, GM_W), lambda i: (i, Z_V // GM_W)),
            pl.BlockSpec((1, GM_W), lambda i: (0, 0)),
            pl.BlockSpec((1, GM_W), lambda i: (0, 0)),
            pl.BlockSpec((GM_GROUPS, GM_CHUNK, GM_CHUNK), lambda i: (0, 0, 0)),
            pl.BlockSpec((GM_CHUNK, GM_W), lambda i: (0, 0)),
        ],
        out_specs=pl.BlockSpec((SGU_TM, GM_W), lambda i: (i, 0)),
        out_shape=jax.ShapeDtypeStruct((TOK, GM_W), BF16),
        compiler_params=_cparams(("parallel",)),
        name="sgu",
    )(z, z, ln_g, ln_b, w_s, bs_exp)


def _out_body(a_ref, sg_ref, m1_ref, m2_ref, x_ref, wn_ref, ws_ref, wo_ref, gn_ref, o_ref, h_ref):
    pa = jnp.dot(a_ref[...], wn_ref[...], preferred_element_type=F32)
    ps = jnp.dot(sg_ref[...], ws_ref[...], preferred_element_type=F32)
    merged = (m1_ref[...] * pa + m2_ref[...] * ps).astype(BF16)
    y = x_ref[...] + jnp.dot(merged, wo_ref[...], preferred_element_type=F32)
    o_ref[...] = y
    h_ref[...] = _rms_bf16(y, gn_ref[...])


def _merge_out(a, sg, z, x, wn, ws, wo, l, next_gain):
    const = lambda i: (l, 0, 0)
    return pl.pallas_call(
        _out_body,
        grid=(TOK // OUT_TM,),
        in_specs=[
            pl.BlockSpec((OUT_TM, NSA_W), lambda i: (i, 0)),
            pl.BlockSpec((OUT_TM, GM_W), lambda i: (i, 0)),
            pl.BlockSpec((OUT_TM, D_MODEL), lambda i: (i, 0)),
            pl.BlockSpec((OUT_TM, D_MODEL), lambda i: (i, 1)),
            pl.BlockSpec((OUT_TM, D_MODEL), lambda i: (i, 0)),
            pl.BlockSpec((None, NSA_W, D_MODEL), const),
            pl.BlockSpec((None, GM_W, D_MODEL), const),
            pl.BlockSpec((None, D_MODEL, D_MODEL), const),
            pl.BlockSpec((1, D_MODEL), lambda i: (0, 0)),
        ],
        out_specs=[pl.BlockSpec((OUT_TM, D_MODEL), lambda i: (i, 0)),
                   pl.BlockSpec((OUT_TM, D_MODEL), lambda i: (i, 0))],
        out_shape=[jax.ShapeDtypeStruct((TOK, D_MODEL), F32),
                   jax.ShapeDtypeStruct((TOK, D_MODEL), BF16)],
        compiler_params=_cparams(("parallel",)),
        name="merge_out",
    )(a, sg, z, z, x, wn, ws, wo, next_gain)


def _overlap_T():
    ci = np.arange(N_CMP)[None, :] * CMP_STRIDE
    sj = np.arange(N_SBLK)[:, None] * SEL_BLOCK
    ov = np.clip(np.minimum(ci + CMP_BLOCK, sj + SEL_BLOCK) - np.maximum(ci, sj), 0, None)
    ov = ov.astype(np.float32) / CMP_BLOCK
    ov[:, N_CMP - 1] = 0.0
    return ov


def _block_diag_ones():
    r = np.arange(LANE)
    return (r[:, None] // HEAD_DIM == r[None, :] // HEAD_DIM).astype(np.float32)


def _prep_w_in(w):
    L = w.shape[0]
    kv = w[:, :, OFF_KV:OFF_NG].reshape(L, D_MODEL, 3, 2, N_KV, HEAD_DIM)
    kv = kv.transpose(0, 1, 2, 4, 3, 5).reshape(L, D_MODEL, 6 * KV_W)
    ng = w[:, :, OFF_NG:OFF_UV].reshape(L, D_MODEL, N_KV, HPG, 3).transpose(0, 1, 2, 4, 3)
    ng = ng.reshape(L, D_MODEL, N_KV, 3 * HPG)
    ng = jnp.pad(ng, ((0, 0), (0, 0), (0, 0), (0, GATE_GW - 3 * HPG))).reshape(L, D_MODEL, N_KV * GATE_GW)
    ng = jnp.pad(ng, ((0, 0), (0, 0), (0, PIN_TN - N_KV * GATE_GW)))
    return jnp.concatenate(
        [w[:, :, OFF_MG:], w[:, :, OFF_UV:OFF_MG], w[:, :, :OFF_KV], kv, ng], axis=2).astype(BF16)


def _mixer(x, z, l, p, tables):
    t0, t1, tc, ov = tables
    pos = jnp.stack([p["cmp_pos_k"][l], p["cmp_pos_v"][l]]).reshape(2, 1, CMP_BLOCK * HEAD_DIM)
    w1 = jnp.stack([p["cmp_k_w1"][l], p["cmp_v_w1"][l]]).astype(BF16)
    kc, vcT = _compress(z, pos, w1, p["cmp_k_w2"][l].astype(BF16), p["cmp_v_w2"][l].T.astype(BF16),
                        p["k_norm"][l, 0].reshape(1, HEAD_DIM))
    a = _attention(z, kc, vcT, t0, t1, tc, ov)
    bs_exp = jnp.repeat(p["sgu_b"][l].T, GM_GW, axis=1)
    sg = _sgu(z, p["sgu_norm_g"][l].reshape(1, GM_W), p["sgu_norm_b"][l].reshape(1, GM_W),
              p["sgu_w"][l], bs_exp)
    return _merge_out(a, sg, z, x, p["wn"], p["ws"], p["wo"], l, p["ffn2_norm"][l].reshape(1, D_MODEL))


def kernel(x, rel_bias, ffn1_norm, ffn1_w_gate, ffn1_w_up, ffn1_w_down, mix_norm, w_in, q_norm, k_norm, cmp_pos_k, cmp_pos_v, cmp_k_w1, cmp_k_w2, cmp_v_w1, cmp_v_w2, sgu_norm_g, sgu_norm_b, sgu_w, sgu_b, w_proj_nsa, w_proj_sgu, w_out, ffn2_norm, ffn2_w_gate, ffn2_w_up, ffn2_w_down):
    p = dict(cmp_pos_k=cmp_pos_k, cmp_pos_v=cmp_pos_v, cmp_k_w1=cmp_k_w1, cmp_k_w2=cmp_k_w2,
             cmp_v_w1=cmp_v_w1, cmp_v_w2=cmp_v_w2, k_norm=k_norm, sgu_norm_g=sgu_norm_g,
             sgu_norm_b=sgu_norm_b, sgu_w=sgu_w, sgu_b=sgu_b, wn=w_proj_nsa.astype(BF16),
             ws=w_proj_sgu.astype(BF16), wo=w_out.astype(BF16), ffn2_norm=ffn2_norm)
    near, cmpb = _bucket_tables()
    t0, t1, tc = _bias_tables(rel_bias, jnp.asarray(near), jnp.asarray(cmpb))
    tables = (t0, t1, tc, jnp.asarray(_overlap_T()))
    bd = jnp.asarray(_block_diag_ones()).astype(BF16)
    w_in_r = _prep_w_in(w_in)
    f1 = (ffn1_w_gate, ffn1_w_up, ffn1_w_down)
    f2 = (ffn2_w_gate, ffn2_w_up, ffn2_w_down)

    y = x.reshape(TOK, D_MODEL)
    for l in range(DEPTH):
        y = _ffn(y, _rmsnorm(y, ffn1_norm[l].reshape(1, D_MODEL)), *f1, l)
        z = _proj_in(y, mix_norm[l].reshape(1, D_MODEL), w_in_r, l, bd,
                     jnp.tile(q_norm[l], PIN_TN // HEAD_DIM).reshape(1, PIN_TN),
                     jnp.tile(k_norm[l, 1], PIN_TN // HEAD_DIM).reshape(1, PIN_TN),
                     jnp.tile(k_norm[l, 2], PIN_TN // HEAD_DIM).reshape(1, PIN_TN))
        y, h = _mixer(y, z, l, p, tables)
        y = _ffn(y, h, *f2, l)
    return y.reshape(BATCH, SEQ, D_MODEL)
```

```python
import math

import numpy as np
import jax
import jax.numpy as jnp
from jax import lax
from jax.experimental import pallas as pl
from jax.experimental.pallas import tpu as pltpu

F32 = jnp.float32
BF16 = jnp.bfloat16

D_MODEL = 2048
BATCH = 4
SEQ = 2048
DEPTH = 2
TOK = BATCH * SEQ
HEAD_DIM = 64
N_HEADS = 16
N_KV = 4
HPG = 4
NSA_W = 1024
KV_W = 256
CMP_BLOCK = 32
CMP_STRIDE = 16
CMP_HIDDEN = 256
SEL_BLOCK = 64
N_SEL = 16
WINDOW = 512
GM_W = 1024
GM_CHUNK = 128
GM_GW = 128
GM_GROUPS = 8
N_BUCKETS = 32
MAX_DISTANCE = 128
D_FF = 5504
EPS = 1e-6
NEG = -1e30
FORCE = 1e4
OFF_KV = NSA_W
OFF_NG = OFF_KV + 6 * KV_W
OFF_UV = OFF_NG + 3 * N_HEADS
OFF_MG = OFF_UV + 2 * GM_W
LOG2E = 1.4426950408889634

LANE = 128
VMEM_LIMIT = 56 * 1024 * 1024

FFN_TF = 512
FFN_UP_TM = 1024
NORM_TM = 512
FFN_DN_TM = 512
FFN_DN_TN = 512
PIN_TM = 1024
PIN_TN = 512
PIN_RC = 256
Z_MG = 0
Z_U = Z_MG + 2 * D_MODEL
Z_V = Z_U + GM_W
Z_Q = Z_V + GM_W
Z_KV = Z_Q + NSA_W
Z_NG = Z_KV + 6 * KV_W
Z_W = Z_NG + PIN_TN
KV_GW = 2 * HEAD_DIM
GATE_GW = 16
TQ = 128
NQ = SEQ // TQ
NK = SEQ // TQ
QL = HPG * TQ
N_CMP = 128
CMP_BAND = 16
N_SBLK = SEQ // SEL_BLOCK
V_ROWS = HEAD_DIM + 16
FAR_GROUP = 4
Q_PAIR = 4
SGU_TM = 512
OUT_TM = 256


def _cparams(sem):
    return pltpu.CompilerParams(dimension_semantics=sem, vmem_limit_bytes=VMEM_LIMIT)


def _cast_rows(src_ref, dst_ref, chunk):
    def body(r, carry):
        rows = pl.ds(pl.multiple_of(r * chunk, chunk), chunk)
        dst_ref[rows, :] = src_ref[rows, :].astype(BF16)
        return carry

    lax.fori_loop(0, src_ref.shape[0] // chunk, body, 0)


def _rms_bf16(x, gain):
    ms = jnp.mean(x * x, axis=-1, keepdims=True)
    return (x * lax.rsqrt(ms + EPS) * gain).astype(BF16)


def _norm_body(x_ref, g_ref, h_ref):
    h_ref[...] = _rms_bf16(x_ref[...], g_ref[...])


def _rmsnorm(x, gain):
    return pl.pallas_call(
        _norm_body,
        grid=(TOK // NORM_TM,),
        in_specs=[pl.BlockSpec((NORM_TM, D_MODEL), lambda i: (i, 0)),
                  pl.BlockSpec((1, D_MODEL), lambda i: (0, 0))],
        out_specs=pl.BlockSpec((NORM_TM, D_MODEL), lambda i: (i, 0)),
        out_shape=jax.ShapeDtypeStruct((TOK, D_MODEL), BF16),
        compiler_params=_cparams(("parallel",)),
        name="rmsnorm",
    )(x, gain)


def _ffn_up_body(h_ref, wg_ref, wu_ref, a_ref, wgb_ref, wub_ref):
    @pl.when(pl.program_id(1) == 0)
    def _():
        _cast_rows(wg_ref, wgb_ref, LANE)
        _cast_rows(wu_ref, wub_ref, LANE)

    h = h_ref[...]
    g = jnp.dot(h, wgb_ref[...], preferred_element_type=F32)
    u = jnp.dot(h, wub_ref[...], preferred_element_type=F32)
    a_ref[...] = (g * jax.nn.sigmoid(g) * u).astype(BF16)


def _ffn_down_body(a_ref, wd_ref, x_ref, o_ref, wdb_ref):
    @pl.when(pl.program_id(1) == 0)
    def _():
        _cast_rows(wd_ref, wdb_ref, LANE)

    o_ref[...] = x_ref[...] + 0.5 * jnp.dot(a_ref[...], wdb_ref[...], preferred_element_type=F32)


def _ffn(x, h, wg, wu, wd, l):
    nf = pl.cdiv(D_FF, FFN_TF)
    a = pl.pallas_call(
        _ffn_up_body,
        grid=(nf, TOK // FFN_UP_TM),
        in_specs=[
            pl.BlockSpec((FFN_UP_TM, D_MODEL), lambda f, m: (m, 0)),
            pl.BlockSpec((None, D_MODEL, FFN_TF), lambda f, m: (l, 0, f)),
            pl.BlockSpec((None, D_MODEL, FFN_TF), lambda f, m: (l, 0, f)),
        ],
        out_specs=pl.BlockSpec((FFN_UP_TM, FFN_TF), lambda f, m: (m, f)),
        out_shape=jax.ShapeDtypeStruct((TOK, D_FF), BF16),
        scratch_shapes=[pltpu.VMEM((D_MODEL, FFN_TF), BF16), pltpu.VMEM((D_MODEL, FFN_TF), BF16)],
        compiler_params=_cparams(("arbitrary", "arbitrary")),
        name="ffn_up",
    )(h, wg, wu)
    return pl.pallas_call(
        _ffn_down_body,
        grid=(D_MODEL // FFN_DN_TN, TOK // FFN_DN_TM),
        in_specs=[
            pl.BlockSpec((FFN_DN_TM, D_FF), lambda n, m: (m, 0)),
            pl.BlockSpec((None, D_FF, FFN_DN_TN), lambda n, m: (l, 0, n)),
            pl.BlockSpec((FFN_DN_TM, FFN_DN_TN), lambda n, m: (m, n)),
        ],
        out_specs=pl.BlockSpec((FFN_DN_TM, FFN_DN_TN), lambda n, m: (m, n)),
        out_shape=jax.ShapeDtypeStruct((TOK, D_MODEL), F32),
        scratch_shapes=[pltpu.VMEM((D_FF, FFN_DN_TN), BF16)],
        compiler_params=_cparams(("arbitrary", "arbitrary")),
        name="ffn_down",
    )(a, wd, x)


def _head_rms(y, bd):
    y2 = (y * y).astype(BF16)
    ss = jnp.concatenate(
        [jnp.dot(y2[:, k * LANE:(k + 1) * LANE], bd, preferred_element_type=F32)
         for k in range(y.shape[1] // LANE)], axis=1)
    return y * lax.rsqrt(ss * (1.0 / HEAD_DIM) + EPS)


def _pin_body(x_ref, g_ref, w_ref, bd_ref, qg_ref, k1g_ref, k2g_ref, o_ref, h_ref):
    j = pl.program_id(1)

    @pl.when(j == 0)
    def _():
        x = x_ref[...]
        ms = jnp.mean(x * x, axis=-1, keepdims=True)
        h_ref[...] = (x * lax.rsqrt(ms + EPS) * g_ref[...]).astype(BF16)

    t_u = Z_U // PIN_TN
    t_q = Z_Q // PIN_TN
    t_kv = Z_KV // PIN_TN
    t_ng = Z_NG // PIN_TN

    def emit(act):
        for r in range(PIN_TM // PIN_RC):
            rows = slice(r * PIN_RC, (r + 1) * PIN_RC)
            o_ref[rows, :] = act(jnp.dot(h_ref[rows, :], w_ref[...], preferred_element_type=F32))

    @pl.when((j < t_u) | (j == t_ng))
    def _():
        emit(jax.nn.sigmoid)

    @pl.when((j >= t_u) & (j < t_q))
    def _():
        emit(jax.nn.gelu)

    @pl.when((j >= t_q) & (j < t_kv))
    def _():
        emit(lambda y: _head_rms(y, bd_ref[...]) * (qg_ref[...] * (HEAD_DIM ** -0.5 * LOG2E)))

    @pl.when(j == t_kv)
    def _():
        emit(lambda y: y)

    def kv_act(kg_ref):
        is_k = (lax.broadcasted_iota(jnp.int32, (1, PIN_TN), 1) & (KV_GW - 1)) < HEAD_DIM
        return lambda y: jnp.where(is_k, _head_rms(y, bd_ref[...]) * kg_ref[...], y)

    @pl.when(j == t_kv + 1)
    def _():
        emit(kv_act(k1g_ref))

    @pl.when(j == t_kv + 2)
    def _():
        emit(kv_act(k2g_ref))


def _proj_in(x, gain, w, l, bd, qg, k1g, k2g):
    return pl.pallas_call(
        _pin_body,
        grid=(TOK // PIN_TM, Z_W // PIN_TN),
        in_specs=[
            pl.BlockSpec((PIN_TM, D_MODEL), lambda i, j: (i, 0)),
            pl.BlockSpec((1, D_MODEL), lambda i, j: (0, 0)),
            pl.BlockSpec((None, D_MODEL, PIN_TN), lambda i, j: (l, 0, j)),
            pl.BlockSpec((LANE, LANE), lambda i, j: (0, 0)),
            pl.BlockSpec((1, PIN_TN), lambda i, j: (0, 0)),
            pl.BlockSpec((1, PIN_TN), lambda i, j: (0, 0)),
            pl.BlockSpec((1, PIN_TN), lambda i, j: (0, 0)),
        ],
        out_specs=pl.BlockSpec((PIN_TM, PIN_TN), lambda i, j: (i, j)),
        out_shape=jax.ShapeDtypeStruct((TOK, Z_W), F32),
        scratch_shapes=[pltpu.VMEM((PIN_TM, D_MODEL), BF16)],
        compiler_params=_cparams(("parallel", "arbitrary")),
        name="proj_in",
    )(x, gain, w, bd, qg, k1g, k2g)


def _cmp_body(z_ref, pos_ref, w1_ref, w2k_ref, w2vT_ref, kg_ref, kc_ref, vcT_ref):
    half = CMP_STRIDE * HEAD_DIM
    lane = lax.broadcasted_iota(jnp.int32, (N_CMP, LANE), 1)
    first = lane < HEAD_DIM
    xk, xv = [], []
    for j in range(CMP_STRIDE // 2):
        ev = z_ref[pl.ds(2 * j, N_CMP, stride=CMP_STRIDE), :]
        od = z_ref[pl.ds(2 * j + 1, N_CMP, stride=CMP_STRIDE), :]
        xk.append(jnp.where(first, ev, pltpu.roll(od, HEAD_DIM, 1)))
        xv.append(jnp.where(first, pltpu.roll(ev, HEAD_DIM, 1), od))

    def hidden(x, kind):
        xa = (x + pos_ref[kind, :, :half]).astype(BF16)
        xb = (x + pos_ref[kind, :, half:]).astype(BF16)
        a = jnp.dot(xa, w1_ref[kind, :half, :], preferred_element_type=F32)
        b = jnp.dot(xb, w1_ref[kind, half:, :], preferred_element_type=F32)
        hid = a + pltpu.roll(b, N_CMP - 1, 0)
        return (hid * jax.nn.sigmoid(hid)).astype(BF16)

    yk = jnp.dot(hidden(jnp.concatenate(xk, axis=1), 0), w2k_ref[...], preferred_element_type=F32)
    ms = jnp.mean(yk * yk, axis=-1, keepdims=True)
    kc_ref[0, 0] = (yk * lax.rsqrt(ms + EPS) * kg_ref[...]).astype(BF16)
    hv = hidden(jnp.concatenate(xv, axis=1), 1)
    vcT_ref[0, 0] = lax.dot_general(w2vT_ref[...], hv, (((1,), (1,)), ((), ())),
                                    preferred_element_type=F32).astype(BF16)


def _compress(z, pos, w1, w2k, w2vT, kg):
    return pl.pallas_call(
        _cmp_body,
        grid=(BATCH, N_KV),
        in_specs=[
            pl.BlockSpec((SEQ, KV_GW), lambda b, g: (b, Z_KV // KV_GW + g)),
            pl.BlockSpec((2, 1, CMP_BLOCK * HEAD_DIM), lambda b, g: (0, 0, 0)),
            pl.BlockSpec((2, CMP_BLOCK * HEAD_DIM, CMP_HIDDEN), lambda b, g: (0, 0, 0)),
            pl.BlockSpec((CMP_HIDDEN, HEAD_DIM), lambda b, g: (0, 0)),
            pl.BlockSpec((HEAD_DIM, CMP_HIDDEN), lambda b, g: (0, 0)),
            pl.BlockSpec((1, HEAD_DIM), lambda b, g: (0, 0)),
        ],
        out_specs=[
            pl.BlockSpec((1, 1, N_CMP, HEAD_DIM), lambda b, g: (b, g, 0, 0)),
            pl.BlockSpec((1, 1, HEAD_DIM, N_CMP), lambda b, g: (b, g, 0, 0)),
        ],
        out_shape=[
            jax.ShapeDtypeStruct((BATCH, N_KV, N_CMP, HEAD_DIM), BF16),
            jax.ShapeDtypeStruct((BATCH, N_KV, HEAD_DIM, N_CMP), BF16),
        ],
        compiler_params=_cparams(("parallel", "parallel")),
        name="compress",
    )(z, pos, w1, w2k, w2vT, kg)


def _np_bucket(dist):
    n = np.maximum(dist, 0)
    max_exact = N_BUCKETS // 2
    nf = np.maximum(n, 1).astype(np.float32)
    large = max_exact + (np.log(nf / np.float32(max_exact)) / np.float32(math.log(MAX_DISTANCE / max_exact))
                         * np.float32(N_BUCKETS - max_exact)).astype(np.int32)
    large = np.minimum(large, N_BUCKETS - 1)
    return np.where(n < max_exact, n, large).astype(np.int32)


def _bucket_tables():
    kj = np.arange(TQ)[:, None]
    qi = np.arange(TQ)[None, :]
    near = np.stack([_np_bucket(qi - kj), _np_bucket(TQ + qi - kj)])
    start = np.array([_cmp_band_start(i) for i in range(NQ)])[:, None, None]
    c = start + np.arange(CMP_BAND)[None, :, None]
    t = (np.arange(NQ)[:, None, None] * TQ + qi[None])
    cmp_b = _np_bucket(t - (c * CMP_STRIDE + CMP_BLOCK - 1))
    full = _np_bucket(t - (np.arange(N_CMP)[None, :, None] * CMP_STRIDE + CMP_BLOCK - 1))
    for i in range(NQ):
        s0 = _cmp_band_start(i)
        rest = np.ones(N_CMP, bool)
        rest[s0:s0 + CMP_BAND] = False
        dist = t[i] - (np.arange(N_CMP)[:, None] * CMP_STRIDE + CMP_BLOCK - 1)
        assert np.all((dist[rest] < 0) | (full[i][rest] == N_BUCKETS - 1))
    return near, cmp_b.astype(np.int32)


def _cmp_band_start(i):
    return max((TQ // CMP_STRIDE) * i - 8, 0)


def _bias_body(rb_ref, near_ref, cmpb_ref, t0_ref, t1_ref, tc_ref):
    g = pl.program_id(0)

    def lut(bk, h):
        val = jnp.zeros(bk.shape, F32)
        for b in range(N_BUCKETS):
            val = jnp.where(bk == b, rb_ref[b, h], val)
        return val

    for hh in range(HPG):
        h = g * HPG + hh
        sl = slice(hh * TQ, (hh + 1) * TQ)
        far = rb_ref[N_BUCKETS - 1, h]
        t0_ref[0, :, sl] = (lut(near_ref[0], h) - far) * LOG2E
        t1_ref[0, :, sl] = (lut(near_ref[1], h) - far) * LOG2E

        def body(i, carry):
            start = pl.multiple_of(jnp.maximum((TQ // CMP_STRIDE) * i - 8, 0), 8)
            tc_ref[0, i, :, sl] = jnp.full((N_CMP, TQ), far * LOG2E, F32)
            tc_ref[0, i, pl.ds(start, CMP_BAND), sl] = lut(cmpb_ref[i], h) * LOG2E
            return carry

        lax.fori_loop(0, NQ, body, 0)


def _bias_tables(rel_bias, near, cmpb):
    return pl.pallas_call(
        _bias_body,
        grid=(N_KV,),
        in_specs=[
            pl.BlockSpec(memory_space=pltpu.SMEM),
            pl.BlockSpec((2, TQ, TQ), lambda g: (0, 0, 0)),
            pl.BlockSpec((NQ, CMP_BAND, TQ), lambda g: (0, 0, 0)),
        ],
        out_specs=[
            pl.BlockSpec((1, TQ, QL), lambda g: (g, 0, 0)),
            pl.BlockSpec((1, TQ, QL), lambda g: (g, 0, 0)),
            pl.BlockSpec((1, NQ, N_CMP, QL), lambda g: (g, 0, 0, 0)),
        ],
        out_shape=[
            jax.ShapeDtypeStruct((N_KV, TQ, QL), F32),
            jax.ShapeDtypeStruct((N_KV, TQ, QL), F32),
            jax.ShapeDtypeStruct((N_KV, NQ, N_CMP, QL), F32),
        ],
        compiler_params=_cparams(("arbitrary",)),
        name="bias_tables",
    )(rel_bias, near, cmpb)


def _attn_body(zq_ref, zs_ref, zw_ref, zg_ref, kc_ref, vcT_ref, t0_ref, t1_ref, tc_ref, ov_ref, o_ref,
               ks_ref, vsT_ref, kw_ref, vwT_ref, sel_ref, sfar_ref, gsc_ref):
    g = pl.program_id(1)
    ip = pl.program_id(2)

    @pl.when(ip == 0)
    def _():
        ones = jnp.ones((V_ROWS - HEAD_DIM, TQ), BF16)

        def fill(n, carry):
            for z_ref, k_ref, vT_ref in ((zs_ref, ks_ref, vsT_ref), (zw_ref, kw_ref, vwT_ref)):
                blk = z_ref[pl.ds(pl.multiple_of(n * TQ, TQ), TQ), :]
                k_ref[n] = blk[:, :HEAD_DIM].astype(BF16)
                vT_ref[n, :HEAD_DIM, :] = blk.T[HEAD_DIM:, :].astype(BF16)
                vT_ref[n, HEAD_DIM:, :] = ones
            return carry

        lax.fori_loop(0, NK, fill, 0)

    key = lax.broadcasted_iota(jnp.int32, (TQ, QL), 0)
    qi = lax.broadcasted_iota(jnp.int32, (TQ, QL), 1) & (TQ - 1)
    blk = lax.broadcasted_iota(jnp.int32, (N_SBLK, TQ), 0)
    ql = lax.broadcasted_iota(jnp.int32, (N_SBLK, TQ), 1)
    key1 = lax.broadcasted_iota(jnp.int32, (TQ, TQ), 0)
    q1 = lax.broadcasted_iota(jnp.int32, (TQ, TQ), 1)
    causal = key1 <= q1
    nlast = WINDOW // TQ

    def sel_mask(t, n):
        r0 = sel_ref[t, pl.ds(2 * n, 1), :]
        r1 = sel_ref[t, pl.ds(2 * n + 1, 1), :]
        return jnp.where(key1 < SEL_BLOCK, r0, r1) > 0.5

    def scores(qT, k_ref, n, bias, mask):
        sc = jnp.dot(k_ref[n], qT, preferred_element_type=F32)
        if bias is not None:
            sc = sc + bias
        return jnp.concatenate(
            [jnp.where(mask, sc[:, h * TQ:(h + 1) * TQ], NEG) for h in range(HPG)], axis=1)

    def colmax8(sc):
        return jnp.max(sc.reshape(TQ // 8, 8, QL), axis=0)

    def probs(sc, mrow):
        return jnp.exp2(sc - mrow).astype(BF16)

    def pv(vT_ref, ns, ps):
        vv = vT_ref[ns[0]] if len(ns) == 1 else jnp.concatenate([vT_ref[n] for n in ns], axis=1)
        pp = ps[0] if len(ps) == 1 else jnp.concatenate(ps, axis=0)
        return jnp.dot(vv, pp, preferred_element_type=F32)

    def finish(acc):
        return acc[:HEAD_DIM] * (1.0 / acc[HEAD_DIM:HEAD_DIM + 1])

    def front(t):
        i = Q_PAIR * ip + t
        rows = slice(t * TQ, (t + 1) * TQ)
        qt = zq_ref[rows, :].T
        qT = jnp.concatenate([qt[h * HEAD_DIM:(h + 1) * HEAD_DIM] for h in range(HPG)], axis=1).astype(BF16)
        gsc_ref[t] = zg_ref[rows, :].T
        iv = jnp.full((TQ, QL), i, jnp.int32)
        i1 = jnp.full((TQ, TQ), i, jnp.int32)

        s = jnp.dot(kc_ref[0, 0], qT, preferred_element_type=F32) + tc_ref[0, t]
        mask_c = (iv * TQ + qi) >= (key * CMP_STRIDE + (CMP_BLOCK - 1))
        s = jnp.where(mask_c, s, NEG)
        m = jnp.max(s, axis=0, keepdims=True)
        e = jnp.where(mask_c, jnp.exp2(s - m), 0.0)
        l = jnp.sum(e, axis=0, keepdims=True)
        p = e * jnp.where(l > 0.0, 1.0 / l, 0.0)
        ocT = jnp.dot(vcT_ref[0, 0], p.astype(BF16), preferred_element_type=F32)

        psum = p[:, 0:TQ] + p[:, TQ:2 * TQ] + p[:, 2 * TQ:3 * TQ] + p[:, 3 * TQ:4 * TQ]
        imp = jnp.dot(ov_ref[...], psum, preferred_element_type=F32,
                      precision=lax.Precision.HIGHEST)
        cur = jnp.full((N_SBLK, TQ), i, jnp.int32) * (TQ // SEL_BLOCK) + ql // SEL_BLOCK
        forced = (blk == 0) | (blk == cur) | (blk == cur - 1)
        score = jnp.where(blk <= cur, jnp.where(forced, FORCE, imp), NEG)
        rank = jnp.zeros((N_SBLK, TQ), jnp.int32)
        for jj in range(N_SBLK):
            row = score[jj:jj + 1, :]
            beats = (row > score) | ((row == score) & (blk > jj))
            rank = rank + beats.astype(jnp.int32)
        sel_ref[t] = ((rank < N_SEL) & (blk <= cur)).astype(F32)

        n1 = jnp.maximum(i - 1, 0)
        n2, n3, n4 = jnp.maximum(i - 2, 0), jnp.maximum(i - 3, 0), jnp.maximum(i - nlast, 0)
        w_d = scores(qT, kw_ref, i, t0_ref[0], causal)
        w_1 = scores(qT, kw_ref, n1, t1_ref[0], i1 >= 1)
        w_2 = scores(qT, kw_ref, n2, None, i1 >= 2)
        w_3 = scores(qT, kw_ref, n3, None, i1 >= 3)
        w_4 = scores(qT, kw_ref, n4, None, (q1 < key1) & (i1 >= nlast))
        mxw = jnp.maximum(jnp.maximum(jnp.maximum(colmax8(w_d), colmax8(w_1)),
                                      jnp.maximum(colmax8(w_2), colmax8(w_3))), colmax8(w_4))
        m_w = jnp.max(mxw, axis=0, keepdims=True)
        owT = finish(pv(vwT_ref, [i], [probs(w_d, m_w)])
                     + pv(vwT_ref, [n1, n2], [probs(w_1, m_w), probs(w_2, m_w)])
                     + pv(vwT_ref, [n3, n4], [probs(w_3, m_w), probs(w_4, m_w)]))

        s_d = scores(qT, ks_ref, i, t0_ref[0], causal & sel_mask(t, i))
        s_1 = scores(qT, ks_ref, n1, t1_ref[0], sel_mask(t, n1) & (i1 >= 1))
        return dict(i=i, n1=n1, qT=qT, ocT=ocT, owT=owT, s_d=s_d, s_1=s_1,
                    mx=jnp.maximum(colmax8(s_d), colmax8(s_1)),
                    nfar=jnp.maximum(i - 1, 0))

    tiles = [front(t) for t in range(Q_PAIR)]
    assert Q_PAIR == FAR_GROUP == 4
    extra = (2, 3)

    def far_scores(tr, mxs, ts):
        out = list(mxs)
        for t in ts:
            tl = tiles[t]
            for c in range(FAR_GROUP):
                n = FAR_GROUP * tr + c
                sc = scores(tl["qT"], ks_ref, n, None,
                            sel_mask(t, n) & (jnp.full((TQ, TQ), n, jnp.int32) < tl["nfar"]))
                sfar_ref[t, n] = sc
                out[t] = jnp.maximum(out[t], colmax8(sc))
        return tuple(out)

    every = tuple(range(Q_PAIR))
    mxs = lax.fori_loop(0, ip, lambda tr, c: far_scores(tr, c, every), tuple(tl["mx"] for tl in tiles))
    mxs = far_scores(ip, mxs, extra)
    m_s = [jnp.max(mx, axis=0, keepdims=True) for mx in mxs]
    accs = tuple(pv(vsT_ref, [tl["i"], tl["n1"]], [probs(tl["s_d"], m_s[t]), probs(tl["s_1"], m_s[t])])
                 for t, tl in enumerate(tiles))

    def far_pv(tr, accs_, ts):
        ns = [FAR_GROUP * tr + c for c in range(FAR_GROUP)]
        out = list(accs_)
        for t in ts:
            out[t] = out[t] + pv(vsT_ref, ns, [probs(sfar_ref[t, n], m_s[t]) for n in ns])
        return tuple(out)

    accs = lax.fori_loop(0, ip, lambda tr, c: far_pv(tr, c, every), accs)
    accs = far_pv(ip, accs, extra)

    for t, tl in enumerate(tiles):
        base = g * GATE_GW
        gate = [jnp.concatenate([gsc_ref[t, pl.ds(base + br * HPG + h, 1), :] for h in range(HPG)], axis=1)
                for br in range(3)]
        aT = gate[0] * tl["ocT"] + gate[1] * finish(accs[t]) + gate[2] * tl["owT"]
        a4 = jnp.concatenate([aT[:, h * TQ:(h + 1) * TQ] for h in range(HPG)], axis=0)
        o_ref[t * TQ:(t + 1) * TQ, :] = a4.T.astype(o_ref.dtype)


def _attention(z, kc, vcT, t0, t1, tc, ov):
    qw = HPG * HEAD_DIM
    rows = Q_PAIR * TQ
    npair = NQ // Q_PAIR
    return pl.pallas_call(
        _attn_body,
        grid=(BATCH, N_KV, npair),
        in_specs=[
            pl.BlockSpec((rows, qw), lambda b, g, i: (b * npair + i, Z_Q // qw + g)),
            pl.BlockSpec((SEQ, KV_GW), lambda b, g, i: (b, (Z_KV + 2 * KV_W) // KV_GW + g)),
            pl.BlockSpec((SEQ, KV_GW), lambda b, g, i: (b, (Z_KV + 4 * KV_W) // KV_GW + g)),
            pl.BlockSpec((rows, LANE), lambda b, g, i: (b * npair + i, Z_NG // LANE)),
            pl.BlockSpec((1, 1, N_CMP, HEAD_DIM), lambda b, g, i: (b, g, 0, 0)),
            pl.BlockSpec((1, 1, HEAD_DIM, N_CMP), lambda b, g, i: (b, g, 0, 0)),
            pl.BlockSpec((1, TQ, QL), lambda b, g, i: (g, 0, 0)),
            pl.BlockSpec((1, TQ, QL), lambda b, g, i: (g, 0, 0)),
            pl.BlockSpec((1, Q_PAIR, N_CMP, QL), lambda b, g, i: (g, i, 0, 0)),
            pl.BlockSpec((N_SBLK, N_CMP), lambda b, g, i: (0, 0)),
        ],
        out_specs=pl.BlockSpec((rows, qw), lambda b, g, i: (b * npair + i, g)),
        out_shape=jax.ShapeDtypeStruct((TOK, NSA_W), BF16),
        scratch_shapes=[
            pltpu.VMEM((NK, TQ, HEAD_DIM), BF16),
            pltpu.VMEM((NK, V_ROWS, TQ), BF16),
            pltpu.VMEM((NK, TQ, HEAD_DIM), BF16),
            pltpu.VMEM((NK, V_ROWS, TQ), BF16),
            pltpu.VMEM((Q_PAIR, N_SBLK, TQ), F32),
            pltpu.VMEM((Q_PAIR, NK, TQ, QL), F32),
            pltpu.VMEM((Q_PAIR, LANE, TQ), F32),
        ],
        compiler_params=_cparams(("parallel", "parallel", "arbitrary")),
        name="nsa_attention",
    )(z, z, z, z, kc, vcT, t0, t1, tc, ov)


def _sgu_body(u_ref, v_ref, g_ref, b_ref, w_ref, bs_ref, o_ref):
    r = lax.broadcasted_iota(jnp.int32, (GM_CHUNK, GM_CHUNK), 0)
    c = lax.broadcasted_iota(jnp.int32, (GM_CHUNK, GM_CHUNK), 1)
    tril = r >= c
    ws = [jnp.where(tril, w_ref[gi], 0.0).astype(BF16) for gi in range(GM_GROUPS)]
    for ch in range(SGU_TM // GM_CHUNK):
        rows = slice(ch * GM_CHUNK, (ch + 1) * GM_CHUNK)
        v = v_ref[rows, :]
        mu = jnp.mean(v, axis=-1, keepdims=True)
        vc = v - mu
        var = jnp.mean(vc * vc, axis=-1, keepdims=True)
        vn = (vc * lax.rsqrt(var + EPS) * g_ref[...] + b_ref[...]).astype(BF16)
        for gi in range(GM_GROUPS):
            cols = slice(gi * GM_GW, (gi + 1) * GM_GW)
            mixed = jnp.dot(ws[gi], vn[:, cols], preferred_element_type=F32) + bs_ref[:, cols]
            o_ref[rows, cols] = (u_ref[rows, cols] * mixed).astype(o_ref.dtype)


def _sgu(z, ln_g, ln_b, w_s, bs_exp):
    return pl.pallas_call(
        _sgu_body,
        grid=(TOK // SGU_TM,),
        in_specs=[
            pl.BlockSpec((SGU_TM, GM_W), lambda i: (i, Z_U // GM_W)),
            pl.BlockSpec((SGU_TM, GM_W), lambda i: (i, Z_V // GM_W)),
            pl.BlockSpec((1, GM_W), lambda i: (0, 0)),
            pl.BlockSpec((1, GM_W), lambda i: (0, 0)),
            pl.BlockSpec((GM_GROUPS, GM_CHUNK, GM_CHUNK), lambda i: (0, 0, 0)),
            pl.BlockSpec((GM_CHUNK, GM_W), lambda i: (0, 0)),
        ],
        out_specs=pl.BlockSpec((SGU_TM, GM_W), lambda i: (i, 0)),
        out_shape=jax.ShapeDtypeStruct((TOK, GM_W), BF16),
        compiler_params=_cparams(("parallel",)),
        name="sgu",
    )(z, z, ln_g, ln_b, w_s, bs_exp)


def _out_body(a_ref, sg_ref, m1_ref, m2_ref, x_ref, wn_ref, ws_ref, wo_ref, gn_ref, o_ref, h_ref):
    pa = jnp.dot(a_ref[...], wn_ref[...], preferred_element_type=F32)
    ps = jnp.dot(sg_ref[...], ws_ref[...], preferred_element_type=F32)
    merged = (m1_ref[...] * pa + m2_ref[...] * ps).astype(BF16)
    y = x_ref[...] + jnp.dot(merged, wo_ref[...], preferred_element_type=F32)
    o_ref[...] = y
    h_ref[...] = _rms_bf16(y, gn_ref[...])


def _merge_out(a, sg, z, x, wn, ws, wo, l, next_gain):
    const = lambda i: (l, 0, 0)
    return pl.pallas_call(
        _out_body,
        grid=(TOK // OUT_TM,),
        in_specs=[
            pl.BlockSpec((OUT_TM, NSA_W), lambda i: (i, 0)),
            pl.BlockSpec((OUT_TM, GM_W), lambda i: (i, 0)),
            pl.BlockSpec((OUT_TM, D_MODEL), lambda i: (i, 0)),
            pl.BlockSpec((OUT_TM, D_MODEL), lambda i: (i, 1)),
            pl.BlockSpec((OUT_TM, D_MODEL), lambda i: (i, 0)),
            pl.BlockSpec((None, NSA_W, D_MODEL), const),
            pl.BlockSpec((None, GM_W, D_MODEL), const),
            pl.BlockSpec((None, D_MODEL, D_MODEL), const),
            pl.BlockSpec((1, D_MODEL), lambda i: (0, 0)),
        ],
        out_specs=[pl.BlockSpec((OUT_TM, D_MODEL), lambda i: (i, 0)),
                   pl.BlockSpec((OUT_TM, D_MODEL), lambda i: (i, 0))],
        out_shape=[jax.ShapeDtypeStruct((TOK, D_MODEL), F32),
                   jax.ShapeDtypeStruct((TOK, D_MODEL), BF16)],
        compiler_params=_cparams(("parallel",)),
        name="merge_out",
    )(a, sg, z, z, x, wn, ws, wo, next_gain)


def _overlap_T():
    ci = np.arange(N_CMP)[None, :] * CMP_STRIDE
    sj = np.arange(N_SBLK)[:, None] * SEL_BLOCK
    ov = np.clip(np.minimum(ci + CMP_BLOCK, sj + SEL_BLOCK) - np.maximum(ci, sj), 0, None)
    ov = ov.astype(np.float32) / CMP_BLOCK
    ov[:, N_CMP - 1] = 0.0
    return ov


def _block_diag_ones():
    r = np.arange(LANE)
    return (r[:, None] // HEAD_DIM == r[None, :] // HEAD_DIM).astype(np.float32)


def _prep_w_in(w):
    L = w.shape[0]
    kv = w[:, :, OFF_KV:OFF_NG].reshape(L, D_MODEL, 3, 2, N_KV, HEAD_DIM)
    kv = kv.transpose(0, 1, 2, 4, 3, 5).reshape(L, D_MODEL, 6 * KV_W)
    ng = w[:, :, OFF_NG:OFF_UV].reshape(L, D_MODEL, N_KV, HPG, 3).transpose(0, 1, 2, 4, 3)
    ng = ng.reshape(L, D_MODEL, N_KV, 3 * HPG)
    ng = jnp.pad(ng, ((0, 0), (0, 0), (0, 0), (0, GATE_GW - 3 * HPG))).reshape(L, D_MODEL, N_KV * GATE_GW)
    ng = jnp.pad(ng, ((0, 0), (0, 0), (0, PIN_TN - N_KV * GATE_GW)))
    return jnp.concatenate(
        [w[:, :, OFF_MG:], w[:, :, OFF_UV:OFF_MG], w[:, :, :OFF_KV], kv, ng], axis=2).astype(BF16)


def _mixer(x, z, l, p, tables):
    t0, t1, tc, ov = tables
    pos = jnp.stack([p["cmp_pos_k"][l], p["cmp_pos_v"][l]]).reshape(2, 1, CMP_BLOCK * HEAD_DIM)
    w1 = jnp.stack([p["cmp_k_w1"][l], p["cmp_v_w1"][l]]).astype(BF16)
    kc, vcT = _compress(z, pos, w1, p["cmp_k_w2"][l].astype(BF16), p["cmp_v_w2"][l].T.astype(BF16),
                        p["k_norm"][l, 0].reshape(1, HEAD_DIM))
    a = _attention(z, kc, vcT, t0, t1, tc, ov)
    bs_exp = jnp.repeat(p["sgu_b"][l].T, GM_GW, axis=1)
    sg = _sgu(z, p["sgu_norm_g"][l].reshape(1, GM_W), p["sgu_norm_b"][l].reshape(1, GM_W),
              p["sgu_w"][l], bs_exp)
    return _merge_out(a, sg, z, x, p["wn"], p["ws"], p["wo"], l, p["ffn2_norm"][l].reshape(1, D_MODEL))


def kernel(x, rel_bias, ffn1_norm, ffn1_w_gate, ffn1_w_up, ffn1_w_down, mix_norm, w_in, q_norm, k_norm, cmp_pos_k, cmp_pos_v, cmp_k_w1, cmp_k_w2, cmp_v_w1, cmp_v_w2, sgu_norm_g, sgu_norm_b, sgu_w, sgu_b, w_proj_nsa, w_proj_sgu, w_out, ffn2_norm, ffn2_w_gate, ffn2_w_up, ffn2_w_down):
    p = dict(cmp_pos_k=cmp_pos_k, cmp_pos_v=cmp_pos_v, cmp_k_w1=cmp_k_w1, cmp_k_w2=cmp_k_w2,
             cmp_v_w1=cmp_v_w1, cmp_v_w2=cmp_v_w2, k_norm=k_norm, sgu_norm_g=sgu_norm_g,
             sgu_norm_b=sgu_norm_b, sgu_w=sgu_w, sgu_b=sgu_b, wn=w_proj_nsa.astype(BF16),
             ws=w_proj_sgu.astype(BF16), wo=w_out.astype(BF16), ffn2_norm=ffn2_norm)
    near, cmpb = _bucket_tables()
    t0, t1, tc = _bias_tables(rel_bias, jnp.asarray(near), jnp.asarray(cmpb))
    tables = (t0, t1, tc, jnp.asarray(_overlap_T()))
    bd = jnp.asarray(_block_diag_ones()).astype(BF16)
    w_in_r = _prep_w_in(w_in)
    f1 = (ffn1_w_gate, ffn1_w_up, ffn1_w_down)
    f2 = (ffn2_w_gate, ffn2_w_up, ffn2_w_down)

    y = x.reshape(TOK, D_MODEL)
    for l in range(DEPTH):
        y = _ffn(y, _rmsnorm(y, ffn1_norm[l].reshape(1, D_MODEL)), *f1, l)
        z = _proj_in(y, mix_norm[l].reshape(1, D_MODEL), w_in_r, l, bd,
                     jnp.tile(q_norm[l], PIN_TN // HEAD_DIM).reshape(1, PIN_TN),
                     jnp.tile(k_norm[l, 1], PIN_TN // HEAD_DIM).reshape(1, PIN_TN),
                     jnp.tile(k_norm[l, 2], PIN_TN // HEAD_DIM).reshape(1, PIN_TN))
        y, h = _mixer(y, z, l, p, tables)
        y = _ffn(y, h, *f2, l)
    return y.reshape(BATCH, SEQ, D_MODEL)
```

```python
import math

import numpy as np
import jax
import jax.numpy as jnp
from jax import lax
from jax.experimental import pallas as pl
from jax.experimental.pallas import tpu as pltpu

F32 = jnp.float32
BF16 = jnp.bfloat16

D_MODEL = 2048
BATCH = 4
SEQ = 2048
DEPTH = 2
TOK = BATCH * SEQ
HEAD_DIM = 64
N_HEADS = 16
N_KV = 4
HPG = 4
NSA_W = 1024
KV_W = 256
CMP_BLOCK = 32
CMP_STRIDE = 16
CMP_HIDDEN = 256
SEL_BLOCK = 64
N_SEL = 16
WINDOW = 512
GM_W = 1024
GM_CHUNK = 128
GM_GW = 128
GM_GROUPS = 8
N_BUCKETS = 32
MAX_DISTANCE = 128
D_FF = 5504
EPS = 1e-6
NEG = -1e30
FORCE = 1e4
OFF_KV = NSA_W
OFF_NG = OFF_KV + 6 * KV_W
OFF_UV = OFF_NG + 3 * N_HEADS
OFF_MG = OFF_UV + 2 * GM_W
LOG2E = 1.4426950408889634

LANE = 128
VMEM_LIMIT = 56 * 1024 * 1024

FFN_TF = 512
FFN_UP_TM = 1024
NORM_TM = 512
FFN_DN_TM = 512
FFN_DN_TN = 512
PIN_TM = 1024
PIN_TN = 512
PIN_RC = 256
Z_MG = 0
Z_U = Z_MG + 2 * D_MODEL
Z_V = Z_U + GM_W
Z_Q = Z_V + GM_W
Z_KV = Z_Q + NSA_W
Z_NG = Z_KV + 6 * KV_W
Z_W = Z_NG + PIN_TN
KV_GW = 2 * HEAD_DIM
GATE_GW = 16
TQ = 128
NQ = SEQ // TQ
NK = SEQ // TQ
QL = HPG * TQ
N_CMP = 128
CMP_BAND = 16
N_SBLK = SEQ // SEL_BLOCK
V_ROWS = HEAD_DIM + 16
FAR_GROUP = 4
Q_PAIR = 4
SGU_TM = 512
OUT_TM = 256


def _cparams(sem):
    return pltpu.CompilerParams(dimension_semantics=sem, vmem_limit_bytes=VMEM_LIMIT)


def _cast_rows(src_ref, dst_ref, chunk):
    def body(r, carry):
        rows = pl.ds(pl.multiple_of(r * chunk, chunk), chunk)
        dst_ref[rows, :] = src_ref[rows, :].astype(BF16)
        return carry

    lax.fori_loop(0, src_ref.shape[0] // chunk, body, 0)


def _rms_bf16(x, gain):
    ms = jnp.mean(x * x, axis=-1, keepdims=True)
    return (x * lax.rsqrt(ms + EPS) * gain).astype(BF16)


def _norm_body(x_ref, g_ref, h_ref):
    h_ref[...] = _rms_bf16(x_ref[...], g_ref[...])


def _rmsnorm(x, gain):
    return pl.pallas_call(
        _norm_body,
        grid=(TOK // NORM_TM,),
        in_specs=[pl.BlockSpec((NORM_TM, D_MODEL), lambda i: (i, 0)),
                  pl.BlockSpec((1, D_MODEL), lambda i: (0, 0))],
        out_specs=pl.BlockSpec((NORM_TM, D_MODEL), lambda i: (i, 0)),
        out_shape=jax.ShapeDtypeStruct((TOK, D_MODEL), BF16),
        compiler_params=_cparams(("parallel",)),
        name="rmsnorm",
    )(x, gain)


def _ffn_up_body(h_ref, wg_ref, wu_ref, a_ref, wgb_ref, wub_ref):
    @pl.when(pl.program_id(1) == 0)
    def _():
        _cast_rows(wg_ref, wgb_ref, LANE)
        _cast_rows(wu_ref, wub_ref, LANE)

    h = h_ref[...]
    g = jnp.dot(h, wgb_ref[...], preferred_element_type=F32)
    u = jnp.dot(h, wub_ref[...], preferred_element_type=F32)
    a_ref[...] = (g * jax.nn.sigmoid(g) * u).astype(BF16)


def _ffn_down_body(a_ref, wd_ref, x_ref, o_ref, wdb_ref):
    @pl.when(pl.program_id(1) == 0)
    def _():
        _cast_rows(wd_ref, wdb_ref, LANE)

    o_ref[...] = x_ref[...] + 0.5 * jnp.dot(a_ref[...], wdb_ref[...], preferred_element_type=F32)


def _ffn(x, h, wg, wu, wd, l):
    nf = pl.cdiv(D_FF, FFN_TF)
    a = pl.pallas_call(
        _ffn_up_body,
        grid=(nf, TOK // FFN_UP_TM),
        in_specs=[
            pl.BlockSpec((FFN_UP_TM, D_MODEL), lambda f, m: (m, 0)),
            pl.BlockSpec((None, D_MODEL, FFN_TF), lambda f, m: (l, 0, f)),
            pl.BlockSpec((None, D_MODEL, FFN_TF), lambda f, m: (l, 0, f)),
        ],
        out_specs=pl.BlockSpec((FFN_UP_TM, FFN_TF), lambda f, m: (m, f)),
        out_shape=jax.ShapeDtypeStruct((TOK, D_FF), BF16),
        scratch_shapes=[pltpu.VMEM((D_MODEL, FFN_TF), BF16), pltpu.VMEM((D_MODEL, FFN_TF), BF16)],
        compiler_params=_cparams(("arbitrary", "arbitrary")),
        name="ffn_up",
    )(h, wg, wu)
    return pl.pallas_call(
        _ffn_down_body,
        grid=(D_MODEL // FFN_DN_TN, TOK // FFN_DN_TM),
        in_specs=[
            pl.BlockSpec((FFN_DN_TM, D_FF), lambda n, m: (m, 0)),
            pl.BlockSpec((None, D_FF, FFN_DN_TN), lambda n, m: (l, 0, n)),
            pl.BlockSpec((FFN_DN_TM, FFN_DN_TN), lambda n, m: (m, n)),
        ],
        out_specs=pl.BlockSpec((FFN_DN_TM, FFN_DN_TN), lambda n, m: (m, n)),
        out_shape=jax.ShapeDtypeStruct((TOK, D_MODEL), F32),
        scratch_shapes=[pltpu.VMEM((D_FF, FFN_DN_TN), BF16)],
        compiler_params=_cparams(("arbitrary", "arbitrary")),
        name="ffn_down",
    )(a, wd, x)


def _head_rms(y, bd):
    y2 = (y * y).astype(BF16)
    ss = jnp.concatenate(
        [jnp.dot(y2[:, k * LANE:(k + 1) * LANE], bd, preferred_element_type=F32)
         for k in range(y.shape[1] // LANE)], axis=1)
    return y * lax.rsqrt(ss * (1.0 / HEAD_DIM) + EPS)


def _pin_body(x_ref, g_ref, w_ref, bd_ref, qg_ref, k1g_ref, k2g_ref, o_ref, h_ref):
    j = pl.program_id(1)

    @pl.when(j == 0)
    def _():
        x = x_ref[...]
        ms = jnp.mean(x * x, axis=-1, keepdims=True)
        h_ref[...] = (x * lax.rsqrt(ms + EPS) * g_ref[...]).astype(BF16)

    t_u = Z_U // PIN_TN
    t_q = Z_Q // PIN_TN
    t_kv = Z_KV // PIN_TN
    t_ng = Z_NG // PIN_TN

    def emit(act):
        for r in range(PIN_TM // PIN_RC):
            rows = slice(r * PIN_RC, (r + 1) * PIN_RC)
            o_ref[rows, :] = act(jnp.dot(h_ref[rows, :], w_ref[...], preferred_element_type=F32))

    @pl.when((j < t_u) | (j == t_ng))
    def _():
        emit(jax.nn.sigmoid)

    @pl.when((j >= t_u) & (j < t_q))
    def _():
        emit(jax.nn.gelu)

    @pl.when((j >= t_q) & (j < t_kv))
    def _():
        emit(lambda y: _head_rms(y, bd_ref[...]) * (qg_ref[...] * (HEAD_DIM ** -0.5 * LOG2E)))

    @pl.when(j == t_kv)
    def _():
        emit(lambda y: y)

    def kv_act(kg_ref):
        is_k = (lax.broadcasted_iota(jnp.int32, (1, PIN_TN), 1) & (KV_GW - 1)) < HEAD_DIM
        return lambda y: jnp.where(is_k, _head_rms(y, bd_ref[...]) * kg_ref[...], y)

    @pl.when(j == t_kv + 1)
    def _():
        emit(kv_act(k1g_ref))

    @pl.when(j == t_kv + 2)
    def _():
        emit(kv_act(k2g_ref))


def _proj_in(x, gain, w, l, bd, qg, k1g, k2g):
    return pl.pallas_call(
        _pin_body,
        grid=(TOK // PIN_TM, Z_W // PIN_TN),
        in_specs=[
            pl.BlockSpec((PIN_TM, D_MODEL), lambda i, j: (i, 0)),
            pl.BlockSpec((1, D_MODEL), lambda i, j: (0, 0)),
            pl.BlockSpec((None, D_MODEL, PIN_TN), lambda i, j: (l, 0, j)),
            pl.BlockSpec((LANE, LANE), lambda i, j: (0, 0)),
            pl.BlockSpec((1, PIN_TN), lambda i, j: (0, 0)),
            pl.BlockSpec((1, PIN_TN), lambda i, j: (0, 0)),
            pl.BlockSpec((1, PIN_TN), lambda i, j: (0, 0)),
        ],
        out_specs=pl.BlockSpec((PIN_TM, PIN_TN), lambda i, j: (i, j)),
        out_shape=jax.ShapeDtypeStruct((TOK, Z_W), F32),
        scratch_shapes=[pltpu.VMEM((PIN_TM, D_MODEL), BF16)],
        compiler_params=_cparams(("parallel", "arbitrary")),
        name="proj_in",
    )(x, gain, w, bd, qg, k1g, k2g)


def _cmp_body(z_ref, pos_ref, w1_ref, w2k_ref, w2vT_ref, kg_ref, kc_ref, vcT_ref):
    half = CMP_STRIDE * HEAD_DIM
    lane = lax.broadcasted_iota(jnp.int32, (N_CMP, LANE), 1)
    first = lane < HEAD_DIM
    xk, xv = [], []
    for j in range(CMP_STRIDE // 2):
        ev = z_ref[pl.ds(2 * j, N_CMP, stride=CMP_STRIDE), :]
        od = z_ref[pl.ds(2 * j + 1, N_CMP, stride=CMP_STRIDE), :]
        xk.append(jnp.where(first, ev, pltpu.roll(od, HEAD_DIM, 1)))
        xv.append(jnp.where(first, pltpu.roll(ev, HEAD_DIM, 1), od))

    def hidden(x, kind):
        xa = (x + pos_ref[kind, :, :half]).astype(BF16)
        xb = (x + pos_ref[kind, :, half:]).astype(BF16)
        a = jnp.dot(xa, w1_ref[kind, :half, :], preferred_element_type=F32)
        b = jnp.dot(xb, w1_ref[kind, half:, :], preferred_element_type=F32)
        hid = a + pltpu.roll(b, N_CMP - 1, 0)
        return (hid * jax.nn.sigmoid(hid)).astype(BF16)

    yk = jnp.dot(hidden(jnp.concatenate(xk, axis=1), 0), w2k_ref[...], preferred_element_type=F32)
    ms = jnp.mean(yk * yk, axis=-1, keepdims=True)
    kc_ref[0, 0] = (yk * lax.rsqrt(ms + EPS) * kg_ref[...]).astype(BF16)
    hv = hidden(jnp.concatenate(xv, axis=1), 1)
    vcT_ref[0, 0] = lax.dot_general(w2vT_ref[...], hv, (((1,), (1,)), ((), ())),
                                    preferred_element_type=F32).astype(BF16)


def _compress(z, pos, w1, w2k, w2vT, kg):
    return pl.pallas_call(
        _cmp_body,
        grid=(BATCH, N_KV),
        in_specs=[
            pl.BlockSpec((SEQ, KV_GW), lambda b, g: (b, Z_KV // KV_GW + g)),
            pl.BlockSpec((2, 1, CMP_BLOCK * HEAD_DIM), lambda b, g: (0, 0, 0)),
            pl.BlockSpec((2, CMP_BLOCK * HEAD_DIM, CMP_HIDDEN), lambda b, g: (0, 0, 0)),
            pl.BlockSpec((CMP_HIDDEN, HEAD_DIM), lambda b, g: (0, 0)),
            pl.BlockSpec((HEAD_DIM, CMP_HIDDEN), lambda b, g: (0, 0)),
            pl.BlockSpec((1, HEAD_DIM), lambda b, g: (0, 0)),
        ],
        out_specs=[
            pl.BlockSpec((1, 1, N_CMP, HEAD_DIM), lambda b, g: (b, g, 0, 0)),
            pl.BlockSpec((1, 1, HEAD_DIM, N_CMP), lambda b, g: (b, g, 0, 0)),
        ],
        out_shape=[
            jax.ShapeDtypeStruct((BATCH, N_KV, N_CMP, HEAD_DIM), BF16),
            jax.ShapeDtypeStruct((BATCH, N_KV, HEAD_DIM, N_CMP), BF16),
        ],
        compiler_params=_cparams(("parallel", "parallel")),
        name="compress",
    )(z, pos, w1, w2k, w2vT, kg)


def _np_bucket(dist):
    n = np.maximum(dist, 0)
    max_exact = N_BUCKETS // 2
    nf = np.maximum(n, 1).astype(np.float32)
    large = max_exact + (np.log(nf / np.float32(max_exact)) / np.float32(math.log(MAX_DISTANCE / max_exact))
                         * np.float32(N_BUCKETS - max_exact)).astype(np.int32)
    large = np.minimum(large, N_BUCKETS - 1)
    return np.where(n < max_exact, n, large).astype(np.int32)


def _bucket_tables():
    kj = np.arange(TQ)[:, None]
    qi = np.arange(TQ)[None, :]
    near = np.stack([_np_bucket(qi - kj), _np_bucket(TQ + qi - kj)])
    start = np.array([_cmp_band_start(i) for i in range(NQ)])[:, None, None]
    c = start + np.arange(CMP_BAND)[None, :, None]
    t = (np.arange(NQ)[:, None, None] * TQ + qi[None])
    cmp_b = _np_bucket(t - (c * CMP_STRIDE + CMP_BLOCK - 1))
    full = _np_bucket(t - (np.arange(N_CMP)[None, :, None] * CMP_STRIDE + CMP_BLOCK - 1))
    for i in range(NQ):
        s0 = _cmp_band_start(i)
        rest = np.ones(N_CMP, bool)
        rest[s0:s0 + CMP_BAND] = False
        dist = t[i] - (np.arange(N_CMP)[:, None] * CMP_STRIDE + CMP_BLOCK - 1)
        assert np.all((dist[rest] < 0) | (full[i][rest] == N_BUCKETS - 1))
    return near, cmp_b.astype(np.int32)


def _cmp_band_start(i):
    return max((TQ // CMP_STRIDE) * i - 8, 0)


def _bias_body(rb_ref, near_ref, cmpb_ref, t0_ref, t1_ref, tc_ref):
    g = pl.program_id(0)

    def lut(bk, h):
        val = jnp.zeros(bk.shape, F32)
        for b in range(N_BUCKETS):
            val = jnp.where(bk == b, rb_ref[b, h], val)
        return val

    for hh in range(HPG):
        h = g * HPG + hh
        sl = slice(hh * TQ, (hh + 1) * TQ)
        far = rb_ref[N_BUCKETS - 1, h]
        t0_ref[0, :, sl] = (lut(near_ref[0], h) - far) * LOG2E
        t1_ref[0, :, sl] = (lut(near_ref[1], h) - far) * LOG2E

        def body(i, carry):
            start = pl.multiple_of(jnp.maximum((TQ // CMP_STRIDE) * i - 8, 0), 8)
            tc_ref[0, i, :, sl] = jnp.full((N_CMP, TQ), far * LOG2E, F32)
            tc_ref[0, i, pl.ds(start, CMP_BAND), sl] = lut(cmpb_ref[i], h) * LOG2E
            return carry

        lax.fori_loop(0, NQ, body, 0)


def _bias_tables(rel_bias, near, cmpb):
    return pl.pallas_call(
        _bias_body,
        grid=(N_KV,),
        in_specs=[
            pl.BlockSpec(memory_space=pltpu.SMEM),
            pl.BlockSpec((2, TQ, TQ), lambda g: (0, 0, 0)),
            pl.BlockSpec((NQ, CMP_BAND, TQ), lambda g: (0, 0, 0)),
        ],
        out_specs=[
            pl.BlockSpec((1, TQ, QL), lambda g: (g, 0, 0)),
            pl.BlockSpec((1, TQ, QL), lambda g: (g, 0, 0)),
            pl.BlockSpec((1, NQ, N_CMP, QL), lambda g: (g, 0, 0, 0)),
        ],
        out_shape=[
            jax.ShapeDtypeStruct((N_KV, TQ, QL), F32),
            jax.ShapeDtypeStruct((N_KV, TQ, QL), F32),
            jax.ShapeDtypeStruct((N_KV, NQ, N_CMP, QL), F32),
        ],
        compiler_params=_cparams(("arbitrary",)),
        name="bias_tables",
    )(rel_bias, near, cmpb)


def _attn_body(zq_ref, zs_ref, zw_ref, zg_ref, kc_ref, vcT_ref, t0_ref, t1_ref, tc_ref, o_ref,
               ks_ref, vsT_ref, kw_ref, vwT_ref, sel_ref, sfar_ref, gsc_ref, psum_ref):
    g = pl.program_id(1)
    ip = pl.program_id(2)

    @pl.when(ip == 0)
    def _():
        ones = jnp.ones((V_ROWS - HEAD_DIM, TQ), BF16)

        def fill(n, carry):
            for z_ref, k_ref, vT_ref in ((zs_ref, ks_ref, vsT_ref), (zw_ref, kw_ref, vwT_ref)):
                blk = z_ref[pl.ds(pl.multiple_of(n * TQ, TQ), TQ), :]
                k_ref[n] = blk[:, :HEAD_DIM].astype(BF16)
                vT_ref[n, :HEAD_DIM, :] = blk.T[HEAD_DIM:, :].astype(BF16)
                vT_ref[n, HEAD_DIM:, :] = ones
            return carry

        lax.fori_loop(0, NK, fill, 0)

    key = lax.broadcasted_iota(jnp.int32, (TQ, QL), 0)
    qi = lax.broadcasted_iota(jnp.int32, (TQ, QL), 1) & (TQ - 1)
    blk = lax.broadcasted_iota(jnp.int32, (N_SBLK, TQ), 0)
    ql = lax.broadcasted_iota(jnp.int32, (N_SBLK, TQ), 1)
    key1 = lax.broadcasted_iota(jnp.int32, (TQ, TQ), 0)
    q1 = lax.broadcasted_iota(jnp.int32, (TQ, TQ), 1)
    causal = key1 <= q1
    nlast = WINDOW // TQ

    def sel_mask(t, n):
        r0 = sel_ref[t, pl.ds(2 * n, 1), :]
        r1 = sel_ref[t, pl.ds(2 * n + 1, 1), :]
        return jnp.where(key1 < SEL_BLOCK, r0, r1) > 0.5

    def scores(qT, k_ref, n, bias, mask):
        sc = jnp.dot(k_ref[n], qT, preferred_element_type=F32)
        if bias is not None:
            sc = sc + bias
        return jnp.concatenate(
            [jnp.where(mask, sc[:, h * TQ:(h + 1) * TQ], NEG) for h in range(HPG)], axis=1)

    def colmax8(sc):
        return jnp.max(sc.reshape(TQ // 8, 8, QL), axis=0)

    def probs(sc, mrow):
        return jnp.exp2(sc - mrow).astype(BF16)

    def pv(vT_ref, ns, ps):
        vv = vT_ref[ns[0]] if len(ns) == 1 else jnp.concatenate([vT_ref[n] for n in ns], axis=1)
        pp = ps[0] if len(ps) == 1 else jnp.concatenate(ps, axis=0)
        return jnp.dot(vv, pp, preferred_element_type=F32)

    def finish(acc):
        return acc[:HEAD_DIM] * (1.0 / acc[HEAD_DIM:HEAD_DIM + 1])

    def stage_q(t):
        i = Q_PAIR * ip + t
        rows = slice(t * TQ, (t + 1) * TQ)
        qt = zq_ref[rows, :].T
        qT = jnp.concatenate([qt[h * HEAD_DIM:(h + 1) * HEAD_DIM] for h in range(HPG)], axis=1).astype(BF16)
        gsc_ref[t] = zg_ref[rows, :].T
        return dict(i=i, qT=qT, n1=jnp.maximum(i - 1, 0), i1=jnp.full((TQ, TQ), i, jnp.int32),
                    nfar=jnp.maximum(i - 1, 0))

    def stage_cmp_scores(t, tl):
        iv = jnp.full((TQ, QL), tl["i"], jnp.int32)
        s = jnp.dot(kc_ref[0, 0], tl["qT"], preferred_element_type=F32) + tc_ref[0, t]
        mask_c = (iv * TQ + qi) >= (key * CMP_STRIDE + (CMP_BLOCK - 1))
        tl["s_c"] = jnp.where(mask_c, s, NEG)
        tl["m_c"] = jnp.maximum(jnp.max(colmax8(tl["s_c"]), axis=0, keepdims=True), 0.1 * NEG)

    def stage_cmp(t, tl):
        e = jnp.exp2(tl["s_c"] - tl["m_c"])
        l = jnp.sum(e, axis=0, keepdims=True)
        p = e * jnp.where(l > 0.0, 1.0 / l, 0.0)
        tl["ocT"] = jnp.dot(vcT_ref[0, 0], p.astype(BF16), preferred_element_type=F32)
        psum_ref[t] = p[:, 0:TQ] + p[:, TQ:2 * TQ] + p[:, 2 * TQ:3 * TQ] + p[:, 3 * TQ:4 * TQ]

    def stage_select(t, tl):
        i = tl["i"]
        ratio = SEL_BLOCK // CMP_STRIDE
        part = [psum_ref[t, pl.ds(r, N_SBLK, stride=ratio), :] for r in range(ratio)]
        prev_last = jnp.where(blk == 0, 0.0, pltpu.roll(part[3], 1, 0))
        imp = (part[0] + part[1] + part[2]) + 0.5 * (part[3] + prev_last)
        cur = jnp.full((N_SBLK, TQ), i, jnp.int32) * (TQ // SEL_BLOCK) + ql // SEL_BLOCK
        forced = (blk == 0) | (blk == cur) | (blk == cur - 1)
        score = jnp.where(blk <= cur, jnp.where(forced, FORCE, imp), NEG)
        rank = jnp.zeros((N_SBLK, TQ), jnp.int32)
        for jj in range(N_SBLK):
            row = score[jj:jj + 1, :]
            beats = (row > score) | ((row == score) & (blk > jj))
            rank = rank + beats.astype(jnp.int32)
        sel_ref[t] = ((rank < N_SEL) & (blk <= cur)).astype(F32)

    def stage_window_scores(t, tl):
        i, n1, i1, qT = tl["i"], tl["n1"], tl["i1"], tl["qT"]
        n2, n3, n4 = jnp.maximum(i - 2, 0), jnp.maximum(i - 3, 0), jnp.maximum(i - nlast, 0)
        w_d = scores(qT, kw_ref, i, t0_ref[0], causal)
        w_1 = scores(qT, kw_ref, n1, t1_ref[0], i1 >= 1)
        w_2 = scores(qT, kw_ref, n2, None, i1 >= 2)
        w_3 = scores(qT, kw_ref, n3, None, i1 >= 3)
        w_4 = scores(qT, kw_ref, n4, None, (q1 < key1) & (i1 >= nlast))
        mxw = jnp.maximum(jnp.maximum(jnp.maximum(colmax8(w_d), colmax8(w_1)),
                                      jnp.maximum(colmax8(w_2), colmax8(w_3))), colmax8(w_4))
        tl["w"] = (w_d, w_1, w_2, w_3, w_4)
        tl["wn"] = (i, n1, n2, n3, n4)
        tl["m_w"] = jnp.max(mxw, axis=0, keepdims=True)

    def stage_window_pv(t, tl):
        (w_d, w_1, w_2, w_3, w_4), (i, n1, n2, n3, n4), m_w = tl["w"], tl["wn"], tl["m_w"]
        tl["owT"] = finish(pv(vwT_ref, [i], [probs(w_d, m_w)])
                           + pv(vwT_ref, [n1, n2], [probs(w_1, m_w), probs(w_2, m_w)])
                           + pv(vwT_ref, [n3, n4], [probs(w_3, m_w), probs(w_4, m_w)]))

    def stage_near(t, tl):
        i, n1, i1, qT = tl["i"], tl["n1"], tl["i1"], tl["qT"]
        tl["s_d"] = scores(qT, ks_ref, i, t0_ref[0], causal & sel_mask(t, i))
        tl["s_1"] = scores(qT, ks_ref, n1, t1_ref[0], sel_mask(t, n1) & (i1 >= 1))
        tl["mx"] = jnp.maximum(colmax8(tl["s_d"]), colmax8(tl["s_1"]))

    tiles = [stage_q(t) for t in range(Q_PAIR)]
    for stage in (stage_cmp_scores, stage_cmp, stage_window_scores, stage_select, stage_near, stage_window_pv):
        for t, tl in enumerate(tiles):
            stage(t, tl)
    assert Q_PAIR == FAR_GROUP == 4
    extra = (2, 3)

    def far_scores(tr, mxs, ts):
        out = list(mxs)
        for t in ts:
            tl = tiles[t]
            for c in range(FAR_GROUP):
                n = FAR_GROUP * tr + c
                sc = scores(tl["qT"], ks_ref, n, None,
                            sel_mask(t, n) & (jnp.full((TQ, TQ), n, jnp.int32) < tl["nfar"]))
                sfar_ref[t, n] = sc
                out[t] = jnp.maximum(out[t], colmax8(sc))
        return tuple(out)

    every = tuple(range(Q_PAIR))
    mxs = lax.fori_loop(0, ip, lambda tr, c: far_scores(tr, c, every), tuple(tl["mx"] for tl in tiles))
    mxs = far_scores(ip, mxs, extra)
    m_s = [jnp.max(mx, axis=0, keepdims=True) for mx in mxs]
    accs = tuple(pv(vsT_ref, [tl["i"], tl["n1"]], [probs(tl["s_d"], m_s[t]), probs(tl["s_1"], m_s[t])])
                 for t, tl in enumerate(tiles))

    def far_pv(tr, accs_, ts):
        ns = [FAR_GROUP * tr + c for c in range(FAR_GROUP)]
        out = list(accs_)
        for t in ts:
            out[t] = out[t] + pv(vsT_ref, ns, [probs(sfar_ref[t, n], m_s[t]) for n in ns])
        return tuple(out)

    accs = lax.fori_loop(0, ip, lambda tr, c: far_pv(tr, c, every), accs)
    accs = far_pv(ip, accs, extra)

    for t, tl in enumerate(tiles):
        base = g * GATE_GW
        gate = [jnp.concatenate([gsc_ref[t, pl.ds(base + br * HPG + h, 1), :] for h in range(HPG)], axis=1)
                for br in range(3)]
        aT = gate[0] * tl["ocT"] + gate[1] * finish(accs[t]) + gate[2] * tl["owT"]
        a4 = jnp.concatenate([aT[:, h * TQ:(h + 1) * TQ] for h in range(HPG)], axis=0)
        o_ref[t * TQ:(t + 1) * TQ, :] = a4.T.astype(o_ref.dtype)


def _attention(z, kc, vcT, t0, t1, tc):
    qw = HPG * HEAD_DIM
    rows = Q_PAIR * TQ
    npair = NQ // Q_PAIR
    return pl.pallas_call(
        _attn_body,
        grid=(BATCH, N_KV, npair),
        in_specs=[
            pl.BlockSpec((rows, qw), lambda b, g, i: (b * npair + i, Z_Q // qw + g)),
            pl.BlockSpec((SEQ, KV_GW), lambda b, g, i: (b, (Z_KV + 2 * KV_W) // KV_GW + g)),
            pl.BlockSpec((SEQ, KV_GW), lambda b, g, i: (b, (Z_KV + 4 * KV_W) // KV_GW + g)),
            pl.BlockSpec((rows, LANE), lambda b, g, i: (b * npair + i, Z_NG // LANE)),
            pl.BlockSpec((1, 1, N_CMP, HEAD_DIM), lambda b, g, i: (b, g, 0, 0)),
            pl.BlockSpec((1, 1, HEAD_DIM, N_CMP), lambda b, g, i: (b, g, 0, 0)),
            pl.BlockSpec((1, TQ, QL), lambda b, g, i: (g, 0, 0)),
            pl.BlockSpec((1, TQ, QL), lambda b, g, i: (g, 0, 0)),
            pl.BlockSpec((1, Q_PAIR, N_CMP, QL), lambda b, g, i: (g, i, 0, 0)),
        ],
        out_specs=pl.BlockSpec((rows, qw), lambda b, g, i: (b * npair + i, g)),
        out_shape=jax.ShapeDtypeStruct((TOK, NSA_W), BF16),
        scratch_shapes=[
            pltpu.VMEM((NK, TQ, HEAD_DIM), BF16),
            pltpu.VMEM((NK, V_ROWS, TQ), BF16),
            pltpu.VMEM((NK, TQ, HEAD_DIM), BF16),
            pltpu.VMEM((NK, V_ROWS, TQ), BF16),
            pltpu.VMEM((Q_PAIR, N_SBLK, TQ), F32),
            pltpu.VMEM((Q_PAIR, NK, TQ, QL), F32),
            pltpu.VMEM((Q_PAIR, LANE, TQ), F32),
            pltpu.VMEM((Q_PAIR, N_CMP, TQ), F32),
        ],
        compiler_params=_cparams(("parallel", "parallel", "arbitrary")),
        name="nsa_attention",
    )(z, z, z, z, kc, vcT, t0, t1, tc)


def _sgu_body(u_ref, v_ref, g_ref, b_ref, w_ref, bs_ref, o_ref):
    r = lax.broadcasted_iota(jnp.int32, (GM_CHUNK, GM_CHUNK), 0)
    c = lax.broadcasted_iota(jnp.int32, (GM_CHUNK, GM_CHUNK), 1)
    tril = r >= c
    ws = [jnp.where(tril, w_ref[gi], 0.0).astype(BF16) for gi in range(GM_GROUPS)]
    for ch in range(SGU_TM // GM_CHUNK):
        rows = slice(ch * GM_CHUNK, (ch + 1) * GM_CHUNK)
        v = v_ref[rows, :]
        mu = jnp.mean(v, axis=-1, keepdims=True)
        vc = v - mu
        var = jnp.mean(vc * vc, axis=-1, keepdims=True)
        vn = (vc * lax.rsqrt(var + EPS) * g_ref[...] + b_ref[...]).astype(BF16)
        for gi in range(GM_GROUPS):
            cols = slice(gi * GM_GW, (gi + 1) * GM_GW)
            mixed = jnp.dot(ws[gi], vn[:, cols], preferred_element_type=F32) + bs_ref[:, cols]
            o_ref[rows, cols] = (u_ref[rows, cols] * mixed).astype(o_ref.dtype)


def _sgu(z, ln_g, ln_b, w_s, bs_exp):
    return pl.pallas_call(
        _sgu_body,
        grid=(TOK // SGU_TM,),
        in_specs=[
            pl.BlockSpec((SGU_TM, GM_W), lambda i: (i, Z_U // GM_W)),
            pl.BlockSpec((SGU_TM, GM_W), lambda i: (i, Z_V // GM_W)),
            pl.BlockSpec((1, GM_W), lambda i: (0, 0)),
            pl.BlockSpec((1, GM_W), lambda i: (0, 0)),
            pl.BlockSpec((GM_GROUPS, GM_CHUNK, GM_CHUNK), lambda i: (0, 0, 0)),
            pl.BlockSpec((GM_CHUNK, GM_W), lambda i: (0, 0)),
        ],
        out_specs=pl.BlockSpec((SGU_TM, GM_W), lambda i: (i, 0)),
        out_shape=jax.ShapeDtypeStruct((TOK, GM_W), BF16),
        compiler_params=_cparams(("parallel",)),
        name="sgu",
    )(z, z, ln_g, ln_b, w_s, bs_exp)


def _out_body(a_ref, sg_ref, m1_ref, m2_ref, x_ref, wn_ref, ws_ref, wo_ref, gn_ref, o_ref, h_ref):
    pa = jnp.dot(a_ref[...], wn_ref[...], preferred_element_type=F32)
    ps = jnp.dot(sg_ref[...], ws_ref[...], preferred_element_type=F32)
    merged = (m1_ref[...] * pa + m2_ref[...] * ps).astype(BF16)
    y = x_ref[...] + jnp.dot(merged, wo_ref[...], preferred_element_type=F32)
    o_ref[...] = y
    h_ref[...] = _rms_bf16(y, gn_ref[...])


def _merge_out(a, sg, z, x, wn, ws, wo, l, next_gain):
    const = lambda i: (l, 0, 0)
    return pl.pallas_call(
        _out_body,
        grid=(TOK // OUT_TM,),
        in_specs=[
            pl.BlockSpec((OUT_TM, NSA_W), lambda i: (i, 0)),
            pl.BlockSpec((OUT_TM, GM_W), lambda i: (i, 0)),
            pl.BlockSpec((OUT_TM, D_MODEL), lambda i: (i, 0)),
            pl.BlockSpec((OUT_TM, D_MODEL), lambda i: (i, 1)),
            pl.BlockSpec((OUT_TM, D_MODEL), lambda i: (i, 0)),
            pl.BlockSpec((None, NSA_W, D_MODEL), const),
            pl.BlockSpec((None, GM_W, D_MODEL), const),
            pl.BlockSpec((None, D_MODEL, D_MODEL), const),
            pl.BlockSpec((1, D_MODEL), lambda i: (0, 0)),
        ],
        out_specs=[pl.BlockSpec((OUT_TM, D_MODEL), lambda i: (i, 0)),
                   pl.BlockSpec((OUT_TM, D_MODEL), lambda i: (i, 0))],
        out_shape=[jax.ShapeDtypeStruct((TOK, D_MODEL), F32),
                   jax.ShapeDtypeStruct((TOK, D_MODEL), BF16)],
        compiler_params=_cparams(("parallel",)),
        name="merge_out",
    )(a, sg, z, z, x, wn, ws, wo, next_gain)


def _check_overlap_stencil():
    ci = np.arange(N_CMP - 1)[None, :] * CMP_STRIDE
    sj = np.arange(N_SBLK)[:, None] * SEL_BLOCK
    ov = np.clip(np.minimum(ci + CMP_BLOCK, sj + SEL_BLOCK) - np.maximum(ci, sj), 0, None) / CMP_BLOCK
    stencil = np.zeros((N_SBLK, N_CMP - 1))
    for j in range(N_SBLK):
        for c, w in ((4 * j - 1, 0.5), (4 * j, 1.0), (4 * j + 1, 1.0), (4 * j + 2, 1.0), (4 * j + 3, 0.5)):
            if 0 <= c < N_CMP - 1:
                stencil[j, c] = w
    assert np.array_equal(ov, stencil)


def _block_diag_ones():
    r = np.arange(LANE)
    return (r[:, None] // HEAD_DIM == r[None, :] // HEAD_DIM).astype(np.float32)


def _prep_w_in(w):
    L = w.shape[0]
    kv = w[:, :, OFF_KV:OFF_NG].reshape(L, D_MODEL, 3, 2, N_KV, HEAD_DIM)
    kv = kv.transpose(0, 1, 2, 4, 3, 5).reshape(L, D_MODEL, 6 * KV_W)
    ng = w[:, :, OFF_NG:OFF_UV].reshape(L, D_MODEL, N_KV, HPG, 3).transpose(0, 1, 2, 4, 3)
    ng = ng.reshape(L, D_MODEL, N_KV, 3 * HPG)
    ng = jnp.pad(ng, ((0, 0), (0, 0), (0, 0), (0, GATE_GW - 3 * HPG))).reshape(L, D_MODEL, N_KV * GATE_GW)
    ng = jnp.pad(ng, ((0, 0), (0, 0), (0, PIN_TN - N_KV * GATE_GW)))
    return jnp.concatenate(
        [w[:, :, OFF_MG:], w[:, :, OFF_UV:OFF_MG], w[:, :, :OFF_KV], kv, ng], axis=2).astype(BF16)


def _mixer(x, z, l, p, tables):
    t0, t1, tc = tables
    pos = jnp.stack([p["cmp_pos_k"][l], p["cmp_pos_v"][l]]).reshape(2, 1, CMP_BLOCK * HEAD_DIM)
    w1 = jnp.stack([p["cmp_k_w1"][l], p["cmp_v_w1"][l]]).astype(BF16)
    kc, vcT = _compress(z, pos, w1, p["cmp_k_w2"][l].astype(BF16), p["cmp_v_w2"][l].T.astype(BF16),
                        p["k_norm"][l, 0].reshape(1, HEAD_DIM))
    a = _attention(z, kc, vcT, t0, t1, tc)
    bs_exp = jnp.repeat(p["sgu_b"][l].T, GM_GW, axis=1)
    sg = _sgu(z, p["sgu_norm_g"][l].reshape(1, GM_W), p["sgu_norm_b"][l].reshape(1, GM_W),
              p["sgu_w"][l], bs_exp)
    return _merge_out(a, sg, z, x, p["wn"], p["ws"], p["wo"], l, p["ffn2_norm"][l].reshape(1, D_MODEL))


def kernel(x, rel_bias, ffn1_norm, ffn1_w_gate, ffn1_w_up, ffn1_w_down, mix_norm, w_in, q_norm, k_norm, cmp_pos_k, cmp_pos_v, cmp_k_w1, cmp_k_w2, cmp_v_w1, cmp_v_w2, sgu_norm_g, sgu_norm_b, sgu_w, sgu_b, w_proj_nsa, w_proj_sgu, w_out, ffn2_norm, ffn2_w_gate, ffn2_w_up, ffn2_w_down):
    p = dict(cmp_pos_k=cmp_pos_k, cmp_pos_v=cmp_pos_v, cmp_k_w1=cmp_k_w1, cmp_k_w2=cmp_k_w2,
             cmp_v_w1=cmp_v_w1, cmp_v_w2=cmp_v_w2, k_norm=k_norm, sgu_norm_g=sgu_norm_g,
             sgu_norm_b=sgu_norm_b, sgu_w=sgu_w, sgu_b=sgu_b, wn=w_proj_nsa.astype(BF16),
             ws=w_proj_sgu.astype(BF16), wo=w_out.astype(BF16), ffn2_norm=ffn2_norm)
    near, cmpb = _bucket_tables()
    t0, t1, tc = _bias_tables(rel_bias, jnp.asarray(near), jnp.asarray(cmpb))
    tables = (t0, t1, tc)
    _check_overlap_stencil()
    bd = jnp.asarray(_block_diag_ones()).astype(BF16)
    w_in_r = _prep_w_in(w_in)
    f1 = (ffn1_w_gate, ffn1_w_up, ffn1_w_down)
    f2 = (ffn2_w_gate, ffn2_w_up, ffn2_w_down)

    y = x.reshape(TOK, D_MODEL)
    for l in range(DEPTH):
        y = _ffn(y, _rmsnorm(y, ffn1_norm[l].reshape(1, D_MODEL)), *f1, l)
        z = _proj_in(y, mix_norm[l].reshape(1, D_MODEL), w_in_r, l, bd,
                     jnp.tile(q_norm[l], PIN_TN // HEAD_DIM).reshape(1, PIN_TN),
                     jnp.tile(k_norm[l, 1], PIN_TN // HEAD_DIM).reshape(1, PIN_TN),
                     jnp.tile(k_norm[l, 2], PIN_TN // HEAD_DIM).reshape(1, PIN_TN))
        y, h = _mixer(y, z, l, p, tables)
        y = _ffn(y, h, *f2, l)
    return y.reshape(BATCH, SEQ, D_MODEL)
```

```python
import math

import numpy as np
import jax
import jax.numpy as jnp
from jax import lax
from jax.experimental import pallas as pl
from jax.experimental.pallas import tpu as pltpu

F32 = jnp.float32
BF16 = jnp.bfloat16

D_MODEL = 2048
BATCH = 4
SEQ = 2048
DEPTH = 2
TOK = BATCH * SEQ
HEAD_DIM = 64
N_HEADS = 16
N_KV = 4
HPG = 4
NSA_W = 1024
KV_W = 256
CMP_BLOCK = 32
CMP_STRIDE = 16
CMP_HIDDEN = 256
SEL_BLOCK = 64
N_SEL = 16
WINDOW = 512
GM_W = 1024
GM_CHUNK = 128
GM_GW = 128
GM_GROUPS = 8
N_BUCKETS = 32
MAX_DISTANCE = 128
D_FF = 5504
EPS = 1e-6
NEG = -1e30
FORCE = 1e4
OFF_KV = NSA_W
OFF_NG = OFF_KV + 6 * KV_W
OFF_UV = OFF_NG + 3 * N_HEADS
OFF_MG = OFF_UV + 2 * GM_W
LOG2E = 1.4426950408889634

LANE = 128
VMEM_LIMIT = 56 * 1024 * 1024

FFN_TF = 512
FFN_UP_TM = 1024
NORM_TM = 512
FFN_DN_TM = 512
FFN_DN_TN = 512
PIN_TM = 1024
PIN_TN = 512
PIN_RC = 256
Z_MG = 0
Z_U = Z_MG + 2 * D_MODEL
Z_V = Z_U + GM_W
Z_Q = Z_V + GM_W
Z_KV = Z_Q + NSA_W
Z_NG = Z_KV + 6 * KV_W
Z_W = Z_NG + PIN_TN
KV_GW = 2 * HEAD_DIM
GATE_GW = 16
TQ = 128
NQ = SEQ // TQ
NK = SEQ // TQ
QL = HPG * TQ
N_CMP = 128
CMP_BAND = 16
MASKED = -1
N_SBLK = SEQ // SEL_BLOCK
V_ROWS = HEAD_DIM + 16
FAR_GROUP = 4
Q_PAIR = 4
SGU_TM = 512
OUT_TM = 256


def _cparams(sem):
    return pltpu.CompilerParams(dimension_semantics=sem, vmem_limit_bytes=VMEM_LIMIT)


def _cast_rows(src_ref, dst_ref, chunk):
    def body(r, carry):
        rows = pl.ds(pl.multiple_of(r * chunk, chunk), chunk)
        dst_ref[rows, :] = src_ref[rows, :].astype(BF16)
        return carry

    lax.fori_loop(0, src_ref.shape[0] // chunk, body, 0)


def _rms_bf16(x, gain):
    ms = jnp.mean(x * x, axis=-1, keepdims=True)
    return (x * lax.rsqrt(ms + EPS) * gain).astype(BF16)


def _norm_body(x_ref, g_ref, h_ref):
    h_ref[...] = _rms_bf16(x_ref[...], g_ref[...])


def _rmsnorm(x, gain):
    return pl.pallas_call(
        _norm_body,
        grid=(TOK // NORM_TM,),
        in_specs=[pl.BlockSpec((NORM_TM, D_MODEL), lambda i: (i, 0)),
                  pl.BlockSpec((1, D_MODEL), lambda i: (0, 0))],
        out_specs=pl.BlockSpec((NORM_TM, D_MODEL), lambda i: (i, 0)),
        out_shape=jax.ShapeDtypeStruct((TOK, D_MODEL), BF16),
        compiler_params=_cparams(("parallel",)),
        name="rmsnorm",
    )(x, gain)


def _ffn_up_body(h_ref, wg_ref, wu_ref, a_ref, wgb_ref, wub_ref):
    @pl.when(pl.program_id(1) == 0)
    def _():
        _cast_rows(wg_ref, wgb_ref, LANE)
        _cast_rows(wu_ref, wub_ref, LANE)

    h = h_ref[...]
    g = jnp.dot(h, wgb_ref[...], preferred_element_type=F32)
    u = jnp.dot(h, wub_ref[...], preferred_element_type=F32)
    a_ref[...] = (g * jax.nn.sigmoid(g) * u).astype(BF16)


def _ffn_down_body(a_ref, wd_ref, x_ref, o_ref, wdb_ref):
    @pl.when(pl.program_id(1) == 0)
    def _():
        _cast_rows(wd_ref, wdb_ref, LANE)

    o_ref[...] = x_ref[...] + 0.5 * jnp.dot(a_ref[...], wdb_ref[...], preferred_element_type=F32)


def _ffn(x, h, wg, wu, wd, l):
    nf = pl.cdiv(D_FF, FFN_TF)
    a = pl.pallas_call(
        _ffn_up_body,
        grid=(nf, TOK // FFN_UP_TM),
        in_specs=[
            pl.BlockSpec((FFN_UP_TM, D_MODEL), lambda f, m: (m, 0)),
            pl.BlockSpec((None, D_MODEL, FFN_TF), lambda f, m: (l, 0, f)),
            pl.BlockSpec((None, D_MODEL, FFN_TF), lambda f, m: (l, 0, f)),
        ],
        out_specs=pl.BlockSpec((FFN_UP_TM, FFN_TF), lambda f, m: (m, f)),
        out_shape=jax.ShapeDtypeStruct((TOK, D_FF), BF16),
        scratch_shapes=[pltpu.VMEM((D_MODEL, FFN_TF), BF16), pltpu.VMEM((D_MODEL, FFN_TF), BF16)],
        compiler_params=_cparams(("arbitrary", "arbitrary")),
        name="ffn_up",
    )(h, wg, wu)
    return pl.pallas_call(
        _ffn_down_body,
        grid=(D_MODEL // FFN_DN_TN, TOK // FFN_DN_TM),
        in_specs=[
            pl.BlockSpec((FFN_DN_TM, D_FF), lambda n, m: (m, 0)),
            pl.BlockSpec((None, D_FF, FFN_DN_TN), lambda n, m: (l, 0, n)),
            pl.BlockSpec((FFN_DN_TM, FFN_DN_TN), lambda n, m: (m, n)),
        ],
        out_specs=pl.BlockSpec((FFN_DN_TM, FFN_DN_TN), lambda n, m: (m, n)),
        out_shape=jax.ShapeDtypeStruct((TOK, D_MODEL), F32),
        scratch_shapes=[pltpu.VMEM((D_FF, FFN_DN_TN), BF16)],
        compiler_params=_cparams(("arbitrary", "arbitrary")),
        name="ffn_down",
    )(a, wd, x)


def _head_rms(y, bd):
    y2 = (y * y).astype(BF16)
    ss = jnp.concatenate(
        [jnp.dot(y2[:, k * LANE:(k + 1) * LANE], bd, preferred_element_type=F32)
         for k in range(y.shape[1] // LANE)], axis=1)
    return y * lax.rsqrt(ss * (1.0 / HEAD_DIM) + EPS)


def _pin_body(x_ref, g_ref, w_ref, bd_ref, qg_ref, k1g_ref, k2g_ref, o_ref, h_ref):
    j = pl.program_id(1)

    @pl.when(j == 0)
    def _():
        x = x_ref[...]
        ms = jnp.mean(x * x, axis=-1, keepdims=True)
        h_ref[...] = (x * lax.rsqrt(ms + EPS) * g_ref[...]).astype(BF16)

    t_u = Z_U // PIN_TN
    t_q = Z_Q // PIN_TN
    t_kv = Z_KV // PIN_TN
    t_ng = Z_NG // PIN_TN

    def emit(act):
        for r in range(PIN_TM // PIN_RC):
            rows = slice(r * PIN_RC, (r + 1) * PIN_RC)
            o_ref[rows, :] = act(jnp.dot(h_ref[rows, :], w_ref[...], preferred_element_type=F32))

    @pl.when((j < t_u) | (j == t_ng))
    def _():
        emit(jax.nn.sigmoid)

    @pl.when((j >= t_u) & (j < t_q))
    def _():
        emit(jax.nn.gelu)

    @pl.when((j >= t_q) & (j < t_kv))
    def _():
        emit(lambda y: _head_rms(y, bd_ref[...]) * (qg_ref[...] * (HEAD_DIM ** -0.5 * LOG2E)))

    @pl.when(j == t_kv)
    def _():
        emit(lambda y: y)

    def kv_act(kg_ref):
        is_k = (lax.broadcasted_iota(jnp.int32, (1, PIN_TN), 1) & (KV_GW - 1)) < HEAD_DIM
        return lambda y: jnp.where(is_k, _head_rms(y, bd_ref[...]) * kg_ref[...], y)

    @pl.when(j == t_kv + 1)
    def _():
        emit(kv_act(k1g_ref))

    @pl.when(j == t_kv + 2)
    def _():
        emit(kv_act(k2g_ref))


def _proj_in(x, gain, w, l, bd, qg, k1g, k2g):
    return pl.pallas_call(
        _pin_body,
        grid=(TOK // PIN_TM, Z_W // PIN_TN),
        in_specs=[
            pl.BlockSpec((PIN_TM, D_MODEL), lambda i, j: (i, 0)),
            pl.BlockSpec((1, D_MODEL), lambda i, j: (0, 0)),
            pl.BlockSpec((None, D_MODEL, PIN_TN), lambda i, j: (l, 0, j)),
            pl.BlockSpec((LANE, LANE), lambda i, j: (0, 0)),
            pl.BlockSpec((1, PIN_TN), lambda i, j: (0, 0)),
            pl.BlockSpec((1, PIN_TN), lambda i, j: (0, 0)),
            pl.BlockSpec((1, PIN_TN), lambda i, j: (0, 0)),
        ],
        out_specs=pl.BlockSpec((PIN_TM, PIN_TN), lambda i, j: (i, j)),
        out_shape=jax.ShapeDtypeStruct((TOK, Z_W), F32),
        scratch_shapes=[pltpu.VMEM((PIN_TM, D_MODEL), BF16)],
        compiler_params=_cparams(("parallel", "arbitrary")),
        name="proj_in",
    )(x, gain, w, bd, qg, k1g, k2g)


def _cmp_body(z_ref, pos_ref, w1_ref, w2k_ref, w2vT_ref, kg_ref, kc_ref, vcT_ref):
    half = CMP_STRIDE * HEAD_DIM
    lane = lax.broadcasted_iota(jnp.int32, (N_CMP, LANE), 1)
    first = lane < HEAD_DIM
    xk, xv = [], []
    for j in range(CMP_STRIDE // 2):
        ev = z_ref[pl.ds(2 * j, N_CMP, stride=CMP_STRIDE), :]
        od = z_ref[pl.ds(2 * j + 1, N_CMP, stride=CMP_STRIDE), :]
        xk.append(jnp.where(first, ev, pltpu.roll(od, HEAD_DIM, 1)))
        xv.append(jnp.where(first, pltpu.roll(ev, HEAD_DIM, 1), od))

    def hidden(x, kind):
        xa = (x + pos_ref[kind, :, :half]).astype(BF16)
        xb = (x + pos_ref[kind, :, half:]).astype(BF16)
        a = jnp.dot(xa, w1_ref[kind, :half, :], preferred_element_type=F32)
        b = jnp.dot(xb, w1_ref[kind, half:, :], preferred_element_type=F32)
        hid = a + pltpu.roll(b, N_CMP - 1, 0)
        return (hid * jax.nn.sigmoid(hid)).astype(BF16)

    yk = jnp.dot(hidden(jnp.concatenate(xk, axis=1), 0), w2k_ref[...], preferred_element_type=F32)
    ms = jnp.mean(yk * yk, axis=-1, keepdims=True)
    kc_ref[0, 0] = (yk * lax.rsqrt(ms + EPS) * kg_ref[...]).astype(BF16)
    hv = hidden(jnp.concatenate(xv, axis=1), 1)
    vcT_ref[0, 0] = lax.dot_general(w2vT_ref[...], hv, (((1,), (1,)), ((), ())),
                                    preferred_element_type=F32).astype(BF16)


def _compress(z, pos, w1, w2k, w2vT, kg):
    return pl.pallas_call(
        _cmp_body,
        grid=(BATCH, N_KV),
        in_specs=[
            pl.BlockSpec((SEQ, KV_GW), lambda b, g: (b, Z_KV // KV_GW + g)),
            pl.BlockSpec((2, 1, CMP_BLOCK * HEAD_DIM), lambda b, g: (0, 0, 0)),
            pl.BlockSpec((2, CMP_BLOCK * HEAD_DIM, CMP_HIDDEN), lambda b, g: (0, 0, 0)),
            pl.BlockSpec((CMP_HIDDEN, HEAD_DIM), lambda b, g: (0, 0)),
            pl.BlockSpec((HEAD_DIM, CMP_HIDDEN), lambda b, g: (0, 0)),
            pl.BlockSpec((1, HEAD_DIM), lambda b, g: (0, 0)),
        ],
        out_specs=[
            pl.BlockSpec((1, 1, N_CMP, HEAD_DIM), lambda b, g: (b, g, 0, 0)),
            pl.BlockSpec((1, 1, HEAD_DIM, N_CMP), lambda b, g: (b, g, 0, 0)),
        ],
        out_shape=[
            jax.ShapeDtypeStruct((BATCH, N_KV, N_CMP, HEAD_DIM), BF16),
            jax.ShapeDtypeStruct((BATCH, N_KV, HEAD_DIM, N_CMP), BF16),
        ],
        compiler_params=_cparams(("parallel", "parallel")),
        name="compress",
    )(z, pos, w1, w2k, w2vT, kg)


def _np_bucket(dist):
    n = np.maximum(dist, 0)
    max_exact = N_BUCKETS // 2
    nf = np.maximum(n, 1).astype(np.float32)
    large = max_exact + (np.log(nf / np.float32(max_exact)) / np.float32(math.log(MAX_DISTANCE / max_exact))
                         * np.float32(N_BUCKETS - max_exact)).astype(np.int32)
    large = np.minimum(large, N_BUCKETS - 1)
    return np.where(n < max_exact, n, large).astype(np.int32)


def _bucket_tables():
    kj = np.arange(TQ)[:, None]
    qi = np.arange(TQ)[None, :]
    near = np.stack([np.where(qi >= kj, _np_bucket(qi - kj), MASKED), _np_bucket(TQ + qi - kj)])
    start = np.array([_cmp_band_start(i) for i in range(NQ)])[:, None, None]
    c = start + np.arange(CMP_BAND)[None, :, None]
    t = (np.arange(NQ)[:, None, None] * TQ + qi[None])
    dist_band = t - (c * CMP_STRIDE + CMP_BLOCK - 1)
    cmp_b = np.where(dist_band >= 0, _np_bucket(dist_band), MASKED)
    dist = t - (np.arange(N_CMP)[None, :, None] * CMP_STRIDE + CMP_BLOCK - 1)
    for i in range(NQ):
        s0 = _cmp_band_start(i)
        assert np.all(dist[i, :s0] >= 0) and np.all(_np_bucket(dist[i, :s0]) == N_BUCKETS - 1)
        assert np.all(dist[i, s0 + CMP_BAND:] < 0)
    return near.astype(np.int32), cmp_b.astype(np.int32)


def _cmp_band_start(i):
    return max((TQ // CMP_STRIDE) * i - 8, 0)


def _bias_body(rb_ref, near_ref, cmpb_ref, t0_ref, t1_ref, tc_ref):
    g = pl.program_id(0)

    def lut(bk, h):
        val = jnp.full(bk.shape, NEG, F32)
        for b in range(N_BUCKETS):
            val = jnp.where(bk == b, rb_ref[b, h], val)
        return val

    for hh in range(HPG):
        h = g * HPG + hh
        sl = slice(hh * TQ, (hh + 1) * TQ)
        far = rb_ref[N_BUCKETS - 1, h]
        t0_ref[0, :, sl] = (lut(near_ref[0], h) - far) * LOG2E
        t1_ref[0, :, sl] = (lut(near_ref[1], h) - far) * LOG2E

        def body(i, carry):
            start = pl.multiple_of(jnp.maximum((TQ // CMP_STRIDE) * i - 8, 0), 8)
            row = lax.broadcasted_iota(jnp.int32, (N_CMP, TQ), 0)
            tc_ref[0, i, :, sl] = jnp.where(row < start, far * LOG2E, NEG)
            tc_ref[0, i, pl.ds(start, CMP_BAND), sl] = lut(cmpb_ref[i], h) * LOG2E
            return carry

        lax.fori_loop(0, NQ, body, 0)


def _bias_tables(rel_bias, near, cmpb):
    return pl.pallas_call(
        _bias_body,
        grid=(N_KV,),
        in_specs=[
            pl.BlockSpec(memory_space=pltpu.SMEM),
            pl.BlockSpec((2, TQ, TQ), lambda g: (0, 0, 0)),
            pl.BlockSpec((NQ, CMP_BAND, TQ), lambda g: (0, 0, 0)),
        ],
        out_specs=[
            pl.BlockSpec((1, TQ, QL), lambda g: (g, 0, 0)),
            pl.BlockSpec((1, TQ, QL), lambda g: (g, 0, 0)),
            pl.BlockSpec((1, NQ, N_CMP, QL), lambda g: (g, 0, 0, 0)),
        ],
        out_shape=[
            jax.ShapeDtypeStruct((N_KV, TQ, QL), F32),
            jax.ShapeDtypeStruct((N_KV, TQ, QL), F32),
            jax.ShapeDtypeStruct((N_KV, NQ, N_CMP, QL), F32),
        ],
        compiler_params=_cparams(("arbitrary",)),
        name="bias_tables",
    )(rel_bias, near, cmpb)


def _attn_body(zq_ref, zs_ref, zw_ref, zg_ref, kc_ref, vcT_ref, t0_ref, t1_ref, tc_ref, o_ref,
               ks_ref, vsT_ref, kw_ref, vwT_ref, sel_ref, sfar_ref, gsc_ref, psum_ref):
    g = pl.program_id(1)
    ip = pl.program_id(2)

    @pl.when(ip == 0)
    def _():
        ones = jnp.ones((V_ROWS - HEAD_DIM, TQ), BF16)

        def fill(n, carry):
            for z_ref, k_ref, vT_ref in ((zs_ref, ks_ref, vsT_ref), (zw_ref, kw_ref, vwT_ref)):
                blk = z_ref[pl.ds(pl.multiple_of(n * TQ, TQ), TQ), :]
                k_ref[n] = blk[:, :HEAD_DIM].astype(BF16)
                vT_ref[n, :HEAD_DIM, :] = blk.T[HEAD_DIM:, :].astype(BF16)
                vT_ref[n, HEAD_DIM:, :] = ones
            return carry

        lax.fori_loop(0, NK, fill, 0)

    blk = lax.broadcasted_iota(jnp.int32, (N_SBLK, TQ), 0)
    ql = lax.broadcasted_iota(jnp.int32, (N_SBLK, TQ), 1)
    key1 = lax.broadcasted_iota(jnp.int32, (TQ, TQ), 0)
    q1 = lax.broadcasted_iota(jnp.int32, (TQ, TQ), 1)
    nlast = WINDOW // TQ

    def sel_mask(t, n):
        r0 = sel_ref[t, pl.ds(2 * n, 1), :]
        r1 = sel_ref[t, pl.ds(2 * n + 1, 1), :]
        return jnp.where(key1 < SEL_BLOCK, r0, r1) > 0.5

    def scores(qT, k_ref, n, bias, mask):
        sc = jnp.dot(k_ref[n], qT, preferred_element_type=F32)
        if bias is not None:
            sc = sc + bias
        if mask is None:
            return sc
        return jnp.concatenate(
            [jnp.where(mask, sc[:, h * TQ:(h + 1) * TQ], NEG) for h in range(HPG)], axis=1)

    def colmax8(sc):
        return jnp.max(sc.reshape(TQ // 8, 8, QL), axis=0)

    def probs(sc, mrow):
        return jnp.exp2(sc - mrow).astype(BF16)

    def pv(vT_ref, ns, ps):
        vv = vT_ref[ns[0]] if len(ns) == 1 else jnp.concatenate([vT_ref[n] for n in ns], axis=1)
        pp = ps[0] if len(ps) == 1 else jnp.concatenate(ps, axis=0)
        return jnp.dot(vv, pp, preferred_element_type=F32)

    def finish(acc):
        return acc[:HEAD_DIM] * (1.0 / acc[HEAD_DIM:HEAD_DIM + 1])

    def stage_q(t):
        i = Q_PAIR * ip + t
        rows = slice(t * TQ, (t + 1) * TQ)
        qt = zq_ref[rows, :].T
        qT = jnp.concatenate([qt[h * HEAD_DIM:(h + 1) * HEAD_DIM] for h in range(HPG)], axis=1).astype(BF16)
        gsc_ref[t] = zg_ref[rows, :].T
        return dict(i=i, qT=qT, n1=jnp.maximum(i - 1, 0), i1=jnp.full((TQ, TQ), i, jnp.int32),
                    nfar=jnp.maximum(i - 1, 0))

    def stage_cmp_scores(t, tl):
        tl["s_c"] = jnp.dot(kc_ref[0, 0], tl["qT"], preferred_element_type=F32) + tc_ref[0, t]
        tl["m_c"] = jnp.maximum(jnp.max(colmax8(tl["s_c"]), axis=0, keepdims=True), 0.1 * NEG)

    def stage_cmp(t, tl):
        e = jnp.exp2(tl["s_c"] - tl["m_c"])
        l = jnp.sum(e, axis=0, keepdims=True)
        p = e * jnp.where(l > 0.0, 1.0 / l, 0.0)
        tl["ocT"] = jnp.dot(vcT_ref[0, 0], p.astype(BF16), preferred_element_type=F32)
        psum_ref[t] = p[:, 0:TQ] + p[:, TQ:2 * TQ] + p[:, 2 * TQ:3 * TQ] + p[:, 3 * TQ:4 * TQ]

    def stage_select(t, tl):
        i = tl["i"]
        ratio = SEL_BLOCK // CMP_STRIDE
        part = [psum_ref[t, pl.ds(r, N_SBLK, stride=ratio), :] for r in range(ratio)]
        prev_last = jnp.where(blk == 0, 0.0, pltpu.roll(part[3], 1, 0))
        imp = (part[0] + part[1] + part[2]) + 0.5 * (part[3] + prev_last)
        cur = jnp.full((N_SBLK, TQ), i, jnp.int32) * (TQ // SEL_BLOCK) + ql // SEL_BLOCK
        forced = (blk == 0) | (blk == cur) | (blk == cur - 1)
        score = jnp.where(blk <= cur, jnp.where(forced, FORCE, imp), NEG)
        sub = 8
        pieces = [score[k * sub:(k + 1) * sub, :] for k in range(N_SBLK // sub)]
        ranks = [jnp.zeros((sub, TQ), F32) for _ in pieces]
        for jj in range(N_SBLK):
            row = score[jj:jj + 1, :]
            for k, pc in enumerate(pieces):
                if k * sub > jj:
                    beats = row >= pc
                elif (k + 1) * sub - 1 < jj:
                    beats = row > pc
                else:
                    beats = (row > pc) | ((row == pc) & (blk[k * sub:(k + 1) * sub, :] > jj))
                ranks[k] = ranks[k] + jnp.where(beats, 1.0, 0.0)
        rank = jnp.concatenate(ranks, axis=0)
        sel_ref[t] = jnp.where((rank < N_SEL) & (blk <= cur), 1.0, 0.0)

    def stage_window_scores(t, tl):
        i, n1, i1, qT = tl["i"], tl["n1"], tl["i1"], tl["qT"]
        n2, n3, n4 = jnp.maximum(i - 2, 0), jnp.maximum(i - 3, 0), jnp.maximum(i - nlast, 0)
        w_d = scores(qT, kw_ref, i, t0_ref[0], None)
        w_1 = scores(qT, kw_ref, n1, t1_ref[0], i1 >= 1)
        w_2 = scores(qT, kw_ref, n2, None, i1 >= 2)
        w_3 = scores(qT, kw_ref, n3, None, i1 >= 3)
        w_4 = scores(qT, kw_ref, n4, None, (q1 < key1) & (i1 >= nlast))
        mxw = jnp.maximum(jnp.maximum(jnp.maximum(colmax8(w_d), colmax8(w_1)),
                                      jnp.maximum(colmax8(w_2), colmax8(w_3))), colmax8(w_4))
        tl["w"] = (w_d, w_1, w_2, w_3, w_4)
        tl["wn"] = (i, n1, n2, n3, n4)
        tl["m_w"] = jnp.max(mxw, axis=0, keepdims=True)

    def stage_window_pv(t, tl):
        (w_d, w_1, w_2, w_3, w_4), (i, n1, n2, n3, n4), m_w = tl["w"], tl["wn"], tl["m_w"]
        tl["owT"] = finish(pv(vwT_ref, [i], [probs(w_d, m_w)])
                           + pv(vwT_ref, [n1, n2], [probs(w_1, m_w), probs(w_2, m_w)])
                           + pv(vwT_ref, [n3, n4], [probs(w_3, m_w), probs(w_4, m_w)]))

    def stage_near(t, tl):
        i, n1, i1, qT = tl["i"], tl["n1"], tl["i1"], tl["qT"]
        tl["s_d"] = scores(qT, ks_ref, i, t0_ref[0], None)
        tl["s_1"] = scores(qT, ks_ref, n1, t1_ref[0], sel_mask(t, n1) & (i1 >= 1))
        tl["mx"] = jnp.maximum(colmax8(tl["s_d"]), colmax8(tl["s_1"]))

    tiles = [stage_q(t) for t in range(Q_PAIR)]
    for stage in (stage_cmp_scores, stage_cmp, stage_window_scores, stage_select, stage_near, stage_window_pv):
        for t, tl in enumerate(tiles):
            stage(t, tl)
    assert Q_PAIR == FAR_GROUP == 4
    extra = (2, 3)

    def far_scores(tr, mxs, ts):
        out = list(mxs)
        for t in ts:
            tl = tiles[t]
            for c in range(FAR_GROUP):
                n = FAR_GROUP * tr + c
                sc = scores(tl["qT"], ks_ref, n, None,
                            sel_mask(t, n) & (jnp.full((TQ, TQ), n, jnp.int32) < tl["nfar"]))
                sfar_ref[t, n] = sc
                out[t] = jnp.maximum(out[t], colmax8(sc))
        return tuple(out)

    every = tuple(range(Q_PAIR))
    mxs = lax.fori_loop(0, ip, lambda tr, c: far_scores(tr, c, every), tuple(tl["mx"] for tl in tiles))
    mxs = far_scores(ip, mxs, extra)
    m_s = [jnp.max(mx, axis=0, keepdims=True) for mx in mxs]
    accs = tuple(pv(vsT_ref, [tl["i"], tl["n1"]], [probs(tl["s_d"], m_s[t]), probs(tl["s_1"], m_s[t])])
                 for t, tl in enumerate(tiles))

    def far_pv(tr, accs_, ts):
        ns = [FAR_GROUP * tr + c for c in range(FAR_GROUP)]
        out = list(accs_)
        for t in ts:
            out[t] = out[t] + pv(vsT_ref, ns, [probs(sfar_ref[t, n], m_s[t]) for n in ns])
        return tuple(out)

    accs = lax.fori_loop(0, ip, lambda tr, c: far_pv(tr, c, every), accs)
    accs = far_pv(ip, accs, extra)

    for t, tl in enumerate(tiles):
        base = g * GATE_GW
        gate = [jnp.concatenate([gsc_ref[t, pl.ds(base + br * HPG + h, 1), :] for h in range(HPG)], axis=1)
                for br in range(3)]
        aT = gate[0] * tl["ocT"] + gate[1] * finish(accs[t]) + gate[2] * tl["owT"]
        a4 = jnp.concatenate([aT[:, h * TQ:(h + 1) * TQ] for h in range(HPG)], axis=0)
        o_ref[t * TQ:(t + 1) * TQ, :] = a4.T.astype(o_ref.dtype)


def _attention(z, kc, vcT, t0, t1, tc):
    qw = HPG * HEAD_DIM
    rows = Q_PAIR * TQ
    npair = NQ // Q_PAIR
    return pl.pallas_call(
        _attn_body,
        grid=(BATCH, N_KV, npair),
        in_specs=[
            pl.BlockSpec((rows, qw), lambda b, g, i: (b * npair + i, Z_Q // qw + g)),
            pl.BlockSpec((SEQ, KV_GW), lambda b, g, i: (b, (Z_KV + 2 * KV_W) // KV_GW + g)),
            pl.BlockSpec((SEQ, KV_GW), lambda b, g, i: (b, (Z_KV + 4 * KV_W) // KV_GW + g)),
            pl.BlockSpec((rows, LANE), lambda b, g, i: (b * npair + i, Z_NG // LANE)),
            pl.BlockSpec((1, 1, N_CMP, HEAD_DIM), lambda b, g, i: (b, g, 0, 0)),
            pl.BlockSpec((1, 1, HEAD_DIM, N_CMP), lambda b, g, i: (b, g, 0, 0)),
            pl.BlockSpec((1, TQ, QL), lambda b, g, i: (g, 0, 0)),
            pl.BlockSpec((1, TQ, QL), lambda b, g, i: (g, 0, 0)),
            pl.BlockSpec((1, Q_PAIR, N_CMP, QL), lambda b, g, i: (g, i, 0, 0)),
        ],
        out_specs=pl.BlockSpec((rows, qw), lambda b, g, i: (b * npair + i, g)),
        out_shape=jax.ShapeDtypeStruct((TOK, NSA_W), BF16),
        scratch_shapes=[
            pltpu.VMEM((NK, TQ, HEAD_DIM), BF16),
            pltpu.VMEM((NK, V_ROWS, TQ), BF16),
            pltpu.VMEM((NK, TQ, HEAD_DIM), BF16),
            pltpu.VMEM((NK, V_ROWS, TQ), BF16),
            pltpu.VMEM((Q_PAIR, N_SBLK, TQ), F32),
            pltpu.VMEM((Q_PAIR, NK, TQ, QL), F32),
            pltpu.VMEM((Q_PAIR, LANE, TQ), F32),
            pltpu.VMEM((Q_PAIR, N_CMP, TQ), F32),
        ],
        compiler_params=_cparams(("parallel", "parallel", "arbitrary")),
        name="nsa_attention",
    )(z, z, z, z, kc, vcT, t0, t1, tc)


def _sgu_body(u_ref, v_ref, g_ref, b_ref, w_ref, bs_ref, o_ref):
    r = lax.broadcasted_iota(jnp.int32, (GM_CHUNK, GM_CHUNK), 0)
    c = lax.broadcasted_iota(jnp.int32, (GM_CHUNK, GM_CHUNK), 1)
    tril = r >= c
    ws = [jnp.where(tril, w_ref[gi], 0.0).astype(BF16) for gi in range(GM_GROUPS)]
    for ch in range(SGU_TM // GM_CHUNK):
        rows = slice(ch * GM_CHUNK, (ch + 1) * GM_CHUNK)
        v = v_ref[rows, :]
        mu = jnp.mean(v, axis=-1, keepdims=True)
        vc = v - mu
        var = jnp.mean(vc * vc, axis=-1, keepdims=True)
        vn = (vc * lax.rsqrt(var + EPS) * g_ref[...] + b_ref[...]).astype(BF16)
        for gi in range(GM_GROUPS):
            cols = slice(gi * GM_GW, (gi + 1) * GM_GW)
            mixed = jnp.dot(ws[gi], vn[:, cols], preferred_element_type=F32) + bs_ref[:, cols]
            o_ref[rows, cols] = (u_ref[rows, cols] * mixed).astype(o_ref.dtype)


def _sgu(z, ln_g, ln_b, w_s, bs_exp):
    return pl.pallas_call(
        _sgu_body,
        grid=(TOK // SGU_TM,),
        in_specs=[
            pl.BlockSpec((SGU_TM, GM_W), lambda i: (i, Z_U // GM_W)),
            pl.BlockSpec((SGU_TM, GM_W), lambda i: (i, Z_V // GM_W)),
            pl.BlockSpec((1, GM_W), lambda i: (0, 0)),
            pl.BlockSpec((1, GM_W), lambda i: (0, 0)),
            pl.BlockSpec((GM_GROUPS, GM_CHUNK, GM_CHUNK), lambda i: (0, 0, 0)),
            pl.BlockSpec((GM_CHUNK, GM_W), lambda i: (0, 0)),
        ],
        out_specs=pl.BlockSpec((SGU_TM, GM_W), lambda i: (i, 0)),
        out_shape=jax.ShapeDtypeStruct((TOK, GM_W), BF16),
        compiler_params=_cparams(("parallel",)),
        name="sgu",
    )(z, z, ln_g, ln_b, w_s, bs_exp)


def _out_body(a_ref, sg_ref, m1_ref, m2_ref, x_ref, wn_ref, ws_ref, wo_ref, gn_ref, o_ref, h_ref):
    pa = jnp.dot(a_ref[...], wn_ref[...], preferred_element_type=F32)
    ps = jnp.dot(sg_ref[...], ws_ref[...], preferred_element_type=F32)
    merged = (m1_ref[...] * pa + m2_ref[...] * ps).astype(BF16)
    y = x_ref[...] + jnp.dot(merged, wo_ref[...], preferred_element_type=F32)
    o_ref[...] = y
    h_ref[...] = _rms_bf16(y, gn_ref[...])


def _merge_out(a, sg, z, x, wn, ws, wo, l, next_gain):
    const = lambda i: (l, 0, 0)
    return pl.pallas_call(
        _out_body,
        grid=(TOK // OUT_TM,),
        in_specs=[
            pl.BlockSpec((OUT_TM, NSA_W), lambda i: (i, 0)),
            pl.BlockSpec((OUT_TM, GM_W), lambda i: (i, 0)),
            pl.BlockSpec((OUT_TM, D_MODEL), lambda i: (i, 0)),
            pl.BlockSpec((OUT_TM, D_MODEL), lambda i: (i, 1)),
            pl.BlockSpec((OUT_TM, D_MODEL), lambda i: (i, 0)),
            pl.BlockSpec((None, NSA_W, D_MODEL), const),
            pl.BlockSpec((None, GM_W, D_MODEL), const),
            pl.BlockSpec((None, D_MODEL, D_MODEL), const),
            pl.BlockSpec((1, D_MODEL), lambda i: (0, 0)),
        ],
        out_specs=[pl.BlockSpec((OUT_TM, D_MODEL), lambda i: (i, 0)),
                   pl.BlockSpec((OUT_TM, D_MODEL), lambda i: (i, 0))],
        out_shape=[jax.ShapeDtypeStruct((TOK, D_MODEL), F32),
                   jax.ShapeDtypeStruct((TOK, D_MODEL), BF16)],
        compiler_params=_cparams(("parallel",)),
        name="merge_out",
    )(a, sg, z, z, x, wn, ws, wo, next_gain)


def _check_overlap_stencil():
    ci = np.arange(N_CMP - 1)[None, :] * CMP_STRIDE
    sj = np.arange(N_SBLK)[:, None] * SEL_BLOCK
    ov = np.clip(np.minimum(ci + CMP_BLOCK, sj + SEL_BLOCK) - np.maximum(ci, sj), 0, None) / CMP_BLOCK
    stencil = np.zeros((N_SBLK, N_CMP - 1))
    for j in range(N_SBLK):
        for c, w in ((4 * j - 1, 0.5), (4 * j, 1.0), (4 * j + 1, 1.0), (4 * j + 2, 1.0), (4 * j + 3, 0.5)):
            if 0 <= c < N_CMP - 1:
                stencil[j, c] = w
    assert np.array_equal(ov, stencil)


def _block_diag_ones():
    r = np.arange(LANE)
    return (r[:, None] // HEAD_DIM == r[None, :] // HEAD_DIM).astype(np.float32)


def _prep_w_in(w):
    L = w.shape[0]
    kv = w[:, :, OFF_KV:OFF_NG].reshape(L, D_MODEL, 3, 2, N_KV, HEAD_DIM)
    kv = kv.transpose(0, 1, 2, 4, 3, 5).reshape(L, D_MODEL, 6 * KV_W)
    ng = w[:, :, OFF_NG:OFF_UV].reshape(L, D_MODEL, N_KV, HPG, 3).transpose(0, 1, 2, 4, 3)
    ng = ng.reshape(L, D_MODEL, N_KV, 3 * HPG)
    ng = jnp.pad(ng, ((0, 0), (0, 0), (0, 0), (0, GATE_GW - 3 * HPG))).reshape(L, D_MODEL, N_KV * GATE_GW)
    ng = jnp.pad(ng, ((0, 0), (0, 0), (0, PIN_TN - N_KV * GATE_GW)))
    return jnp.concatenate(
        [w[:, :, OFF_MG:], w[:, :, OFF_UV:OFF_MG], w[:, :, :OFF_KV], kv, ng], axis=2).astype(BF16)


def _mixer(x, z, l, p, tables):
    t0, t1, tc = tables
    pos = jnp.stack([p["cmp_pos_k"][l], p["cmp_pos_v"][l]]).reshape(2, 1, CMP_BLOCK * HEAD_DIM)
    w1 = jnp.stack([p["cmp_k_w1"][l], p["cmp_v_w1"][l]]).astype(BF16)
    kc, vcT = _compress(z, pos, w1, p["cmp_k_w2"][l].astype(BF16), p["cmp_v_w2"][l].T.astype(BF16),
                        p["k_norm"][l, 0].reshape(1, HEAD_DIM))
    a = _attention(z, kc, vcT, t0, t1, tc)
    bs_exp = jnp.repeat(p["sgu_b"][l].T, GM_GW, axis=1)
    sg = _sgu(z, p["sgu_norm_g"][l].reshape(1, GM_W), p["sgu_norm_b"][l].reshape(1, GM_W),
              p["sgu_w"][l], bs_exp)
    return _merge_out(a, sg, z, x, p["wn"], p["ws"], p["wo"], l, p["ffn2_norm"][l].reshape(1, D_MODEL))


def kernel(x, rel_bias, ffn1_norm, ffn1_w_gate, ffn1_w_up, ffn1_w_down, mix_norm, w_in, q_norm, k_norm, cmp_pos_k, cmp_pos_v, cmp_k_w1, cmp_k_w2, cmp_v_w1, cmp_v_w2, sgu_norm_g, sgu_norm_b, sgu_w, sgu_b, w_proj_nsa, w_proj_sgu, w_out, ffn2_norm, ffn2_w_gate, ffn2_w_up, ffn2_w_down):
    p = dict(cmp_pos_k=cmp_pos_k, cmp_pos_v=cmp_pos_v, cmp_k_w1=cmp_k_w1, cmp_k_w2=cmp_k_w2,
             cmp_v_w1=cmp_v_w1, cmp_v_w2=cmp_v_w2, k_norm=k_norm, sgu_norm_g=sgu_norm_g,
             sgu_norm_b=sgu_norm_b, sgu_w=sgu_w, sgu_b=sgu_b, wn=w_proj_nsa.astype(BF16),
             ws=w_proj_sgu.astype(BF16), wo=w_out.astype(BF16), ffn2_norm=ffn2_norm)
    near, cmpb = _bucket_tables()
    t0, t1, tc = _bias_tables(rel_bias, jnp.asarray(near), jnp.asarray(cmpb))
    tables = (t0, t1, tc)
    _check_overlap_stencil()
    bd = jnp.asarray(_block_diag_ones()).astype(BF16)
    w_in_r = _prep_w_in(w_in)
    f1 = (ffn1_w_gate, ffn1_w_up, ffn1_w_down)
    f2 = (ffn2_w_gate, ffn2_w_up, ffn2_w_down)

    y = x.reshape(TOK, D_MODEL)
    for l in range(DEPTH):
        y = _ffn(y, _rmsnorm(y, ffn1_norm[l].reshape(1, D_MODEL)), *f1, l)
        z = _proj_in(y, mix_norm[l].reshape(1, D_MODEL), w_in_r, l, bd,
                     jnp.tile(q_norm[l], PIN_TN // HEAD_DIM).reshape(1, PIN_TN),
                     jnp.tile(k_norm[l, 1], PIN_TN // HEAD_DIM).reshape(1, PIN_TN),
                     jnp.tile(k_norm[l, 2], PIN_TN // HEAD_DIM).reshape(1, PIN_TN))
        y, h = _mixer(y, z, l, p, tables)
        y = _ffn(y, h, *f2, l)
    return y.reshape(BATCH, SEQ, D_MODEL)
```

```python
import math

import numpy as np
import jax
import jax.numpy as jnp
from jax import lax
from jax.experimental import pallas as pl
from jax.experimental.pallas import tpu as pltpu

F32 = jnp.float32
BF16 = jnp.bfloat16

D_MODEL = 2048
BATCH = 4
SEQ = 2048
DEPTH = 2
TOK = BATCH * SEQ
HEAD_DIM = 64
N_HEADS = 16
N_KV = 4
HPG = 4
NSA_W = 1024
KV_W = 256
CMP_BLOCK = 32
CMP_STRIDE = 16
CMP_HIDDEN = 256
SEL_BLOCK = 64
N_SEL = 16
WINDOW = 512
GM_W = 1024
GM_CHUNK = 128
GM_GW = 128
GM_GROUPS = 8
N_BUCKETS = 32
MAX_DISTANCE = 128
D_FF = 5504
EPS = 1e-6
NEG = -1e30
FORCE = 1e4
OFF_KV = NSA_W
OFF_NG = OFF_KV + 6 * KV_W
OFF_UV = OFF_NG + 3 * N_HEADS
OFF_MG = OFF_UV + 2 * GM_W
LOG2E = 1.4426950408889634

LANE = 128
VMEM_LIMIT = 56 * 1024 * 1024

FFN_TF = 512
FFN_UP_TM = 2048
NORM_TM = 512
FFN_DN_TM = 512
FFN_DN_TN = 512
PIN_TM = 1024
PIN_TN = 1024
PIN_RC = 256
SEC_W = PIN_TN // 2
Z_MG = 0
Z_U = Z_MG + 2 * D_MODEL
Z_V = Z_U + GM_W
Z_Q = Z_V + GM_W
Z_KV = Z_Q + NSA_W
Z_NG = Z_KV + 6 * KV_W
Z_W = Z_NG + SEC_W
KV_GW = 2 * HEAD_DIM
GATE_GW = 16
TQ = 128
NQ = SEQ // TQ
NK = SEQ // TQ
QL = HPG * TQ
N_CMP = 128
CMP_BAND = 16
MASKED = -1
N_SBLK = SEQ // SEL_BLOCK
V_ROWS = HEAD_DIM + 16
FAR_GROUP = 4
Q_PAIR = 4
SGU_TM = 512
OUT_TM = 256


def _cparams(sem):
    return pltpu.CompilerParams(dimension_semantics=sem, vmem_limit_bytes=VMEM_LIMIT)


def _cast_rows(src_ref, dst_ref, chunk):
    def body(r, carry):
        rows = pl.ds(pl.multiple_of(r * chunk, chunk), chunk)
        dst_ref[rows, :] = src_ref[rows, :].astype(BF16)
        return carry

    lax.fori_loop(0, src_ref.shape[0] // chunk, body, 0)


def _rms_bf16(x, gain):
    ms = jnp.mean(x * x, axis=-1, keepdims=True)
    return (x * lax.rsqrt(ms + EPS) * gain).astype(BF16)


def _norm_body(x_ref, g_ref, h_ref):
    h_ref[...] = _rms_bf16(x_ref[...], g_ref[...])


def _rmsnorm(x, gain):
    return pl.pallas_call(
        _norm_body,
        grid=(TOK // NORM_TM,),
        in_specs=[pl.BlockSpec((NORM_TM, D_MODEL), lambda i: (i, 0)),
                  pl.BlockSpec((1, D_MODEL), lambda i: (0, 0))],
        out_specs=pl.BlockSpec((NORM_TM, D_MODEL), lambda i: (i, 0)),
        out_shape=jax.ShapeDtypeStruct((TOK, D_MODEL), BF16),
        compiler_params=_cparams(("parallel",)),
        name="rmsnorm",
    )(x, gain)


def _ffn_up_body(h_ref, wg_ref, wu_ref, a_ref, wgb_ref, wub_ref):
    @pl.when(pl.program_id(1) == 0)
    def _():
        _cast_rows(wg_ref, wgb_ref, LANE)
        _cast_rows(wu_ref, wub_ref, LANE)

    h = h_ref[...]
    g = jnp.dot(h, wgb_ref[...], preferred_element_type=F32)
    u = jnp.dot(h, wub_ref[...], preferred_element_type=F32)
    a_ref[...] = (g * jax.nn.sigmoid(g) * u).astype(BF16)


def _ffn_down_body(a_ref, wd_ref, x_ref, o_ref, wdb_ref):
    @pl.when(pl.program_id(1) == 0)
    def _():
        _cast_rows(wd_ref, wdb_ref, LANE)

    o_ref[...] = x_ref[...] + 0.5 * jnp.dot(a_ref[...], wdb_ref[...], preferred_element_type=F32)


def _ffn(x, h, wg, wu, wd, l):
    nf = pl.cdiv(D_FF, FFN_TF)
    a = pl.pallas_call(
        _ffn_up_body,
        grid=(nf, TOK // FFN_UP_TM),
        in_specs=[
            pl.BlockSpec((FFN_UP_TM, D_MODEL), lambda f, m: (m, 0)),
            pl.BlockSpec((None, D_MODEL, FFN_TF), lambda f, m: (l, 0, f)),
            pl.BlockSpec((None, D_MODEL, FFN_TF), lambda f, m: (l, 0, f)),
        ],
        out_specs=pl.BlockSpec((FFN_UP_TM, FFN_TF), lambda f, m: (m, f)),
        out_shape=jax.ShapeDtypeStruct((TOK, D_FF), BF16),
        scratch_shapes=[pltpu.VMEM((D_MODEL, FFN_TF), BF16), pltpu.VMEM((D_MODEL, FFN_TF), BF16)],
        compiler_params=_cparams(("arbitrary", "arbitrary")),
        name="ffn_up",
    )(h, wg, wu)
    return pl.pallas_call(
        _ffn_down_body,
        grid=(D_MODEL // FFN_DN_TN, TOK // FFN_DN_TM),
        in_specs=[
            pl.BlockSpec((FFN_DN_TM, D_FF), lambda n, m: (m, 0)),
            pl.BlockSpec((None, D_FF, FFN_DN_TN), lambda n, m: (l, 0, n)),
            pl.BlockSpec((FFN_DN_TM, FFN_DN_TN), lambda n, m: (m, n)),
        ],
        out_specs=pl.BlockSpec((FFN_DN_TM, FFN_DN_TN), lambda n, m: (m, n)),
        out_shape=jax.ShapeDtypeStruct((TOK, D_MODEL), F32),
        scratch_shapes=[pltpu.VMEM((D_FF, FFN_DN_TN), BF16)],
        compiler_params=_cparams(("arbitrary", "arbitrary")),
        name="ffn_down",
    )(a, wd, x)


def _head_rms(y, bd):
    y2 = (y * y).astype(BF16)
    ss = jnp.concatenate(
        [jnp.dot(y2[:, k * LANE:(k + 1) * LANE], bd, preferred_element_type=F32)
         for k in range(y.shape[1] // LANE)], axis=1)
    return y * lax.rsqrt(ss * (1.0 / HEAD_DIM) + EPS)


def _pin_body(x_ref, g_ref, w_ref, bd_ref, qg_ref, k1g_ref, k2g_ref, o_ref, h_ref):
    j = pl.program_id(1)

    @pl.when(j == 0)
    def _():
        x = x_ref[...]
        ms = jnp.mean(x * x, axis=-1, keepdims=True)
        h_ref[...] = (x * lax.rsqrt(ms + EPS) * g_ref[...]).astype(BF16)

    t_u = Z_U // PIN_TN
    t_q = Z_Q // PIN_TN
    t_kv = Z_KV // PIN_TN
    half = SEC_W

    def emit(act):
        for r in range(PIN_TM // PIN_RC):
            rows = slice(r * PIN_RC, (r + 1) * PIN_RC)
            o_ref[rows, :] = act(jnp.dot(h_ref[rows, :], w_ref[...], preferred_element_type=F32))

    def halves(act_lo, act_hi):
        return lambda y: jnp.concatenate([act_lo(y[:, :half]), act_hi(y[:, half:])], axis=1)

    def kv_act(kg_ref):
        is_k = (lax.broadcasted_iota(jnp.int32, (1, half), 1) & (KV_GW - 1)) < HEAD_DIM
        return lambda y: jnp.where(is_k, _head_rms(y, bd_ref[...]) * kg_ref[...], y)

    @pl.when(j < t_u)
    def _():
        emit(jax.nn.sigmoid)

    @pl.when((j >= t_u) & (j < t_q))
    def _():
        emit(jax.nn.gelu)

    @pl.when((j >= t_q) & (j < t_kv))
    def _():
        emit(lambda y: _head_rms(y, bd_ref[...]) * (qg_ref[...] * (HEAD_DIM ** -0.5 * LOG2E)))

    @pl.when(j == t_kv)
    def _():
        emit(halves(lambda y: y, kv_act(k1g_ref)))

    @pl.when(j == t_kv + 1)
    def _():
        emit(halves(kv_act(k2g_ref), jax.nn.sigmoid))


def _proj_in(x, gain, w, l, bd, qg, k1g, k2g):
    return pl.pallas_call(
        _pin_body,
        grid=(TOK // PIN_TM, Z_W // PIN_TN),
        in_specs=[
            pl.BlockSpec((PIN_TM, D_MODEL), lambda i, j: (i, 0)),
            pl.BlockSpec((1, D_MODEL), lambda i, j: (0, 0)),
            pl.BlockSpec((None, D_MODEL, PIN_TN), lambda i, j: (l, 0, j)),
            pl.BlockSpec((LANE, LANE), lambda i, j: (0, 0)),
            pl.BlockSpec((1, PIN_TN), lambda i, j: (0, 0)),
            pl.BlockSpec((1, SEC_W), lambda i, j: (0, 0)),
            pl.BlockSpec((1, SEC_W), lambda i, j: (0, 0)),
        ],
        out_specs=pl.BlockSpec((PIN_TM, PIN_TN), lambda i, j: (i, j)),
        out_shape=jax.ShapeDtypeStruct((TOK, Z_W), F32),
        scratch_shapes=[pltpu.VMEM((PIN_TM, D_MODEL), BF16)],
        compiler_params=_cparams(("parallel", "arbitrary")),
        name="proj_in",
    )(x, gain, w, bd, qg, k1g, k2g)


def _cmp_body(z_ref, pos_ref, w1_ref, w2k_ref, w2vT_ref, kg_ref, kc_ref, vcT_ref):
    half = CMP_STRIDE * HEAD_DIM
    lane = lax.broadcasted_iota(jnp.int32, (N_CMP, LANE), 1)
    first = lane < HEAD_DIM
    xk, xv = [], []
    for j in range(CMP_STRIDE // 2):
        ev = z_ref[pl.ds(2 * j, N_CMP, stride=CMP_STRIDE), :]
        od = z_ref[pl.ds(2 * j + 1, N_CMP, stride=CMP_STRIDE), :]
        xk.append(jnp.where(first, ev, pltpu.roll(od, HEAD_DIM, 1)))
        xv.append(jnp.where(first, pltpu.roll(ev, HEAD_DIM, 1), od))

    def hidden(x, kind):
        xa = (x + pos_ref[kind, :, :half]).astype(BF16)
        xb = (x + pos_ref[kind, :, half:]).astype(BF16)
        a = jnp.dot(xa, w1_ref[kind, :half, :], preferred_element_type=F32)
        b = jnp.dot(xb, w1_ref[kind, half:, :], preferred_element_type=F32)
        hid = a + pltpu.roll(b, N_CMP - 1, 0)
        return (hid * jax.nn.sigmoid(hid)).astype(BF16)

    yk = jnp.dot(hidden(jnp.concatenate(xk, axis=1), 0), w2k_ref[...], preferred_element_type=F32)
    ms = jnp.mean(yk * yk, axis=-1, keepdims=True)
    kc_ref[0, 0] = (yk * lax.rsqrt(ms + EPS) * kg_ref[...]).astype(BF16)
    hv = hidden(jnp.concatenate(xv, axis=1), 1)
    vcT_ref[0, 0] = lax.dot_general(w2vT_ref[...], hv, (((1,), (1,)), ((), ())),
                                    preferred_element_type=F32).astype(BF16)


def _compress(z, pos, w1, w2k, w2vT, kg):
    return pl.pallas_call(
        _cmp_body,
        grid=(BATCH, N_KV),
        in_specs=[
            pl.BlockSpec((SEQ, KV_GW), lambda b, g: (b, Z_KV // KV_GW + g)),
            pl.BlockSpec((2, 1, CMP_BLOCK * HEAD_DIM), lambda b, g: (0, 0, 0)),
            pl.BlockSpec((2, CMP_BLOCK * HEAD_DIM, CMP_HIDDEN), lambda b, g: (0, 0, 0)),
            pl.BlockSpec((CMP_HIDDEN, HEAD_DIM), lambda b, g: (0, 0)),
            pl.BlockSpec((HEAD_DIM, CMP_HIDDEN), lambda b, g: (0, 0)),
            pl.BlockSpec((1, HEAD_DIM), lambda b, g: (0, 0)),
        ],
        out_specs=[
            pl.BlockSpec((1, 1, N_CMP, HEAD_DIM), lambda b, g: (b, g, 0, 0)),
            pl.BlockSpec((1, 1, HEAD_DIM, N_CMP), lambda b, g: (b, g, 0, 0)),
        ],
        out_shape=[
            jax.ShapeDtypeStruct((BATCH, N_KV, N_CMP, HEAD_DIM), BF16),
            jax.ShapeDtypeStruct((BATCH, N_KV, HEAD_DIM, N_CMP), BF16),
        ],
        compiler_params=_cparams(("parallel", "parallel")),
        name="compress",
    )(z, pos, w1, w2k, w2vT, kg)


def _np_bucket(dist):
    n = np.maximum(dist, 0)
    max_exact = N_BUCKETS // 2
    nf = np.maximum(n, 1).astype(np.float32)
    large = max_exact + (np.log(nf / np.float32(max_exact)) / np.float32(math.log(MAX_DISTANCE / max_exact))
                         * np.float32(N_BUCKETS - max_exact)).astype(np.int32)
    large = np.minimum(large, N_BUCKETS - 1)
    return np.where(n < max_exact, n, large).astype(np.int32)


def _bucket_tables():
    kj = np.arange(TQ)[:, None]
    qi = np.arange(TQ)[None, :]
    near = np.stack([np.where(qi >= kj, _np_bucket(qi - kj), MASKED), _np_bucket(TQ + qi - kj)])
    start = np.array([_cmp_band_start(i) for i in range(NQ)])[:, None, None]
    c = start + np.arange(CMP_BAND)[None, :, None]
    t = (np.arange(NQ)[:, None, None] * TQ + qi[None])
    dist_band = t - (c * CMP_STRIDE + CMP_BLOCK - 1)
    cmp_b = np.where(dist_band >= 0, _np_bucket(dist_band), MASKED)
    dist = t - (np.arange(N_CMP)[None, :, None] * CMP_STRIDE + CMP_BLOCK - 1)
    for i in range(NQ):
        s0 = _cmp_band_start(i)
        assert np.all(dist[i, :s0] >= 0) and np.all(_np_bucket(dist[i, :s0]) == N_BUCKETS - 1)
        assert np.all(dist[i, s0 + CMP_BAND:] < 0)
    return near.astype(np.int32), cmp_b.astype(np.int32)


def _cmp_band_start(i):
    return max((TQ // CMP_STRIDE) * i - 8, 0)


def _bias_body(rb_ref, near_ref, cmpb_ref, t0_ref, t1_ref, tc_ref):
    g = pl.program_id(0)

    def lut(bk, h):
        val = jnp.full(bk.shape, NEG, F32)
        for b in range(N_BUCKETS):
            val = jnp.where(bk == b, rb_ref[b, h], val)
        return val

    for hh in range(HPG):
        h = g * HPG + hh
        sl = slice(hh * TQ, (hh + 1) * TQ)
        far = rb_ref[N_BUCKETS - 1, h]
        t0_ref[0, :, sl] = (lut(near_ref[0], h) - far) * LOG2E
        t1_ref[0, :, sl] = (lut(near_ref[1], h) - far) * LOG2E

        def body(i, carry):
            start = pl.multiple_of(jnp.maximum((TQ // CMP_STRIDE) * i - 8, 0), 8)
            row = lax.broadcasted_iota(jnp.int32, (N_CMP, TQ), 0)
            tc_ref[0, i, :, sl] = jnp.where(row < start, far * LOG2E, NEG)
            tc_ref[0, i, pl.ds(start, CMP_BAND), sl] = lut(cmpb_ref[i], h) * LOG2E
            return carry

        lax.fori_loop(0, NQ, body, 0)


def _bias_tables(rel_bias, near, cmpb):
    return pl.pallas_call(
        _bias_body,
        grid=(N_KV,),
        in_specs=[
            pl.BlockSpec(memory_space=pltpu.SMEM),
            pl.BlockSpec((2, TQ, TQ), lambda g: (0, 0, 0)),
            pl.BlockSpec((NQ, CMP_BAND, TQ), lambda g: (0, 0, 0)),
        ],
        out_specs=[
            pl.BlockSpec((1, TQ, QL), lambda g: (g, 0, 0)),
            pl.BlockSpec((1, TQ, QL), lambda g: (g, 0, 0)),
            pl.BlockSpec((1, NQ, N_CMP, QL), lambda g: (g, 0, 0, 0)),
        ],
        out_shape=[
            jax.ShapeDtypeStruct((N_KV, TQ, QL), F32),
            jax.ShapeDtypeStruct((N_KV, TQ, QL), F32),
            jax.ShapeDtypeStruct((N_KV, NQ, N_CMP, QL), F32),
        ],
        compiler_params=_cparams(("arbitrary",)),
        name="bias_tables",
    )(rel_bias, near, cmpb)


def _attn_body(zq_ref, zs_ref, zw_ref, zg_ref, kc_ref, vcT_ref, t0_ref, t1_ref, tc_ref, o_ref,
               ks_ref, vsT_ref, kw_ref, vwT_ref, sel_ref, sfar_ref, gsc_ref, psum_ref):
    g = pl.program_id(1)
    ip = pl.program_id(2)

    @pl.when(ip == 0)
    def _():
        ones = jnp.ones((V_ROWS - HEAD_DIM, TQ), BF16)

        def fill(n, carry):
            for z_ref, k_ref, vT_ref in ((zs_ref, ks_ref, vsT_ref), (zw_ref, kw_ref, vwT_ref)):
                blk = z_ref[pl.ds(pl.multiple_of(n * TQ, TQ), TQ), :]
                k_ref[n] = blk[:, :HEAD_DIM].astype(BF16)
                vT_ref[n, :HEAD_DIM, :] = blk.T[HEAD_DIM:, :].astype(BF16)
                vT_ref[n, HEAD_DIM:, :] = ones
            return carry

        lax.fori_loop(0, NK, fill, 0)

    blk = lax.broadcasted_iota(jnp.int32, (N_SBLK, TQ), 0)
    ql = lax.broadcasted_iota(jnp.int32, (N_SBLK, TQ), 1)
    key1 = lax.broadcasted_iota(jnp.int32, (TQ, TQ), 0)
    q1 = lax.broadcasted_iota(jnp.int32, (TQ, TQ), 1)
    nlast = WINDOW // TQ

    def sel_mask(t, n):
        r0 = sel_ref[t, pl.ds(2 * n, 1), :]
        r1 = sel_ref[t, pl.ds(2 * n + 1, 1), :]
        return jnp.where(key1 < SEL_BLOCK, r0, r1) > 0.5

    def scores(qT, k_ref, n, bias, mask):
        sc = jnp.dot(k_ref[n], qT, preferred_element_type=F32)
        if bias is not None:
            sc = sc + bias
        if mask is None:
            return sc
        return jnp.concatenate(
            [jnp.where(mask, sc[:, h * TQ:(h + 1) * TQ], NEG) for h in range(HPG)], axis=1)

    def colmax8(sc):
        return jnp.max(sc.reshape(TQ // 8, 8, QL), axis=0)

    def probs(sc, mrow):
        return jnp.exp2(sc - mrow).astype(BF16)

    def pv(vT_ref, ns, ps):
        vv = vT_ref[ns[0]] if len(ns) == 1 else jnp.concatenate([vT_ref[n] for n in ns], axis=1)
        pp = ps[0] if len(ps) == 1 else jnp.concatenate(ps, axis=0)
        return jnp.dot(vv, pp, preferred_element_type=F32)

    def finish(acc):
        return acc[:HEAD_DIM] * (1.0 / acc[HEAD_DIM:HEAD_DIM + 1])

    def stage_q(t):
        i = Q_PAIR * ip + t
        rows = slice(t * TQ, (t + 1) * TQ)
        qt = zq_ref[rows, :].T
        qT = jnp.concatenate([qt[h * HEAD_DIM:(h + 1) * HEAD_DIM] for h in range(HPG)], axis=1).astype(BF16)
        gsc_ref[t] = zg_ref[rows, :].T
        return dict(i=i, qT=qT, n1=jnp.maximum(i - 1, 0), i1=jnp.full((TQ, TQ), i, jnp.int32),
                    nfar=jnp.maximum(i - 1, 0))

    def stage_cmp_scores(t, tl):
        tl["s_c"] = jnp.dot(kc_ref[0, 0], tl["qT"], preferred_element_type=F32) + tc_ref[0, t]
        tl["m_c"] = jnp.maximum(jnp.max(colmax8(tl["s_c"]), axis=0, keepdims=True), 0.1 * NEG)

    def stage_cmp(t, tl):
        e = jnp.exp2(tl["s_c"] - tl["m_c"])
        l = jnp.sum(e, axis=0, keepdims=True)
        p = e * jnp.where(l > 0.0, 1.0 / l, 0.0)
        tl["ocT"] = jnp.dot(vcT_ref[0, 0], p.astype(BF16), preferred_element_type=F32)
        psum_ref[t] = p[:, 0:TQ] + p[:, TQ:2 * TQ] + p[:, 2 * TQ:3 * TQ] + p[:, 3 * TQ:4 * TQ]

    def stage_select(t, tl):
        i = tl["i"]
        ratio = SEL_BLOCK // CMP_STRIDE
        part = [psum_ref[t, pl.ds(r, N_SBLK, stride=ratio), :] for r in range(ratio)]
        prev_last = jnp.where(blk == 0, 0.0, pltpu.roll(part[3], 1, 0))
        imp = (part[0] + part[1] + part[2]) + 0.5 * (part[3] + prev_last)
        cur = jnp.full((N_SBLK, TQ), i, jnp.int32) * (TQ // SEL_BLOCK) + ql // SEL_BLOCK
        forced = (blk == 0) | (blk == cur) | (blk == cur - 1)
        score = jnp.where(blk <= cur, jnp.where(forced, FORCE, imp), NEG)
        sub = 8
        pieces = [score[k * sub:(k + 1) * sub, :] for k in range(N_SBLK // sub)]
        ranks = [jnp.zeros((sub, TQ), F32) for _ in pieces]
        for jj in range(N_SBLK):
            row = score[jj:jj + 1, :]
            for k, pc in enumerate(pieces):
                if k * sub > jj:
                    beats = row >= pc
                elif (k + 1) * sub - 1 < jj:
                    beats = row > pc
                else:
                    beats = (row > pc) | ((row == pc) & (blk[k * sub:(k + 1) * sub, :] > jj))
                ranks[k] = ranks[k] + jnp.where(beats, 1.0, 0.0)
        rank = jnp.concatenate(ranks, axis=0)
        sel_ref[t] = jnp.where((rank < N_SEL) & (blk <= cur), 1.0, 0.0)

    def stage_window_scores(t, tl):
        i, n1, i1, qT = tl["i"], tl["n1"], tl["i1"], tl["qT"]
        n2, n3, n4 = jnp.maximum(i - 2, 0), jnp.maximum(i - 3, 0), jnp.maximum(i - nlast, 0)
        w_d = scores(qT, kw_ref, i, t0_ref[0], None)
        w_1 = scores(qT, kw_ref, n1, t1_ref[0], i1 >= 1)
        w_2 = scores(qT, kw_ref, n2, None, i1 >= 2)
        w_3 = scores(qT, kw_ref, n3, None, i1 >= 3)
        w_4 = scores(qT, kw_ref, n4, None, (q1 < key1) & (i1 >= nlast))
        mxw = jnp.maximum(jnp.maximum(jnp.maximum(colmax8(w_d), colmax8(w_1)),
                                      jnp.maximum(colmax8(w_2), colmax8(w_3))), colmax8(w_4))
        tl["w"] = (w_d, w_1, w_2, w_3, w_4)
        tl["wn"] = (i, n1, n2, n3, n4)
        tl["m_w"] = jnp.max(mxw, axis=0, keepdims=True)

    def stage_window_pv(t, tl):
        (w_d, w_1, w_2, w_3, w_4), (i, n1, n2, n3, n4), m_w = tl["w"], tl["wn"], tl["m_w"]
        tl["owT"] = finish(pv(vwT_ref, [i], [probs(w_d, m_w)])
                           + pv(vwT_ref, [n1, n2], [probs(w_1, m_w), probs(w_2, m_w)])
                           + pv(vwT_ref, [n3, n4], [probs(w_3, m_w), probs(w_4, m_w)]))

    def stage_near(t, tl):
        i, n1, i1, qT = tl["i"], tl["n1"], tl["i1"], tl["qT"]
        tl["s_d"] = scores(qT, ks_ref, i, t0_ref[0], None)
        tl["s_1"] = scores(qT, ks_ref, n1, t1_ref[0], sel_mask(t, n1) & (i1 >= 1))
        tl["mx"] = jnp.maximum(colmax8(tl["s_d"]), colmax8(tl["s_1"]))

    tiles = [stage_q(t) for t in range(Q_PAIR)]
    for stage in (stage_cmp_scores, stage_cmp, stage_window_scores, stage_select, stage_near, stage_window_pv):
        for t, tl in enumerate(tiles):
            stage(t, tl)
    assert Q_PAIR == FAR_GROUP == 4
    extra = (2, 3)

    def far_scores(tr, mxs, ts):
        out = list(mxs)
        for t in ts:
            tl = tiles[t]
            for c in range(FAR_GROUP):
                n = FAR_GROUP * tr + c
                sc = scores(tl["qT"], ks_ref, n, None,
                            sel_mask(t, n) & (jnp.full((TQ, TQ), n, jnp.int32) < tl["nfar"]))
                sfar_ref[t, n] = sc
                out[t] = jnp.maximum(out[t], colmax8(sc))
        return tuple(out)

    every = tuple(range(Q_PAIR))
    mxs = lax.fori_loop(0, ip, lambda tr, c: far_scores(tr, c, every), tuple(tl["mx"] for tl in tiles))
    mxs = far_scores(ip, mxs, extra)
    m_s = [jnp.max(mx, axis=0, keepdims=True) for mx in mxs]
    accs = tuple(pv(vsT_ref, [tl["i"], tl["n1"]], [probs(tl["s_d"], m_s[t]), probs(tl["s_1"], m_s[t])])
                 for t, tl in enumerate(tiles))

    def far_pv(tr, accs_, ts):
        ns = [FAR_GROUP * tr + c for c in range(FAR_GROUP)]
        out = list(accs_)
        for t in ts:
            out[t] = out[t] + pv(vsT_ref, ns, [probs(sfar_ref[t, n], m_s[t]) for n in ns])
        return tuple(out)

    accs = lax.fori_loop(0, ip, lambda tr, c: far_pv(tr, c, every), accs)
    accs = far_pv(ip, accs, extra)

    for t, tl in enumerate(tiles):
        base = g * GATE_GW
        gate = [jnp.concatenate([gsc_ref[t, pl.ds(base + br * HPG + h, 1), :] for h in range(HPG)], axis=1)
                for br in range(3)]
        aT = gate[0] * tl["ocT"] + gate[1] * finish(accs[t]) + gate[2] * tl["owT"]
        a4 = jnp.concatenate([aT[:, h * TQ:(h + 1) * TQ] for h in range(HPG)], axis=0)
        o_ref[t * TQ:(t + 1) * TQ, :] = a4.T.astype(o_ref.dtype)


def _attention(z, kc, vcT, t0, t1, tc):
    qw = HPG * HEAD_DIM
    rows = Q_PAIR * TQ
    npair = NQ // Q_PAIR
    return pl.pallas_call(
        _attn_body,
        grid=(BATCH, N_KV, npair),
        in_specs=[
            pl.BlockSpec((rows, qw), lambda b, g, i: (b * npair + i, Z_Q // qw + g)),
            pl.BlockSpec((SEQ, KV_GW), lambda b, g, i: (b, (Z_KV + 2 * KV_W) // KV_GW + g)),
            pl.BlockSpec((SEQ, KV_GW), lambda b, g, i: (b, (Z_KV + 4 * KV_W) // KV_GW + g)),
            pl.BlockSpec((rows, LANE), lambda b, g, i: (b * npair + i, Z_NG // LANE)),
            pl.BlockSpec((1, 1, N_CMP, HEAD_DIM), lambda b, g, i: (b, g, 0, 0)),
            pl.BlockSpec((1, 1, HEAD_DIM, N_CMP), lambda b, g, i: (b, g, 0, 0)),
            pl.BlockSpec((1, TQ, QL), lambda b, g, i: (g, 0, 0)),
            pl.BlockSpec((1, TQ, QL), lambda b, g, i: (g, 0, 0)),
            pl.BlockSpec((1, Q_PAIR, N_CMP, QL), lambda b, g, i: (g, i, 0, 0)),
        ],
        out_specs=pl.BlockSpec((rows, qw), lambda b, g, i: (b * npair + i, g)),
        out_shape=jax.ShapeDtypeStruct((TOK, NSA_W), BF16),
        scratch_shapes=[
            pltpu.VMEM((NK, TQ, HEAD_DIM), BF16),
            pltpu.VMEM((NK, V_ROWS, TQ), BF16),
            pltpu.VMEM((NK, TQ, HEAD_DIM), BF16),
            pltpu.VMEM((NK, V_ROWS, TQ), BF16),
            pltpu.VMEM((Q_PAIR, N_SBLK, TQ), F32),
            pltpu.VMEM((Q_PAIR, NK, TQ, QL), F32),
            pltpu.VMEM((Q_PAIR, LANE, TQ), F32),
            pltpu.VMEM((Q_PAIR, N_CMP, TQ), F32),
        ],
        compiler_params=_cparams(("parallel", "parallel", "arbitrary")),
        name="nsa_attention",
    )(z, z, z, z, kc, vcT, t0, t1, tc)


def _sgu_body(u_ref, v_ref, g_ref, b_ref, w_ref, bs_ref, o_ref):
    r = lax.broadcasted_iota(jnp.int32, (GM_CHUNK, GM_CHUNK), 0)
    c = lax.broadcasted_iota(jnp.int32, (GM_CHUNK, GM_CHUNK), 1)
    tril = r >= c
    ws = [jnp.where(tril, w_ref[gi], 0.0).astype(BF16) for gi in range(GM_GROUPS)]
    for ch in range(SGU_TM // GM_CHUNK):
        rows = slice(ch * GM_CHUNK, (ch + 1) * GM_CHUNK)
        v = v_ref[rows, :]
        mu = jnp.mean(v, axis=-1, keepdims=True)
        vc = v - mu
        var = jnp.mean(vc * vc, axis=-1, keepdims=True)
        vn = (vc * lax.rsqrt(var + EPS) * g_ref[...] + b_ref[...]).astype(BF16)
        for gi in range(GM_GROUPS):
            cols = slice(gi * GM_GW, (gi + 1) * GM_GW)
            mixed = jnp.dot(ws[gi], vn[:, cols], preferred_element_type=F32) + bs_ref[:, cols]
            o_ref[rows, cols] = (u_ref[rows, cols] * mixed).astype(o_ref.dtype)


def _sgu(z, ln_g, ln_b, w_s, bs_exp):
    return pl.pallas_call(
        _sgu_body,
        grid=(TOK // SGU_TM,),
        in_specs=[
            pl.BlockSpec((SGU_TM, GM_W), lambda i: (i, Z_U // GM_W)),
            pl.BlockSpec((SGU_TM, GM_W), lambda i: (i, Z_V // GM_W)),
            pl.BlockSpec((1, GM_W), lambda i: (0, 0)),
            pl.BlockSpec((1, GM_W), lambda i: (0, 0)),
            pl.BlockSpec((GM_GROUPS, GM_CHUNK, GM_CHUNK), lambda i: (0, 0, 0)),
            pl.BlockSpec((GM_CHUNK, GM_W), lambda i: (0, 0)),
        ],
        out_specs=pl.BlockSpec((SGU_TM, GM_W), lambda i: (i, 0)),
        out_shape=jax.ShapeDtypeStruct((TOK, GM_W), BF16),
        compiler_params=_cparams(("parallel",)),
        name="sgu",
    )(z, z, ln_g, ln_b, w_s, bs_exp)


def _out_body(a_ref, sg_ref, m1_ref, m2_ref, x_ref, wn_ref, ws_ref, wo_ref, gn_ref, o_ref, h_ref):
    pa = jnp.dot(a_ref[...], wn_ref[...], preferred_element_type=F32)
    ps = jnp.dot(sg_ref[...], ws_ref[...], preferred_element_type=F32)
    merged = (m1_ref[...] * pa + m2_ref[...] * ps).astype(BF16)
    y = x_ref[...] + jnp.dot(merged, wo_ref[...], preferred_element_type=F32)
    o_ref[...] = y
    h_ref[...] = _rms_bf16(y, gn_ref[...])


def _merge_out(a, sg, z, x, wn, ws, wo, l, next_gain):
    const = lambda i: (l, 0, 0)
    return pl.pallas_call(
        _out_body,
        grid=(TOK // OUT_TM,),
        in_specs=[
            pl.BlockSpec((OUT_TM, NSA_W), lambda i: (i, 0)),
            pl.BlockSpec((OUT_TM, GM_W), lambda i: (i, 0)),
            pl.BlockSpec((OUT_TM, D_MODEL), lambda i: (i, 0)),
            pl.BlockSpec((OUT_TM, D_MODEL), lambda i: (i, 1)),
            pl.BlockSpec((OUT_TM, D_MODEL), lambda i: (i, 0)),
            pl.BlockSpec((None, NSA_W, D_MODEL), const),
            pl.BlockSpec((None, GM_W, D_MODEL), const),
            pl.BlockSpec((None, D_MODEL, D_MODEL), const),
            pl.BlockSpec((1, D_MODEL), lambda i: (0, 0)),
        ],
        out_specs=[pl.BlockSpec((OUT_TM, D_MODEL), lambda i: (i, 0)),
                   pl.BlockSpec((OUT_TM, D_MODEL), lambda i: (i, 0))],
        out_shape=[jax.ShapeDtypeStruct((TOK, D_MODEL), F32),
                   jax.ShapeDtypeStruct((TOK, D_MODEL), BF16)],
        compiler_params=_cparams(("parallel",)),
        name="merge_out",
    )(a, sg, z, z, x, wn, ws, wo, next_gain)


def _check_overlap_stencil():
    ci = np.arange(N_CMP - 1)[None, :] * CMP_STRIDE
    sj = np.arange(N_SBLK)[:, None] * SEL_BLOCK
    ov = np.clip(np.minimum(ci + CMP_BLOCK, sj + SEL_BLOCK) - np.maximum(ci, sj), 0, None) / CMP_BLOCK
    stencil = np.zeros((N_SBLK, N_CMP - 1))
    for j in range(N_SBLK):
        for c, w in ((4 * j - 1, 0.5), (4 * j, 1.0), (4 * j + 1, 1.0), (4 * j + 2, 1.0), (4 * j + 3, 0.5)):
            if 0 <= c < N_CMP - 1:
                stencil[j, c] = w
    assert np.array_equal(ov, stencil)


def _block_diag_ones():
    r = np.arange(LANE)
    return (r[:, None] // HEAD_DIM == r[None, :] // HEAD_DIM).astype(np.float32)


def _prep_w_in(w):
    L = w.shape[0]
    w = w.astype(BF16)
    kv = w[:, :, OFF_KV:OFF_NG].reshape(L, D_MODEL, 3, 2, N_KV, HEAD_DIM)
    kv = kv.transpose(0, 1, 2, 4, 3, 5).reshape(L, D_MODEL, 6 * KV_W)
    ng = w[:, :, OFF_NG:OFF_UV].reshape(L, D_MODEL, N_KV, HPG, 3).transpose(0, 1, 2, 4, 3)
    ng = ng.reshape(L, D_MODEL, N_KV, 3 * HPG)
    ng = jnp.pad(ng, ((0, 0), (0, 0), (0, 0), (0, GATE_GW - 3 * HPG))).reshape(L, D_MODEL, N_KV * GATE_GW)
    ng = jnp.pad(ng, ((0, 0), (0, 0), (0, SEC_W - N_KV * GATE_GW)))
    return jnp.concatenate([w[:, :, OFF_MG:], w[:, :, OFF_UV:OFF_MG], w[:, :, :OFF_KV], kv, ng], axis=2)


def _mixer(x, z, l, p, tables):
    t0, t1, tc = tables
    pos = jnp.stack([p["cmp_pos_k"][l], p["cmp_pos_v"][l]]).reshape(2, 1, CMP_BLOCK * HEAD_DIM)
    w1 = jnp.stack([p["cmp_k_w1"][l], p["cmp_v_w1"][l]]).astype(BF16)
    kc, vcT = _compress(z, pos, w1, p["cmp_k_w2"][l].astype(BF16), p["cmp_v_w2"][l].T.astype(BF16),
                        p["k_norm"][l, 0].reshape(1, HEAD_DIM))
    a = _attention(z, kc, vcT, t0, t1, tc)
    bs_exp = jnp.repeat(p["sgu_b"][l].T, GM_GW, axis=1)
    sg = _sgu(z, p["sgu_norm_g"][l].reshape(1, GM_W), p["sgu_norm_b"][l].reshape(1, GM_W),
              p["sgu_w"][l], bs_exp)
    return _merge_out(a, sg, z, x, p["wn"], p["ws"], p["wo"], l, p["ffn2_norm"][l].reshape(1, D_MODEL))


def kernel(x, rel_bias, ffn1_norm, ffn1_w_gate, ffn1_w_up, ffn1_w_down, mix_norm, w_in, q_norm, k_norm, cmp_pos_k, cmp_pos_v, cmp_k_w1, cmp_k_w2, cmp_v_w1, cmp_v_w2, sgu_norm_g, sgu_norm_b, sgu_w, sgu_b, w_proj_nsa, w_proj_sgu, w_out, ffn2_norm, ffn2_w_gate, ffn2_w_up, ffn2_w_down):
    p = dict(cmp_pos_k=cmp_pos_k, cmp_pos_v=cmp_pos_v, cmp_k_w1=cmp_k_w1, cmp_k_w2=cmp_k_w2,
             cmp_v_w1=cmp_v_w1, cmp_v_w2=cmp_v_w2, k_norm=k_norm, sgu_norm_g=sgu_norm_g,
             sgu_norm_b=sgu_norm_b, sgu_w=sgu_w, sgu_b=sgu_b, wn=w_proj_nsa.astype(BF16),
             ws=w_proj_sgu.astype(BF16), wo=w_out.astype(BF16), ffn2_norm=ffn2_norm)
    near, cmpb = _bucket_tables()
    t0, t1, tc = _bias_tables(rel_bias, jnp.asarray(near), jnp.asarray(cmpb))
    tables = (t0, t1, tc)
    _check_overlap_stencil()
    bd = jnp.asarray(_block_diag_ones()).astype(BF16)
    w_in_r = _prep_w_in(w_in)
    f1 = (ffn1_w_gate, ffn1_w_up, ffn1_w_down)
    f2 = (ffn2_w_gate, ffn2_w_up, ffn2_w_down)

    y = x.reshape(TOK, D_MODEL)
    for l in range(DEPTH):
        y = _ffn(y, _rmsnorm(y, ffn1_norm[l].reshape(1, D_MODEL)), *f1, l)
        z = _proj_in(y, mix_norm[l].reshape(1, D_MODEL), w_in_r, l, bd,
                     jnp.tile(q_norm[l], PIN_TN // HEAD_DIM).reshape(1, PIN_TN),
                     jnp.tile(k_norm[l, 1], SEC_W // HEAD_DIM).reshape(1, SEC_W),
                     jnp.tile(k_norm[l, 2], SEC_W // HEAD_DIM).reshape(1, SEC_W))
        y, h = _mixer(y, z, l, p, tables)
        y = _ffn(y, h, *f2, l)
    return y.reshape(BATCH, SEQ, D_MODEL)
```

```python
import math

import numpy as np
import jax
import jax.numpy as jnp
from jax import lax
from jax.experimental import pallas as pl
from jax.experimental.pallas import tpu as pltpu

F32 = jnp.float32
BF16 = jnp.bfloat16

D_MODEL = 2048
BATCH = 4
SEQ = 2048
DEPTH = 2
TOK = BATCH * SEQ
HEAD_DIM = 64
N_HEADS = 16
N_KV = 4
HPG = 4
NSA_W = 1024
KV_W = 256
CMP_BLOCK = 32
CMP_STRIDE = 16
CMP_HIDDEN = 256
SEL_BLOCK = 64
N_SEL = 16
WINDOW = 512
GM_W = 1024
GM_CHUNK = 128
GM_GW = 128
GM_GROUPS = 8
N_BUCKETS = 32
MAX_DISTANCE = 128
D_FF = 5504
EPS = 1e-6
NEG = -1e30
FORCE = 1e4
OFF_KV = NSA_W
OFF_NG = OFF_KV + 6 * KV_W
OFF_UV = OFF_NG + 3 * N_HEADS
OFF_MG = OFF_UV + 2 * GM_W
LOG2E = 1.4426950408889634

LANE = 128
VMEM_LIMIT = 56 * 1024 * 1024

FFN_TF = 512
FFN_UP_TM = 1024
NORM_TM = 512
FFN_DN_TM = 512
FFN_DN_TN = 512
PIN_TM = 1024
PIN_TN = 1024
PIN_RC = 256
SEC_W = PIN_TN // 2
Z_MG = 0
Z_U = Z_MG + 2 * D_MODEL
Z_V = Z_U + GM_W
Z_Q = Z_V + GM_W
Z_KV = Z_Q + NSA_W
Z_NG = Z_KV + 6 * KV_W
Z_W = Z_NG + SEC_W
KV_GW = 2 * HEAD_DIM
GATE_GW = 16
TQ = 128
NQ = SEQ // TQ
NK = SEQ // TQ
QL = HPG * TQ
N_CMP = 128
CMP_BAND = 16
MASKED = -1
N_SBLK = SEQ // SEL_BLOCK
V_ROWS = HEAD_DIM + 16
FAR_GROUP = 4
Q_PAIR = 4
OUT_TM = 256


def _cparams(sem):
    return pltpu.CompilerParams(dimension_semantics=sem, vmem_limit_bytes=VMEM_LIMIT)


def _cast_rows(src_ref, dst_ref, chunk):
    def body(r, carry):
        rows = pl.ds(pl.multiple_of(r * chunk, chunk), chunk)
        dst_ref[rows, :] = src_ref[rows, :].astype(BF16)
        return carry

    lax.fori_loop(0, src_ref.shape[0] // chunk, body, 0)


def _rms_bf16(x, gain):
    ms = jnp.mean(x * x, axis=-1, keepdims=True)
    return (x * lax.rsqrt(ms + EPS) * gain).astype(BF16)


def _norm_body(x_ref, g_ref, h_ref):
    h_ref[...] = _rms_bf16(x_ref[...], g_ref[...])


def _rmsnorm(x, gain):
    return pl.pallas_call(
        _norm_body,
        grid=(TOK // NORM_TM,),
        in_specs=[pl.BlockSpec((NORM_TM, D_MODEL), lambda i: (i, 0)),
                  pl.BlockSpec((1, D_MODEL), lambda i: (0, 0))],
        out_specs=pl.BlockSpec((NORM_TM, D_MODEL), lambda i: (i, 0)),
        out_shape=jax.ShapeDtypeStruct((TOK, D_MODEL), BF16),
        compiler_params=_cparams(("parallel",)),
        name="rmsnorm",
    )(x, gain)


def _ffn_up_body(h_ref, wg_ref, wu_ref, a_ref, wgb_ref, wub_ref):
    @pl.when(pl.program_id(1) == 0)
    def _():
        _cast_rows(wg_ref, wgb_ref, LANE)
        _cast_rows(wu_ref, wub_ref, LANE)

    h = h_ref[...]
    g = jnp.dot(h, wgb_ref[...], preferred_element_type=F32)
    u = jnp.dot(h, wub_ref[...], preferred_element_type=F32)
    a_ref[...] = (g * jax.nn.sigmoid(g) * u).astype(BF16)


def _ffn_down_body(a_ref, wd_ref, x_ref, o_ref, wdb_ref):
    @pl.when(pl.program_id(1) == 0)
    def _():
        _cast_rows(wd_ref, wdb_ref, LANE)

    o_ref[...] = x_ref[...] + 0.5 * jnp.dot(a_ref[...], wdb_ref[...], preferred_element_type=F32)


def _ffn(x, h, wg, wu, wd, l):
    nf = pl.cdiv(D_FF, FFN_TF)
    a = pl.pallas_call(
        _ffn_up_body,
        grid=(nf, TOK // FFN_UP_TM),
        in_specs=[
            pl.BlockSpec((FFN_UP_TM, D_MODEL), lambda f, m: (m, 0)),
            pl.BlockSpec((None, D_MODEL, FFN_TF), lambda f, m: (l, 0, f)),
            pl.BlockSpec((None, D_MODEL, FFN_TF), lambda f, m: (l, 0, f)),
        ],
        out_specs=pl.BlockSpec((FFN_UP_TM, FFN_TF), lambda f, m: (m, f)),
        out_shape=jax.ShapeDtypeStruct((TOK, D_FF), BF16),
        scratch_shapes=[pltpu.VMEM((D_MODEL, FFN_TF), BF16), pltpu.VMEM((D_MODEL, FFN_TF), BF16)],
        compiler_params=_cparams(("arbitrary", "arbitrary")),
        name="ffn_up",
    )(h, wg, wu)
    return pl.pallas_call(
        _ffn_down_body,
        grid=(D_MODEL // FFN_DN_TN, TOK // FFN_DN_TM),
        in_specs=[
            pl.BlockSpec((FFN_DN_TM, D_FF), lambda n, m: (m, 0)),
            pl.BlockSpec((None, D_FF, FFN_DN_TN), lambda n, m: (l, 0, n)),
            pl.BlockSpec((FFN_DN_TM, FFN_DN_TN), lambda n, m: (m, n)),
        ],
        out_specs=pl.BlockSpec((FFN_DN_TM, FFN_DN_TN), lambda n, m: (m, n)),
        out_shape=jax.ShapeDtypeStruct((TOK, D_MODEL), F32),
        scratch_shapes=[pltpu.VMEM((D_FF, FFN_DN_TN), BF16)],
        compiler_params=_cparams(("arbitrary", "arbitrary")),
        name="ffn_down",
    )(a, wd, x)


def _head_rms(y, bd):
    y2 = (y * y).astype(BF16)
    ss = jnp.concatenate(
        [jnp.dot(y2[:, k * LANE:(k + 1) * LANE], bd, preferred_element_type=F32)
         for k in range(y.shape[1] // LANE)], axis=1)
    return y * lax.rsqrt(ss * (1.0 / HEAD_DIM) + EPS)


def _pin_body(x_ref, g_ref, w_ref, bd_ref, qg_ref, k1g_ref, k2g_ref, o_ref, h_ref):
    j = pl.program_id(1)

    @pl.when(j == 0)
    def _():
        x = x_ref[...]
        ms = jnp.mean(x * x, axis=-1, keepdims=True)
        h_ref[...] = (x * lax.rsqrt(ms + EPS) * g_ref[...]).astype(BF16)

    t_u = Z_U // PIN_TN
    t_q = Z_Q // PIN_TN
    t_kv = Z_KV // PIN_TN
    half = SEC_W

    def emit(act):
        for r in range(PIN_TM // PIN_RC):
            rows = slice(r * PIN_RC, (r + 1) * PIN_RC)
            o_ref[rows, :] = act(jnp.dot(h_ref[rows, :], w_ref[...], preferred_element_type=F32))

    def halves(act_lo, act_hi):
        return lambda y: jnp.concatenate([act_lo(y[:, :half]), act_hi(y[:, half:])], axis=1)

    def kv_act(kg_ref):
        is_k = (lax.broadcasted_iota(jnp.int32, (1, half), 1) & (KV_GW - 1)) < HEAD_DIM
        return lambda y: jnp.where(is_k, _head_rms(y, bd_ref[...]) * kg_ref[...], y)

    @pl.when(j < t_u)
    def _():
        emit(jax.nn.sigmoid)

    @pl.when((j >= t_u) & (j < t_q))
    def _():
        emit(jax.nn.gelu)

    @pl.when((j >= t_q) & (j < t_kv))
    def _():
        emit(lambda y: _head_rms(y, bd_ref[...]) * (qg_ref[...] * (HEAD_DIM ** -0.5 * LOG2E)))

    @pl.when(j == t_kv)
    def _():
        emit(halves(lambda y: y, kv_act(k1g_ref)))

    @pl.when(j == t_kv + 1)
    def _():
        emit(halves(kv_act(k2g_ref), jax.nn.sigmoid))


def _proj_in(x, gain, w, l, bd, qg, k1g, k2g):
    return pl.pallas_call(
        _pin_body,
        grid=(TOK // PIN_TM, Z_W // PIN_TN),
        in_specs=[
            pl.BlockSpec((PIN_TM, D_MODEL), lambda i, j: (i, 0)),
            pl.BlockSpec((1, D_MODEL), lambda i, j: (0, 0)),
            pl.BlockSpec((None, D_MODEL, PIN_TN), lambda i, j: (l, 0, j)),
            pl.BlockSpec((LANE, LANE), lambda i, j: (0, 0)),
            pl.BlockSpec((1, PIN_TN), lambda i, j: (0, 0)),
            pl.BlockSpec((1, SEC_W), lambda i, j: (0, 0)),
            pl.BlockSpec((1, SEC_W), lambda i, j: (0, 0)),
        ],
        out_specs=pl.BlockSpec((PIN_TM, PIN_TN), lambda i, j: (i, j)),
        out_shape=jax.ShapeDtypeStruct((TOK, Z_W), F32),
        scratch_shapes=[pltpu.VMEM((PIN_TM, D_MODEL), BF16)],
        compiler_params=_cparams(("parallel", "arbitrary")),
        name="proj_in",
    )(x, gain, w, bd, qg, k1g, k2g)


def _cmp_body(z_ref, pos_ref, w1_ref, w2k_ref, w2vT_ref, kg_ref, kc_ref, vcT_ref):
    half = CMP_STRIDE * HEAD_DIM
    lane = lax.broadcasted_iota(jnp.int32, (N_CMP, LANE), 1)
    first = lane < HEAD_DIM
    xk, xv = [], []
    for j in range(CMP_STRIDE // 2):
        ev = z_ref[pl.ds(2 * j, N_CMP, stride=CMP_STRIDE), :]
        od = z_ref[pl.ds(2 * j + 1, N_CMP, stride=CMP_STRIDE), :]
        xk.append(jnp.where(first, ev, pltpu.roll(od, HEAD_DIM, 1)))
        xv.append(jnp.where(first, pltpu.roll(ev, HEAD_DIM, 1), od))

    def hidden(x, kind):
        xa = (x + pos_ref[kind, :, :half]).astype(BF16)
        xb = (x + pos_ref[kind, :, half:]).astype(BF16)
        a = jnp.dot(xa, w1_ref[kind, :half, :], preferred_element_type=F32)
        b = jnp.dot(xb, w1_ref[kind, half:, :], preferred_element_type=F32)
        hid = a + pltpu.roll(b, N_CMP - 1, 0)
        return (hid * jax.nn.sigmoid(hid)).astype(BF16)

    yk = jnp.dot(hidden(jnp.concatenate(xk, axis=1), 0), w2k_ref[...], preferred_element_type=F32)
    ms = jnp.mean(yk * yk, axis=-1, keepdims=True)
    kc_ref[0, 0] = (yk * lax.rsqrt(ms + EPS) * kg_ref[...]).astype(BF16)
    hv = hidden(jnp.concatenate(xv, axis=1), 1)
    vcT_ref[0, 0] = lax.dot_general(w2vT_ref[...], hv, (((1,), (1,)), ((), ())),
                                    preferred_element_type=F32).astype(BF16)


def _compress(z, pos, w1, w2k, w2vT, kg):
    return pl.pallas_call(
        _cmp_body,
        grid=(BATCH, N_KV),
        in_specs=[
            pl.BlockSpec((SEQ, KV_GW), lambda b, g: (b, Z_KV // KV_GW + g)),
            pl.BlockSpec((2, 1, CMP_BLOCK * HEAD_DIM), lambda b, g: (0, 0, 0)),
            pl.BlockSpec((2, CMP_BLOCK * HEAD_DIM, CMP_HIDDEN), lambda b, g: (0, 0, 0)),
            pl.BlockSpec((CMP_HIDDEN, HEAD_DIM), lambda b, g: (0, 0)),
            pl.BlockSpec((HEAD_DIM, CMP_HIDDEN), lambda b, g: (0, 0)),
            pl.BlockSpec((1, HEAD_DIM), lambda b, g: (0, 0)),
        ],
        out_specs=[
            pl.BlockSpec((1, 1, N_CMP, HEAD_DIM), lambda b, g: (b, g, 0, 0)),
            pl.BlockSpec((1, 1, HEAD_DIM, N_CMP), lambda b, g: (b, g, 0, 0)),
        ],
        out_shape=[
            jax.ShapeDtypeStruct((BATCH, N_KV, N_CMP, HEAD_DIM), BF16),
            jax.ShapeDtypeStruct((BATCH, N_KV, HEAD_DIM, N_CMP), BF16),
        ],
        compiler_params=_cparams(("parallel", "parallel")),
        name="compress",
    )(z, pos, w1, w2k, w2vT, kg)


def _np_bucket(dist):
    n = np.maximum(dist, 0)
    max_exact = N_BUCKETS // 2
    nf = np.maximum(n, 1).astype(np.float32)
    large = max_exact + (np.log(nf / np.float32(max_exact)) / np.float32(math.log(MAX_DISTANCE / max_exact))
                         * np.float32(N_BUCKETS - max_exact)).astype(np.int32)
    large = np.minimum(large, N_BUCKETS - 1)
    return np.where(n < max_exact, n, large).astype(np.int32)


def _bucket_tables():
    kj = np.arange(TQ)[:, None]
    qi = np.arange(TQ)[None, :]
    near = np.stack([np.where(qi >= kj, _np_bucket(qi - kj), MASKED), _np_bucket(TQ + qi - kj)])
    start = np.array([_cmp_band_start(i) for i in range(NQ)])[:, None, None]
    c = start + np.arange(CMP_BAND)[None, :, None]
    t = (np.arange(NQ)[:, None, None] * TQ + qi[None])
    dist_band = t - (c * CMP_STRIDE + CMP_BLOCK - 1)
    cmp_b = np.where(dist_band >= 0, _np_bucket(dist_band), MASKED)
    dist = t - (np.arange(N_CMP)[None, :, None] * CMP_STRIDE + CMP_BLOCK - 1)
    for i in range(NQ):
        s0 = _cmp_band_start(i)
        assert np.all(dist[i, :s0] >= 0) and np.all(_np_bucket(dist[i, :s0]) == N_BUCKETS - 1)
        assert np.all(dist[i, s0 + CMP_BAND:] < 0)
    return near.astype(np.int32), cmp_b.astype(np.int32)


def _cmp_band_start(i):
    return max((TQ // CMP_STRIDE) * i - 8, 0)


def _bias_body(rb_ref, near_ref, cmpb_ref, t0_ref, t1_ref, tc_ref):
    g = pl.program_id(0)

    def lut(bk, h):
        val = jnp.full(bk.shape, NEG, F32)
        for b in range(N_BUCKETS):
            val = jnp.where(bk == b, rb_ref[b, h], val)
        return val

    for hh in range(HPG):
        h = g * HPG + hh
        sl = slice(hh * TQ, (hh + 1) * TQ)
        far = rb_ref[N_BUCKETS - 1, h]
        t0_ref[0, :, sl] = (lut(near_ref[0], h) - far) * LOG2E
        t1_ref[0, :, sl] = (lut(near_ref[1], h) - far) * LOG2E

        def body(i, carry):
            start = pl.multiple_of(jnp.maximum((TQ // CMP_STRIDE) * i - 8, 0), 8)
            row = lax.broadcasted_iota(jnp.int32, (N_CMP, TQ), 0)
            tc_ref[0, i, :, sl] = jnp.where(row < start, far * LOG2E, NEG)
            tc_ref[0, i, pl.ds(start, CMP_BAND), sl] = lut(cmpb_ref[i], h) * LOG2E
            return carry

        lax.fori_loop(0, NQ, body, 0)


def _bias_tables(rel_bias, near, cmpb):
    return pl.pallas_call(
        _bias_body,
        grid=(N_KV,),
        in_specs=[
            pl.BlockSpec(memory_space=pltpu.SMEM),
            pl.BlockSpec((2, TQ, TQ), lambda g: (0, 0, 0)),
            pl.BlockSpec((NQ, CMP_BAND, TQ), lambda g: (0, 0, 0)),
        ],
        out_specs=[
            pl.BlockSpec((1, TQ, QL), lambda g: (g, 0, 0)),
            pl.BlockSpec((1, TQ, QL), lambda g: (g, 0, 0)),
            pl.BlockSpec((1, NQ, N_CMP, QL), lambda g: (g, 0, 0, 0)),
        ],
        out_shape=[
            jax.ShapeDtypeStruct((N_KV, TQ, QL), F32),
            jax.ShapeDtypeStruct((N_KV, TQ, QL), F32),
            jax.ShapeDtypeStruct((N_KV, NQ, N_CMP, QL), F32),
        ],
        compiler_params=_cparams(("arbitrary",)),
        name="bias_tables",
    )(rel_bias, near, cmpb)


def _attn_body(zq_ref, zs_ref, zw_ref, zg_ref, kc_ref, vcT_ref, t0_ref, t1_ref, tc_ref, o_ref,
               ks_ref, vsT_ref, kw_ref, vwT_ref, sel_ref, sfar_ref, gsc_ref, psum_ref):
    g = pl.program_id(1)
    ip = pl.program_id(2)

    @pl.when(ip == 0)
    def _():
        ones = jnp.ones((V_ROWS - HEAD_DIM, TQ), BF16)

        def fill(n, carry):
            for z_ref, k_ref, vT_ref in ((zs_ref, ks_ref, vsT_ref), (zw_ref, kw_ref, vwT_ref)):
                blk = z_ref[pl.ds(pl.multiple_of(n * TQ, TQ), TQ), :]
                k_ref[n] = blk[:, :HEAD_DIM].astype(BF16)
                vT_ref[n, :HEAD_DIM, :] = blk.T[HEAD_DIM:, :].astype(BF16)
                vT_ref[n, HEAD_DIM:, :] = ones
            return carry

        lax.fori_loop(0, NK, fill, 0)

    blk = lax.broadcasted_iota(jnp.int32, (N_SBLK, TQ), 0)
    ql = lax.broadcasted_iota(jnp.int32, (N_SBLK, TQ), 1)
    key1 = lax.broadcasted_iota(jnp.int32, (TQ, TQ), 0)
    q1 = lax.broadcasted_iota(jnp.int32, (TQ, TQ), 1)
    nlast = WINDOW // TQ

    def sel_mask(t, n):
        r0 = sel_ref[t, pl.ds(2 * n, 1), :]
        r1 = sel_ref[t, pl.ds(2 * n + 1, 1), :]
        return jnp.where(key1 < SEL_BLOCK, r0, r1) > 0.5

    def scores(qT, k_ref, n, bias, mask):
        sc = jnp.dot(k_ref[n], qT, preferred_element_type=F32)
        if bias is not None:
            sc = sc + bias
        if mask is None:
            return sc
        return jnp.concatenate(
            [jnp.where(mask, sc[:, h * TQ:(h + 1) * TQ], NEG) for h in range(HPG)], axis=1)

    def colmax8(sc):
        return jnp.max(sc.reshape(TQ // 8, 8, QL), axis=0)

    def probs(sc, mrow):
        return jnp.exp2(sc - mrow).astype(BF16)

    def pv(vT_ref, ns, ps):
        vv = vT_ref[ns[0]] if len(ns) == 1 else jnp.concatenate([vT_ref[n] for n in ns], axis=1)
        pp = ps[0] if len(ps) == 1 else jnp.concatenate(ps, axis=0)
        return jnp.dot(vv, pp, preferred_element_type=F32)

    def finish(acc):
        return acc[:HEAD_DIM] * (1.0 / acc[HEAD_DIM:HEAD_DIM + 1])

    def stage_q(t):
        i = Q_PAIR * ip + t
        rows = slice(t * TQ, (t + 1) * TQ)
        qt = zq_ref[rows, :].T
        qT = jnp.concatenate([qt[h * HEAD_DIM:(h + 1) * HEAD_DIM] for h in range(HPG)], axis=1).astype(BF16)
        gsc_ref[t] = zg_ref[rows, :].T
        return dict(i=i, qT=qT, n1=jnp.maximum(i - 1, 0), i1=jnp.full((TQ, TQ), i, jnp.int32),
                    nfar=jnp.maximum(i - 1, 0))

    def stage_cmp_scores(t, tl):
        tl["s_c"] = jnp.dot(kc_ref[0, 0], tl["qT"], preferred_element_type=F32) + tc_ref[0, t]
        tl["m_c"] = jnp.maximum(jnp.max(colmax8(tl["s_c"]), axis=0, keepdims=True), 0.1 * NEG)

    def stage_cmp(t, tl):
        e = jnp.exp2(tl["s_c"] - tl["m_c"])
        l = jnp.sum(e, axis=0, keepdims=True)
        p = e * jnp.where(l > 0.0, 1.0 / l, 0.0)
        tl["ocT"] = jnp.dot(vcT_ref[0, 0], p.astype(BF16), preferred_element_type=F32)
        psum_ref[t] = p[:, 0:TQ] + p[:, TQ:2 * TQ] + p[:, 2 * TQ:3 * TQ] + p[:, 3 * TQ:4 * TQ]

    def stage_select(t, tl):
        i = tl["i"]
        ratio = SEL_BLOCK // CMP_STRIDE
        part = [psum_ref[t, pl.ds(r, N_SBLK, stride=ratio), :] for r in range(ratio)]
        prev_last = jnp.where(blk == 0, 0.0, pltpu.roll(part[3], 1, 0))
        imp = (part[0] + part[1] + part[2]) + 0.5 * (part[3] + prev_last)
        cur = jnp.full((N_SBLK, TQ), i, jnp.int32) * (TQ // SEL_BLOCK) + ql // SEL_BLOCK
        forced = (blk == 0) | (blk == cur) | (blk == cur - 1)
        score = jnp.where(blk <= cur, jnp.where(forced, FORCE, imp), NEG)
        sub = 8
        pieces = [score[k * sub:(k + 1) * sub, :] for k in range(N_SBLK // sub)]
        ranks = [jnp.zeros((sub, TQ), F32) for _ in pieces]
        for jj in range(N_SBLK):
            row = score[jj:jj + 1, :]
            for k, pc in enumerate(pieces):
                if k * sub > jj:
                    beats = row >= pc
                elif (k + 1) * sub - 1 < jj:
                    beats = row > pc
                else:
                    beats = (row > pc) | ((row == pc) & (blk[k * sub:(k + 1) * sub, :] > jj))
                ranks[k] = ranks[k] + jnp.where(beats, 1.0, 0.0)
        rank = jnp.concatenate(ranks, axis=0)
        sel_ref[t] = jnp.where((rank < N_SEL) & (blk <= cur), 1.0, 0.0)

    def stage_window_scores(t, tl):
        i, n1, i1, qT = tl["i"], tl["n1"], tl["i1"], tl["qT"]
        n2, n3, n4 = jnp.maximum(i - 2, 0), jnp.maximum(i - 3, 0), jnp.maximum(i - nlast, 0)
        w_d = scores(qT, kw_ref, i, t0_ref[0], None)
        w_1 = scores(qT, kw_ref, n1, t1_ref[0], i1 >= 1)
        w_2 = scores(qT, kw_ref, n2, None, i1 >= 2)
        w_3 = scores(qT, kw_ref, n3, None, i1 >= 3)
        w_4 = scores(qT, kw_ref, n4, None, (q1 < key1) & (i1 >= nlast))
        mxw = jnp.maximum(jnp.maximum(jnp.maximum(colmax8(w_d), colmax8(w_1)),
                                      jnp.maximum(colmax8(w_2), colmax8(w_3))), colmax8(w_4))
        tl["w"] = (w_d, w_1, w_2, w_3, w_4)
        tl["wn"] = (i, n1, n2, n3, n4)
        tl["m_w"] = jnp.max(mxw, axis=0, keepdims=True)

    def stage_window_pv(t, tl):
        (w_d, w_1, w_2, w_3, w_4), (i, n1, n2, n3, n4), m_w = tl["w"], tl["wn"], tl["m_w"]
        tl["owT"] = finish(pv(vwT_ref, [i], [probs(w_d, m_w)])
                           + pv(vwT_ref, [n1, n2], [probs(w_1, m_w), probs(w_2, m_w)])
                           + pv(vwT_ref, [n3, n4], [probs(w_3, m_w), probs(w_4, m_w)]))

    def stage_near(t, tl):
        i, n1, i1, qT = tl["i"], tl["n1"], tl["i1"], tl["qT"]
        tl["s_d"] = scores(qT, ks_ref, i, t0_ref[0], None)
        tl["s_1"] = scores(qT, ks_ref, n1, t1_ref[0], sel_mask(t, n1) & (i1 >= 1))
        tl["mx"] = jnp.maximum(colmax8(tl["s_d"]), colmax8(tl["s_1"]))

    tiles = [stage_q(t) for t in range(Q_PAIR)]
    for stage in (stage_cmp_scores, stage_cmp, stage_window_scores, stage_select, stage_near, stage_window_pv):
        for t, tl in enumerate(tiles):
            stage(t, tl)
    assert Q_PAIR == FAR_GROUP == 4
    extra = (2, 3)

    def far_scores(tr, mxs, ts):
        out = list(mxs)
        for t in ts:
            tl = tiles[t]
            for c in range(FAR_GROUP):
                n = FAR_GROUP * tr + c
                sc = scores(tl["qT"], ks_ref, n, None,
                            sel_mask(t, n) & (jnp.full((TQ, TQ), n, jnp.int32) < tl["nfar"]))
                sfar_ref[t, n] = sc
                out[t] = jnp.maximum(out[t], colmax8(sc))
        return tuple(out)

    every = tuple(range(Q_PAIR))
    mxs = lax.fori_loop(0, ip, lambda tr, c: far_scores(tr, c, every), tuple(tl["mx"] for tl in tiles))
    mxs = far_scores(ip, mxs, extra)
    m_s = [jnp.max(mx, axis=0, keepdims=True) for mx in mxs]
    accs = tuple(pv(vsT_ref, [tl["i"], tl["n1"]], [probs(tl["s_d"], m_s[t]), probs(tl["s_1"], m_s[t])])
                 for t, tl in enumerate(tiles))

    def far_pv(tr, accs_, ts):
        ns = [FAR_GROUP * tr + c for c in range(FAR_GROUP)]
        out = list(accs_)
        for t in ts:
            out[t] = out[t] + pv(vsT_ref, ns, [probs(sfar_ref[t, n], m_s[t]) for n in ns])
        return tuple(out)

    accs = lax.fori_loop(0, ip, lambda tr, c: far_pv(tr, c, every), accs)
    accs = far_pv(ip, accs, extra)

    for t, tl in enumerate(tiles):
        base = g * GATE_GW
        gate = [jnp.concatenate([gsc_ref[t, pl.ds(base + br * HPG + h, 1), :] for h in range(HPG)], axis=1)
                for br in range(3)]
        aT = gate[0] * tl["ocT"] + gate[1] * finish(accs[t]) + gate[2] * tl["owT"]
        a4 = jnp.concatenate([aT[:, h * TQ:(h + 1) * TQ] for h in range(HPG)], axis=0)
        o_ref[t * TQ:(t + 1) * TQ, :] = a4.T.astype(o_ref.dtype)


def _attention(z, kc, vcT, t0, t1, tc):
    qw = HPG * HEAD_DIM
    rows = Q_PAIR * TQ
    npair = NQ // Q_PAIR
    return pl.pallas_call(
        _attn_body,
        grid=(BATCH, N_KV, npair),
        in_specs=[
            pl.BlockSpec((rows, qw), lambda b, g, i: (b * npair + i, Z_Q // qw + g)),
            pl.BlockSpec((SEQ, KV_GW), lambda b, g, i: (b, (Z_KV + 2 * KV_W) // KV_GW + g)),
            pl.BlockSpec((SEQ, KV_GW), lambda b, g, i: (b, (Z_KV + 4 * KV_W) // KV_GW + g)),
            pl.BlockSpec((rows, LANE), lambda b, g, i: (b * npair + i, Z_NG // LANE)),
            pl.BlockSpec((1, 1, N_CMP, HEAD_DIM), lambda b, g, i: (b, g, 0, 0)),
            pl.BlockSpec((1, 1, HEAD_DIM, N_CMP), lambda b, g, i: (b, g, 0, 0)),
            pl.BlockSpec((1, TQ, QL), lambda b, g, i: (g, 0, 0)),
            pl.BlockSpec((1, TQ, QL), lambda b, g, i: (g, 0, 0)),
            pl.BlockSpec((1, Q_PAIR, N_CMP, QL), lambda b, g, i: (g, i, 0, 0)),
        ],
        out_specs=pl.BlockSpec((rows, qw), lambda b, g, i: (b * npair + i, g)),
        out_shape=jax.ShapeDtypeStruct((TOK, NSA_W), BF16),
        scratch_shapes=[
            pltpu.VMEM((NK, TQ, HEAD_DIM), BF16),
            pltpu.VMEM((NK, V_ROWS, TQ), BF16),
            pltpu.VMEM((NK, TQ, HEAD_DIM), BF16),
            pltpu.VMEM((NK, V_ROWS, TQ), BF16),
            pltpu.VMEM((Q_PAIR, N_SBLK, TQ), F32),
            pltpu.VMEM((Q_PAIR, NK, TQ, QL), F32),
            pltpu.VMEM((Q_PAIR, LANE, TQ), F32),
            pltpu.VMEM((Q_PAIR, N_CMP, TQ), F32),
        ],
        compiler_params=_cparams(("parallel", "parallel", "arbitrary")),
        name="nsa_attention",
    )(z, z, z, z, kc, vcT, t0, t1, tc)


def _sgu_tile(u_ref, v_ref, g_ref, b_ref, w_ref, bs_ref):
    r = lax.broadcasted_iota(jnp.int32, (GM_CHUNK, GM_CHUNK), 0)
    c = lax.broadcasted_iota(jnp.int32, (GM_CHUNK, GM_CHUNK), 1)
    tril = r >= c
    ws = [jnp.where(tril, w_ref[gi], 0.0).astype(BF16) for gi in range(GM_GROUPS)]
    chunks = []
    for ch in range(u_ref.shape[0] // GM_CHUNK):
        rows = slice(ch * GM_CHUNK, (ch + 1) * GM_CHUNK)
        v = v_ref[rows, :]
        mu = jnp.mean(v, axis=-1, keepdims=True)
        vc = v - mu
        var = jnp.mean(vc * vc, axis=-1, keepdims=True)
        vn = (vc * lax.rsqrt(var + EPS) * g_ref[...] + b_ref[...]).astype(BF16)
        groups = []
        for gi in range(GM_GROUPS):
            cols = slice(gi * GM_GW, (gi + 1) * GM_GW)
            mixed = jnp.dot(ws[gi], vn[:, cols], preferred_element_type=F32) + bs_ref[:, cols]
            groups.append((u_ref[rows, cols] * mixed).astype(BF16))
        chunks.append(jnp.concatenate(groups, axis=1))
    return jnp.concatenate(chunks, axis=0)


def _out_body(a_ref, u_ref, v_ref, m1_ref, m2_ref, x_ref, lg_ref, lb_ref, sw_ref, bs_ref,
              wn_ref, ws_ref, wo_ref, gn_ref, o_ref, h_ref):
    sg = _sgu_tile(u_ref, v_ref, lg_ref, lb_ref, sw_ref, bs_ref)
    pa = jnp.dot(a_ref[...], wn_ref[...], preferred_element_type=F32)
    ps = jnp.dot(sg, ws_ref[...], preferred_element_type=F32)
    merged = (m1_ref[...] * pa + m2_ref[...] * ps).astype(BF16)
    y = x_ref[...] + jnp.dot(merged, wo_ref[...], preferred_element_type=F32)
    o_ref[...] = y
    h_ref[...] = _rms_bf16(y, gn_ref[...])


def _merge_out(a, z, x, ln_g, ln_b, w_s, bs_exp, wn, ws, wo, l, next_gain):
    const = lambda i: (l, 0, 0)
    once = pl.Buffered(1)
    return pl.pallas_call(
        _out_body,
        grid=(TOK // OUT_TM,),
        in_specs=[
            pl.BlockSpec((OUT_TM, NSA_W), lambda i: (i, 0)),
            pl.BlockSpec((OUT_TM, GM_W), lambda i: (i, Z_U // GM_W)),
            pl.BlockSpec((OUT_TM, GM_W), lambda i: (i, Z_V // GM_W)),
            pl.BlockSpec((OUT_TM, D_MODEL), lambda i: (i, 0)),
            pl.BlockSpec((OUT_TM, D_MODEL), lambda i: (i, 1)),
            pl.BlockSpec((OUT_TM, D_MODEL), lambda i: (i, 0)),
            pl.BlockSpec((1, GM_W), lambda i: (0, 0)),
            pl.BlockSpec((1, GM_W), lambda i: (0, 0)),
            pl.BlockSpec((GM_GROUPS, GM_CHUNK, GM_CHUNK), lambda i: (0, 0, 0), pipeline_mode=once),
            pl.BlockSpec((GM_CHUNK, GM_W), lambda i: (0, 0), pipeline_mode=once),
            pl.BlockSpec((None, NSA_W, D_MODEL), const, pipeline_mode=once),
            pl.BlockSpec((None, GM_W, D_MODEL), const, pipeline_mode=once),
            pl.BlockSpec((None, D_MODEL, D_MODEL), const, pipeline_mode=once),
            pl.BlockSpec((1, D_MODEL), lambda i: (0, 0)),
        ],
        out_specs=[pl.BlockSpec((OUT_TM, D_MODEL), lambda i: (i, 0)),
                   pl.BlockSpec((OUT_TM, D_MODEL), lambda i: (i, 0))],
        out_shape=[jax.ShapeDtypeStruct((TOK, D_MODEL), F32),
                   jax.ShapeDtypeStruct((TOK, D_MODEL), BF16)],
        compiler_params=_cparams(("parallel",)),
        name="merge_out",
    )(a, z, z, z, z, x, ln_g, ln_b, w_s, bs_exp, wn, ws, wo, next_gain)


def _check_overlap_stencil():
    ci = np.arange(N_CMP - 1)[None, :] * CMP_STRIDE
    sj = np.arange(N_SBLK)[:, None] * SEL_BLOCK
    ov = np.clip(np.minimum(ci + CMP_BLOCK, sj + SEL_BLOCK) - np.maximum(ci, sj), 0, None) / CMP_BLOCK
    stencil = np.zeros((N_SBLK, N_CMP - 1))
    for j in range(N_SBLK):
        for c, w in ((4 * j - 1, 0.5), (4 * j, 1.0), (4 * j + 1, 1.0), (4 * j + 2, 1.0), (4 * j + 3, 0.5)):
            if 0 <= c < N_CMP - 1:
                stencil[j, c] = w
    assert np.array_equal(ov, stencil)


def _block_diag_ones():
    r = np.arange(LANE)
    return (r[:, None] // HEAD_DIM == r[None, :] // HEAD_DIM).astype(np.float32)


def _prep_w_in(w):
    L = w.shape[0]
    w = w.astype(BF16)
    kv = w[:, :, OFF_KV:OFF_NG].reshape(L, D_MODEL, 3, 2, N_KV, HEAD_DIM)
    kv = kv.transpose(0, 1, 2, 4, 3, 5).reshape(L, D_MODEL, 6 * KV_W)
    ng = w[:, :, OFF_NG:OFF_UV].reshape(L, D_MODEL, N_KV, HPG, 3).transpose(0, 1, 2, 4, 3)
    ng = ng.reshape(L, D_MODEL, N_KV, 3 * HPG)
    ng = jnp.pad(ng, ((0, 0), (0, 0), (0, 0), (0, GATE_GW - 3 * HPG))).reshape(L, D_MODEL, N_KV * GATE_GW)
    ng = jnp.pad(ng, ((0, 0), (0, 0), (0, SEC_W - N_KV * GATE_GW)))
    return jnp.concatenate([w[:, :, OFF_MG:], w[:, :, OFF_UV:OFF_MG], w[:, :, :OFF_KV], kv, ng], axis=2)


def _mixer(x, z, l, p, tables):
    t0, t1, tc = tables
    pos = jnp.stack([p["cmp_pos_k"][l], p["cmp_pos_v"][l]]).reshape(2, 1, CMP_BLOCK * HEAD_DIM)
    w1 = jnp.stack([p["cmp_k_w1"][l], p["cmp_v_w1"][l]]).astype(BF16)
    kc, vcT = _compress(z, pos, w1, p["cmp_k_w2"][l].astype(BF16), p["cmp_v_w2"][l].T.astype(BF16),
                        p["k_norm"][l, 0].reshape(1, HEAD_DIM))
    a = _attention(z, kc, vcT, t0, t1, tc)
    bs_exp = jnp.repeat(p["sgu_b"][l].T, GM_GW, axis=1)
    return _merge_out(a, z, x, p["sgu_norm_g"][l].reshape(1, GM_W), p["sgu_norm_b"][l].reshape(1, GM_W),
                      p["sgu_w"][l], bs_exp, p["wn"], p["ws"], p["wo"], l,
                      p["ffn2_norm"][l].reshape(1, D_MODEL))


def kernel(x, rel_bias, ffn1_norm, ffn1_w_gate, ffn1_w_up, ffn1_w_down, mix_norm, w_in, q_norm, k_norm, cmp_pos_k, cmp_pos_v, cmp_k_w1, cmp_k_w2, cmp_v_w1, cmp_v_w2, sgu_norm_g, sgu_norm_b, sgu_w, sgu_b, w_proj_nsa, w_proj_sgu, w_out, ffn2_norm, ffn2_w_gate, ffn2_w_up, ffn2_w_down):
    p = dict(cmp_pos_k=cmp_pos_k, cmp_pos_v=cmp_pos_v, cmp_k_w1=cmp_k_w1, cmp_k_w2=cmp_k_w2,
             cmp_v_w1=cmp_v_w1, cmp_v_w2=cmp_v_w2, k_norm=k_norm, sgu_norm_g=sgu_norm_g,
             sgu_norm_b=sgu_norm_b, sgu_w=sgu_w, sgu_b=sgu_b, wn=w_proj_nsa.astype(BF16),
             ws=w_proj_sgu.astype(BF16), wo=w_out.astype(BF16), ffn2_norm=ffn2_norm)
    near, cmpb = _bucket_tables()
    t0, t1, tc = _bias_tables(rel_bias, jnp.asarray(near), jnp.asarray(cmpb))
    tables = (t0, t1, tc)
    _check_overlap_stencil()
    bd = jnp.asarray(_block_diag_ones()).astype(BF16)
    w_in_r = _prep_w_in(w_in)
    f1 = (ffn1_w_gate, ffn1_w_up, ffn1_w_down)
    f2 = (ffn2_w_gate, ffn2_w_up, ffn2_w_down)

    y = x.reshape(TOK, D_MODEL)
    for l in range(DEPTH):
        y = _ffn(y, _rmsnorm(y, ffn1_norm[l].reshape(1, D_MODEL)), *f1, l)
        z = _proj_in(y, mix_norm[l].reshape(1, D_MODEL), w_in_r, l, bd,
                     jnp.tile(q_norm[l], PIN_TN // HEAD_DIM).reshape(1, PIN_TN),
                     jnp.tile(k_norm[l, 1], SEC_W // HEAD_DIM).reshape(1, SEC_W),
                     jnp.tile(k_norm[l, 2], SEC_W // HEAD_DIM).reshape(1, SEC_W))
        y, h = _mixer(y, z, l, p, tables)
        y = _ffn(y, h, *f2, l)
    return y.reshape(BATCH, SEQ, D_MODEL)
```

```python
import math

import numpy as np
import jax
import jax.numpy as jnp
from jax import lax
from jax.experimental import pallas as pl
from jax.experimental.pallas import tpu as pltpu

F32 = jnp.float32
BF16 = jnp.bfloat16

D_MODEL = 2048
BATCH = 4
SEQ = 2048
DEPTH = 2
TOK = BATCH * SEQ
HEAD_DIM = 64
N_HEADS = 16
N_KV = 4
HPG = 4
NSA_W = 1024
KV_W = 256
CMP_BLOCK = 32
CMP_STRIDE = 16
CMP_HIDDEN = 256
SEL_BLOCK = 64
N_SEL = 16
WINDOW = 512
GM_W = 1024
GM_CHUNK = 128
GM_GW = 128
GM_GROUPS = 8
N_BUCKETS = 32
MAX_DISTANCE = 128
D_FF = 5504
EPS = 1e-6
NEG = -1e30
FORCE = 1e4
OFF_KV = NSA_W
OFF_NG = OFF_KV + 6 * KV_W
OFF_UV = OFF_NG + 3 * N_HEADS
OFF_MG = OFF_UV + 2 * GM_W
LOG2E = 1.4426950408889634

LANE = 128
VMEM_LIMIT = 56 * 1024 * 1024

FFN_TF = 512
FFN_UP_TM = 1024
NORM_TM = 512
FFN_DN_TM = 1024
FFN_DN_TN = 512
PIN_TM = 1024
PIN_TN = 1024
PIN_RC = 256
SEC_W = PIN_TN // 2
Z_MG = 0
Z_U = Z_MG + 2 * D_MODEL
Z_V = Z_U + GM_W
Z_Q = Z_V + GM_W
Z_KV = Z_Q + NSA_W
Z_NG = Z_KV + 6 * KV_W
Z_W = Z_NG + SEC_W
KV_GW = 2 * HEAD_DIM
GATE_GW = 16
TQ = 128
NQ = SEQ // TQ
NK = SEQ // TQ
QL = HPG * TQ
N_CMP = 128
CMP_BAND = 16
MASKED = -1
N_SBLK = SEQ // SEL_BLOCK
V_ROWS = HEAD_DIM + 16
FAR_GROUP = 4
Q_PAIR = 4
OUT_TM = 256


def _cparams(sem):
    return pltpu.CompilerParams(dimension_semantics=sem, vmem_limit_bytes=VMEM_LIMIT)


def _cast_rows(src_ref, dst_ref, chunk):
    def body(r, carry):
        rows = pl.ds(pl.multiple_of(r * chunk, chunk), chunk)
        dst_ref[rows, :] = src_ref[rows, :].astype(BF16)
        return carry

    lax.fori_loop(0, src_ref.shape[0] // chunk, body, 0)


def _rms_bf16(x, gain):
    ms = jnp.mean(x * x, axis=-1, keepdims=True)
    return (x * lax.rsqrt(ms + EPS) * gain).astype(BF16)


def _norm_body(x_ref, g_ref, h_ref):
    h_ref[...] = _rms_bf16(x_ref[...], g_ref[...])


def _rmsnorm(x, gain):
    return pl.pallas_call(
        _norm_body,
        grid=(TOK // NORM_TM,),
        in_specs=[pl.BlockSpec((NORM_TM, D_MODEL), lambda i: (i, 0)),
                  pl.BlockSpec((1, D_MODEL), lambda i: (0, 0))],
        out_specs=pl.BlockSpec((NORM_TM, D_MODEL), lambda i: (i, 0)),
        out_shape=jax.ShapeDtypeStruct((TOK, D_MODEL), BF16),
        compiler_params=_cparams(("parallel",)),
        name="rmsnorm",
    )(x, gain)


def _ffn_up_body(h_ref, wg_ref, wu_ref, a_ref, wgb_ref, wub_ref):
    @pl.when(pl.program_id(1) == 0)
    def _():
        _cast_rows(wg_ref, wgb_ref, LANE)
        _cast_rows(wu_ref, wub_ref, LANE)

    h = h_ref[...]
    g = jnp.dot(h, wgb_ref[...], preferred_element_type=F32)
    u = jnp.dot(h, wub_ref[...], preferred_element_type=F32)
    a_ref[...] = (g * jax.nn.sigmoid(g) * u).astype(BF16)


def _ffn_down_body(a_ref, wd_ref, x_ref, o_ref, wdb_ref):
    @pl.when(pl.program_id(1) == 0)
    def _():
        _cast_rows(wd_ref, wdb_ref, LANE)

    o_ref[...] = x_ref[...] + 0.5 * jnp.dot(a_ref[...], wdb_ref[...], preferred_element_type=F32)


def _ffn(x, h, wg, wu, wd, l):
    nf = pl.cdiv(D_FF, FFN_TF)
    a = pl.pallas_call(
        _ffn_up_body,
        grid=(nf, TOK // FFN_UP_TM),
        in_specs=[
            pl.BlockSpec((FFN_UP_TM, D_MODEL), lambda f, m: (m, 0)),
            pl.BlockSpec((None, D_MODEL, FFN_TF), lambda f, m: (l, 0, f)),
            pl.BlockSpec((None, D_MODEL, FFN_TF), lambda f, m: (l, 0, f)),
        ],
        out_specs=pl.BlockSpec((FFN_UP_TM, FFN_TF), lambda f, m: (m, f)),
        out_shape=jax.ShapeDtypeStruct((TOK, D_FF), BF16),
        scratch_shapes=[pltpu.VMEM((D_MODEL, FFN_TF), BF16), pltpu.VMEM((D_MODEL, FFN_TF), BF16)],
        compiler_params=_cparams(("arbitrary", "arbitrary")),
        name="ffn_up",
    )(h, wg, wu)
    return pl.pallas_call(
        _ffn_down_body,
        grid=(D_MODEL // FFN_DN_TN, TOK // FFN_DN_TM),
        in_specs=[
            pl.BlockSpec((FFN_DN_TM, D_FF), lambda n, m: (m, 0)),
            pl.BlockSpec((None, D_FF, FFN_DN_TN), lambda n, m: (l, 0, n), pipeline_mode=pl.Buffered(1)),
            pl.BlockSpec((FFN_DN_TM, FFN_DN_TN), lambda n, m: (m, n)),
        ],
        out_specs=pl.BlockSpec((FFN_DN_TM, FFN_DN_TN), lambda n, m: (m, n)),
        out_shape=jax.ShapeDtypeStruct((TOK, D_MODEL), F32),
        scratch_shapes=[pltpu.VMEM((D_FF, FFN_DN_TN), BF16)],
        compiler_params=_cparams(("arbitrary", "arbitrary")),
        name="ffn_down",
    )(a, wd, x)


def _head_rms(y, bd):
    y2 = (y * y).astype(BF16)
    ss = jnp.concatenate(
        [jnp.dot(y2[:, k * LANE:(k + 1) * LANE], bd, preferred_element_type=F32)
         for k in range(y.shape[1] // LANE)], axis=1)
    return y * lax.rsqrt(ss * (1.0 / HEAD_DIM) + EPS)


def _pin_body(x_ref, g_ref, w_ref, bd_ref, qg_ref, k1g_ref, k2g_ref, o_ref, h_ref):
    j = pl.program_id(1)

    @pl.when(j == 0)
    def _():
        x = x_ref[...]
        ms = jnp.mean(x * x, axis=-1, keepdims=True)
        h_ref[...] = (x * lax.rsqrt(ms + EPS) * g_ref[...]).astype(BF16)

    t_u = Z_U // PIN_TN
    t_q = Z_Q // PIN_TN
    t_kv = Z_KV // PIN_TN
    half = SEC_W

    def emit(act):
        for r in range(PIN_TM // PIN_RC):
            rows = slice(r * PIN_RC, (r + 1) * PIN_RC)
            o_ref[rows, :] = act(jnp.dot(h_ref[rows, :], w_ref[...], preferred_element_type=F32))

    def halves(act_lo, act_hi):
        return lambda y: jnp.concatenate([act_lo(y[:, :half]), act_hi(y[:, half:])], axis=1)

    def kv_act(kg_ref):
        is_k = (lax.broadcasted_iota(jnp.int32, (1, half), 1) & (KV_GW - 1)) < HEAD_DIM
        return lambda y: jnp.where(is_k, _head_rms(y, bd_ref[...]) * kg_ref[...], y)

    @pl.when(j < t_u)
    def _():
        emit(jax.nn.sigmoid)

    @pl.when((j >= t_u) & (j < t_q))
    def _():
        emit(jax.nn.gelu)

    @pl.when((j >= t_q) & (j < t_kv))
    def _():
        emit(lambda y: _head_rms(y, bd_ref[...]) * (qg_ref[...] * (HEAD_DIM ** -0.5 * LOG2E)))

    @pl.when(j == t_kv)
    def _():
        emit(halves(lambda y: y, kv_act(k1g_ref)))

    @pl.when(j == t_kv + 1)
    def _():
        emit(halves(kv_act(k2g_ref), jax.nn.sigmoid))


def _proj_in(x, gain, w, l, bd, qg, k1g, k2g):
    return pl.pallas_call(
        _pin_body,
        grid=(TOK // PIN_TM, Z_W // PIN_TN),
        in_specs=[
            pl.BlockSpec((PIN_TM, D_MODEL), lambda i, j: (i, 0)),
            pl.BlockSpec((1, D_MODEL), lambda i, j: (0, 0)),
            pl.BlockSpec((None, D_MODEL, PIN_TN), lambda i, j: (l, 0, j)),
            pl.BlockSpec((LANE, LANE), lambda i, j: (0, 0)),
            pl.BlockSpec((1, PIN_TN), lambda i, j: (0, 0)),
            pl.BlockSpec((1, SEC_W), lambda i, j: (0, 0)),
            pl.BlockSpec((1, SEC_W), lambda i, j: (0, 0)),
        ],
        out_specs=pl.BlockSpec((PIN_TM, PIN_TN), lambda i, j: (i, j)),
        out_shape=jax.ShapeDtypeStruct((TOK, Z_W), F32),
        scratch_shapes=[pltpu.VMEM((PIN_TM, D_MODEL), BF16)],
        compiler_params=_cparams(("parallel", "arbitrary")),
        name="proj_in",
    )(x, gain, w, bd, qg, k1g, k2g)


def _cmp_body(z_ref, pos_ref, w1_ref, w2k_ref, w2vT_ref, kg_ref, kc_ref, vcT_ref):
    half = CMP_STRIDE * HEAD_DIM
    lane = lax.broadcasted_iota(jnp.int32, (N_CMP, LANE), 1)
    first = lane < HEAD_DIM
    xk, xv = [], []
    for j in range(CMP_STRIDE // 2):
        ev = z_ref[pl.ds(2 * j, N_CMP, stride=CMP_STRIDE), :]
        od = z_ref[pl.ds(2 * j + 1, N_CMP, stride=CMP_STRIDE), :]
        xk.append(jnp.where(first, ev, pltpu.roll(od, HEAD_DIM, 1)))
        xv.append(jnp.where(first, pltpu.roll(ev, HEAD_DIM, 1), od))

    def hidden(x, kind):
        xa = (x + pos_ref[kind, :, :half]).astype(BF16)
        xb = (x + pos_ref[kind, :, half:]).astype(BF16)
        a = jnp.dot(xa, w1_ref[kind, :half, :], preferred_element_type=F32)
        b = jnp.dot(xb, w1_ref[kind, half:, :], preferred_element_type=F32)
        hid = a + pltpu.roll(b, N_CMP - 1, 0)
        return (hid * jax.nn.sigmoid(hid)).astype(BF16)

    yk = jnp.dot(hidden(jnp.concatenate(xk, axis=1), 0), w2k_ref[...], preferred_element_type=F32)
    ms = jnp.mean(yk * yk, axis=-1, keepdims=True)
    kc_ref[0, 0] = (yk * lax.rsqrt(ms + EPS) * kg_ref[...]).astype(BF16)
    hv = hidden(jnp.concatenate(xv, axis=1), 1)
    vcT_ref[0, 0] = lax.dot_general(w2vT_ref[...], hv, (((1,), (1,)), ((), ())),
                                    preferred_element_type=F32).astype(BF16)


def _compress(z, pos, w1, w2k, w2vT, kg):
    return pl.pallas_call(
        _cmp_body,
        grid=(BATCH, N_KV),
        in_specs=[
            pl.BlockSpec((SEQ, KV_GW), lambda b, g: (b, Z_KV // KV_GW + g)),
            pl.BlockSpec((2, 1, CMP_BLOCK * HEAD_DIM), lambda b, g: (0, 0, 0)),
            pl.BlockSpec((2, CMP_BLOCK * HEAD_DIM, CMP_HIDDEN), lambda b, g: (0, 0, 0)),
            pl.BlockSpec((CMP_HIDDEN, HEAD_DIM), lambda b, g: (0, 0)),
            pl.BlockSpec((HEAD_DIM, CMP_HIDDEN), lambda b, g: (0, 0)),
            pl.BlockSpec((1, HEAD_DIM), lambda b, g: (0, 0)),
        ],
        out_specs=[
            pl.BlockSpec((1, 1, N_CMP, HEAD_DIM), lambda b, g: (b, g, 0, 0)),
            pl.BlockSpec((1, 1, HEAD_DIM, N_CMP), lambda b, g: (b, g, 0, 0)),
        ],
        out_shape=[
            jax.ShapeDtypeStruct((BATCH, N_KV, N_CMP, HEAD_DIM), BF16),
            jax.ShapeDtypeStruct((BATCH, N_KV, HEAD_DIM, N_CMP), BF16),
        ],
        compiler_params=_cparams(("parallel", "parallel")),
        name="compress",
    )(z, pos, w1, w2k, w2vT, kg)


def _np_bucket(dist):
    n = np.maximum(dist, 0)
    max_exact = N_BUCKETS // 2
    nf = np.maximum(n, 1).astype(np.float32)
    large = max_exact + (np.log(nf / np.float32(max_exact)) / np.float32(math.log(MAX_DISTANCE / max_exact))
                         * np.float32(N_BUCKETS - max_exact)).astype(np.int32)
    large = np.minimum(large, N_BUCKETS - 1)
    return np.where(n < max_exact, n, large).astype(np.int32)


def _bucket_tables():
    kj = np.arange(TQ)[:, None]
    qi = np.arange(TQ)[None, :]
    near = np.stack([np.where(qi >= kj, _np_bucket(qi - kj), MASKED), _np_bucket(TQ + qi - kj)])
    start = np.array([_cmp_band_start(i) for i in range(NQ)])[:, None, None]
    c = start + np.arange(CMP_BAND)[None, :, None]
    t = (np.arange(NQ)[:, None, None] * TQ + qi[None])
    dist_band = t - (c * CMP_STRIDE + CMP_BLOCK - 1)
    cmp_b = np.where(dist_band >= 0, _np_bucket(dist_band), MASKED)
    dist = t - (np.arange(N_CMP)[None, :, None] * CMP_STRIDE + CMP_BLOCK - 1)
    for i in range(NQ):
        s0 = _cmp_band_start(i)
        assert np.all(dist[i, :s0] >= 0) and np.all(_np_bucket(dist[i, :s0]) == N_BUCKETS - 1)
        assert np.all(dist[i, s0 + CMP_BAND:] < 0)
    return near.astype(np.int32), cmp_b.astype(np.int32)


def _cmp_band_start(i):
    return max((TQ // CMP_STRIDE) * i - 8, 0)


def _bias_body(rb_ref, near_ref, cmpb_ref, t0_ref, t1_ref, tc_ref):
    g = pl.program_id(0)

    def lut(bk, h):
        val = jnp.full(bk.shape, NEG, F32)
        for b in range(N_BUCKETS):
            val = jnp.where(bk == b, rb_ref[b, h], val)
        return val

    for hh in range(HPG):
        h = g * HPG + hh
        sl = slice(hh * TQ, (hh + 1) * TQ)
        far = rb_ref[N_BUCKETS - 1, h]
        t0_ref[0, :, sl] = (lut(near_ref[0], h) - far) * LOG2E
        t1_ref[0, :, sl] = (lut(near_ref[1], h) - far) * LOG2E

        def body(i, carry):
            start = pl.multiple_of(jnp.maximum((TQ // CMP_STRIDE) * i - 8, 0), 8)
            row = lax.broadcasted_iota(jnp.int32, (N_CMP, TQ), 0)
            tc_ref[0, i, :, sl] = jnp.where(row < start, far * LOG2E, NEG)
            tc_ref[0, i, pl.ds(start, CMP_BAND), sl] = lut(cmpb_ref[i], h) * LOG2E
            return carry

        lax.fori_loop(0, NQ, body, 0)


def _bias_tables(rel_bias, near, cmpb):
    return pl.pallas_call(
        _bias_body,
        grid=(N_KV,),
        in_specs=[
            pl.BlockSpec(memory_space=pltpu.SMEM),
            pl.BlockSpec((2, TQ, TQ), lambda g: (0, 0, 0)),
            pl.BlockSpec((NQ, CMP_BAND, TQ), lambda g: (0, 0, 0)),
        ],
        out_specs=[
            pl.BlockSpec((1, TQ, QL), lambda g: (g, 0, 0)),
            pl.BlockSpec((1, TQ, QL), lambda g: (g, 0, 0)),
            pl.BlockSpec((1, NQ, N_CMP, QL), lambda g: (g, 0, 0, 0)),
        ],
        out_shape=[
            jax.ShapeDtypeStruct((N_KV, TQ, QL), F32),
            jax.ShapeDtypeStruct((N_KV, TQ, QL), F32),
            jax.ShapeDtypeStruct((N_KV, NQ, N_CMP, QL), F32),
        ],
        compiler_params=_cparams(("arbitrary",)),
        name="bias_tables",
    )(rel_bias, near, cmpb)


def _attn_body(zq_ref, zs_ref, zw_ref, zg_ref, kc_ref, vcT_ref, t0_ref, t1_ref, tc_ref, o_ref,
               ks_ref, vsT_ref, kw_ref, vwT_ref, sel_ref, sfar_ref, gsc_ref, psum_ref):
    g = pl.program_id(1)
    ip = pl.program_id(2)

    @pl.when(ip == 0)
    def _():
        ones = jnp.ones((V_ROWS - HEAD_DIM, TQ), BF16)

        def fill(n, carry):
            for z_ref, k_ref, vT_ref in ((zs_ref, ks_ref, vsT_ref), (zw_ref, kw_ref, vwT_ref)):
                blk = z_ref[pl.ds(pl.multiple_of(n * TQ, TQ), TQ), :]
                k_ref[n] = blk[:, :HEAD_DIM].astype(BF16)
                vT_ref[n, :HEAD_DIM, :] = blk.T[HEAD_DIM:, :].astype(BF16)
                vT_ref[n, HEAD_DIM:, :] = ones
            return carry

        lax.fori_loop(0, NK, fill, 0)

    blk = lax.broadcasted_iota(jnp.int32, (N_SBLK, TQ), 0)
    ql = lax.broadcasted_iota(jnp.int32, (N_SBLK, TQ), 1)
    key1 = lax.broadcasted_iota(jnp.int32, (TQ, TQ), 0)
    q1 = lax.broadcasted_iota(jnp.int32, (TQ, TQ), 1)
    nlast = WINDOW // TQ

    def sel_mask(t, n):
        r0 = sel_ref[t, pl.ds(2 * n, 1), :]
        r1 = sel_ref[t, pl.ds(2 * n + 1, 1), :]
        return jnp.where(key1 < SEL_BLOCK, r0, r1) > 0.5

    def scores(qT, k_ref, n, bias, mask):
        sc = jnp.dot(k_ref[n], qT, preferred_element_type=F32)
        if bias is not None:
            sc = sc + bias
        if mask is None:
            return sc
        return jnp.concatenate(
            [jnp.where(mask, sc[:, h * TQ:(h + 1) * TQ], NEG) for h in range(HPG)], axis=1)

    def colmax8(sc):
        return jnp.max(sc.reshape(TQ // 8, 8, QL), axis=0)

    def probs(sc, mrow):
        return jnp.exp2(sc - mrow).astype(BF16)

    def pv(vT_ref, ns, ps):
        vv = vT_ref[ns[0]] if len(ns) == 1 else jnp.concatenate([vT_ref[n] for n in ns], axis=1)
        pp = ps[0] if len(ps) == 1 else jnp.concatenate(ps, axis=0)
        return jnp.dot(vv, pp, preferred_element_type=F32)

    def finish(acc):
        return acc[:HEAD_DIM] * (1.0 / acc[HEAD_DIM:HEAD_DIM + 1])

    def stage_q(t):
        i = Q_PAIR * ip + t
        rows = slice(t * TQ, (t + 1) * TQ)
        qt = zq_ref[rows, :].T
        qT = jnp.concatenate([qt[h * HEAD_DIM:(h + 1) * HEAD_DIM] for h in range(HPG)], axis=1).astype(BF16)
        gsc_ref[t] = zg_ref[rows, :].T
        return dict(i=i, qT=qT, n1=jnp.maximum(i - 1, 0), i1=jnp.full((TQ, TQ), i, jnp.int32),
                    nfar=jnp.maximum(i - 1, 0))

    def stage_cmp_scores(t, tl):
        tl["s_c"] = jnp.dot(kc_ref[0, 0], tl["qT"], preferred_element_type=F32) + tc_ref[0, t]
        tl["m_c"] = jnp.maximum(jnp.max(colmax8(tl["s_c"]), axis=0, keepdims=True), 0.1 * NEG)

    def stage_cmp(t, tl):
        e = jnp.exp2(tl["s_c"] - tl["m_c"])
        l = jnp.sum(e, axis=0, keepdims=True)
        p = e * jnp.where(l > 0.0, 1.0 / l, 0.0)
        tl["ocT"] = jnp.dot(vcT_ref[0, 0], p.astype(BF16), preferred_element_type=F32)
        psum_ref[t] = p[:, 0:TQ] + p[:, TQ:2 * TQ] + p[:, 2 * TQ:3 * TQ] + p[:, 3 * TQ:4 * TQ]

    def stage_select(t, tl):
        i = tl["i"]
        ratio = SEL_BLOCK // CMP_STRIDE
        part = [psum_ref[t, pl.ds(r, N_SBLK, stride=ratio), :] for r in range(ratio)]
        prev_last = jnp.where(blk == 0, 0.0, pltpu.roll(part[3], 1, 0))
        imp = (part[0] + part[1] + part[2]) + 0.5 * (part[3] + prev_last)
        cur = jnp.full((N_SBLK, TQ), i, jnp.int32) * (TQ // SEL_BLOCK) + ql // SEL_BLOCK
        forced = (blk == 0) | (blk == cur) | (blk == cur - 1)
        score = jnp.where(blk <= cur, jnp.where(forced, FORCE, imp), NEG)
        sub = 8
        pieces = [score[k * sub:(k + 1) * sub, :] for k in range(N_SBLK // sub)]
        ranks = [jnp.zeros((sub, TQ), F32) for _ in pieces]
        for jj in range(N_SBLK):
            row = score[jj:jj + 1, :]
            for k, pc in enumerate(pieces):
                if k * sub > jj:
                    beats = row >= pc
                elif (k + 1) * sub - 1 < jj:
                    beats = row > pc
                else:
                    beats = (row > pc) | ((row == pc) & (blk[k * sub:(k + 1) * sub, :] > jj))
                ranks[k] = ranks[k] + jnp.where(beats, 1.0, 0.0)
        rank = jnp.concatenate(ranks, axis=0)
        sel_ref[t] = jnp.where((rank < N_SEL) & (blk <= cur), 1.0, 0.0)

    def stage_window_scores(t, tl):
        i, n1, i1, qT = tl["i"], tl["n1"], tl["i1"], tl["qT"]
        n2, n3, n4 = jnp.maximum(i - 2, 0), jnp.maximum(i - 3, 0), jnp.maximum(i - nlast, 0)
        w_d = scores(qT, kw_ref, i, t0_ref[0], None)
        w_1 = scores(qT, kw_ref, n1, t1_ref[0], i1 >= 1)
        w_2 = scores(qT, kw_ref, n2, None, i1 >= 2)
        w_3 = scores(qT, kw_ref, n3, None, i1 >= 3)
        w_4 = scores(qT, kw_ref, n4, None, (q1 < key1) & (i1 >= nlast))
        mxw = jnp.maximum(jnp.maximum(jnp.maximum(colmax8(w_d), colmax8(w_1)),
                                      jnp.maximum(colmax8(w_2), colmax8(w_3))), colmax8(w_4))
        tl["w"] = (w_d, w_1, w_2, w_3, w_4)
        tl["wn"] = (i, n1, n2, n3, n4)
        tl["m_w"] = jnp.max(mxw, axis=0, keepdims=True)

    def stage_window_pv(t, tl):
        (w_d, w_1, w_2, w_3, w_4), (i, n1, n2, n3, n4), m_w = tl["w"], tl["wn"], tl["m_w"]
        tl["owT"] = finish(pv(vwT_ref, [i], [probs(w_d, m_w)])
                           + pv(vwT_ref, [n1, n2], [probs(w_1, m_w), probs(w_2, m_w)])
                           + pv(vwT_ref, [n3, n4], [probs(w_3, m_w), probs(w_4, m_w)]))

    def stage_near(t, tl):
        i, n1, i1, qT = tl["i"], tl["n1"], tl["i1"], tl["qT"]
        tl["s_d"] = scores(qT, ks_ref, i, t0_ref[0], None)
        tl["s_1"] = scores(qT, ks_ref, n1, t1_ref[0], sel_mask(t, n1) & (i1 >= 1))
        tl["mx"] = jnp.maximum(colmax8(tl["s_d"]), colmax8(tl["s_1"]))

    tiles = [stage_q(t) for t in range(Q_PAIR)]
    for stage in (stage_cmp_scores, stage_cmp, stage_window_scores, stage_select, stage_near, stage_window_pv):
        for t, tl in enumerate(tiles):
            stage(t, tl)
    assert Q_PAIR == FAR_GROUP == 4
    extra = (2, 3)

    def far_scores(tr, mxs, ts):
        out = list(mxs)
        for t in ts:
            tl = tiles[t]
            for c in range(FAR_GROUP):
                n = FAR_GROUP * tr + c
                sc = scores(tl["qT"], ks_ref, n, None,
                            sel_mask(t, n) & (jnp.full((TQ, TQ), n, jnp.int32) < tl["nfar"]))
                sfar_ref[t, n] = sc
                out[t] = jnp.maximum(out[t], colmax8(sc))
        return tuple(out)

    every = tuple(range(Q_PAIR))
    mxs = lax.fori_loop(0, ip, lambda tr, c: far_scores(tr, c, every), tuple(tl["mx"] for tl in tiles))
    mxs = far_scores(ip, mxs, extra)
    m_s = [jnp.max(mx, axis=0, keepdims=True) for mx in mxs]
    accs = tuple(pv(vsT_ref, [tl["i"], tl["n1"]], [probs(tl["s_d"], m_s[t]), probs(tl["s_1"], m_s[t])])
                 for t, tl in enumerate(tiles))

    def far_pv(tr, accs_, ts):
        ns = [FAR_GROUP * tr + c for c in range(FAR_GROUP)]
        out = list(accs_)
        for t in ts:
            out[t] = out[t] + pv(vsT_ref, ns, [probs(sfar_ref[t, n], m_s[t]) for n in ns])
        return tuple(out)

    accs = lax.fori_loop(0, ip, lambda tr, c: far_pv(tr, c, every), accs)
    accs = far_pv(ip, accs, extra)

    for t, tl in enumerate(tiles):
        base = g * GATE_GW
        gate = [jnp.concatenate([gsc_ref[t, pl.ds(base + br * HPG + h, 1), :] for h in range(HPG)], axis=1)
                for br in range(3)]
        aT = gate[0] * tl["ocT"] + gate[1] * finish(accs[t]) + gate[2] * tl["owT"]
        a4 = jnp.concatenate([aT[:, h * TQ:(h + 1) * TQ] for h in range(HPG)], axis=0)
        o_ref[t * TQ:(t + 1) * TQ, :] = a4.T.astype(o_ref.dtype)


def _attention(z, kc, vcT, t0, t1, tc):
    qw = HPG * HEAD_DIM
    rows = Q_PAIR * TQ
    npair = NQ // Q_PAIR
    return pl.pallas_call(
        _attn_body,
        grid=(BATCH, N_KV, npair),
        in_specs=[
            pl.BlockSpec((rows, qw), lambda b, g, i: (b * npair + i, Z_Q // qw + g)),
            pl.BlockSpec((SEQ, KV_GW), lambda b, g, i: (b, (Z_KV + 2 * KV_W) // KV_GW + g)),
            pl.BlockSpec((SEQ, KV_GW), lambda b, g, i: (b, (Z_KV + 4 * KV_W) // KV_GW + g)),
            pl.BlockSpec((rows, LANE), lambda b, g, i: (b * npair + i, Z_NG // LANE)),
            pl.BlockSpec((1, 1, N_CMP, HEAD_DIM), lambda b, g, i: (b, g, 0, 0)),
            pl.BlockSpec((1, 1, HEAD_DIM, N_CMP), lambda b, g, i: (b, g, 0, 0)),
            pl.BlockSpec((1, TQ, QL), lambda b, g, i: (g, 0, 0)),
            pl.BlockSpec((1, TQ, QL), lambda b, g, i: (g, 0, 0)),
            pl.BlockSpec((1, Q_PAIR, N_CMP, QL), lambda b, g, i: (g, i, 0, 0)),
        ],
        out_specs=pl.BlockSpec((rows, qw), lambda b, g, i: (b * npair + i, g)),
        out_shape=jax.ShapeDtypeStruct((TOK, NSA_W), BF16),
        scratch_shapes=[
            pltpu.VMEM((NK, TQ, HEAD_DIM), BF16),
            pltpu.VMEM((NK, V_ROWS, TQ), BF16),
            pltpu.VMEM((NK, TQ, HEAD_DIM), BF16),
            pltpu.VMEM((NK, V_ROWS, TQ), BF16),
            pltpu.VMEM((Q_PAIR, N_SBLK, TQ), F32),
            pltpu.VMEM((Q_PAIR, NK, TQ, QL), F32),
            pltpu.VMEM((Q_PAIR, LANE, TQ), F32),
            pltpu.VMEM((Q_PAIR, N_CMP, TQ), F32),
        ],
        compiler_params=_cparams(("parallel", "parallel", "arbitrary")),
        name="nsa_attention",
    )(z, z, z, z, kc, vcT, t0, t1, tc)


def _sgu_tile(u_ref, v_ref, g_ref, b_ref, w_ref, bs_ref):
    r = lax.broadcasted_iota(jnp.int32, (GM_CHUNK, GM_CHUNK), 0)
    c = lax.broadcasted_iota(jnp.int32, (GM_CHUNK, GM_CHUNK), 1)
    tril = r >= c
    ws = [jnp.where(tril, w_ref[gi], 0.0).astype(BF16) for gi in range(GM_GROUPS)]
    chunks = []
    for ch in range(u_ref.shape[0] // GM_CHUNK):
        rows = slice(ch * GM_CHUNK, (ch + 1) * GM_CHUNK)
        v = v_ref[rows, :]
        mu = jnp.mean(v, axis=-1, keepdims=True)
        vc = v - mu
        var = jnp.mean(vc * vc, axis=-1, keepdims=True)
        vn = (vc * lax.rsqrt(var + EPS) * g_ref[...] + b_ref[...]).astype(BF16)
        groups = []
        for gi in range(GM_GROUPS):
            cols = slice(gi * GM_GW, (gi + 1) * GM_GW)
            mixed = jnp.dot(ws[gi], vn[:, cols], preferred_element_type=F32) + bs_ref[:, cols]
            groups.append((u_ref[rows, cols] * mixed).astype(BF16))
        chunks.append(jnp.concatenate(groups, axis=1))
    return jnp.concatenate(chunks, axis=0)


def _out_body(a_ref, u_ref, v_ref, m1_ref, m2_ref, x_ref, lg_ref, lb_ref, sw_ref, bs_ref,
              wn_ref, ws_ref, wo_ref, gn_ref, o_ref, h_ref):
    sg = _sgu_tile(u_ref, v_ref, lg_ref, lb_ref, sw_ref, bs_ref)
    pa = jnp.dot(a_ref[...], wn_ref[...], preferred_element_type=F32)
    ps = jnp.dot(sg, ws_ref[...], preferred_element_type=F32)
    merged = (m1_ref[...] * pa + m2_ref[...] * ps).astype(BF16)
    y = x_ref[...] + jnp.dot(merged, wo_ref[...], preferred_element_type=F32)
    o_ref[...] = y
    h_ref[...] = _rms_bf16(y, gn_ref[...])


def _merge_out(a, z, x, ln_g, ln_b, w_s, bs_exp, wn, ws, wo, l, next_gain):
    const = lambda i: (l, 0, 0)
    once = pl.Buffered(1)
    return pl.pallas_call(
        _out_body,
        grid=(TOK // OUT_TM,),
        in_specs=[
            pl.BlockSpec((OUT_TM, NSA_W), lambda i: (i, 0)),
            pl.BlockSpec((OUT_TM, GM_W), lambda i: (i, Z_U // GM_W)),
            pl.BlockSpec((OUT_TM, GM_W), lambda i: (i, Z_V // GM_W)),
            pl.BlockSpec((OUT_TM, D_MODEL), lambda i: (i, 0)),
            pl.BlockSpec((OUT_TM, D_MODEL), lambda i: (i, 1)),
            pl.BlockSpec((OUT_TM, D_MODEL), lambda i: (i, 0)),
            pl.BlockSpec((1, GM_W), lambda i: (0, 0)),
            pl.BlockSpec((1, GM_W), lambda i: (0, 0)),
            pl.BlockSpec((GM_GROUPS, GM_CHUNK, GM_CHUNK), lambda i: (0, 0, 0), pipeline_mode=once),
            pl.BlockSpec((GM_CHUNK, GM_W), lambda i: (0, 0), pipeline_mode=once),
            pl.BlockSpec((None, NSA_W, D_MODEL), const, pipeline_mode=once),
            pl.BlockSpec((None, GM_W, D_MODEL), const, pipeline_mode=once),
            pl.BlockSpec((None, D_MODEL, D_MODEL), const, pipeline_mode=once),
            pl.BlockSpec((1, D_MODEL), lambda i: (0, 0)),
        ],
        out_specs=[pl.BlockSpec((OUT_TM, D_MODEL), lambda i: (i, 0)),
                   pl.BlockSpec((OUT_TM, D_MODEL), lambda i: (i, 0))],
        out_shape=[jax.ShapeDtypeStruct((TOK, D_MODEL), F32),
                   jax.ShapeDtypeStruct((TOK, D_MODEL), BF16)],
        compiler_params=_cparams(("parallel",)),
        name="merge_out",
    )(a, z, z, z, z, x, ln_g, ln_b, w_s, bs_exp, wn, ws, wo, next_gain)


def _check_overlap_stencil():
    ci = np.arange(N_CMP - 1)[None, :] * CMP_STRIDE
    sj = np.arange(N_SBLK)[:, None] * SEL_BLOCK
    ov = np.clip(np.minimum(ci + CMP_BLOCK, sj + SEL_BLOCK) - np.maximum(ci, sj), 0, None) / CMP_BLOCK
    stencil = np.zeros((N_SBLK, N_CMP - 1))
    for j in range(N_SBLK):
        for c, w in ((4 * j - 1, 0.5), (4 * j, 1.0), (4 * j + 1, 1.0), (4 * j + 2, 1.0), (4 * j + 3, 0.5)):
            if 0 <= c < N_CMP - 1:
                stencil[j, c] = w
    assert np.array_equal(ov, stencil)


def _block_diag_ones():
    r = np.arange(LANE)
    return (r[:, None] // HEAD_DIM == r[None, :] // HEAD_DIM).astype(np.float32)


def _prep_w_in(w):
    L = w.shape[0]
    w = w.astype(BF16)
    kv = w[:, :, OFF_KV:OFF_NG].reshape(L, D_MODEL, 3, 2, N_KV, HEAD_DIM)
    kv = kv.transpose(0, 1, 2, 4, 3, 5).reshape(L, D_MODEL, 6 * KV_W)
    ng = w[:, :, OFF_NG:OFF_UV].reshape(L, D_MODEL, N_KV, HPG, 3).transpose(0, 1, 2, 4, 3)
    ng = ng.reshape(L, D_MODEL, N_KV, 3 * HPG)
    ng = jnp.pad(ng, ((0, 0), (0, 0), (0, 0), (0, GATE_GW - 3 * HPG))).reshape(L, D_MODEL, N_KV * GATE_GW)
    ng = jnp.pad(ng, ((0, 0), (0, 0), (0, SEC_W - N_KV * GATE_GW)))
    return jnp.concatenate([w[:, :, OFF_MG:], w[:, :, OFF_UV:OFF_MG], w[:, :, :OFF_KV], kv, ng], axis=2)


def _mixer(x, z, l, p, tables):
    t0, t1, tc = tables
    pos = jnp.stack([p["cmp_pos_k"][l], p["cmp_pos_v"][l]]).reshape(2, 1, CMP_BLOCK * HEAD_DIM)
    w1 = jnp.stack([p["cmp_k_w1"][l], p["cmp_v_w1"][l]]).astype(BF16)
    kc, vcT = _compress(z, pos, w1, p["cmp_k_w2"][l].astype(BF16), p["cmp_v_w2"][l].T.astype(BF16),
                        p["k_norm"][l, 0].reshape(1, HEAD_DIM))
    a = _attention(z, kc, vcT, t0, t1, tc)
    bs_exp = jnp.repeat(p["sgu_b"][l].T, GM_GW, axis=1)
    return _merge_out(a, z, x, p["sgu_norm_g"][l].reshape(1, GM_W), p["sgu_norm_b"][l].reshape(1, GM_W),
                      p["sgu_w"][l], bs_exp, p["wn"], p["ws"], p["wo"], l,
                      p["ffn2_norm"][l].reshape(1, D_MODEL))


def kernel(x, rel_bias, ffn1_norm, ffn1_w_gate, ffn1_w_up, ffn1_w_down, mix_norm, w_in, q_norm, k_norm, cmp_pos_k, cmp_pos_v, cmp_k_w1, cmp_k_w2, cmp_v_w1, cmp_v_w2, sgu_norm_g, sgu_norm_b, sgu_w, sgu_b, w_proj_nsa, w_proj_sgu, w_out, ffn2_norm, ffn2_w_gate, ffn2_w_up, ffn2_w_down):
    p = dict(cmp_pos_k=cmp_pos_k, cmp_pos_v=cmp_pos_v, cmp_k_w1=cmp_k_w1, cmp_k_w2=cmp_k_w2,
             cmp_v_w1=cmp_v_w1, cmp_v_w2=cmp_v_w2, k_norm=k_norm, sgu_norm_g=sgu_norm_g,
             sgu_norm_b=sgu_norm_b, sgu_w=sgu_w, sgu_b=sgu_b, wn=w_proj_nsa.astype(BF16),
             ws=w_proj_sgu.astype(BF16), wo=w_out.astype(BF16), ffn2_norm=ffn2_norm)
    near, cmpb = _bucket_tables()
    t0, t1, tc = _bias_tables(rel_bias, jnp.asarray(near), jnp.asarray(cmpb))
    tables = (t0, t1, tc)
    _check_overlap_stencil()
    bd = jnp.asarray(_block_diag_ones()).astype(BF16)
    w_in_r = _prep_w_in(w_in)
    f1 = (ffn1_w_gate, ffn1_w_up, ffn1_w_down)
    f2 = (ffn2_w_gate, ffn2_w_up, ffn2_w_down)

    y = x.reshape(TOK, D_MODEL)
    for l in range(DEPTH):
        y = _ffn(y, _rmsnorm(y, ffn1_norm[l].reshape(1, D_MODEL)), *f1, l)
        z = _proj_in(y, mix_norm[l].reshape(1, D_MODEL), w_in_r, l, bd,
                     jnp.tile(q_norm[l], PIN_TN // HEAD_DIM).reshape(1, PIN_TN),
                     jnp.tile(k_norm[l, 1], SEC_W // HEAD_DIM).reshape(1, SEC_W),
                     jnp.tile(k_norm[l, 2], SEC_W // HEAD_DIM).reshape(1, SEC_W))
        y, h = _mixer(y, z, l, p, tables)
        y = _ffn(y, h, *f2, l)
    return y.reshape(BATCH, SEQ, D_MODEL)
```

```python
import math

import numpy as np
import jax
import jax.numpy as jnp
from jax import lax
from jax.experimental import pallas as pl
from jax.experimental.pallas import tpu as pltpu

F32 = jnp.float32
BF16 = jnp.bfloat16

D_MODEL = 2048
BATCH = 4
SEQ = 2048
DEPTH = 2
TOK = BATCH * SEQ
HEAD_DIM = 64
N_HEADS = 16
N_KV = 4
HPG = 4
NSA_W = 1024
KV_W = 256
CMP_BLOCK = 32
CMP_STRIDE = 16
CMP_HIDDEN = 256
SEL_BLOCK = 64
N_SEL = 16
WINDOW = 512
GM_W = 1024
GM_CHUNK = 128
GM_GW = 128
GM_GROUPS = 8
N_BUCKETS = 32
MAX_DISTANCE = 128
D_FF = 5504
EPS = 1e-6
NEG = -1e30
FORCE = 1e4
OFF_KV = NSA_W
OFF_NG = OFF_KV + 6 * KV_W
OFF_UV = OFF_NG + 3 * N_HEADS
OFF_MG = OFF_UV + 2 * GM_W
LOG2E = 1.4426950408889634

LANE = 128
VMEM_LIMIT = 56 * 1024 * 1024

FFN_TF = 512
FFN_UP_TM = 1024
NORM_TM = 512
FFN_DN_TM = 512
FFN_DN_TN = 512
PIN_TM = 1024
PIN_TN = 1024
PIN_RC = 256
SEC_W = PIN_TN // 2
Z_U = 0
Z_V = Z_U + GM_W
Z_MG = Z_V + GM_W
Z_Q = Z_MG + 2 * D_MODEL
Z_KV = Z_Q + NSA_W
Z_NG = Z_KV + 6 * KV_W
Z_W = Z_NG + SEC_W
KV_GW = 2 * HEAD_DIM
GATE_GW = 16
TQ = 128
NQ = SEQ // TQ
NK = SEQ // TQ
QL = HPG * TQ
N_CMP = 128
CMP_BAND = 16
MASKED = -1
N_SBLK = SEQ // SEL_BLOCK
V_ROWS = HEAD_DIM + 16
FAR_GROUP = 4
Q_PAIR = 4
OUT_TM = 256


def _cparams(sem):
    return pltpu.CompilerParams(dimension_semantics=sem, vmem_limit_bytes=VMEM_LIMIT)


def _cast_rows(src_ref, dst_ref, chunk):
    def body(r, carry):
        rows = pl.ds(pl.multiple_of(r * chunk, chunk), chunk)
        dst_ref[rows, :] = src_ref[rows, :].astype(BF16)
        return carry

    lax.fori_loop(0, src_ref.shape[0] // chunk, body, 0)


def _rms_bf16(x, gain):
    ms = jnp.mean(x * x, axis=-1, keepdims=True)
    return (x * lax.rsqrt(ms + EPS) * gain).astype(BF16)


def _norm_body(x_ref, g_ref, h_ref):
    h_ref[...] = _rms_bf16(x_ref[...], g_ref[...])


def _rmsnorm(x, gain):
    return pl.pallas_call(
        _norm_body,
        grid=(TOK // NORM_TM,),
        in_specs=[pl.BlockSpec((NORM_TM, D_MODEL), lambda i: (i, 0)),
                  pl.BlockSpec((1, D_MODEL), lambda i: (0, 0))],
        out_specs=pl.BlockSpec((NORM_TM, D_MODEL), lambda i: (i, 0)),
        out_shape=jax.ShapeDtypeStruct((TOK, D_MODEL), BF16),
        compiler_params=_cparams(("parallel",)),
        name="rmsnorm",
    )(x, gain)


def _ffn_up_body(h_ref, wg_ref, wu_ref, a_ref, wgb_ref, wub_ref):
    @pl.when(pl.program_id(1) == 0)
    def _():
        _cast_rows(wg_ref, wgb_ref, LANE)
        _cast_rows(wu_ref, wub_ref, LANE)

    h = h_ref[...]
    g = jnp.dot(h, wgb_ref[...], preferred_element_type=F32)
    u = jnp.dot(h, wub_ref[...], preferred_element_type=F32)
    a_ref[...] = (g * jax.nn.sigmoid(g) * u).astype(BF16)


def _ffn_down_body(a_ref, wd_ref, x_ref, o_ref, wdb_ref):
    @pl.when(pl.program_id(1) == 0)
    def _():
        _cast_rows(wd_ref, wdb_ref, LANE)

    o_ref[...] = x_ref[...] + 0.5 * jnp.dot(a_ref[...], wdb_ref[...], preferred_element_type=F32)


def _ffn(x, h, wg, wu, wd, l):
    nf = pl.cdiv(D_FF, FFN_TF)
    a = pl.pallas_call(
        _ffn_up_body,
        grid=(nf, TOK // FFN_UP_TM),
        in_specs=[
            pl.BlockSpec((FFN_UP_TM, D_MODEL), lambda f, m: (m, 0)),
            pl.BlockSpec((None, D_MODEL, FFN_TF), lambda f, m: (l, 0, f)),
            pl.BlockSpec((None, D_MODEL, FFN_TF), lambda f, m: (l, 0, f)),
        ],
        out_specs=pl.BlockSpec((FFN_UP_TM, FFN_TF), lambda f, m: (m, f)),
        out_shape=jax.ShapeDtypeStruct((TOK, D_FF), BF16),
        scratch_shapes=[pltpu.VMEM((D_MODEL, FFN_TF), BF16), pltpu.VMEM((D_MODEL, FFN_TF), BF16)],
        compiler_params=_cparams(("arbitrary", "arbitrary")),
        name="ffn_up",
    )(h, wg, wu)
    return pl.pallas_call(
        _ffn_down_body,
        grid=(D_MODEL // FFN_DN_TN, TOK // FFN_DN_TM),
        in_specs=[
            pl.BlockSpec((FFN_DN_TM, D_FF), lambda n, m: (m, 0)),
            pl.BlockSpec((None, D_FF, FFN_DN_TN), lambda n, m: (l, 0, n)),
            pl.BlockSpec((FFN_DN_TM, FFN_DN_TN), lambda n, m: (m, n)),
        ],
        out_specs=pl.BlockSpec((FFN_DN_TM, FFN_DN_TN), lambda n, m: (m, n)),
        out_shape=jax.ShapeDtypeStruct((TOK, D_MODEL), F32),
        scratch_shapes=[pltpu.VMEM((D_FF, FFN_DN_TN), BF16)],
        compiler_params=_cparams(("arbitrary", "arbitrary")),
        name="ffn_down",
    )(a, wd, x)


def _head_rms(y, bd):
    y2 = (y * y).astype(BF16)
    ss = jnp.concatenate(
        [jnp.dot(y2[:, k * LANE:(k + 1) * LANE], bd, preferred_element_type=F32)
         for k in range(y.shape[1] // LANE)], axis=1)
    return y * lax.rsqrt(ss * (1.0 / HEAD_DIM) + EPS)


def _pin_body(x_ref, g_ref, wa_ref, wb_ref, bd_ref, qg_ref, k1g_ref, k2g_ref, o_ref, h_ref):
    j = pl.program_id(1)

    @pl.when(j == 0)
    def _():
        h_ref[...] = _rms_bf16(x_ref[...], g_ref[...])

    t_mg = Z_MG // PIN_TN
    t_q = Z_Q // PIN_TN
    t_kv = Z_KV // PIN_TN
    half = SEC_W

    def emit(act, w_ref):
        for r in range(PIN_TM // PIN_RC):
            rows = slice(r * PIN_RC, (r + 1) * PIN_RC)
            o_ref[rows, :] = act(jnp.dot(h_ref[rows, :], w_ref[...], preferred_element_type=F32))

    def halves(act_lo, act_hi):
        return lambda y: jnp.concatenate([act_lo(y[:, :half]), act_hi(y[:, half:])], axis=1)

    def kv_act(kg_ref):
        is_k = (lax.broadcasted_iota(jnp.int32, (1, half), 1) & (KV_GW - 1)) < HEAD_DIM
        return lambda y: jnp.where(is_k, _head_rms(y, bd_ref[...]) * kg_ref[...], y)

    @pl.when(j < t_mg)
    def _():
        emit(jax.nn.gelu, wa_ref)

    @pl.when((j >= t_mg) & (j < t_q))
    def _():
        emit(jax.nn.sigmoid, wa_ref)

    @pl.when((j >= t_q) & (j < t_kv))
    def _():
        emit(lambda y: _head_rms(y, bd_ref[...]) * (qg_ref[...] * (HEAD_DIM ** -0.5 * LOG2E)), wb_ref)

    @pl.when(j == t_kv)
    def _():
        emit(halves(lambda y: y, kv_act(k1g_ref)), wb_ref)

    @pl.when(j == t_kv + 1)
    def _():
        emit(halves(kv_act(k2g_ref), jax.nn.sigmoid), wb_ref)


def _proj_in(x, gain, wa, wb, l, bd, qg, k1g, k2g):
    na = Z_Q // PIN_TN
    return pl.pallas_call(
        _pin_body,
        grid=(TOK // PIN_TM, Z_W // PIN_TN),
        in_specs=[
            pl.BlockSpec((PIN_TM, D_MODEL), lambda i, j: (i, 0)),
            pl.BlockSpec((1, D_MODEL), lambda i, j: (0, 0)),
            pl.BlockSpec((None, D_MODEL, PIN_TN), lambda i, j: (l, 0, jnp.minimum(j, na - 1))),
            pl.BlockSpec((None, D_MODEL, PIN_TN), lambda i, j: (l, 0, jnp.maximum(j - na, 0))),
            pl.BlockSpec((LANE, LANE), lambda i, j: (0, 0)),
            pl.BlockSpec((1, PIN_TN), lambda i, j: (0, 0)),
            pl.BlockSpec((1, SEC_W), lambda i, j: (0, 0)),
            pl.BlockSpec((1, SEC_W), lambda i, j: (0, 0)),
        ],
        out_specs=pl.BlockSpec((PIN_TM, PIN_TN), lambda i, j: (i, j)),
        out_shape=jax.ShapeDtypeStruct((TOK, Z_W), F32),
        scratch_shapes=[pltpu.VMEM((PIN_TM, D_MODEL), BF16)],
        compiler_params=_cparams(("parallel", "arbitrary")),
        name="proj_in",
    )(x, gain, wa, wb, bd, qg, k1g, k2g)


def _cmp_body(z_ref, pos_ref, w1_ref, w2k_ref, w2vT_ref, kg_ref, kc_ref, vcT_ref):
    half = CMP_STRIDE * HEAD_DIM
    lane = lax.broadcasted_iota(jnp.int32, (N_CMP, LANE), 1)
    first = lane < HEAD_DIM
    xk, xv = [], []
    for j in range(CMP_STRIDE // 2):
        ev = z_ref[pl.ds(2 * j, N_CMP, stride=CMP_STRIDE), :]
        od = z_ref[pl.ds(2 * j + 1, N_CMP, stride=CMP_STRIDE), :]
        xk.append(jnp.where(first, ev, pltpu.roll(od, HEAD_DIM, 1)))
        xv.append(jnp.where(first, pltpu.roll(ev, HEAD_DIM, 1), od))

    def hidden(x, kind):
        xa = (x + pos_ref[kind, :, :half]).astype(BF16)
        xb = (x + pos_ref[kind, :, half:]).astype(BF16)
        a = jnp.dot(xa, w1_ref[kind, :half, :], preferred_element_type=F32)
        b = jnp.dot(xb, w1_ref[kind, half:, :], preferred_element_type=F32)
        hid = a + pltpu.roll(b, N_CMP - 1, 0)
        return (hid * jax.nn.sigmoid(hid)).astype(BF16)

    yk = jnp.dot(hidden(jnp.concatenate(xk, axis=1), 0), w2k_ref[...], preferred_element_type=F32)
    ms = jnp.mean(yk * yk, axis=-1, keepdims=True)
    kc_ref[0, 0] = (yk * lax.rsqrt(ms + EPS) * kg_ref[...]).astype(BF16)
    hv = hidden(jnp.concatenate(xv, axis=1), 1)
    vcT_ref[0, 0] = lax.dot_general(w2vT_ref[...], hv, (((1,), (1,)), ((), ())),
                                    preferred_element_type=F32).astype(BF16)


def _compress(z, pos, w1, w2k, w2vT, kg):
    return pl.pallas_call(
        _cmp_body,
        grid=(BATCH, N_KV),
        in_specs=[
            pl.BlockSpec((SEQ, KV_GW), lambda b, g: (b, Z_KV // KV_GW + g)),
            pl.BlockSpec((2, 1, CMP_BLOCK * HEAD_DIM), lambda b, g: (0, 0, 0)),
            pl.BlockSpec((2, CMP_BLOCK * HEAD_DIM, CMP_HIDDEN), lambda b, g: (0, 0, 0)),
            pl.BlockSpec((CMP_HIDDEN, HEAD_DIM), lambda b, g: (0, 0)),
            pl.BlockSpec((HEAD_DIM, CMP_HIDDEN), lambda b, g: (0, 0)),
            pl.BlockSpec((1, HEAD_DIM), lambda b, g: (0, 0)),
        ],
        out_specs=[
            pl.BlockSpec((1, 1, N_CMP, HEAD_DIM), lambda b, g: (b, g, 0, 0)),
            pl.BlockSpec((1, 1, HEAD_DIM, N_CMP), lambda b, g: (b, g, 0, 0)),
        ],
        out_shape=[
            jax.ShapeDtypeStruct((BATCH, N_KV, N_CMP, HEAD_DIM), BF16),
            jax.ShapeDtypeStruct((BATCH, N_KV, HEAD_DIM, N_CMP), BF16),
        ],
        compiler_params=_cparams(("parallel", "parallel")),
        name="compress",
    )(z, pos, w1, w2k, w2vT, kg)


def _np_bucket(dist):
    n = np.maximum(dist, 0)
    max_exact = N_BUCKETS // 2
    nf = np.maximum(n, 1).astype(np.float32)
    large = max_exact + (np.log(nf / np.float32(max_exact)) / np.float32(math.log(MAX_DISTANCE / max_exact))
                         * np.float32(N_BUCKETS - max_exact)).astype(np.int32)
    large = np.minimum(large, N_BUCKETS - 1)
    return np.where(n < max_exact, n, large).astype(np.int32)


def _bucket_tables():
    kj = np.arange(TQ)[:, None]
    qi = np.arange(TQ)[None, :]
    near = np.stack([np.where(qi >= kj, _np_bucket(qi - kj), MASKED), _np_bucket(TQ + qi - kj)])
    start = np.array([_cmp_band_start(i) for i in range(NQ)])[:, None, None]
    c = start + np.arange(CMP_BAND)[None, :, None]
    t = (np.arange(NQ)[:, None, None] * TQ + qi[None])
    dist_band = t - (c * CMP_STRIDE + CMP_BLOCK - 1)
    cmp_b = np.where(dist_band >= 0, _np_bucket(dist_band), MASKED)
    dist = t - (np.arange(N_CMP)[None, :, None] * CMP_STRIDE + CMP_BLOCK - 1)
    for i in range(NQ):
        s0 = _cmp_band_start(i)
        assert np.all(dist[i, :s0] >= 0) and np.all(_np_bucket(dist[i, :s0]) == N_BUCKETS - 1)
        assert np.all(dist[i, s0 + CMP_BAND:] < 0)
    return near.astype(np.int32), cmp_b.astype(np.int32)


def _cmp_band_start(i):
    return max((TQ // CMP_STRIDE) * i - 8, 0)


def _bias_body(rb_ref, near_ref, cmpb_ref, t0_ref, t1_ref, tc_ref):
    g = pl.program_id(0)

    def lut(bk, h):
        val = jnp.full(bk.shape, NEG, F32)
        for b in range(N_BUCKETS):
            val = jnp.where(bk == b, rb_ref[b, h], val)
        return val

    for hh in range(HPG):
        h = g * HPG + hh
        sl = slice(hh * TQ, (hh + 1) * TQ)
        far = rb_ref[N_BUCKETS - 1, h]
        t0_ref[0, :, sl] = (lut(near_ref[0], h) - far) * LOG2E
        t1_ref[0, :, sl] = (lut(near_ref[1], h) - far) * LOG2E

        def body(i, carry):
            start = pl.multiple_of(jnp.maximum((TQ // CMP_STRIDE) * i - 8, 0), 8)
            row = lax.broadcasted_iota(jnp.int32, (N_CMP, TQ), 0)
            tc_ref[0, i, :, sl] = jnp.where(row < start, far * LOG2E, NEG)
            tc_ref[0, i, pl.ds(start, CMP_BAND), sl] = lut(cmpb_ref[i], h) * LOG2E
            return carry

        lax.fori_loop(0, NQ, body, 0)


def _bias_tables(rel_bias, near, cmpb):
    return pl.pallas_call(
        _bias_body,
        grid=(N_KV,),
        in_specs=[
            pl.BlockSpec(memory_space=pltpu.SMEM),
            pl.BlockSpec((2, TQ, TQ), lambda g: (0, 0, 0)),
            pl.BlockSpec((NQ, CMP_BAND, TQ), lambda g: (0, 0, 0)),
        ],
        out_specs=[
            pl.BlockSpec((1, TQ, QL), lambda g: (g, 0, 0)),
            pl.BlockSpec((1, TQ, QL), lambda g: (g, 0, 0)),
            pl.BlockSpec((1, NQ, N_CMP, QL), lambda g: (g, 0, 0, 0)),
        ],
        out_shape=[
            jax.ShapeDtypeStruct((N_KV, TQ, QL), F32),
            jax.ShapeDtypeStruct((N_KV, TQ, QL), F32),
            jax.ShapeDtypeStruct((N_KV, NQ, N_CMP, QL), F32),
        ],
        compiler_params=_cparams(("arbitrary",)),
        name="bias_tables",
    )(rel_bias, near, cmpb)


def _attn_body(zq_ref, zs_ref, zw_ref, zg_ref, kc_ref, vcT_ref, t0_ref, t1_ref, tc_ref, o_ref,
               ks_ref, vsT_ref, kw_ref, vwT_ref, sel_ref, sfar_ref, gsc_ref, psum_ref):
    g = pl.program_id(1)
    ip = pl.program_id(2)

    @pl.when(ip == 0)
    def _():
        ones = jnp.ones((V_ROWS - HEAD_DIM, TQ), BF16)

        def fill(n, carry):
            for z_ref, k_ref, vT_ref in ((zs_ref, ks_ref, vsT_ref), (zw_ref, kw_ref, vwT_ref)):
                blk = z_ref[pl.ds(pl.multiple_of(n * TQ, TQ), TQ), :]
                k_ref[n] = blk[:, :HEAD_DIM].astype(BF16)
                vT_ref[n, :HEAD_DIM, :] = blk.T[HEAD_DIM:, :].astype(BF16)
                vT_ref[n, HEAD_DIM:, :] = ones
            return carry

        lax.fori_loop(0, NK, fill, 0)

    blk = lax.broadcasted_iota(jnp.int32, (N_SBLK, TQ), 0)
    ql = lax.broadcasted_iota(jnp.int32, (N_SBLK, TQ), 1)
    key1 = lax.broadcasted_iota(jnp.int32, (TQ, TQ), 0)
    q1 = lax.broadcasted_iota(jnp.int32, (TQ, TQ), 1)
    nlast = WINDOW // TQ

    def sel_mask(t, n):
        r0 = sel_ref[t, pl.ds(2 * n, 1), :]
        r1 = sel_ref[t, pl.ds(2 * n + 1, 1), :]
        return jnp.where(key1 < SEL_BLOCK, r0, r1) > 0.5

    def scores(qT, k_ref, n, bias, mask):
        sc = jnp.dot(k_ref[n], qT, preferred_element_type=F32)
        if bias is not None:
            sc = sc + bias
        if mask is None:
            return sc
        return jnp.concatenate(
            [jnp.where(mask, sc[:, h * TQ:(h + 1) * TQ], NEG) for h in range(HPG)], axis=1)

    def colmax8(sc):
        return jnp.max(sc.reshape(TQ // 8, 8, QL), axis=0)

    def probs(sc, mrow):
        return jnp.exp2(sc - mrow).astype(BF16)

    def pv(vT_ref, ns, ps):
        vv = vT_ref[ns[0]] if len(ns) == 1 else jnp.concatenate([vT_ref[n] for n in ns], axis=1)
        pp = ps[0] if len(ps) == 1 else jnp.concatenate(ps, axis=0)
        return jnp.dot(vv, pp, preferred_element_type=F32)

    def finish(acc):
        return acc[:HEAD_DIM] * (1.0 / acc[HEAD_DIM:HEAD_DIM + 1])

    def stage_q(t):
        i = Q_PAIR * ip + t
        rows = slice(t * TQ, (t + 1) * TQ)
        qt = zq_ref[rows, :].T
        qT = jnp.concatenate([qt[h * HEAD_DIM:(h + 1) * HEAD_DIM] for h in range(HPG)], axis=1).astype(BF16)
        gsc_ref[t] = zg_ref[rows, :].T
        return dict(i=i, qT=qT, n1=jnp.maximum(i - 1, 0), i1=jnp.full((TQ, TQ), i, jnp.int32),
                    nfar=jnp.maximum(i - 1, 0))

    def stage_cmp_scores(t, tl):
        tl["s_c"] = jnp.dot(kc_ref[0, 0], tl["qT"], preferred_element_type=F32) + tc_ref[0, t]
        tl["m_c"] = jnp.maximum(jnp.max(colmax8(tl["s_c"]), axis=0, keepdims=True), 0.1 * NEG)

    def stage_cmp(t, tl):
        e = jnp.exp2(tl["s_c"] - tl["m_c"])
        l = jnp.sum(e, axis=0, keepdims=True)
        p = e * jnp.where(l > 0.0, 1.0 / l, 0.0)
        tl["ocT"] = jnp.dot(vcT_ref[0, 0], p.astype(BF16), preferred_element_type=F32)
        psum_ref[t] = p[:, 0:TQ] + p[:, TQ:2 * TQ] + p[:, 2 * TQ:3 * TQ] + p[:, 3 * TQ:4 * TQ]

    def stage_select(t, tl):
        i = tl["i"]
        ratio = SEL_BLOCK // CMP_STRIDE
        part = [psum_ref[t, pl.ds(r, N_SBLK, stride=ratio), :] for r in range(ratio)]
        prev_last = jnp.where(blk == 0, 0.0, pltpu.roll(part[3], 1, 0))
        imp = (part[0] + part[1] + part[2]) + 0.5 * (part[3] + prev_last)
        cur = jnp.full((N_SBLK, TQ), i, jnp.int32) * (TQ // SEL_BLOCK) + ql // SEL_BLOCK
        forced = (blk == 0) | (blk == cur) | (blk == cur - 1)
        score = jnp.where(blk <= cur, jnp.where(forced, FORCE, imp), NEG)
        sub = 8
        pieces = [score[k * sub:(k + 1) * sub, :] for k in range(N_SBLK // sub)]
        ranks = [jnp.zeros((sub, TQ), F32) for _ in pieces]
        for jj in range(N_SBLK):
            row = score[jj:jj + 1, :]
            for k, pc in enumerate(pieces):
                if k * sub > jj:
                    beats = row >= pc
                elif (k + 1) * sub - 1 < jj:
                    beats = row > pc
                else:
                    beats = (row > pc) | ((row == pc) & (blk[k * sub:(k + 1) * sub, :] > jj))
                ranks[k] = ranks[k] + jnp.where(beats, 1.0, 0.0)
        rank = jnp.concatenate(ranks, axis=0)
        sel_ref[t] = jnp.where((rank < N_SEL) & (blk <= cur), 1.0, 0.0)

    def stage_window_scores(t, tl):
        i, n1, i1, qT = tl["i"], tl["n1"], tl["i1"], tl["qT"]
        n2, n3, n4 = jnp.maximum(i - 2, 0), jnp.maximum(i - 3, 0), jnp.maximum(i - nlast, 0)
        w_d = scores(qT, kw_ref, i, t0_ref[0], None)
        w_1 = scores(qT, kw_ref, n1, t1_ref[0], i1 >= 1)
        w_2 = scores(qT, kw_ref, n2, None, i1 >= 2)
        w_3 = scores(qT, kw_ref, n3, None, i1 >= 3)
        w_4 = scores(qT, kw_ref, n4, None, (q1 < key1) & (i1 >= nlast))
        mxw = jnp.maximum(jnp.maximum(jnp.maximum(colmax8(w_d), colmax8(w_1)),
                                      jnp.maximum(colmax8(w_2), colmax8(w_3))), colmax8(w_4))
        tl["w"] = (w_d, w_1, w_2, w_3, w_4)
        tl["wn"] = (i, n1, n2, n3, n4)
        tl["m_w"] = jnp.max(mxw, axis=0, keepdims=True)

    def stage_window_pv(t, tl):
        (w_d, w_1, w_2, w_3, w_4), (i, n1, n2, n3, n4), m_w = tl["w"], tl["wn"], tl["m_w"]
        tl["owT"] = finish(pv(vwT_ref, [i], [probs(w_d, m_w)])
                           + pv(vwT_ref, [n1, n2], [probs(w_1, m_w), probs(w_2, m_w)])
                           + pv(vwT_ref, [n3, n4], [probs(w_3, m_w), probs(w_4, m_w)]))

    def stage_near(t, tl):
        i, n1, i1, qT = tl["i"], tl["n1"], tl["i1"], tl["qT"]
        tl["s_d"] = scores(qT, ks_ref, i, t0_ref[0], None)
        tl["s_1"] = scores(qT, ks_ref, n1, t1_ref[0], sel_mask(t, n1) & (i1 >= 1))
        tl["mx"] = jnp.maximum(colmax8(tl["s_d"]), colmax8(tl["s_1"]))

    tiles = [stage_q(t) for t in range(Q_PAIR)]
    for stage in (stage_cmp_scores, stage_cmp, stage_window_scores, stage_select, stage_near, stage_window_pv):
        for t, tl in enumerate(tiles):
            stage(t, tl)
    assert Q_PAIR == FAR_GROUP == 4
    extra = (2, 3)

    def far_scores(tr, mxs, ts):
        out = list(mxs)
        for t in ts:
            tl = tiles[t]
            for c in range(FAR_GROUP):
                n = FAR_GROUP * tr + c
                sc = scores(tl["qT"], ks_ref, n, None,
                            sel_mask(t, n) & (jnp.full((TQ, TQ), n, jnp.int32) < tl["nfar"]))
                sfar_ref[t, n] = sc
                out[t] = jnp.maximum(out[t], colmax8(sc))
        return tuple(out)

    every = tuple(range(Q_PAIR))
    mxs = lax.fori_loop(0, ip, lambda tr, c: far_scores(tr, c, every), tuple(tl["mx"] for tl in tiles))
    mxs = far_scores(ip, mxs, extra)
    m_s = [jnp.max(mx, axis=0, keepdims=True) for mx in mxs]
    accs = tuple(pv(vsT_ref, [tl["i"], tl["n1"]], [probs(tl["s_d"], m_s[t]), probs(tl["s_1"], m_s[t])])
                 for t, tl in enumerate(tiles))

    def far_pv(tr, accs_, ts):
        ns = [FAR_GROUP * tr + c for c in range(FAR_GROUP)]
        out = list(accs_)
        for t in ts:
            out[t] = out[t] + pv(vsT_ref, ns, [probs(sfar_ref[t, n], m_s[t]) for n in ns])
        return tuple(out)

    accs = lax.fori_loop(0, ip, lambda tr, c: far_pv(tr, c, every), accs)
    accs = far_pv(ip, accs, extra)

    for t, tl in enumerate(tiles):
        base = g * GATE_GW
        gate = [jnp.concatenate([gsc_ref[t, pl.ds(base + br * HPG + h, 1), :] for h in range(HPG)], axis=1)
                for br in range(3)]
        aT = gate[0] * tl["ocT"] + gate[1] * finish(accs[t]) + gate[2] * tl["owT"]
        a4 = jnp.concatenate([aT[:, h * TQ:(h + 1) * TQ] for h in range(HPG)], axis=0)
        o_ref[t * TQ:(t + 1) * TQ, :] = a4.T.astype(o_ref.dtype)


def _attention(z, kc, vcT, t0, t1, tc):
    qw = HPG * HEAD_DIM
    rows = Q_PAIR * TQ
    npair = NQ // Q_PAIR
    return pl.pallas_call(
        _attn_body,
        grid=(BATCH, N_KV, npair),
        in_specs=[
            pl.BlockSpec((rows, qw), lambda b, g, i: (b * npair + i, Z_Q // qw + g)),
            pl.BlockSpec((SEQ, KV_GW), lambda b, g, i: (b, (Z_KV + 2 * KV_W) // KV_GW + g)),
            pl.BlockSpec((SEQ, KV_GW), lambda b, g, i: (b, (Z_KV + 4 * KV_W) // KV_GW + g)),
            pl.BlockSpec((rows, LANE), lambda b, g, i: (b * npair + i, Z_NG // LANE)),
            pl.BlockSpec((1, 1, N_CMP, HEAD_DIM), lambda b, g, i: (b, g, 0, 0)),
            pl.BlockSpec((1, 1, HEAD_DIM, N_CMP), lambda b, g, i: (b, g, 0, 0)),
            pl.BlockSpec((1, TQ, QL), lambda b, g, i: (g, 0, 0)),
            pl.BlockSpec((1, TQ, QL), lambda b, g, i: (g, 0, 0)),
            pl.BlockSpec((1, Q_PAIR, N_CMP, QL), lambda b, g, i: (g, i, 0, 0)),
        ],
        out_specs=pl.BlockSpec((rows, qw), lambda b, g, i: (b * npair + i, g)),
        out_shape=jax.ShapeDtypeStruct((TOK, NSA_W), BF16),
        scratch_shapes=[
            pltpu.VMEM((NK, TQ, HEAD_DIM), BF16),
            pltpu.VMEM((NK, V_ROWS, TQ), BF16),
            pltpu.VMEM((NK, TQ, HEAD_DIM), BF16),
            pltpu.VMEM((NK, V_ROWS, TQ), BF16),
            pltpu.VMEM((Q_PAIR, N_SBLK, TQ), F32),
            pltpu.VMEM((Q_PAIR, NK, TQ, QL), F32),
            pltpu.VMEM((Q_PAIR, LANE, TQ), F32),
            pltpu.VMEM((Q_PAIR, N_CMP, TQ), F32),
        ],
        compiler_params=_cparams(("parallel", "parallel", "arbitrary")),
        name="nsa_attention",
    )(z, z, z, z, kc, vcT, t0, t1, tc)


def _sgu_tile(u_ref, v_ref, g_ref, b_ref, w_ref, bs_ref):
    r = lax.broadcasted_iota(jnp.int32, (GM_CHUNK, GM_CHUNK), 0)
    c = lax.broadcasted_iota(jnp.int32, (GM_CHUNK, GM_CHUNK), 1)
    tril = r >= c
    ws = [jnp.where(tril, w_ref[gi], 0.0).astype(BF16) for gi in range(GM_GROUPS)]
    chunks = []
    for ch in range(u_ref.shape[0] // GM_CHUNK):
        rows = slice(ch * GM_CHUNK, (ch + 1) * GM_CHUNK)
        v = v_ref[rows, :]
        mu = jnp.mean(v, axis=-1, keepdims=True)
        vc = v - mu
        var = jnp.mean(vc * vc, axis=-1, keepdims=True)
        vn = (vc * lax.rsqrt(var + EPS) * g_ref[...] + b_ref[...]).astype(BF16)
        groups = []
        for gi in range(GM_GROUPS):
            cols = slice(gi * GM_GW, (gi + 1) * GM_GW)
            mixed = jnp.dot(ws[gi], vn[:, cols], preferred_element_type=F32) + bs_ref[:, cols]
            groups.append((u_ref[rows, cols] * mixed).astype(BF16))
        chunks.append(jnp.concatenate(groups, axis=1))
    return jnp.concatenate(chunks, axis=0)


def _out_body(a_ref, u_ref, v_ref, m1_ref, m2_ref, x_ref, lg_ref, lb_ref, sw_ref, bs_ref,
              wn_ref, ws_ref, wo_ref, gn_ref, o_ref, h_ref):
    sg = _sgu_tile(u_ref, v_ref, lg_ref, lb_ref, sw_ref, bs_ref)
    pa = jnp.dot(a_ref[...], wn_ref[...], preferred_element_type=F32)
    ps = jnp.dot(sg, ws_ref[...], preferred_element_type=F32)
    merged = (m1_ref[...] * pa + m2_ref[...] * ps).astype(BF16)
    y = x_ref[...] + jnp.dot(merged, wo_ref[...], preferred_element_type=F32)
    o_ref[...] = y
    h_ref[...] = _rms_bf16(y, gn_ref[...])


def _merge_out(a, z, x, ln_g, ln_b, w_s, bs_exp, wn, ws, wo, l, next_gain):
    const = lambda i: (l, 0, 0)
    once = pl.Buffered(1)
    return pl.pallas_call(
        _out_body,
        grid=(TOK // OUT_TM,),
        in_specs=[
            pl.BlockSpec((OUT_TM, NSA_W), lambda i: (i, 0)),
            pl.BlockSpec((OUT_TM, GM_W), lambda i: (i, Z_U // GM_W)),
            pl.BlockSpec((OUT_TM, GM_W), lambda i: (i, Z_V // GM_W)),
            pl.BlockSpec((OUT_TM, D_MODEL), lambda i: (i, Z_MG // D_MODEL)),
            pl.BlockSpec((OUT_TM, D_MODEL), lambda i: (i, Z_MG // D_MODEL + 1)),
            pl.BlockSpec((OUT_TM, D_MODEL), lambda i: (i, 0)),
            pl.BlockSpec((1, GM_W), lambda i: (0, 0)),
            pl.BlockSpec((1, GM_W), lambda i: (0, 0)),
            pl.BlockSpec((GM_GROUPS, GM_CHUNK, GM_CHUNK), lambda i: (0, 0, 0), pipeline_mode=once),
            pl.BlockSpec((GM_CHUNK, GM_W), lambda i: (0, 0), pipeline_mode=once),
            pl.BlockSpec((None, NSA_W, D_MODEL), const, pipeline_mode=once),
            pl.BlockSpec((None, GM_W, D_MODEL), const, pipeline_mode=once),
            pl.BlockSpec((None, D_MODEL, D_MODEL), const, pipeline_mode=once),
            pl.BlockSpec((1, D_MODEL), lambda i: (0, 0)),
        ],
        out_specs=[pl.BlockSpec((OUT_TM, D_MODEL), lambda i: (i, 0)),
                   pl.BlockSpec((OUT_TM, D_MODEL), lambda i: (i, 0))],
        out_shape=[jax.ShapeDtypeStruct((TOK, D_MODEL), F32),
                   jax.ShapeDtypeStruct((TOK, D_MODEL), BF16)],
        compiler_params=_cparams(("parallel",)),
        name="merge_out",
    )(a, z, z, z, z, x, ln_g, ln_b, w_s, bs_exp, wn, ws, wo, next_gain)


def _check_overlap_stencil():
    ci = np.arange(N_CMP - 1)[None, :] * CMP_STRIDE
    sj = np.arange(N_SBLK)[:, None] * SEL_BLOCK
    ov = np.clip(np.minimum(ci + CMP_BLOCK, sj + SEL_BLOCK) - np.maximum(ci, sj), 0, None) / CMP_BLOCK
    stencil = np.zeros((N_SBLK, N_CMP - 1))
    for j in range(N_SBLK):
        for c, w in ((4 * j - 1, 0.5), (4 * j, 1.0), (4 * j + 1, 1.0), (4 * j + 2, 1.0), (4 * j + 3, 0.5)):
            if 0 <= c < N_CMP - 1:
                stencil[j, c] = w
    assert np.array_equal(ov, stencil)


def _block_diag_ones():
    r = np.arange(LANE)
    return (r[:, None] // HEAD_DIM == r[None, :] // HEAD_DIM).astype(np.float32)


def _prep_w_in(w):
    L = w.shape[0]
    wa = w[:, :, OFF_UV:].astype(BF16)
    kv = w[:, :, OFF_KV:OFF_NG].reshape(L, D_MODEL, 3, 2, N_KV, HEAD_DIM)
    kv = kv.transpose(0, 1, 2, 4, 3, 5).reshape(L, D_MODEL, 6 * KV_W)
    ng = w[:, :, OFF_NG:OFF_UV].reshape(L, D_MODEL, N_KV, HPG, 3).transpose(0, 1, 2, 4, 3)
    ng = ng.reshape(L, D_MODEL, N_KV, 3 * HPG)
    ng = jnp.pad(ng, ((0, 0), (0, 0), (0, 0), (0, GATE_GW - 3 * HPG))).reshape(L, D_MODEL, N_KV * GATE_GW)
    ng = jnp.pad(ng, ((0, 0), (0, 0), (0, SEC_W - N_KV * GATE_GW)))
    wb = jnp.concatenate([w[:, :, :OFF_KV], kv, ng], axis=2).astype(BF16)
    return wa, wb


def _mixer(x, z, l, p, tables):
    t0, t1, tc = tables
    pos = jnp.stack([p["cmp_pos_k"][l], p["cmp_pos_v"][l]]).reshape(2, 1, CMP_BLOCK * HEAD_DIM)
    w1 = jnp.stack([p["cmp_k_w1"][l], p["cmp_v_w1"][l]]).astype(BF16)
    kc, vcT = _compress(z, pos, w1, p["cmp_k_w2"][l].astype(BF16), p["cmp_v_w2"][l].T.astype(BF16),
                        p["k_norm"][l, 0].reshape(1, HEAD_DIM))
    a = _attention(z, kc, vcT, t0, t1, tc)
    bs_exp = jnp.repeat(p["sgu_b"][l].T, GM_GW, axis=1)
    return _merge_out(a, z, x, p["sgu_norm_g"][l].reshape(1, GM_W), p["sgu_norm_b"][l].reshape(1, GM_W),
                      p["sgu_w"][l], bs_exp, p["wn"], p["ws"], p["wo"], l,
                      p["ffn2_norm"][l].reshape(1, D_MODEL))


def kernel(x, rel_bias, ffn1_norm, ffn1_w_gate, ffn1_w_up, ffn1_w_down, mix_norm, w_in, q_norm, k_norm, cmp_pos_k, cmp_pos_v, cmp_k_w1, cmp_k_w2, cmp_v_w1, cmp_v_w2, sgu_norm_g, sgu_norm_b, sgu_w, sgu_b, w_proj_nsa, w_proj_sgu, w_out, ffn2_norm, ffn2_w_gate, ffn2_w_up, ffn2_w_down):
    p = dict(cmp_pos_k=cmp_pos_k, cmp_pos_v=cmp_pos_v, cmp_k_w1=cmp_k_w1, cmp_k_w2=cmp_k_w2,
             cmp_v_w1=cmp_v_w1, cmp_v_w2=cmp_v_w2, k_norm=k_norm, sgu_norm_g=sgu_norm_g,
             sgu_norm_b=sgu_norm_b, sgu_w=sgu_w, sgu_b=sgu_b, wn=w_proj_nsa.astype(BF16),
             ws=w_proj_sgu.astype(BF16), wo=w_out.astype(BF16), ffn2_norm=ffn2_norm)
    near, cmpb = _bucket_tables()
    t0, t1, tc = _bias_tables(rel_bias, jnp.asarray(near), jnp.asarray(cmpb))
    tables = (t0, t1, tc)
    _check_overlap_stencil()
    bd = jnp.asarray(_block_diag_ones()).astype(BF16)
    w_in_a, w_in_b = _prep_w_in(w_in)
    f1 = (ffn1_w_gate, ffn1_w_up, ffn1_w_down)
    f2 = (ffn2_w_gate, ffn2_w_up, ffn2_w_down)

    y = x.reshape(TOK, D_MODEL)
    for l in range(DEPTH):
        y = _ffn(y, _rmsnorm(y, ffn1_norm[l].reshape(1, D_MODEL)), *f1, l)
        z = _proj_in(y, mix_norm[l].reshape(1, D_MODEL), w_in_a, w_in_b, l, bd,
                     jnp.tile(q_norm[l], PIN_TN // HEAD_DIM).reshape(1, PIN_TN),
                     jnp.tile(k_norm[l, 1], SEC_W // HEAD_DIM).reshape(1, SEC_W),
                     jnp.tile(k_norm[l, 2], SEC_W // HEAD_DIM).reshape(1, SEC_W))
        y, h = _mixer(y, z, l, p, tables)
        y = _ffn(y, h, *f2, l)
    return y.reshape(BATCH, SEQ, D_MODEL)
```

```python
import math

import numpy as np
import jax
import jax.numpy as jnp
from jax import lax
from jax.experimental import pallas as pl
from jax.experimental.pallas import tpu as pltpu

F32 = jnp.float32
BF16 = jnp.bfloat16

D_MODEL = 2048
BATCH = 4
SEQ = 2048
DEPTH = 2
TOK = BATCH * SEQ
HEAD_DIM = 64
N_HEADS = 16
N_KV = 4
HPG = 4
NSA_W = 1024
KV_W = 256
CMP_BLOCK = 32
CMP_STRIDE = 16
CMP_HIDDEN = 256
SEL_BLOCK = 64
N_SEL = 16
WINDOW = 512
GM_W = 1024
GM_CHUNK = 128
GM_GW = 128
GM_GROUPS = 8
N_BUCKETS = 32
MAX_DISTANCE = 128
D_FF = 5504
EPS = 1e-6
NEG = -1e30
FORCE = 1e4
OFF_KV = NSA_W
OFF_NG = OFF_KV + 6 * KV_W
OFF_UV = OFF_NG + 3 * N_HEADS
OFF_MG = OFF_UV + 2 * GM_W
LOG2E = 1.4426950408889634

LANE = 128
SUBLANE = 8
VMEM_LIMIT = 56 * 1024 * 1024

FFN_TF = 512
FFN_UP_TM = 1024
NORM_TM = 512
FFN_DN_TM = 512
FFN_DN_TN = 512
PIN_TM = 1024
PIN_TN = 1024
PIN_RC = 256
SEC_W = PIN_TN // 2
Z_MG = 0
Z_U = Z_MG + 2 * D_MODEL
Z_V = Z_U + GM_W
Z_Q = Z_V + GM_W
Z_KV = Z_Q + NSA_W
Z_NG = Z_KV + 6 * KV_W
Z_W = Z_NG + SEC_W
KV_GW = 2 * HEAD_DIM
GATE_GW = 16
TQ = 128
NQ = SEQ // TQ
NK = SEQ // TQ
QL = HPG * TQ
N_CMP = 128
CMP_BAND = 16
MASKED = -1
N_SBLK = SEQ // SEL_BLOCK
V_ROWS = HEAD_DIM + 16
FAR_GROUP = 4
Q_PAIR = 4
OUT_TM = 256


def _cparams(sem):
    return pltpu.CompilerParams(dimension_semantics=sem, vmem_limit_bytes=VMEM_LIMIT)


def _cast_rows(src_ref, dst_ref, chunk):
    def body(r, carry):
        rows = pl.ds(pl.multiple_of(r * chunk, chunk), chunk)
        dst_ref[rows, :] = src_ref[rows, :].astype(BF16)
        return carry

    lax.fori_loop(0, src_ref.shape[0] // chunk, body, 0)


def _rms_bf16(x, gain):
    ms = jnp.mean(x * x, axis=-1, keepdims=True)
    return (x * lax.rsqrt(ms + EPS) * gain).astype(BF16)


def _norm_body(x_ref, g_ref, h_ref):
    h_ref[...] = _rms_bf16(x_ref[...], g_ref[...])


def _rmsnorm(x, gain):
    return pl.pallas_call(
        _norm_body,
        grid=(TOK // NORM_TM,),
        in_specs=[pl.BlockSpec((NORM_TM, D_MODEL), lambda i: (i, 0)),
                  pl.BlockSpec((1, D_MODEL), lambda i: (0, 0))],
        out_specs=pl.BlockSpec((NORM_TM, D_MODEL), lambda i: (i, 0)),
        out_shape=jax.ShapeDtypeStruct((TOK, D_MODEL), BF16),
        compiler_params=_cparams(("parallel",)),
        name="rmsnorm",
    )(x, gain)


def _ffn_up_body(h_ref, wg_ref, wu_ref, a_ref, wgb_ref, wub_ref):
    @pl.when(pl.program_id(1) == 0)
    def _():
        _cast_rows(wg_ref, wgb_ref, LANE)
        _cast_rows(wu_ref, wub_ref, LANE)

    h = h_ref[...]
    g = jnp.dot(h, wgb_ref[...], preferred_element_type=F32)
    u = jnp.dot(h, wub_ref[...], preferred_element_type=F32)
    a_ref[...] = (g * jax.nn.sigmoid(g) * u).astype(BF16)


def _ffn_down_body(a_ref, wd_ref, x_ref, o_ref, wdb_ref):
    @pl.when(pl.program_id(1) == 0)
    def _():
        _cast_rows(wd_ref, wdb_ref, LANE)

    o_ref[...] = x_ref[...] + 0.5 * jnp.dot(a_ref[...], wdb_ref[...], preferred_element_type=F32)


def _ffn(x, h, wg, wu, wd, l):
    nf = pl.cdiv(D_FF, FFN_TF)
    a = pl.pallas_call(
        _ffn_up_body,
        grid=(nf, TOK // FFN_UP_TM),
        in_specs=[
            pl.BlockSpec((FFN_UP_TM, D_MODEL), lambda f, m: (m, 0)),
            pl.BlockSpec((None, D_MODEL, FFN_TF), lambda f, m: (l, 0, f)),
            pl.BlockSpec((None, D_MODEL, FFN_TF), lambda f, m: (l, 0, f)),
        ],
        out_specs=pl.BlockSpec((FFN_UP_TM, FFN_TF), lambda f, m: (m, f)),
        out_shape=jax.ShapeDtypeStruct((TOK, D_FF), BF16),
        scratch_shapes=[pltpu.VMEM((D_MODEL, FFN_TF), BF16), pltpu.VMEM((D_MODEL, FFN_TF), BF16)],
        compiler_params=_cparams(("arbitrary", "arbitrary")),
        name="ffn_up",
    )(h, wg, wu)
    return pl.pallas_call(
        _ffn_down_body,
        grid=(D_MODEL // FFN_DN_TN, TOK // FFN_DN_TM),
        in_specs=[
            pl.BlockSpec((FFN_DN_TM, D_FF), lambda n, m: (m, 0)),
            pl.BlockSpec((None, D_FF, FFN_DN_TN), lambda n, m: (l, 0, n)),
            pl.BlockSpec((FFN_DN_TM, FFN_DN_TN), lambda n, m: (m, n)),
        ],
        out_specs=pl.BlockSpec((FFN_DN_TM, FFN_DN_TN), lambda n, m: (m, n)),
        out_shape=jax.ShapeDtypeStruct((TOK, D_MODEL), F32),
        scratch_shapes=[pltpu.VMEM((D_FF, FFN_DN_TN), BF16)],
        compiler_params=_cparams(("arbitrary", "arbitrary")),
        name="ffn_down",
    )(a, wd, x)


def _head_rms(y, bd):
    y2 = (y * y).astype(BF16)
    ss = jnp.concatenate(
        [jnp.dot(y2[:, k * LANE:(k + 1) * LANE], bd, preferred_element_type=F32)
         for k in range(y.shape[1] // LANE)], axis=1)
    return y * lax.rsqrt(ss * (1.0 / HEAD_DIM) + EPS)


def _pin_body(x_ref, g_ref, w_ref, bd_ref, qg_ref, k1g_ref, k2g_ref, o_ref, h_ref):
    j = pl.program_id(1)

    @pl.when(j == 0)
    def _():
        x = x_ref[...]
        ms = jnp.mean(x * x, axis=-1, keepdims=True)
        h_ref[...] = (x * lax.rsqrt(ms + EPS) * g_ref[...]).astype(BF16)

    t_u = Z_U // PIN_TN
    t_q = Z_Q // PIN_TN
    t_kv = Z_KV // PIN_TN
    half = SEC_W

    def emit(act):
        for r in range(PIN_TM // PIN_RC):
            rows = slice(r * PIN_RC, (r + 1) * PIN_RC)
            o_ref[rows, :] = act(jnp.dot(h_ref[rows, :], w_ref[...], preferred_element_type=F32))

    def halves(act_lo, act_hi):
        return lambda y: jnp.concatenate([act_lo(y[:, :half]), act_hi(y[:, half:])], axis=1)

    def kv_act(kg_ref):
        is_k = (lax.broadcasted_iota(jnp.int32, (1, half), 1) & (KV_GW - 1)) < HEAD_DIM
        return lambda y: jnp.where(is_k, _head_rms(y, bd_ref[...]) * kg_ref[...], y)

    @pl.when(j < t_u)
    def _():
        emit(jax.nn.sigmoid)

    @pl.when((j >= t_u) & (j < t_q))
    def _():
        emit(jax.nn.gelu)

    @pl.when((j >= t_q) & (j < t_kv))
    def _():
        emit(lambda y: _head_rms(y, bd_ref[...]) * (qg_ref[...] * (HEAD_DIM ** -0.5 * LOG2E)))

    @pl.when(j == t_kv)
    def _():
        emit(halves(lambda y: y, kv_act(k1g_ref)))

    @pl.when(j == t_kv + 1)
    def _():
        emit(halves(kv_act(k2g_ref), jax.nn.sigmoid))


def _proj_in(x, gain, w, l, bd, qg, k1g, k2g):
    return pl.pallas_call(
        _pin_body,
        grid=(TOK // PIN_TM, Z_W // PIN_TN),
        in_specs=[
            pl.BlockSpec((PIN_TM, D_MODEL), lambda i, j: (i, 0)),
            pl.BlockSpec((1, D_MODEL), lambda i, j: (0, 0)),
            pl.BlockSpec((None, D_MODEL, PIN_TN), lambda i, j: (l, 0, j)),
            pl.BlockSpec((LANE, LANE), lambda i, j: (0, 0)),
            pl.BlockSpec((1, PIN_TN), lambda i, j: (0, 0)),
            pl.BlockSpec((1, SEC_W), lambda i, j: (0, 0)),
            pl.BlockSpec((1, SEC_W), lambda i, j: (0, 0)),
        ],
        out_specs=pl.BlockSpec((PIN_TM, PIN_TN), lambda i, j: (i, j)),
        out_shape=jax.ShapeDtypeStruct((TOK, Z_W), F32),
        scratch_shapes=[pltpu.VMEM((PIN_TM, D_MODEL), BF16)],
        compiler_params=_cparams(("parallel", "arbitrary")),
        name="proj_in",
    )(x, gain, w, bd, qg, k1g, k2g)


def _cmp_body(z_ref, pos_ref, w1_ref, w2k_ref, w2vT_ref, kg_ref, kc_ref, vcT_ref):
    half = CMP_STRIDE * HEAD_DIM
    lane = lax.broadcasted_iota(jnp.int32, (N_CMP, LANE), 1)
    first = lane < HEAD_DIM
    xk, xv = [], []
    for j in range(CMP_STRIDE // 2):
        ev = z_ref[pl.ds(2 * j, N_CMP, stride=CMP_STRIDE), :]
        od = z_ref[pl.ds(2 * j + 1, N_CMP, stride=CMP_STRIDE), :]
        xk.append(jnp.where(first, ev, pltpu.roll(od, HEAD_DIM, 1)))
        xv.append(jnp.where(first, pltpu.roll(ev, HEAD_DIM, 1), od))

    def hidden(x, kind):
        xa = (x + pos_ref[kind, :, :half]).astype(BF16)
        xb = (x + pos_ref[kind, :, half:]).astype(BF16)
        a = jnp.dot(xa, w1_ref[kind, :half, :], preferred_element_type=F32)
        b = jnp.dot(xb, w1_ref[kind, half:, :], preferred_element_type=F32)
        hid = a + pltpu.roll(b, N_CMP - 1, 0)
        return (hid * jax.nn.sigmoid(hid)).astype(BF16)

    yk = jnp.dot(hidden(jnp.concatenate(xk, axis=1), 0), w2k_ref[...], preferred_element_type=F32)
    ms = jnp.mean(yk * yk, axis=-1, keepdims=True)
    kc_ref[0, 0] = (yk * lax.rsqrt(ms + EPS) * kg_ref[...]).astype(BF16)
    hv = hidden(jnp.concatenate(xv, axis=1), 1)
    vcT_ref[0, 0] = lax.dot_general(w2vT_ref[...], hv, (((1,), (1,)), ((), ())),
                                    preferred_element_type=F32).astype(BF16)


def _compress(z, pos, w1, w2k, w2vT, kg):
    return pl.pallas_call(
        _cmp_body,
        grid=(BATCH, N_KV),
        in_specs=[
            pl.BlockSpec((SEQ, KV_GW), lambda b, g: (b, Z_KV // KV_GW + g)),
            pl.BlockSpec((2, 1, CMP_BLOCK * HEAD_DIM), lambda b, g: (0, 0, 0)),
            pl.BlockSpec((2, CMP_BLOCK * HEAD_DIM, CMP_HIDDEN), lambda b, g: (0, 0, 0)),
            pl.BlockSpec((CMP_HIDDEN, HEAD_DIM), lambda b, g: (0, 0)),
            pl.BlockSpec((HEAD_DIM, CMP_HIDDEN), lambda b, g: (0, 0)),
            pl.BlockSpec((1, HEAD_DIM), lambda b, g: (0, 0)),
        ],
        out_specs=[
            pl.BlockSpec((1, 1, N_CMP, HEAD_DIM), lambda b, g: (b, g, 0, 0)),
            pl.BlockSpec((1, 1, HEAD_DIM, N_CMP), lambda b, g: (b, g, 0, 0)),
        ],
        out_shape=[
            jax.ShapeDtypeStruct((BATCH, N_KV, N_CMP, HEAD_DIM), BF16),
            jax.ShapeDtypeStruct((BATCH, N_KV, HEAD_DIM, N_CMP), BF16),
        ],
        compiler_params=_cparams(("parallel", "parallel")),
        name="compress",
    )(z, pos, w1, w2k, w2vT, kg)


def _np_bucket(dist):
    n = np.maximum(dist, 0)
    max_exact = N_BUCKETS // 2
    nf = np.maximum(n, 1).astype(np.float32)
    large = max_exact + (np.log(nf / np.float32(max_exact)) / np.float32(math.log(MAX_DISTANCE / max_exact))
                         * np.float32(N_BUCKETS - max_exact)).astype(np.int32)
    large = np.minimum(large, N_BUCKETS - 1)
    return np.where(n < max_exact, n, large).astype(np.int32)


def _bucket_tables():
    kj = np.arange(TQ)[:, None]
    qi = np.arange(TQ)[None, :]
    near = np.stack([np.where(qi >= kj, _np_bucket(qi - kj), MASKED), _np_bucket(TQ + qi - kj)])
    start = np.array([_cmp_band_start(i) for i in range(NQ)])[:, None, None]
    c = start + np.arange(CMP_BAND)[None, :, None]
    t = (np.arange(NQ)[:, None, None] * TQ + qi[None])
    dist_band = t - (c * CMP_STRIDE + CMP_BLOCK - 1)
    cmp_b = np.where(dist_band >= 0, _np_bucket(dist_band), MASKED)
    dist = t - (np.arange(N_CMP)[None, :, None] * CMP_STRIDE + CMP_BLOCK - 1)
    for i in range(NQ):
        s0 = _cmp_band_start(i)
        assert np.all(dist[i, :s0] >= 0) and np.all(_np_bucket(dist[i, :s0]) == N_BUCKETS - 1)
        assert np.all(dist[i, s0 + CMP_BAND:] < 0)
    return near.astype(np.int32), cmp_b.astype(np.int32)


def _cmp_band_start(i):
    return max((TQ // CMP_STRIDE) * (i - 1), 0)


def _bias_body(rb_ref, near_ref, cmpb_ref, t0_ref, t1_ref, tc_ref):
    g = pl.program_id(0)

    def lut(bk, h):
        val = jnp.full(bk.shape, NEG, F32)
        for b in range(N_BUCKETS):
            val = jnp.where(bk == b, rb_ref[b, h], val)
        return val

    for hh in range(HPG):
        h = g * HPG + hh
        sl = slice(hh * TQ, (hh + 1) * TQ)
        far = rb_ref[N_BUCKETS - 1, h]
        t0_ref[0, :, sl] = (lut(near_ref[0], h) - far) * LOG2E
        t1_ref[0, :, sl] = (lut(near_ref[1], h) - far) * LOG2E

        def body(i, carry):
            start = pl.multiple_of(jnp.maximum((TQ // CMP_STRIDE) * (i - 1), 0), SUBLANE)
            row = lax.broadcasted_iota(jnp.int32, (N_CMP, TQ), 0)
            tc_ref[0, i, :, sl] = jnp.where(row < start, far * LOG2E, NEG)
            tc_ref[0, i, pl.ds(start, CMP_BAND), sl] = lut(cmpb_ref[i], h) * LOG2E
            return carry

        lax.fori_loop(0, NQ, body, 0)


def _bias_tables(rel_bias, near, cmpb):
    return pl.pallas_call(
        _bias_body,
        grid=(N_KV,),
        in_specs=[
            pl.BlockSpec(memory_space=pltpu.SMEM),
            pl.BlockSpec((2, TQ, TQ), lambda g: (0, 0, 0)),
            pl.BlockSpec((NQ, CMP_BAND, TQ), lambda g: (0, 0, 0)),
        ],
        out_specs=[
            pl.BlockSpec((1, TQ, QL), lambda g: (g, 0, 0)),
            pl.BlockSpec((1, TQ, QL), lambda g: (g, 0, 0)),
            pl.BlockSpec((1, NQ, N_CMP, QL), lambda g: (g, 0, 0, 0)),
        ],
        out_shape=[
            jax.ShapeDtypeStruct((N_KV, TQ, QL), F32),
            jax.ShapeDtypeStruct((N_KV, TQ, QL), F32),
            jax.ShapeDtypeStruct((N_KV, NQ, N_CMP, QL), F32),
        ],
        compiler_params=_cparams(("arbitrary",)),
        name="bias_tables",
    )(rel_bias, near, cmpb)


def _attn_body(zq_ref, zs_ref, zw_ref, zg_ref, kc_ref, vcT_ref, t0_ref, t1_ref, tc_ref, o_ref,
               ks_ref, vsT_ref, kw_ref, vwT_ref, sel_ref, sfar_ref, gsc_ref, psum_ref):
    g = pl.program_id(1)
    ip = pl.program_id(2)

    @pl.when(ip == 0)
    def _():
        ones = jnp.ones((V_ROWS - HEAD_DIM, TQ), BF16)

        def fill(n, carry):
            for z_ref, k_ref, vT_ref in ((zs_ref, ks_ref, vsT_ref), (zw_ref, kw_ref, vwT_ref)):
                blk = z_ref[pl.ds(pl.multiple_of(n * TQ, TQ), TQ), :]
                k_ref[n] = blk[:, :HEAD_DIM].astype(BF16)
                vT_ref[n, :HEAD_DIM, :] = blk.T[HEAD_DIM:, :].astype(BF16)
                vT_ref[n, HEAD_DIM:, :] = ones
            return carry

        lax.fori_loop(0, NK, fill, 0, unroll=4)

    blk = lax.broadcasted_iota(jnp.int32, (N_SBLK, TQ), 0)
    ql = lax.broadcasted_iota(jnp.int32, (N_SBLK, TQ), 1)
    key1 = lax.broadcasted_iota(jnp.int32, (TQ, TQ), 0)
    q1 = lax.broadcasted_iota(jnp.int32, (TQ, TQ), 1)
    nlast = WINDOW // TQ

    def sel_mask(t, n):
        r0 = sel_ref[t, pl.ds(2 * n, 1), :]
        r1 = sel_ref[t, pl.ds(2 * n + 1, 1), :]
        return jnp.where(key1 < SEL_BLOCK, r0, r1) > 0.5

    def scores(qT, k_ref, n, bias, mask):
        sc = jnp.dot(k_ref[n], qT, preferred_element_type=F32)
        if bias is not None:
            sc = sc + bias
        if mask is None:
            return sc
        return jnp.concatenate(
            [jnp.where(mask, sc[:, h * TQ:(h + 1) * TQ], NEG) for h in range(HPG)], axis=1)

    def colmax8(sc):
        return jnp.max(sc.reshape(TQ // SUBLANE, SUBLANE, QL), axis=0)

    def probs(sc, mrow):
        return jnp.exp2(sc - mrow).astype(BF16)

    def pv(vT_ref, ns, ps):
        vv = vT_ref[ns[0]] if len(ns) == 1 else jnp.concatenate([vT_ref[n] for n in ns], axis=1)
        pp = ps[0] if len(ps) == 1 else jnp.concatenate(ps, axis=0)
        return jnp.dot(vv, pp, preferred_element_type=F32)

    def finish(acc):
        return acc[:HEAD_DIM] * (1.0 / acc[HEAD_DIM:HEAD_DIM + 1])

    def stage_q(t):
        i = Q_PAIR * ip + t
        rows = slice(t * TQ, (t + 1) * TQ)
        qt = zq_ref[rows, :].T
        qT = jnp.concatenate([qt[h * HEAD_DIM:(h + 1) * HEAD_DIM] for h in range(HPG)], axis=1).astype(BF16)
        gsc_ref[t] = zg_ref[rows, :].T
        return dict(i=i, qT=qT, n1=jnp.maximum(i - 1, 0), i1=jnp.full((TQ, TQ), i, jnp.int32),
                    nfar=jnp.maximum(i - 1, 0))

    def stage_cmp_scores(t, tl):
        tl["s_c"] = jnp.dot(kc_ref[0, 0], tl["qT"], preferred_element_type=F32) + tc_ref[0, t]
        tl["m_c"] = jnp.maximum(jnp.max(colmax8(tl["s_c"]), axis=0, keepdims=True), 0.1 * NEG)

    def stage_cmp(t, tl):
        e = jnp.exp2(tl["s_c"] - tl["m_c"])
        l = jnp.sum(e, axis=0, keepdims=True)
        p = e * jnp.where(l > 0.0, 1.0 / l, 0.0)
        tl["ocT"] = jnp.dot(vcT_ref[0, 0], p.astype(BF16), preferred_element_type=F32)
        psum_ref[t] = p[:, 0:TQ] + p[:, TQ:2 * TQ] + p[:, 2 * TQ:3 * TQ] + p[:, 3 * TQ:4 * TQ]

    def stage_select(t, tl):
        i = tl["i"]
        ratio = SEL_BLOCK // CMP_STRIDE
        part = [psum_ref[t, pl.ds(r, N_SBLK, stride=ratio), :] for r in range(ratio)]
        prev_last = jnp.where(blk == 0, 0.0, pltpu.roll(part[3], 1, 0))
        imp = (part[0] + part[1] + part[2]) + 0.5 * (part[3] + prev_last)
        cur = jnp.full((N_SBLK, TQ), i, jnp.int32) * (TQ // SEL_BLOCK) + ql // SEL_BLOCK
        forced = (blk == 0) | (blk == cur) | (blk == cur - 1)
        score = jnp.where(blk <= cur, jnp.where(forced, FORCE, imp), NEG)
        sub = SUBLANE
        pieces = [score[k * sub:(k + 1) * sub, :] for k in range(N_SBLK // sub)]
        ranks = [jnp.zeros((sub, TQ), F32) for _ in pieces]
        for jj in range(N_SBLK):
            row = score[jj:jj + 1, :]
            for k, pc in enumerate(pieces):
                if k * sub > jj:
                    beats = row >= pc
                elif (k + 1) * sub - 1 < jj:
                    beats = row > pc
                else:
                    beats = (row > pc) | ((row == pc) & (blk[k * sub:(k + 1) * sub, :] > jj))
                ranks[k] = ranks[k] + jnp.where(beats, 1.0, 0.0)
        rank = jnp.concatenate(ranks, axis=0)
        sel_ref[t] = jnp.where((rank < N_SEL) & (blk <= cur), 1.0, 0.0)

    def stage_window_scores(t, tl):
        i, n1, i1, qT = tl["i"], tl["n1"], tl["i1"], tl["qT"]
        n2, n3, n4 = jnp.maximum(i - 2, 0), jnp.maximum(i - 3, 0), jnp.maximum(i - nlast, 0)
        w_d = scores(qT, kw_ref, i, t0_ref[0], None)
        w_1 = scores(qT, kw_ref, n1, t1_ref[0], i1 >= 1)
        w_2 = scores(qT, kw_ref, n2, None, i1 >= 2)
        w_3 = scores(qT, kw_ref, n3, None, i1 >= 3)
        w_4 = scores(qT, kw_ref, n4, None, (q1 < key1) & (i1 >= nlast))
        mxw = jnp.maximum(jnp.maximum(jnp.maximum(colmax8(w_d), colmax8(w_1)),
                                      jnp.maximum(colmax8(w_2), colmax8(w_3))), colmax8(w_4))
        tl["w"] = (w_d, w_1, w_2, w_3, w_4)
        tl["wn"] = (i, n1, n2, n3, n4)
        tl["m_w"] = jnp.max(mxw, axis=0, keepdims=True)

    def stage_window_pv(t, tl):
        (w_d, w_1, w_2, w_3, w_4), (i, n1, n2, n3, n4), m_w = tl["w"], tl["wn"], tl["m_w"]
        tl["owT"] = finish(pv(vwT_ref, [i], [probs(w_d, m_w)])
                           + pv(vwT_ref, [n1, n2], [probs(w_1, m_w), probs(w_2, m_w)])
                           + pv(vwT_ref, [n3, n4], [probs(w_3, m_w), probs(w_4, m_w)]))

    def stage_near(t, tl):
        i, n1, i1, qT = tl["i"], tl["n1"], tl["i1"], tl["qT"]
        tl["s_d"] = scores(qT, ks_ref, i, t0_ref[0], None)
        tl["s_1"] = scores(qT, ks_ref, n1, t1_ref[0], sel_mask(t, n1) & (i1 >= 1))
        tl["mx"] = jnp.maximum(colmax8(tl["s_d"]), colmax8(tl["s_1"]))

    tiles = [stage_q(t) for t in range(Q_PAIR)]
    for stage in (stage_cmp_scores, stage_cmp, stage_window_scores, stage_select, stage_near, stage_window_pv):
        for t, tl in enumerate(tiles):
            stage(t, tl)
    assert Q_PAIR % FAR_GROUP == 0
    common = (Q_PAIR // FAR_GROUP) * ip
    extras = [tuple(t for t in range(Q_PAIR) if t - 1 > FAR_GROUP * e) for e in range(Q_PAIR // FAR_GROUP)]

    def far_scores(tr, mxs, ts):
        out = list(mxs)
        for t in ts:
            tl = tiles[t]
            for c in range(FAR_GROUP):
                n = FAR_GROUP * tr + c
                sc = scores(tl["qT"], ks_ref, n, None,
                            sel_mask(t, n) & (jnp.full((TQ, TQ), n, jnp.int32) < tl["nfar"]))
                sfar_ref[t, n] = sc
                out[t] = jnp.maximum(out[t], colmax8(sc))
        return tuple(out)

    every = tuple(range(Q_PAIR))
    mxs = lax.fori_loop(0, common, lambda tr, c: far_scores(tr, c, every), tuple(tl["mx"] for tl in tiles))
    for e, ts in enumerate(extras):
        mxs = far_scores(common + e, mxs, ts)
    m_s = [jnp.max(mx, axis=0, keepdims=True) for mx in mxs]
    accs = tuple(pv(vsT_ref, [tl["i"], tl["n1"]], [probs(tl["s_d"], m_s[t]), probs(tl["s_1"], m_s[t])])
                 for t, tl in enumerate(tiles))

    def far_pv(tr, accs_, ts):
        ns = [FAR_GROUP * tr + c for c in range(FAR_GROUP)]
        out = list(accs_)
        for t in ts:
            out[t] = out[t] + pv(vsT_ref, ns, [probs(sfar_ref[t, n], m_s[t]) for n in ns])
        return tuple(out)

    accs = lax.fori_loop(0, common, lambda tr, c: far_pv(tr, c, every), accs)
    for e, ts in enumerate(extras):
        accs = far_pv(common + e, accs, ts)

    for t, tl in enumerate(tiles):
        base = g * GATE_GW
        gate = [jnp.concatenate([gsc_ref[t, pl.ds(base + br * HPG + h, 1), :] for h in range(HPG)], axis=1)
                for br in range(3)]
        aT = gate[0] * tl["ocT"] + gate[1] * finish(accs[t]) + gate[2] * tl["owT"]
        a4 = jnp.concatenate([aT[:, h * TQ:(h + 1) * TQ] for h in range(HPG)], axis=0)
        o_ref[t * TQ:(t + 1) * TQ, :] = a4.T.astype(o_ref.dtype)


def _attention(z, kc, vcT, t0, t1, tc):
    qw = HPG * HEAD_DIM
    rows = Q_PAIR * TQ
    npair = NQ // Q_PAIR
    return pl.pallas_call(
        _attn_body,
        grid=(BATCH, N_KV, npair),
        in_specs=[
            pl.BlockSpec((rows, qw), lambda b, g, i: (b * npair + i, Z_Q // qw + g)),
            pl.BlockSpec((SEQ, KV_GW), lambda b, g, i: (b, (Z_KV + 2 * KV_W) // KV_GW + g)),
            pl.BlockSpec((SEQ, KV_GW), lambda b, g, i: (b, (Z_KV + 4 * KV_W) // KV_GW + g)),
            pl.BlockSpec((rows, LANE), lambda b, g, i: (b * npair + i, Z_NG // LANE)),
            pl.BlockSpec((1, 1, N_CMP, HEAD_DIM), lambda b, g, i: (b, g, 0, 0)),
            pl.BlockSpec((1, 1, HEAD_DIM, N_CMP), lambda b, g, i: (b, g, 0, 0)),
            pl.BlockSpec((1, TQ, QL), lambda b, g, i: (g, 0, 0)),
            pl.BlockSpec((1, TQ, QL), lambda b, g, i: (g, 0, 0)),
            pl.BlockSpec((1, Q_PAIR, N_CMP, QL), lambda b, g, i: (g, i, 0, 0)),
        ],
        out_specs=pl.BlockSpec((rows, qw), lambda b, g, i: (b * npair + i, g)),
        out_shape=jax.ShapeDtypeStruct((TOK, NSA_W), BF16),
        scratch_shapes=[
            pltpu.VMEM((NK, TQ, HEAD_DIM), BF16),
            pltpu.VMEM((NK, V_ROWS, TQ), BF16),
            pltpu.VMEM((NK, TQ, HEAD_DIM), BF16),
            pltpu.VMEM((NK, V_ROWS, TQ), BF16),
            pltpu.VMEM((Q_PAIR, N_SBLK, TQ), F32),
            pltpu.VMEM((Q_PAIR, NK, TQ, QL), F32),
            pltpu.VMEM((Q_PAIR, LANE, TQ), F32),
            pltpu.VMEM((Q_PAIR, N_CMP, TQ), F32),
        ],
        compiler_params=_cparams(("parallel", "parallel", "arbitrary")),
        name="nsa_attention",
    )(z, z, z, z, kc, vcT, t0, t1, tc)


def _sgu_tile(u_ref, v_ref, g_ref, b_ref, w_ref, bs_ref):
    r = lax.broadcasted_iota(jnp.int32, (GM_CHUNK, GM_CHUNK), 0)
    c = lax.broadcasted_iota(jnp.int32, (GM_CHUNK, GM_CHUNK), 1)
    tril = r >= c
    ws = [jnp.where(tril, w_ref[gi], 0.0).astype(BF16) for gi in range(GM_GROUPS)]
    chunks = []
    for ch in range(u_ref.shape[0] // GM_CHUNK):
        rows = slice(ch * GM_CHUNK, (ch + 1) * GM_CHUNK)
        v = v_ref[rows, :]
        mu = jnp.mean(v, axis=-1, keepdims=True)
        vc = v - mu
        var = jnp.mean(vc * vc, axis=-1, keepdims=True)
        vn = (vc * lax.rsqrt(var + EPS) * g_ref[...] + b_ref[...]).astype(BF16)
        groups = []
        for gi in range(GM_GROUPS):
            cols = slice(gi * GM_GW, (gi + 1) * GM_GW)
            mixed = jnp.dot(ws[gi], vn[:, cols], preferred_element_type=F32) + bs_ref[:, cols]
            groups.append((u_ref[rows, cols] * mixed).astype(BF16))
        chunks.append(jnp.concatenate(groups, axis=1))
    return jnp.concatenate(chunks, axis=0)


def _out_body(a_ref, u_ref, v_ref, m1_ref, m2_ref, x_ref, lg_ref, lb_ref, sw_ref, bs_ref,
              wn_ref, ws_ref, wo_ref, gn_ref, o_ref, h_ref):
    sg = _sgu_tile(u_ref, v_ref, lg_ref, lb_ref, sw_ref, bs_ref)
    pa = jnp.dot(a_ref[...], wn_ref[...], preferred_element_type=F32)
    ps = jnp.dot(sg, ws_ref[...], preferred_element_type=F32)
    merged = (m1_ref[...] * pa + m2_ref[...] * ps).astype(BF16)
    y = x_ref[...] + jnp.dot(merged, wo_ref[...], preferred_element_type=F32)
    o_ref[...] = y
    h_ref[...] = _rms_bf16(y, gn_ref[...])


def _merge_out(a, z, x, ln_g, ln_b, w_s, bs_exp, wn, ws, wo, l, next_gain):
    const = lambda i: (l, 0, 0)
    once = pl.Buffered(1)
    return pl.pallas_call(
        _out_body,
        grid=(TOK // OUT_TM,),
        in_specs=[
            pl.BlockSpec((OUT_TM, NSA_W), lambda i: (i, 0)),
            pl.BlockSpec((OUT_TM, GM_W), lambda i: (i, Z_U // GM_W)),
            pl.BlockSpec((OUT_TM, GM_W), lambda i: (i, Z_V // GM_W)),
            pl.BlockSpec((OUT_TM, D_MODEL), lambda i: (i, 0)),
            pl.BlockSpec((OUT_TM, D_MODEL), lambda i: (i, 1)),
            pl.BlockSpec((OUT_TM, D_MODEL), lambda i: (i, 0)),
            pl.BlockSpec((1, GM_W), lambda i: (0, 0)),
            pl.BlockSpec((1, GM_W), lambda i: (0, 0)),
            pl.BlockSpec((GM_GROUPS, GM_CHUNK, GM_CHUNK), lambda i: (0, 0, 0), pipeline_mode=once),
            pl.BlockSpec((GM_CHUNK, GM_W), lambda i: (0, 0), pipeline_mode=once),
            pl.BlockSpec((None, NSA_W, D_MODEL), const, pipeline_mode=once),
            pl.BlockSpec((None, GM_W, D_MODEL), const, pipeline_mode=once),
            pl.BlockSpec((None, D_MODEL, D_MODEL), const, pipeline_mode=once),
            pl.BlockSpec((1, D_MODEL), lambda i: (0, 0)),
        ],
        out_specs=[pl.BlockSpec((OUT_TM, D_MODEL), lambda i: (i, 0)),
                   pl.BlockSpec((OUT_TM, D_MODEL), lambda i: (i, 0))],
        out_shape=[jax.ShapeDtypeStruct((TOK, D_MODEL), F32),
                   jax.ShapeDtypeStruct((TOK, D_MODEL), BF16)],
        compiler_params=_cparams(("parallel",)),
        name="merge_out",
    )(a, z, z, z, z, x, ln_g, ln_b, w_s, bs_exp, wn, ws, wo, next_gain)


def _check_overlap_stencil():
    ci = np.arange(N_CMP - 1)[None, :] * CMP_STRIDE
    sj = np.arange(N_SBLK)[:, None] * SEL_BLOCK
    ov = np.clip(np.minimum(ci + CMP_BLOCK, sj + SEL_BLOCK) - np.maximum(ci, sj), 0, None) / CMP_BLOCK
    stencil = np.zeros((N_SBLK, N_CMP - 1))
    for j in range(N_SBLK):
        for c, w in ((4 * j - 1, 0.5), (4 * j, 1.0), (4 * j + 1, 1.0), (4 * j + 2, 1.0), (4 * j + 3, 0.5)):
            if 0 <= c < N_CMP - 1:
                stencil[j, c] = w
    assert np.array_equal(ov, stencil)


def _block_diag_ones():
    r = np.arange(LANE)
    return (r[:, None] // HEAD_DIM == r[None, :] // HEAD_DIM).astype(np.float32)


def _prep_w_in(w):
    L = w.shape[0]
    w = w.astype(BF16)
    kv = w[:, :, OFF_KV:OFF_NG].reshape(L, D_MODEL, 3, 2, N_KV, HEAD_DIM)
    kv = kv.transpose(0, 1, 2, 4, 3, 5).reshape(L, D_MODEL, 6 * KV_W)
    ng = w[:, :, OFF_NG:OFF_UV].reshape(L, D_MODEL, N_KV, HPG, 3).transpose(0, 1, 2, 4, 3)
    ng = ng.reshape(L, D_MODEL, N_KV, 3 * HPG)
    ng = jnp.pad(ng, ((0, 0), (0, 0), (0, 0), (0, GATE_GW - 3 * HPG))).reshape(L, D_MODEL, N_KV * GATE_GW)
    ng = jnp.pad(ng, ((0, 0), (0, 0), (0, SEC_W - N_KV * GATE_GW)))
    return jnp.concatenate([w[:, :, OFF_MG:], w[:, :, OFF_UV:OFF_MG], w[:, :, :OFF_KV], kv, ng], axis=2)


def _mixer(x, z, l, p, tables):
    t0, t1, tc = tables
    pos = jnp.stack([p["cmp_pos_k"][l], p["cmp_pos_v"][l]]).reshape(2, 1, CMP_BLOCK * HEAD_DIM)
    w1 = jnp.stack([p["cmp_k_w1"][l], p["cmp_v_w1"][l]]).astype(BF16)
    kc, vcT = _compress(z, pos, w1, p["cmp_k_w2"][l].astype(BF16), p["cmp_v_w2"][l].T.astype(BF16),
                        p["k_norm"][l, 0].reshape(1, HEAD_DIM))
    a = _attention(z, kc, vcT, t0, t1, tc)
    bs_exp = jnp.repeat(p["sgu_b"][l].T, GM_GW, axis=1)
    return _merge_out(a, z, x, p["sgu_norm_g"][l].reshape(1, GM_W), p["sgu_norm_b"][l].reshape(1, GM_W),
                      p["sgu_w"][l], bs_exp, p["wn"], p["ws"], p["wo"], l,
                      p["ffn2_norm"][l].reshape(1, D_MODEL))


def kernel(x, rel_bias, ffn1_norm, ffn1_w_gate, ffn1_w_up, ffn1_w_down, mix_norm, w_in, q_norm, k_norm, cmp_pos_k, cmp_pos_v, cmp_k_w1, cmp_k_w2, cmp_v_w1, cmp_v_w2, sgu_norm_g, sgu_norm_b, sgu_w, sgu_b, w_proj_nsa, w_proj_sgu, w_out, ffn2_norm, ffn2_w_gate, ffn2_w_up, ffn2_w_down):
    p = dict(cmp_pos_k=cmp_pos_k, cmp_pos_v=cmp_pos_v, cmp_k_w1=cmp_k_w1, cmp_k_w2=cmp_k_w2,
             cmp_v_w1=cmp_v_w1, cmp_v_w2=cmp_v_w2, k_norm=k_norm, sgu_norm_g=sgu_norm_g,
             sgu_norm_b=sgu_norm_b, sgu_w=sgu_w, sgu_b=sgu_b, wn=w_proj_nsa.astype(BF16),
             ws=w_proj_sgu.astype(BF16), wo=w_out.astype(BF16), ffn2_norm=ffn2_norm)
    near, cmpb = _bucket_tables()
    t0, t1, tc = _bias_tables(rel_bias, jnp.asarray(near), jnp.asarray(cmpb))
    tables = (t0, t1, tc)
    _check_overlap_stencil()
    bd = jnp.asarray(_block_diag_ones()).astype(BF16)
    w_in_r = _prep_w_in(w_in)
    f1 = (ffn1_w_gate, ffn1_w_up, ffn1_w_down)
    f2 = (ffn2_w_gate, ffn2_w_up, ffn2_w_down)

    y = x.reshape(TOK, D_MODEL)
    for l in range(DEPTH):
        y = _ffn(y, _rmsnorm(y, ffn1_norm[l].reshape(1, D_MODEL)), *f1, l)
        z = _proj_in(y, mix_norm[l].reshape(1, D_MODEL), w_in_r, l, bd,
                     jnp.tile(q_norm[l], PIN_TN // HEAD_DIM).reshape(1, PIN_TN),
                     jnp.tile(k_norm[l, 1], SEC_W // HEAD_DIM).reshape(1, SEC_W),
                     jnp.tile(k_norm[l, 2], SEC_W // HEAD_DIM).reshape(1, SEC_W))
        y, h = _mixer(y, z, l, p, tables)
        y = _ffn(y, h, *f2, l)
    return y.reshape(BATCH, SEQ, D_MODEL)
```

```python
import math

import numpy as np
import jax
import jax.numpy as jnp
from jax import lax
from jax.experimental import pallas as pl
from jax.experimental.pallas import tpu as pltpu

F32 = jnp.float32
BF16 = jnp.bfloat16

D_MODEL = 2048
BATCH = 4
SEQ = 2048
DEPTH = 2
TOK = BATCH * SEQ
HEAD_DIM = 64
N_HEADS = 16
N_KV = 4
HPG = 4
NSA_W = 1024
KV_W = 256
CMP_BLOCK = 32
CMP_STRIDE = 16
CMP_HIDDEN = 256
SEL_BLOCK = 64
N_SEL = 16
WINDOW = 512
GM_W = 1024
GM_CHUNK = 128
GM_GW = 128
GM_GROUPS = 8
N_BUCKETS = 32
MAX_DISTANCE = 128
D_FF = 5504
EPS = 1e-6
NEG = -1e30
FORCE = 1e4
OFF_KV = NSA_W
OFF_NG = OFF_KV + 6 * KV_W
OFF_UV = OFF_NG + 3 * N_HEADS
OFF_MG = OFF_UV + 2 * GM_W
LOG2E = 1.4426950408889634

LANE = 128
SUBLANE = 8
VMEM_LIMIT = 56 * 1024 * 1024

FFN_TF = 512
FFN_UP_TM = 1024
NORM_TM = 1024
FFN_DN_TM = 512
FFN_DN_TN = 512
PIN_TM = 1024
PIN_TN = 1024
PIN_RC = 256
SEC_W = PIN_TN // 2
Z_MG = 0
Z_U = Z_MG + 2 * D_MODEL
Z_V = Z_U + GM_W
Z_Q = Z_V + GM_W
Z_KV = Z_Q + NSA_W
Z_NG = Z_KV + 6 * KV_W
Z_W = Z_NG + SEC_W
KV_GW = 2 * HEAD_DIM
GATE_GW = 16
TQ = 128
NQ = SEQ // TQ
NK = SEQ // TQ
QL = HPG * TQ
N_CMP = 128
CMP_BAND = 16
MASKED = -1
N_SBLK = SEQ // SEL_BLOCK
V_ROWS = HEAD_DIM + 16
FAR_GROUP = 4
Q_PAIR = 4
OUT_TM = 256


def _cparams(sem):
    return pltpu.CompilerParams(dimension_semantics=sem, vmem_limit_bytes=VMEM_LIMIT)


def _cast_rows(src_ref, dst_ref, chunk):
    def body(r, carry):
        rows = pl.ds(pl.multiple_of(r * chunk, chunk), chunk)
        dst_ref[rows, :] = src_ref[rows, :].astype(BF16)
        return carry

    lax.fori_loop(0, src_ref.shape[0] // chunk, body, 0)


def _rms_bf16(x, gain):
    ms = jnp.mean(x * x, axis=-1, keepdims=True)
    return (x * lax.rsqrt(ms + EPS) * gain).astype(BF16)


def _norm_body(x_ref, g_ref, h_ref):
    h_ref[...] = _rms_bf16(x_ref[...], g_ref[...])


def _rmsnorm(x, gain):
    return pl.pallas_call(
        _norm_body,
        grid=(TOK // NORM_TM,),
        in_specs=[pl.BlockSpec((NORM_TM, D_MODEL), lambda i: (i, 0)),
                  pl.BlockSpec((1, D_MODEL), lambda i: (0, 0))],
        out_specs=pl.BlockSpec((NORM_TM, D_MODEL), lambda i: (i, 0)),
        out_shape=jax.ShapeDtypeStruct((TOK, D_MODEL), BF16),
        compiler_params=_cparams(("parallel",)),
        name="rmsnorm",
    )(x, gain)


def _ffn_up_body(h_ref, wg_ref, wu_ref, a_ref, wgb_ref, wub_ref):
    @pl.when(pl.program_id(1) == 0)
    def _():
        _cast_rows(wg_ref, wgb_ref, LANE)
        _cast_rows(wu_ref, wub_ref, LANE)

    h = h_ref[...]
    g = jnp.dot(h, wgb_ref[...], preferred_element_type=F32)
    u = jnp.dot(h, wub_ref[...], preferred_element_type=F32)
    a_ref[...] = (g * jax.nn.sigmoid(g) * u).astype(BF16)


def _ffn_down_body(a_ref, wd_ref, x_ref, o_ref, wdb_ref):
    @pl.when(pl.program_id(1) == 0)
    def _():
        _cast_rows(wd_ref, wdb_ref, LANE)

    o_ref[...] = x_ref[...] + 0.5 * jnp.dot(a_ref[...], wdb_ref[...], preferred_element_type=F32)


def _ffn(x, h, wg, wu, wd, l):
    nf = pl.cdiv(D_FF, FFN_TF)
    a = pl.pallas_call(
        _ffn_up_body,
        grid=(nf, TOK // FFN_UP_TM),
        in_specs=[
            pl.BlockSpec((FFN_UP_TM, D_MODEL), lambda f, m: (m, 0)),
            pl.BlockSpec((None, D_MODEL, FFN_TF), lambda f, m: (l, 0, f)),
            pl.BlockSpec((None, D_MODEL, FFN_TF), lambda f, m: (l, 0, f)),
        ],
        out_specs=pl.BlockSpec((FFN_UP_TM, FFN_TF), lambda f, m: (m, f)),
        out_shape=jax.ShapeDtypeStruct((TOK, D_FF), BF16),
        scratch_shapes=[pltpu.VMEM((D_MODEL, FFN_TF), BF16), pltpu.VMEM((D_MODEL, FFN_TF), BF16)],
        compiler_params=_cparams(("arbitrary", "arbitrary")),
        name="ffn_up",
    )(h, wg, wu)
    return pl.pallas_call(
        _ffn_down_body,
        grid=(D_MODEL // FFN_DN_TN, TOK // FFN_DN_TM),
        in_specs=[
            pl.BlockSpec((FFN_DN_TM, D_FF), lambda n, m: (m, 0)),
            pl.BlockSpec((None, D_FF, FFN_DN_TN), lambda n, m: (l, 0, n)),
            pl.BlockSpec((FFN_DN_TM, FFN_DN_TN), lambda n, m: (m, n)),
        ],
        out_specs=pl.BlockSpec((FFN_DN_TM, FFN_DN_TN), lambda n, m: (m, n)),
        out_shape=jax.ShapeDtypeStruct((TOK, D_MODEL), F32),
        scratch_shapes=[pltpu.VMEM((D_FF, FFN_DN_TN), BF16)],
        compiler_params=_cparams(("arbitrary", "arbitrary")),
        name="ffn_down",
    )(a, wd, x)


def _head_rms(y, bd):
    y2 = (y * y).astype(BF16)
    ss = jnp.concatenate(
        [jnp.dot(y2[:, k * LANE:(k + 1) * LANE], bd, preferred_element_type=F32)
         for k in range(y.shape[1] // LANE)], axis=1)
    return y * lax.rsqrt(ss * (1.0 / HEAD_DIM) + EPS)


def _pin_body(x_ref, g_ref, w_ref, bd_ref, qg_ref, k1g_ref, k2g_ref, o_ref, h_ref):
    j = pl.program_id(1)

    @pl.when(j == 0)
    def _():
        x = x_ref[...]
        ms = jnp.mean(x * x, axis=-1, keepdims=True)
        h_ref[...] = (x * lax.rsqrt(ms + EPS) * g_ref[...]).astype(BF16)

    t_u = Z_U // PIN_TN
    t_q = Z_Q // PIN_TN
    t_kv = Z_KV // PIN_TN
    half = SEC_W

    def emit(act):
        for r in range(PIN_TM // PIN_RC):
            rows = slice(r * PIN_RC, (r + 1) * PIN_RC)
            o_ref[rows, :] = act(jnp.dot(h_ref[rows, :], w_ref[...], preferred_element_type=F32))

    def halves(act_lo, act_hi):
        return lambda y: jnp.concatenate([act_lo(y[:, :half]), act_hi(y[:, half:])], axis=1)

    def kv_act(kg_ref):
        is_k = (lax.broadcasted_iota(jnp.int32, (1, half), 1) & (KV_GW - 1)) < HEAD_DIM
        return lambda y: jnp.where(is_k, _head_rms(y, bd_ref[...]) * kg_ref[...], y)

    @pl.when(j < t_u)
    def _():
        emit(jax.nn.sigmoid)

    @pl.when((j >= t_u) & (j < t_q))
    def _():
        emit(jax.nn.gelu)

    @pl.when((j >= t_q) & (j < t_kv))
    def _():
        emit(lambda y: _head_rms(y, bd_ref[...]) * (qg_ref[...] * (HEAD_DIM ** -0.5 * LOG2E)))

    @pl.when(j == t_kv)
    def _():
        emit(halves(lambda y: y, kv_act(k1g_ref)))

    @pl.when(j == t_kv + 1)
    def _():
        emit(halves(kv_act(k2g_ref), jax.nn.sigmoid))


def _proj_in(x, gain, w, l, bd, qg, k1g, k2g):
    return pl.pallas_call(
        _pin_body,
        grid=(TOK // PIN_TM, Z_W // PIN_TN),
        in_specs=[
            pl.BlockSpec((PIN_TM, D_MODEL), lambda i, j: (i, 0)),
            pl.BlockSpec((1, D_MODEL), lambda i, j: (0, 0)),
            pl.BlockSpec((None, D_MODEL, PIN_TN), lambda i, j: (l, 0, j)),
            pl.BlockSpec((LANE, LANE), lambda i, j: (0, 0)),
            pl.BlockSpec((1, PIN_TN), lambda i, j: (0, 0)),
            pl.BlockSpec((1, SEC_W), lambda i, j: (0, 0)),
            pl.BlockSpec((1, SEC_W), lambda i, j: (0, 0)),
        ],
        out_specs=pl.BlockSpec((PIN_TM, PIN_TN), lambda i, j: (i, j)),
        out_shape=jax.ShapeDtypeStruct((TOK, Z_W), F32),
        scratch_shapes=[pltpu.VMEM((PIN_TM, D_MODEL), BF16)],
        compiler_params=_cparams(("parallel", "arbitrary")),
        name="proj_in",
    )(x, gain, w, bd, qg, k1g, k2g)


def _cmp_body(z_ref, pos_ref, w1_ref, w2k_ref, w2vT_ref, kg_ref, kc_ref, vcT_ref):
    half = CMP_STRIDE * HEAD_DIM
    lane = lax.broadcasted_iota(jnp.int32, (N_CMP, LANE), 1)
    first = lane < HEAD_DIM
    xk, xv = [], []
    for j in range(CMP_STRIDE // 2):
        ev = z_ref[pl.ds(2 * j, N_CMP, stride=CMP_STRIDE), :]
        od = z_ref[pl.ds(2 * j + 1, N_CMP, stride=CMP_STRIDE), :]
        xk.append(jnp.where(first, ev, pltpu.roll(od, HEAD_DIM, 1)))
        xv.append(jnp.where(first, pltpu.roll(ev, HEAD_DIM, 1), od))

    def hidden(x, kind):
        xa = (x + pos_ref[kind, :, :half]).astype(BF16)
        xb = (x + pos_ref[kind, :, half:]).astype(BF16)
        a = jnp.dot(xa, w1_ref[kind, :half, :], preferred_element_type=F32)
        b = jnp.dot(xb, w1_ref[kind, half:, :], preferred_element_type=F32)
        hid = a + pltpu.roll(b, N_CMP - 1, 0)
        return (hid * jax.nn.sigmoid(hid)).astype(BF16)

    yk = jnp.dot(hidden(jnp.concatenate(xk, axis=1), 0), w2k_ref[...], preferred_element_type=F32)
    ms = jnp.mean(yk * yk, axis=-1, keepdims=True)
    kc_ref[0, 0] = (yk * lax.rsqrt(ms + EPS) * kg_ref[...]).astype(BF16)
    hv = hidden(jnp.concatenate(xv, axis=1), 1)
    vcT_ref[0, 0] = lax.dot_general(w2vT_ref[...], hv, (((1,), (1,)), ((), ())),
                                    preferred_element_type=F32).astype(BF16)


def _compress(z, pos, w1, w2k, w2vT, kg):
    return pl.pallas_call(
        _cmp_body,
        grid=(BATCH, N_KV),
        in_specs=[
            pl.BlockSpec((SEQ, KV_GW), lambda b, g: (b, Z_KV // KV_GW + g)),
            pl.BlockSpec((2, 1, CMP_BLOCK * HEAD_DIM), lambda b, g: (0, 0, 0)),
            pl.BlockSpec((2, CMP_BLOCK * HEAD_DIM, CMP_HIDDEN), lambda b, g: (0, 0, 0)),
            pl.BlockSpec((CMP_HIDDEN, HEAD_DIM), lambda b, g: (0, 0)),
            pl.BlockSpec((HEAD_DIM, CMP_HIDDEN), lambda b, g: (0, 0)),
            pl.BlockSpec((1, HEAD_DIM), lambda b, g: (0, 0)),
        ],
        out_specs=[
            pl.BlockSpec((1, 1, N_CMP, HEAD_DIM), lambda b, g: (b, g, 0, 0)),
            pl.BlockSpec((1, 1, HEAD_DIM, N_CMP), lambda b, g: (b, g, 0, 0)),
        ],
        out_shape=[
            jax.ShapeDtypeStruct((BATCH, N_KV, N_CMP, HEAD_DIM), BF16),
            jax.ShapeDtypeStruct((BATCH, N_KV, HEAD_DIM, N_CMP), BF16),
        ],
        compiler_params=_cparams(("parallel", "parallel")),
        name="compress",
    )(z, pos, w1, w2k, w2vT, kg)


def _np_bucket(dist):
    n = np.maximum(dist, 0)
    max_exact = N_BUCKETS // 2
    nf = np.maximum(n, 1).astype(np.float32)
    large = max_exact + (np.log(nf / np.float32(max_exact)) / np.float32(math.log(MAX_DISTANCE / max_exact))
                         * np.float32(N_BUCKETS - max_exact)).astype(np.int32)
    large = np.minimum(large, N_BUCKETS - 1)
    return np.where(n < max_exact, n, large).astype(np.int32)


def _bucket_tables():
    kj = np.arange(TQ)[:, None]
    qi = np.arange(TQ)[None, :]
    near = np.stack([np.where(qi >= kj, _np_bucket(qi - kj), MASKED), _np_bucket(TQ + qi - kj)])
    start = np.array([_cmp_band_start(i) for i in range(NQ)])[:, None, None]
    c = start + np.arange(CMP_BAND)[None, :, None]
    t = (np.arange(NQ)[:, None, None] * TQ + qi[None])
    dist_band = t - (c * CMP_STRIDE + CMP_BLOCK - 1)
    cmp_b = np.where(dist_band >= 0, _np_bucket(dist_band), MASKED)
    dist = t - (np.arange(N_CMP)[None, :, None] * CMP_STRIDE + CMP_BLOCK - 1)
    for i in range(NQ):
        s0 = _cmp_band_start(i)
        assert np.all(dist[i, :s0] >= 0) and np.all(_np_bucket(dist[i, :s0]) == N_BUCKETS - 1)
        assert np.all(dist[i, s0 + CMP_BAND:] < 0)
    return near.astype(np.int32), cmp_b.astype(np.int32)


def _cmp_band_start(i):
    return max((TQ // CMP_STRIDE) * (i - 1), 0)


def _bias_body(rb_ref, near_ref, cmpb_ref, t0_ref, t1_ref, tc_ref):
    g = pl.program_id(0)

    def lut(bk, h):
        val = jnp.full(bk.shape, NEG, F32)
        for b in range(N_BUCKETS):
            val = jnp.where(bk == b, rb_ref[b, h], val)
        return val

    for hh in range(HPG):
        h = g * HPG + hh
        sl = slice(hh * TQ, (hh + 1) * TQ)
        far = rb_ref[N_BUCKETS - 1, h]
        t0_ref[0, :, sl] = (lut(near_ref[0], h) - far) * LOG2E
        t1_ref[0, :, sl] = (lut(near_ref[1], h) - far) * LOG2E

        def body(i, carry):
            start = pl.multiple_of(jnp.maximum((TQ // CMP_STRIDE) * (i - 1), 0), SUBLANE)
            row = lax.broadcasted_iota(jnp.int32, (N_CMP, TQ), 0)
            tc_ref[0, i, :, sl] = jnp.where(row < start, far * LOG2E, NEG)
            tc_ref[0, i, pl.ds(start, CMP_BAND), sl] = lut(cmpb_ref[i], h) * LOG2E
            return carry

        lax.fori_loop(0, NQ, body, 0, unroll=4)


def _bias_tables(rel_bias, near, cmpb):
    return pl.pallas_call(
        _bias_body,
        grid=(N_KV,),
        in_specs=[
            pl.BlockSpec(memory_space=pltpu.SMEM),
            pl.BlockSpec((2, TQ, TQ), lambda g: (0, 0, 0)),
            pl.BlockSpec((NQ, CMP_BAND, TQ), lambda g: (0, 0, 0)),
        ],
        out_specs=[
            pl.BlockSpec((1, TQ, QL), lambda g: (g, 0, 0)),
            pl.BlockSpec((1, TQ, QL), lambda g: (g, 0, 0)),
            pl.BlockSpec((1, NQ, N_CMP, QL), lambda g: (g, 0, 0, 0)),
        ],
        out_shape=[
            jax.ShapeDtypeStruct((N_KV, TQ, QL), F32),
            jax.ShapeDtypeStruct((N_KV, TQ, QL), F32),
            jax.ShapeDtypeStruct((N_KV, NQ, N_CMP, QL), F32),
        ],
        compiler_params=_cparams(("arbitrary",)),
        name="bias_tables",
    )(rel_bias, near, cmpb)


def _attn_body(zq_ref, zs_ref, zw_ref, zg_ref, kc_ref, vcT_ref, t0_ref, t1_ref, tc_ref, o_ref,
               ks_ref, vsT_ref, kw_ref, vwT_ref, sel_ref, sfar_ref, gsc_ref, psum_ref):
    g = pl.program_id(1)
    ip = pl.program_id(2)

    @pl.when(ip == 0)
    def _():
        ones = jnp.ones((V_ROWS - HEAD_DIM, TQ), BF16)

        def fill(n, carry):
            for z_ref, k_ref, vT_ref in ((zs_ref, ks_ref, vsT_ref), (zw_ref, kw_ref, vwT_ref)):
                blk = z_ref[pl.ds(pl.multiple_of(n * TQ, TQ), TQ), :]
                k_ref[n] = blk[:, :HEAD_DIM].astype(BF16)
                vT_ref[n, :HEAD_DIM, :] = blk.T[HEAD_DIM:, :].astype(BF16)
                vT_ref[n, HEAD_DIM:, :] = ones
            return carry

        lax.fori_loop(0, NK, fill, 0, unroll=4)

    blk = lax.broadcasted_iota(jnp.int32, (N_SBLK, TQ), 0)
    ql = lax.broadcasted_iota(jnp.int32, (N_SBLK, TQ), 1)
    key1 = lax.broadcasted_iota(jnp.int32, (TQ, TQ), 0)
    q1 = lax.broadcasted_iota(jnp.int32, (TQ, TQ), 1)
    nlast = WINDOW // TQ

    def sel_mask(t, n):
        r0 = sel_ref[t, pl.ds(2 * n, 1), :]
        r1 = sel_ref[t, pl.ds(2 * n + 1, 1), :]
        return jnp.where(key1 < SEL_BLOCK, r0, r1) > 0.5

    def scores(qT, k_ref, n, bias, mask):
        sc = jnp.dot(k_ref[n], qT, preferred_element_type=F32)
        if bias is not None:
            sc = sc + bias
        if mask is None:
            return sc
        return jnp.concatenate(
            [jnp.where(mask, sc[:, h * TQ:(h + 1) * TQ], NEG) for h in range(HPG)], axis=1)

    def colmax8(sc):
        return jnp.max(sc.reshape(TQ // SUBLANE, SUBLANE, QL), axis=0)

    def probs(sc, mrow):
        return jnp.exp2(sc - mrow).astype(BF16)

    def pv(vT_ref, ns, ps):
        vv = vT_ref[ns[0]] if len(ns) == 1 else jnp.concatenate([vT_ref[n] for n in ns], axis=1)
        pp = ps[0] if len(ps) == 1 else jnp.concatenate(ps, axis=0)
        return jnp.dot(vv, pp, preferred_element_type=F32)

    def finish(acc):
        return acc[:HEAD_DIM] * (1.0 / acc[HEAD_DIM:HEAD_DIM + 1])

    def stage_q(t):
        i = Q_PAIR * ip + t
        rows = slice(t * TQ, (t + 1) * TQ)
        qt = zq_ref[rows, :].T
        qT = jnp.concatenate([qt[h * HEAD_DIM:(h + 1) * HEAD_DIM] for h in range(HPG)], axis=1).astype(BF16)
        gsc_ref[t] = zg_ref[rows, :].T
        return dict(i=i, qT=qT, n1=jnp.maximum(i - 1, 0), i1=jnp.full((TQ, TQ), i, jnp.int32),
                    nfar=jnp.maximum(i - 1, 0))

    def stage_cmp_scores(t, tl):
        tl["s_c"] = jnp.dot(kc_ref[0, 0], tl["qT"], preferred_element_type=F32) + tc_ref[0, t]
        tl["m_c"] = jnp.maximum(jnp.max(colmax8(tl["s_c"]), axis=0, keepdims=True), 0.1 * NEG)

    def stage_cmp(t, tl):
        e = jnp.exp2(tl["s_c"] - tl["m_c"])
        l = jnp.sum(e, axis=0, keepdims=True)
        p = e * jnp.where(l > 0.0, 1.0 / l, 0.0)
        tl["ocT"] = jnp.dot(vcT_ref[0, 0], p.astype(BF16), preferred_element_type=F32)
        psum_ref[t] = p[:, 0:TQ] + p[:, TQ:2 * TQ] + p[:, 2 * TQ:3 * TQ] + p[:, 3 * TQ:4 * TQ]

    def stage_select(t, tl):
        i = tl["i"]
        ratio = SEL_BLOCK // CMP_STRIDE
        part = [psum_ref[t, pl.ds(r, N_SBLK, stride=ratio), :] for r in range(ratio)]
        prev_last = jnp.where(blk == 0, 0.0, pltpu.roll(part[3], 1, 0))
        imp = (part[0] + part[1] + part[2]) + 0.5 * (part[3] + prev_last)
        cur = jnp.full((N_SBLK, TQ), i, jnp.int32) * (TQ // SEL_BLOCK) + ql // SEL_BLOCK
        forced = (blk == 0) | (blk == cur) | (blk == cur - 1)
        score = jnp.where(blk <= cur, jnp.where(forced, FORCE, imp), NEG)
        sub = SUBLANE
        pieces = [score[k * sub:(k + 1) * sub, :] for k in range(N_SBLK // sub)]
        ranks = [jnp.zeros((sub, TQ), F32) for _ in pieces]
        for jj in range(N_SBLK):
            row = score[jj:jj + 1, :]
            for k, pc in enumerate(pieces):
                if k * sub > jj:
                    beats = row >= pc
                elif (k + 1) * sub - 1 < jj:
                    beats = row > pc
                else:
                    beats = (row > pc) | ((row == pc) & (blk[k * sub:(k + 1) * sub, :] > jj))
                ranks[k] = ranks[k] + jnp.where(beats, 1.0, 0.0)
        rank = jnp.concatenate(ranks, axis=0)
        sel_ref[t] = jnp.where((rank < N_SEL) & (blk <= cur), 1.0, 0.0)

    def stage_window_scores(t, tl):
        i, n1, i1, qT = tl["i"], tl["n1"], tl["i1"], tl["qT"]
        n2, n3, n4 = jnp.maximum(i - 2, 0), jnp.maximum(i - 3, 0), jnp.maximum(i - nlast, 0)
        w_d = scores(qT, kw_ref, i, t0_ref[0], None)
        w_1 = scores(qT, kw_ref, n1, t1_ref[0], i1 >= 1)
        w_2 = scores(qT, kw_ref, n2, None, i1 >= 2)
        w_3 = scores(qT, kw_ref, n3, None, i1 >= 3)
        w_4 = scores(qT, kw_ref, n4, None, (q1 < key1) & (i1 >= nlast))
        mxw = jnp.maximum(jnp.maximum(jnp.maximum(colmax8(w_d), colmax8(w_1)),
                                      jnp.maximum(colmax8(w_2), colmax8(w_3))), colmax8(w_4))
        tl["w"] = (w_d, w_1, w_2, w_3, w_4)
        tl["wn"] = (i, n1, n2, n3, n4)
        tl["m_w"] = jnp.max(mxw, axis=0, keepdims=True)

    def stage_window_pv(t, tl):
        (w_d, w_1, w_2, w_3, w_4), (i, n1, n2, n3, n4), m_w = tl["w"], tl["wn"], tl["m_w"]
        tl["owT"] = finish(pv(vwT_ref, [i], [probs(w_d, m_w)])
                           + pv(vwT_ref, [n1, n2], [probs(w_1, m_w), probs(w_2, m_w)])
                           + pv(vwT_ref, [n3, n4], [probs(w_3, m_w), probs(w_4, m_w)]))

    def stage_near(t, tl):
        i, n1, i1, qT = tl["i"], tl["n1"], tl["i1"], tl["qT"]
        tl["s_d"] = scores(qT, ks_ref, i, t0_ref[0], None)
        tl["s_1"] = scores(qT, ks_ref, n1, t1_ref[0], sel_mask(t, n1) & (i1 >= 1))
        tl["mx"] = jnp.maximum(colmax8(tl["s_d"]), colmax8(tl["s_1"]))

    tiles = [stage_q(t) for t in range(Q_PAIR)]
    for stage in (stage_cmp_scores, stage_cmp, stage_window_scores, stage_select, stage_near, stage_window_pv):
        for t, tl in enumerate(tiles):
            stage(t, tl)
    assert Q_PAIR % FAR_GROUP == 0
    common = (Q_PAIR // FAR_GROUP) * ip
    extras = [tuple(t for t in range(Q_PAIR) if t - 1 > FAR_GROUP * e) for e in range(Q_PAIR // FAR_GROUP)]

    def far_scores(tr, mxs, ts):
        out = list(mxs)
        for t in ts:
            tl = tiles[t]
            for c in range(FAR_GROUP):
                n = FAR_GROUP * tr + c
                sc = scores(tl["qT"], ks_ref, n, None,
                            sel_mask(t, n) & (jnp.full((TQ, TQ), n, jnp.int32) < tl["nfar"]))
                sfar_ref[t, n] = sc
                out[t] = jnp.maximum(out[t], colmax8(sc))
        return tuple(out)

    every = tuple(range(Q_PAIR))
    mxs = lax.fori_loop(0, common, lambda tr, c: far_scores(tr, c, every), tuple(tl["mx"] for tl in tiles))
    for e, ts in enumerate(extras):
        mxs = far_scores(common + e, mxs, ts)
    m_s = [jnp.max(mx, axis=0, keepdims=True) for mx in mxs]
    accs = tuple(pv(vsT_ref, [tl["i"], tl["n1"]], [probs(tl["s_d"], m_s[t]), probs(tl["s_1"], m_s[t])])
                 for t, tl in enumerate(tiles))

    def far_pv(tr, accs_, ts):
        ns = [FAR_GROUP * tr + c for c in range(FAR_GROUP)]
        out = list(accs_)
        for t in ts:
            out[t] = out[t] + pv(vsT_ref, ns, [probs(sfar_ref[t, n], m_s[t]) for n in ns])
        return tuple(out)

    accs = lax.fori_loop(0, common, lambda tr, c: far_pv(tr, c, every), accs)
    for e, ts in enumerate(extras):
        accs = far_pv(common + e, accs, ts)

    for t, tl in enumerate(tiles):
        base = g * GATE_GW
        gate = [jnp.concatenate([gsc_ref[t, pl.ds(base + br * HPG + h, 1), :] for h in range(HPG)], axis=1)
                for br in range(3)]
        aT = gate[0] * tl["ocT"] + gate[1] * finish(accs[t]) + gate[2] * tl["owT"]
        a4 = jnp.concatenate([aT[:, h * TQ:(h + 1) * TQ] for h in range(HPG)], axis=0)
        o_ref[t * TQ:(t + 1) * TQ, :] = a4.T.astype(o_ref.dtype)


def _attention(z, kc, vcT, t0, t1, tc):
    qw = HPG * HEAD_DIM
    rows = Q_PAIR * TQ
    npair = NQ // Q_PAIR
    return pl.pallas_call(
        _attn_body,
        grid=(BATCH, N_KV, npair),
        in_specs=[
            pl.BlockSpec((rows, qw), lambda b, g, i: (b * npair + i, Z_Q // qw + g)),
            pl.BlockSpec((SEQ, KV_GW), lambda b, g, i: (b, (Z_KV + 2 * KV_W) // KV_GW + g)),
            pl.BlockSpec((SEQ, KV_GW), lambda b, g, i: (b, (Z_KV + 4 * KV_W) // KV_GW + g)),
            pl.BlockSpec((rows, LANE), lambda b, g, i: (b * npair + i, Z_NG // LANE)),
            pl.BlockSpec((1, 1, N_CMP, HEAD_DIM), lambda b, g, i: (b, g, 0, 0)),
            pl.BlockSpec((1, 1, HEAD_DIM, N_CMP), lambda b, g, i: (b, g, 0, 0)),
            pl.BlockSpec((1, TQ, QL), lambda b, g, i: (g, 0, 0)),
            pl.BlockSpec((1, TQ, QL), lambda b, g, i: (g, 0, 0)),
            pl.BlockSpec((1, Q_PAIR, N_CMP, QL), lambda b, g, i: (g, i, 0, 0)),
        ],
        out_specs=pl.BlockSpec((rows, qw), lambda b, g, i: (b * npair + i, g)),
        out_shape=jax.ShapeDtypeStruct((TOK, NSA_W), BF16),
        scratch_shapes=[
            pltpu.VMEM((NK, TQ, HEAD_DIM), BF16),
            pltpu.VMEM((NK, V_ROWS, TQ), BF16),
            pltpu.VMEM((NK, TQ, HEAD_DIM), BF16),
            pltpu.VMEM((NK, V_ROWS, TQ), BF16),
            pltpu.VMEM((Q_PAIR, N_SBLK, TQ), F32),
            pltpu.VMEM((Q_PAIR, NK, TQ, QL), F32),
            pltpu.VMEM((Q_PAIR, LANE, TQ), F32),
            pltpu.VMEM((Q_PAIR, N_CMP, TQ), F32),
        ],
        compiler_params=_cparams(("parallel", "parallel", "arbitrary")),
        name="nsa_attention",
    )(z, z, z, z, kc, vcT, t0, t1, tc)


def _sgu_tile(u_ref, v_ref, g_ref, b_ref, w_ref, bs_ref):
    r = lax.broadcasted_iota(jnp.int32, (GM_CHUNK, GM_CHUNK), 0)
    c = lax.broadcasted_iota(jnp.int32, (GM_CHUNK, GM_CHUNK), 1)
    tril = r >= c
    ws = [jnp.where(tril, w_ref[gi], 0.0).astype(BF16) for gi in range(GM_GROUPS)]
    chunks = []
    for ch in range(u_ref.shape[0] // GM_CHUNK):
        rows = slice(ch * GM_CHUNK, (ch + 1) * GM_CHUNK)
        v = v_ref[rows, :]
        mu = jnp.mean(v, axis=-1, keepdims=True)
        vc = v - mu
        var = jnp.mean(vc * vc, axis=-1, keepdims=True)
        vn = (vc * lax.rsqrt(var + EPS) * g_ref[...] + b_ref[...]).astype(BF16)
        groups = []
        for gi in range(GM_GROUPS):
            cols = slice(gi * GM_GW, (gi + 1) * GM_GW)
            mixed = jnp.dot(ws[gi], vn[:, cols], preferred_element_type=F32) + bs_ref[:, cols]
            groups.append((u_ref[rows, cols] * mixed).astype(BF16))
        chunks.append(jnp.concatenate(groups, axis=1))
    return jnp.concatenate(chunks, axis=0)


def _out_body(a_ref, u_ref, v_ref, m1_ref, m2_ref, x_ref, lg_ref, lb_ref, sw_ref, bs_ref,
              wn_ref, ws_ref, wo_ref, gn_ref, o_ref, h_ref):
    sg = _sgu_tile(u_ref, v_ref, lg_ref, lb_ref, sw_ref, bs_ref)
    pa = jnp.dot(a_ref[...], wn_ref[...], preferred_element_type=F32)
    ps = jnp.dot(sg, ws_ref[...], preferred_element_type=F32)
    merged = (m1_ref[...] * pa + m2_ref[...] * ps).astype(BF16)
    y = x_ref[...] + jnp.dot(merged, wo_ref[...], preferred_element_type=F32)
    o_ref[...] = y
    h_ref[...] = _rms_bf16(y, gn_ref[...])


def _merge_out(a, z, x, ln_g, ln_b, w_s, bs_exp, wn, ws, wo, l, next_gain):
    const = lambda i: (l, 0, 0)
    once = pl.Buffered(1)
    return pl.pallas_call(
        _out_body,
        grid=(TOK // OUT_TM,),
        in_specs=[
            pl.BlockSpec((OUT_TM, NSA_W), lambda i: (i, 0)),
            pl.BlockSpec((OUT_TM, GM_W), lambda i: (i, Z_U // GM_W)),
            pl.BlockSpec((OUT_TM, GM_W), lambda i: (i, Z_V // GM_W)),
            pl.BlockSpec((OUT_TM, D_MODEL), lambda i: (i, 0)),
            pl.BlockSpec((OUT_TM, D_MODEL), lambda i: (i, 1)),
            pl.BlockSpec((OUT_TM, D_MODEL), lambda i: (i, 0)),
            pl.BlockSpec((1, GM_W), lambda i: (0, 0)),
            pl.BlockSpec((1, GM_W), lambda i: (0, 0)),
            pl.BlockSpec((GM_GROUPS, GM_CHUNK, GM_CHUNK), lambda i: (0, 0, 0), pipeline_mode=once),
            pl.BlockSpec((GM_CHUNK, GM_W), lambda i: (0, 0), pipeline_mode=once),
            pl.BlockSpec((None, NSA_W, D_MODEL), const, pipeline_mode=once),
            pl.BlockSpec((None, GM_W, D_MODEL), const, pipeline_mode=once),
            pl.BlockSpec((None, D_MODEL, D_MODEL), const, pipeline_mode=once),
            pl.BlockSpec((1, D_MODEL), lambda i: (0, 0)),
        ],
        out_specs=[pl.BlockSpec((OUT_TM, D_MODEL), lambda i: (i, 0)),
                   pl.BlockSpec((OUT_TM, D_MODEL), lambda i: (i, 0))],
        out_shape=[jax.ShapeDtypeStruct((TOK, D_MODEL), F32),
                   jax.ShapeDtypeStruct((TOK, D_MODEL), BF16)],
        compiler_params=_cparams(("parallel",)),
        name="merge_out",
    )(a, z, z, z, z, x, ln_g, ln_b, w_s, bs_exp, wn, ws, wo, next_gain)


def _check_overlap_stencil():
    ci = np.arange(N_CMP - 1)[None, :] * CMP_STRIDE
    sj = np.arange(N_SBLK)[:, None] * SEL_BLOCK
    ov = np.clip(np.minimum(ci + CMP_BLOCK, sj + SEL_BLOCK) - np.maximum(ci, sj), 0, None) / CMP_BLOCK
    stencil = np.zeros((N_SBLK, N_CMP - 1))
    for j in range(N_SBLK):
        for c, w in ((4 * j - 1, 0.5), (4 * j, 1.0), (4 * j + 1, 1.0), (4 * j + 2, 1.0), (4 * j + 3, 0.5)):
            if 0 <= c < N_CMP - 1:
                stencil[j, c] = w
    assert np.array_equal(ov, stencil)


def _block_diag_ones():
    r = np.arange(LANE)
    return (r[:, None] // HEAD_DIM == r[None, :] // HEAD_DIM).astype(np.float32)


def _prep_w_in(w):
    L = w.shape[0]
    w = w.astype(BF16)
    kv = w[:, :, OFF_KV:OFF_NG].reshape(L, D_MODEL, 3, 2, N_KV, HEAD_DIM)
    kv = kv.transpose(0, 1, 2, 4, 3, 5).reshape(L, D_MODEL, 6 * KV_W)
    ng = w[:, :, OFF_NG:OFF_UV].reshape(L, D_MODEL, N_KV, HPG, 3).transpose(0, 1, 2, 4, 3)
    ng = ng.reshape(L, D_MODEL, N_KV, 3 * HPG)
    ng = jnp.pad(ng, ((0, 0), (0, 0), (0, 0), (0, GATE_GW - 3 * HPG))).reshape(L, D_MODEL, N_KV * GATE_GW)
    ng = jnp.pad(ng, ((0, 0), (0, 0), (0, SEC_W - N_KV * GATE_GW)))
    return jnp.concatenate([w[:, :, OFF_MG:], w[:, :, OFF_UV:OFF_MG], w[:, :, :OFF_KV], kv, ng], axis=2)


def _mixer(x, z, l, p, tables):
    t0, t1, tc = tables
    pos = jnp.stack([p["cmp_pos_k"][l], p["cmp_pos_v"][l]]).reshape(2, 1, CMP_BLOCK * HEAD_DIM)
    w1 = jnp.stack([p["cmp_k_w1"][l], p["cmp_v_w1"][l]]).astype(BF16)
    kc, vcT = _compress(z, pos, w1, p["cmp_k_w2"][l].astype(BF16), p["cmp_v_w2"][l].T.astype(BF16),
                        p["k_norm"][l, 0].reshape(1, HEAD_DIM))
    a = _attention(z, kc, vcT, t0, t1, tc)
    bs_exp = jnp.repeat(p["sgu_b"][l].T, GM_GW, axis=1)
    return _merge_out(a, z, x, p["sgu_norm_g"][l].reshape(1, GM_W), p["sgu_norm_b"][l].reshape(1, GM_W),
                      p["sgu_w"][l], bs_exp, p["wn"], p["ws"], p["wo"], l,
                      p["ffn2_norm"][l].reshape(1, D_MODEL))


def kernel(x, rel_bias, ffn1_norm, ffn1_w_gate, ffn1_w_up, ffn1_w_down, mix_norm, w_in, q_norm, k_norm, cmp_pos_k, cmp_pos_v, cmp_k_w1, cmp_k_w2, cmp_v_w1, cmp_v_w2, sgu_norm_g, sgu_norm_b, sgu_w, sgu_b, w_proj_nsa, w_proj_sgu, w_out, ffn2_norm, ffn2_w_gate, ffn2_w_up, ffn2_w_down):
    p = dict(cmp_pos_k=cmp_pos_k, cmp_pos_v=cmp_pos_v, cmp_k_w1=cmp_k_w1, cmp_k_w2=cmp_k_w2,
             cmp_v_w1=cmp_v_w1, cmp_v_w2=cmp_v_w2, k_norm=k_norm, sgu_norm_g=sgu_norm_g,
             sgu_norm_b=sgu_norm_b, sgu_w=sgu_w, sgu_b=sgu_b, wn=w_proj_nsa.astype(BF16),
             ws=w_proj_sgu.astype(BF16), wo=w_out.astype(BF16), ffn2_norm=ffn2_norm)
    near, cmpb = _bucket_tables()
    t0, t1, tc = _bias_tables(rel_bias, jnp.asarray(near), jnp.asarray(cmpb))
    tables = (t0, t1, tc)
    _check_overlap_stencil()
    bd = jnp.asarray(_block_diag_ones()).astype(BF16)
    w_in_r = _prep_w_in(w_in)
    f1 = (ffn1_w_gate, ffn1_w_up, ffn1_w_down)
    f2 = (ffn2_w_gate, ffn2_w_up, ffn2_w_down)

    y = x.reshape(TOK, D_MODEL)
    for l in range(DEPTH):
        y = _ffn(y, _rmsnorm(y, ffn1_norm[l].reshape(1, D_MODEL)), *f1, l)
        z = _proj_in(y, mix_norm[l].reshape(1, D_MODEL), w_in_r, l, bd,
                     jnp.tile(q_norm[l], PIN_TN // HEAD_DIM).reshape(1, PIN_TN),
                     jnp.tile(k_norm[l, 1], SEC_W // HEAD_DIM).reshape(1, SEC_W),
                     jnp.tile(k_norm[l, 2], SEC_W // HEAD_DIM).reshape(1, SEC_W))
        y, h = _mixer(y, z, l, p, tables)
        y = _ffn(y, h, *f2, l)
    return y.reshape(BATCH, SEQ, D_MODEL)
```

```python
import math

import numpy as np
import jax
import jax.numpy as jnp
from jax import lax
from jax.experimental import pallas as pl
from jax.experimental.pallas import tpu as pltpu

F32 = jnp.float32
BF16 = jnp.bfloat16

D_MODEL = 2048
BATCH = 4
SEQ = 2048
DEPTH = 2
TOK = BATCH * SEQ
HEAD_DIM = 64
N_HEADS = 16
N_KV = 4
HPG = 4
NSA_W = 1024
KV_W = 256
CMP_BLOCK = 32
CMP_STRIDE = 16
CMP_HIDDEN = 256
SEL_BLOCK = 64
N_SEL = 16
WINDOW = 512
GM_W = 1024
GM_CHUNK = 128
GM_GW = 128
GM_GROUPS = 8
N_BUCKETS = 32
MAX_DISTANCE = 128
D_FF = 5504
EPS = 1e-6
NEG = -1e30
FORCE = 1e4
OFF_KV = NSA_W
OFF_NG = OFF_KV + 6 * KV_W
OFF_UV = OFF_NG + 3 * N_HEADS
OFF_MG = OFF_UV + 2 * GM_W
LOG2E = 1.4426950408889634

LANE = 128
SUBLANE = 8
VMEM_LIMIT = 56 * 1024 * 1024

FFN_TF = 512
FFN_UP_TM = 1024
NORM_TM = 1024
FFN_DN_TM = 512
FFN_DN_TN = 512
PIN_TM = 1024
PIN_TN = 1024
PIN_RC = 256
SEC_W = PIN_TN // 2
Z_MG = 0
Z_U = Z_MG + 2 * D_MODEL
Z_V = Z_U + GM_W
Z_Q = Z_V + GM_W
Z_KV = Z_Q + NSA_W
Z_NG = Z_KV + 6 * KV_W
Z_W = Z_NG + SEC_W
KV_GW = 2 * HEAD_DIM
GATE_GW = 16
TQ = 128
NQ = SEQ // TQ
NK = SEQ // TQ
QL = HPG * TQ
N_CMP = 128
CMP_BAND = 16
MASKED = -1
N_SBLK = SEQ // SEL_BLOCK
V_ROWS = HEAD_DIM + 16
FAR_GROUP = 4
Q_PAIR = 4
OUT_TM = 256


def _cparams(sem):
    return pltpu.CompilerParams(dimension_semantics=sem, vmem_limit_bytes=VMEM_LIMIT)


def _cast_rows(src_ref, dst_ref, chunk):
    def body(r, carry):
        rows = pl.ds(pl.multiple_of(r * chunk, chunk), chunk)
        dst_ref[rows, :] = src_ref[rows, :].astype(BF16)
        return carry

    lax.fori_loop(0, src_ref.shape[0] // chunk, body, 0)


def _rms_bf16(x, gain):
    ms = jnp.mean(x * x, axis=-1, keepdims=True)
    return (x * lax.rsqrt(ms + EPS) * gain).astype(BF16)


def _norm_body(x_ref, g_ref, h_ref):
    h_ref[...] = _rms_bf16(x_ref[...], g_ref[...])


def _rmsnorm(x, gain):
    return pl.pallas_call(
        _norm_body,
        grid=(TOK // NORM_TM,),
        in_specs=[pl.BlockSpec((NORM_TM, D_MODEL), lambda i: (i, 0)),
                  pl.BlockSpec((1, D_MODEL), lambda i: (0, 0))],
        out_specs=pl.BlockSpec((NORM_TM, D_MODEL), lambda i: (i, 0)),
        out_shape=jax.ShapeDtypeStruct((TOK, D_MODEL), BF16),
        compiler_params=_cparams(("parallel",)),
        name="rmsnorm",
    )(x, gain)


def _ffn_up_body(h_ref, wg_ref, wu_ref, a_ref, wgb_ref, wub_ref):
    @pl.when(pl.program_id(1) == 0)
    def _():
        _cast_rows(wg_ref, wgb_ref, LANE)
        _cast_rows(wu_ref, wub_ref, LANE)

    h = h_ref[...]
    g = jnp.dot(h, wgb_ref[...], preferred_element_type=F32)
    u = jnp.dot(h, wub_ref[...], preferred_element_type=F32)
    a_ref[...] = (g * jax.nn.sigmoid(g) * u).astype(BF16)


def _ffn_down_body(a_ref, wd_ref, x_ref, o_ref, wdb_ref):
    @pl.when(pl.program_id(1) == 0)
    def _():
        _cast_rows(wd_ref, wdb_ref, LANE)

    o_ref[...] = x_ref[...] + 0.5 * jnp.dot(a_ref[...], wdb_ref[...], preferred_element_type=F32)


def _ffn(x, h, wg, wu, wd, l):
    nf = pl.cdiv(D_FF, FFN_TF)
    a = pl.pallas_call(
        _ffn_up_body,
        grid=(nf, TOK // FFN_UP_TM),
        in_specs=[
            pl.BlockSpec((FFN_UP_TM, D_MODEL), lambda f, m: (m, 0)),
            pl.BlockSpec((None, D_MODEL, FFN_TF), lambda f, m: (l, 0, f)),
            pl.BlockSpec((None, D_MODEL, FFN_TF), lambda f, m: (l, 0, f)),
        ],
        out_specs=pl.BlockSpec((FFN_UP_TM, FFN_TF), lambda f, m: (m, f)),
        out_shape=jax.ShapeDtypeStruct((TOK, D_FF), BF16),
        scratch_shapes=[pltpu.VMEM((D_MODEL, FFN_TF), BF16), pltpu.VMEM((D_MODEL, FFN_TF), BF16)],
        compiler_params=_cparams(("arbitrary", "arbitrary")),
        name="ffn_up",
    )(h, wg, wu)
    return pl.pallas_call(
        _ffn_down_body,
        grid=(D_MODEL // FFN_DN_TN, TOK // FFN_DN_TM),
        in_specs=[
            pl.BlockSpec((FFN_DN_TM, D_FF), lambda n, m: (m, 0)),
            pl.BlockSpec((None, D_FF, FFN_DN_TN), lambda n, m: (l, 0, n)),
            pl.BlockSpec((FFN_DN_TM, FFN_DN_TN), lambda n, m: (m, n)),
        ],
        out_specs=pl.BlockSpec((FFN_DN_TM, FFN_DN_TN), lambda n, m: (m, n)),
        out_shape=jax.ShapeDtypeStruct((TOK, D_MODEL), F32),
        scratch_shapes=[pltpu.VMEM((D_FF, FFN_DN_TN), BF16)],
        compiler_params=_cparams(("arbitrary", "arbitrary")),
        name="ffn_down",
    )(a, wd, x)


def _head_rms(y, bd):
    y2 = (y * y).astype(BF16)
    ss = jnp.concatenate(
        [jnp.dot(y2[:, k * LANE:(k + 1) * LANE], bd, preferred_element_type=F32)
         for k in range(y.shape[1] // LANE)], axis=1)
    return y * lax.rsqrt(ss * (1.0 / HEAD_DIM) + EPS)


def _pin_body(x_ref, g_ref, w_ref, bd_ref, qg_ref, k1g_ref, k2g_ref, o_ref, h_ref):
    j = pl.program_id(1)

    @pl.when(j == 0)
    def _():
        x = x_ref[...]
        ms = jnp.mean(x * x, axis=-1, keepdims=True)
        h_ref[...] = (x * lax.rsqrt(ms + EPS) * g_ref[...]).astype(BF16)

    t_u = Z_U // PIN_TN
    t_q = Z_Q // PIN_TN
    t_kv = Z_KV // PIN_TN
    half = SEC_W

    def emit(act):
        for r in range(PIN_TM // PIN_RC):
            rows = slice(r * PIN_RC, (r + 1) * PIN_RC)
            o_ref[rows, :] = act(jnp.dot(h_ref[rows, :], w_ref[...], preferred_element_type=F32))

    def halves(act_lo, act_hi):
        return lambda y: jnp.concatenate([act_lo(y[:, :half]), act_hi(y[:, half:])], axis=1)

    def kv_act(kg_ref):
        is_k = (lax.broadcasted_iota(jnp.int32, (1, half), 1) & (KV_GW - 1)) < HEAD_DIM
        return lambda y: jnp.where(is_k, _head_rms(y, bd_ref[...]) * kg_ref[...], y)

    @pl.when(j < t_u)
    def _():
        emit(jax.nn.sigmoid)

    @pl.when((j >= t_u) & (j < t_q))
    def _():
        emit(jax.nn.gelu)

    @pl.when((j >= t_q) & (j < t_kv))
    def _():
        emit(lambda y: _head_rms(y, bd_ref[...]) * (qg_ref[...] * (HEAD_DIM ** -0.5 * LOG2E)))

    @pl.when(j == t_kv)
    def _():
        emit(halves(lambda y: y, kv_act(k1g_ref)))

    @pl.when(j == t_kv + 1)
    def _():
        emit(halves(kv_act(k2g_ref), jax.nn.sigmoid))


def _proj_in(x, gain, w, l, bd, qg, k1g, k2g):
    return pl.pallas_call(
        _pin_body,
        grid=(TOK // PIN_TM, Z_W // PIN_TN),
        in_specs=[
            pl.BlockSpec((PIN_TM, D_MODEL), lambda i, j: (i, 0)),
            pl.BlockSpec((1, D_MODEL), lambda i, j: (0, 0)),
            pl.BlockSpec((None, D_MODEL, PIN_TN), lambda i, j: (l, 0, j)),
            pl.BlockSpec((LANE, LANE), lambda i, j: (0, 0)),
            pl.BlockSpec((1, PIN_TN), lambda i, j: (0, 0)),
            pl.BlockSpec((1, SEC_W), lambda i, j: (0, 0)),
            pl.BlockSpec((1, SEC_W), lambda i, j: (0, 0)),
        ],
        out_specs=pl.BlockSpec((PIN_TM, PIN_TN), lambda i, j: (i, j)),
        out_shape=jax.ShapeDtypeStruct((TOK, Z_W), F32),
        scratch_shapes=[pltpu.VMEM((PIN_TM, D_MODEL), BF16)],
        compiler_params=_cparams(("parallel", "arbitrary")),
        name="proj_in",
    )(x, gain, w, bd, qg, k1g, k2g)


def _cmp_body(z_ref, pos_ref, w1_ref, w2k_ref, w2vT_ref, kg_ref, kc_ref, vcT_ref):
    half = CMP_STRIDE * HEAD_DIM
    lane = lax.broadcasted_iota(jnp.int32, (N_CMP, LANE), 1)
    first = lane < HEAD_DIM
    xk, xv = [], []
    for j in range(CMP_STRIDE // 2):
        ev = z_ref[pl.ds(2 * j, N_CMP, stride=CMP_STRIDE), :]
        od = z_ref[pl.ds(2 * j + 1, N_CMP, stride=CMP_STRIDE), :]
        xk.append(jnp.where(first, ev, pltpu.roll(od, HEAD_DIM, 1)))
        xv.append(jnp.where(first, pltpu.roll(ev, HEAD_DIM, 1), od))

    def hidden(x, kind):
        xa = (x + pos_ref[kind, :, :half]).astype(BF16)
        xb = (x + pos_ref[kind, :, half:]).astype(BF16)
        a = jnp.dot(xa, w1_ref[kind, :half, :], preferred_element_type=F32)
        b = jnp.dot(xb, w1_ref[kind, half:, :], preferred_element_type=F32)
        hid = a + pltpu.roll(b, N_CMP - 1, 0)
        return (hid * jax.nn.sigmoid(hid)).astype(BF16)

    yk = jnp.dot(hidden(jnp.concatenate(xk, axis=1), 0), w2k_ref[...], preferred_element_type=F32)
    ms = jnp.mean(yk * yk, axis=-1, keepdims=True)
    kc_ref[0, 0] = (yk * lax.rsqrt(ms + EPS) * kg_ref[...]).astype(BF16)
    hv = hidden(jnp.concatenate(xv, axis=1), 1)
    vcT_ref[0, 0] = lax.dot_general(w2vT_ref[...], hv, (((1,), (1,)), ((), ())),
                                    preferred_element_type=F32).astype(BF16)


def _compress(z, pos, w1, w2k, w2vT, kg):
    return pl.pallas_call(
        _cmp_body,
        grid=(BATCH, N_KV),
        in_specs=[
            pl.BlockSpec((SEQ, KV_GW), lambda b, g: (b, Z_KV // KV_GW + g)),
            pl.BlockSpec((2, 1, CMP_BLOCK * HEAD_DIM), lambda b, g: (0, 0, 0)),
            pl.BlockSpec((2, CMP_BLOCK * HEAD_DIM, CMP_HIDDEN), lambda b, g: (0, 0, 0)),
            pl.BlockSpec((CMP_HIDDEN, HEAD_DIM), lambda b, g: (0, 0)),
            pl.BlockSpec((HEAD_DIM, CMP_HIDDEN), lambda b, g: (0, 0)),
            pl.BlockSpec((1, HEAD_DIM), lambda b, g: (0, 0)),
        ],
        out_specs=[
            pl.BlockSpec((1, 1, N_CMP, HEAD_DIM), lambda b, g: (b, g, 0, 0)),
            pl.BlockSpec((1, 1, HEAD_DIM, N_CMP), lambda b, g: (b, g, 0, 0)),
        ],
        out_shape=[
            jax.ShapeDtypeStruct((BATCH, N_KV, N_CMP, HEAD_DIM), BF16),
            jax.ShapeDtypeStruct((BATCH, N_KV, HEAD_DIM, N_CMP), BF16),
        ],
        compiler_params=_cparams(("parallel", "parallel")),
        name="compress",
    )(z, pos, w1, w2k, w2vT, kg)


def _np_bucket(dist):
    n = np.maximum(dist, 0)
    max_exact = N_BUCKETS // 2
    nf = np.maximum(n, 1).astype(np.float32)
    large = max_exact + (np.log(nf / np.float32(max_exact)) / np.float32(math.log(MAX_DISTANCE / max_exact))
                         * np.float32(N_BUCKETS - max_exact)).astype(np.int32)
    large = np.minimum(large, N_BUCKETS - 1)
    return np.where(n < max_exact, n, large).astype(np.int32)


def _bucket_tables():
    kj = np.arange(TQ)[:, None]
    qi = np.arange(TQ)[None, :]
    near = np.stack([np.where(qi >= kj, _np_bucket(qi - kj), MASKED), _np_bucket(TQ + qi - kj)])
    start = np.array([_cmp_band_start(i) for i in range(NQ)])[:, None, None]
    c = start + np.arange(CMP_BAND)[None, :, None]
    t = (np.arange(NQ)[:, None, None] * TQ + qi[None])
    dist_band = t - (c * CMP_STRIDE + CMP_BLOCK - 1)
    cmp_b = np.where(dist_band >= 0, _np_bucket(dist_band), MASKED)
    dist = t - (np.arange(N_CMP)[None, :, None] * CMP_STRIDE + CMP_BLOCK - 1)
    for i in range(NQ):
        s0 = _cmp_band_start(i)
        assert np.all(dist[i, :s0] >= 0) and np.all(_np_bucket(dist[i, :s0]) == N_BUCKETS - 1)
        assert np.all(dist[i, s0 + CMP_BAND:] < 0)
    return near.astype(np.int32), cmp_b.astype(np.int32)


def _cmp_band_start(i):
    return max((TQ // CMP_STRIDE) * (i - 1), 0)


def _bias_body(rb_ref, near_ref, cmpb_ref, t0_ref, t1_ref, tc_ref):
    g = pl.program_id(0)

    def lut(bk, h):
        val = jnp.full(bk.shape, NEG, F32)
        for b in range(N_BUCKETS):
            val = jnp.where(bk == b, rb_ref[b, h], val)
        return val

    for hh in range(HPG):
        h = g * HPG + hh
        sl = slice(hh * TQ, (hh + 1) * TQ)
        far = rb_ref[N_BUCKETS - 1, h]
        t0_ref[0, :, sl] = (lut(near_ref[0], h) - far) * LOG2E
        t1_ref[0, :, sl] = (lut(near_ref[1], h) - far) * LOG2E

        def body(i, carry):
            start = pl.multiple_of(jnp.maximum((TQ // CMP_STRIDE) * (i - 1), 0), SUBLANE)
            row = lax.broadcasted_iota(jnp.int32, (N_CMP, TQ), 0)
            tc_ref[0, i, :, sl] = jnp.where(row < start, far * LOG2E, NEG)
            tc_ref[0, i, pl.ds(start, CMP_BAND), sl] = lut(cmpb_ref[i], h) * LOG2E
            return carry

        lax.fori_loop(0, NQ, body, 0, unroll=4)


def _bias_tables(rel_bias, near, cmpb):
    return pl.pallas_call(
        _bias_body,
        grid=(N_KV,),
        in_specs=[
            pl.BlockSpec(memory_space=pltpu.SMEM),
            pl.BlockSpec((2, TQ, TQ), lambda g: (0, 0, 0)),
            pl.BlockSpec((NQ, CMP_BAND, TQ), lambda g: (0, 0, 0)),
        ],
        out_specs=[
            pl.BlockSpec((1, TQ, QL), lambda g: (g, 0, 0)),
            pl.BlockSpec((1, TQ, QL), lambda g: (g, 0, 0)),
            pl.BlockSpec((1, NQ, N_CMP, QL), lambda g: (g, 0, 0, 0)),
        ],
        out_shape=[
            jax.ShapeDtypeStruct((N_KV, TQ, QL), F32),
            jax.ShapeDtypeStruct((N_KV, TQ, QL), F32),
            jax.ShapeDtypeStruct((N_KV, NQ, N_CMP, QL), F32),
        ],
        compiler_params=_cparams(("arbitrary",)),
        name="bias_tables",
    )(rel_bias, near, cmpb)


def _attn_body(zq_ref, zs_ref, zw_ref, zg_ref, kc_ref, vcT_ref, t0_ref, t1_ref, tc_ref, o_ref,
               ks_ref, vsT_ref, kw_ref, vwT_ref, sel_ref, sfar_ref, gsc_ref, psum_ref):
    g = pl.program_id(1)
    ip = pl.program_id(2)

    @pl.when(ip == 0)
    def _():
        ones = jnp.ones((V_ROWS - HEAD_DIM, TQ), BF16)

        def fill(n, carry):
            for z_ref, k_ref, vT_ref in ((zs_ref, ks_ref, vsT_ref), (zw_ref, kw_ref, vwT_ref)):
                blk = z_ref[pl.ds(pl.multiple_of(n * TQ, TQ), TQ), :]
                k_ref[n] = blk[:, :HEAD_DIM].astype(BF16)
                vT_ref[n, :HEAD_DIM, :] = blk.T[HEAD_DIM:, :].astype(BF16)
                vT_ref[n, HEAD_DIM:, :] = ones
            return carry

        lax.fori_loop(0, NK, fill, 0, unroll=4)

    blk = lax.broadcasted_iota(jnp.int32, (N_SBLK, TQ), 0)
    ql = lax.broadcasted_iota(jnp.int32, (N_SBLK, TQ), 1)
    key1 = lax.broadcasted_iota(jnp.int32, (TQ, TQ), 0)
    q1 = lax.broadcasted_iota(jnp.int32, (TQ, TQ), 1)
    nlast = WINDOW // TQ

    def sel_mask(t, n):
        r0 = sel_ref[t, pl.ds(2 * n, 1), :]
        r1 = sel_ref[t, pl.ds(2 * n + 1, 1), :]
        return jnp.where(key1 < SEL_BLOCK, r0, r1) > 0.5

    def scores(qT, k_ref, n, bias, mask):
        sc = jnp.dot(k_ref[n], qT, preferred_element_type=F32)
        if bias is not None:
            sc = sc + bias
        if mask is None:
            return sc
        return jnp.concatenate(
            [jnp.where(mask, sc[:, h * TQ:(h + 1) * TQ], NEG) for h in range(HPG)], axis=1)

    def colmax8(sc):
        return jnp.max(sc.reshape(TQ // SUBLANE, SUBLANE, QL), axis=0)

    def probs(sc, mrow):
        return jnp.exp2(sc - mrow).astype(BF16)

    def pv(vT_ref, ns, ps):
        vv = vT_ref[ns[0]] if len(ns) == 1 else jnp.concatenate([vT_ref[n] for n in ns], axis=1)
        pp = ps[0] if len(ps) == 1 else jnp.concatenate(ps, axis=0)
        return jnp.dot(vv, pp, preferred_element_type=F32)

    def finish(acc):
        return acc[:HEAD_DIM] * (1.0 / acc[HEAD_DIM:HEAD_DIM + 1])

    def stage_q(t):
        i = Q_PAIR * ip + t
        rows = slice(t * TQ, (t + 1) * TQ)
        qt = zq_ref[rows, :].T
        qT = jnp.concatenate([qt[h * HEAD_DIM:(h + 1) * HEAD_DIM] for h in range(HPG)], axis=1).astype(BF16)
        gsc_ref[t] = zg_ref[rows, :].T
        return dict(i=i, qT=qT, n1=jnp.maximum(i - 1, 0), i1=jnp.full((TQ, TQ), i, jnp.int32),
                    nfar=jnp.maximum(i - 1, 0))

    def stage_cmp_scores(t, tl):
        tl["s_c"] = jnp.dot(kc_ref[0, 0], tl["qT"], preferred_element_type=F32) + tc_ref[0, t]
        tl["m_c"] = jnp.maximum(jnp.max(colmax8(tl["s_c"]), axis=0, keepdims=True), 0.1 * NEG)

    def stage_cmp(t, tl):
        e = jnp.exp2(tl["s_c"] - tl["m_c"])
        l = jnp.sum(e, axis=0, keepdims=True)
        p = e * jnp.where(l > 0.0, 1.0 / l, 0.0)
        tl["ocT"] = jnp.dot(vcT_ref[0, 0], p.astype(BF16), preferred_element_type=F32)
        psum_ref[t] = p[:, 0:TQ] + p[:, TQ:2 * TQ] + p[:, 2 * TQ:3 * TQ] + p[:, 3 * TQ:4 * TQ]

    def stage_select(t, tl):
        i = tl["i"]
        ratio = SEL_BLOCK // CMP_STRIDE
        part = [psum_ref[t, pl.ds(r, N_SBLK, stride=ratio), :] for r in range(ratio)]
        prev_last = jnp.where(blk == 0, 0.0, pltpu.roll(part[3], 1, 0))
        imp = (part[0] + part[1] + part[2]) + 0.5 * (part[3] + prev_last)
        cur = jnp.full((N_SBLK, TQ), i, jnp.int32) * (TQ // SEL_BLOCK) + ql // SEL_BLOCK
        forced = (blk == 0) | (blk == cur) | (blk == cur - 1)
        score = jnp.where(blk <= cur, jnp.where(forced, FORCE, imp), NEG)
        sub = SUBLANE
        pieces = [score[k * sub:(k + 1) * sub, :] for k in range(N_SBLK // sub)]
        ranks = [jnp.zeros((sub, TQ), F32) for _ in pieces]
        for jj in range(N_SBLK):
            row = score[jj:jj + 1, :]
            for k, pc in enumerate(pieces):
                if k * sub > jj:
                    beats = row >= pc
                elif (k + 1) * sub - 1 < jj:
                    beats = row > pc
                else:
                    beats = (row > pc) | ((row == pc) & (blk[k * sub:(k + 1) * sub, :] > jj))
                ranks[k] = ranks[k] + jnp.where(beats, 1.0, 0.0)
        rank = jnp.concatenate(ranks, axis=0)
        sel_ref[t] = jnp.where((rank < N_SEL) & (blk <= cur), 1.0, 0.0)

    def stage_window_scores(t, tl):
        i, n1, i1, qT = tl["i"], tl["n1"], tl["i1"], tl["qT"]
        n2, n3, n4 = jnp.maximum(i - 2, 0), jnp.maximum(i - 3, 0), jnp.maximum(i - nlast, 0)
        w_d = scores(qT, kw_ref, i, t0_ref[0], None)
        w_1 = scores(qT, kw_ref, n1, t1_ref[0], i1 >= 1)
        w_2 = scores(qT, kw_ref, n2, None, i1 >= 2)
        w_3 = scores(qT, kw_ref, n3, None, i1 >= 3)
        w_4 = scores(qT, kw_ref, n4, None, (q1 < key1) & (i1 >= nlast))
        mxw = jnp.maximum(jnp.maximum(jnp.maximum(colmax8(w_d), colmax8(w_1)),
                                      jnp.maximum(colmax8(w_2), colmax8(w_3))), colmax8(w_4))
        tl["w"] = (w_d, w_1, w_2, w_3, w_4)
        tl["wn"] = (i, n1, n2, n3, n4)
        tl["m_w"] = jnp.max(mxw, axis=0, keepdims=True)

    def stage_window_pv(t, tl):
        (w_d, w_1, w_2, w_3, w_4), (i, n1, n2, n3, n4), m_w = tl["w"], tl["wn"], tl["m_w"]
        tl["owT"] = finish(pv(vwT_ref, [i], [probs(w_d, m_w)])
                           + pv(vwT_ref, [n1, n2], [probs(w_1, m_w), probs(w_2, m_w)])
                           + pv(vwT_ref, [n3, n4], [probs(w_3, m_w), probs(w_4, m_w)]))

    def stage_near(t, tl):
        i, n1, i1, qT = tl["i"], tl["n1"], tl["i1"], tl["qT"]
        tl["s_d"] = scores(qT, ks_ref, i, t0_ref[0], None)
        tl["s_1"] = scores(qT, ks_ref, n1, t1_ref[0], sel_mask(t, n1) & (i1 >= 1))
        tl["mx"] = jnp.maximum(colmax8(tl["s_d"]), colmax8(tl["s_1"]))

    tiles = [stage_q(t) for t in range(Q_PAIR)]
    for stage in (stage_cmp_scores, stage_cmp, stage_window_scores, stage_select, stage_near, stage_window_pv):
        for t, tl in enumerate(tiles):
            stage(t, tl)
    assert Q_PAIR == FAR_GROUP
    every = {t: list(range(FAR_GROUP)) for t in range(Q_PAIR)}
    leftover = {t: list(range(t - 1)) for t in range(2, Q_PAIR)}

    def far_scores(tr, mxs, offs):
        out = list(mxs)
        for t, cs in offs.items():
            tl = tiles[t]
            for c in cs:
                n = FAR_GROUP * tr + c
                sc = scores(tl["qT"], ks_ref, n, None,
                            sel_mask(t, n) & (jnp.full((TQ, TQ), n, jnp.int32) < tl["nfar"]))
                sfar_ref[t, n] = sc
                out[t] = jnp.maximum(out[t], colmax8(sc))
        return tuple(out)

    mxs = lax.fori_loop(0, ip, lambda tr, c: far_scores(tr, c, every), tuple(tl["mx"] for tl in tiles))
    mxs = far_scores(ip, mxs, leftover)
    m_s = [jnp.max(mx, axis=0, keepdims=True) for mx in mxs]
    accs = tuple(pv(vsT_ref, [tl["i"], tl["n1"]], [probs(tl["s_d"], m_s[t]), probs(tl["s_1"], m_s[t])])
                 for t, tl in enumerate(tiles))

    def far_pv(tr, accs_, offs):
        out = list(accs_)
        for t, cs in offs.items():
            ns = [FAR_GROUP * tr + c for c in cs]
            out[t] = out[t] + pv(vsT_ref, ns, [probs(sfar_ref[t, n], m_s[t]) for n in ns])
        return tuple(out)

    accs = lax.fori_loop(0, ip, lambda tr, c: far_pv(tr, c, every), accs)
    accs = far_pv(ip, accs, leftover)

    for t, tl in enumerate(tiles):
        base = g * GATE_GW
        gate = [jnp.concatenate([gsc_ref[t, pl.ds(base + br * HPG + h, 1), :] for h in range(HPG)], axis=1)
                for br in range(3)]
        aT = gate[0] * tl["ocT"] + gate[1] * finish(accs[t]) + gate[2] * tl["owT"]
        a4 = jnp.concatenate([aT[:, h * TQ:(h + 1) * TQ] for h in range(HPG)], axis=0)
        o_ref[t * TQ:(t + 1) * TQ, :] = a4.T.astype(o_ref.dtype)


def _attention(z, kc, vcT, t0, t1, tc):
    qw = HPG * HEAD_DIM
    rows = Q_PAIR * TQ
    npair = NQ // Q_PAIR
    return pl.pallas_call(
        _attn_body,
        grid=(BATCH, N_KV, npair),
        in_specs=[
            pl.BlockSpec((rows, qw), lambda b, g, i: (b * npair + i, Z_Q // qw + g)),
            pl.BlockSpec((SEQ, KV_GW), lambda b, g, i: (b, (Z_KV + 2 * KV_W) // KV_GW + g)),
            pl.BlockSpec((SEQ, KV_GW), lambda b, g, i: (b, (Z_KV + 4 * KV_W) // KV_GW + g)),
            pl.BlockSpec((rows, LANE), lambda b, g, i: (b * npair + i, Z_NG // LANE)),
            pl.BlockSpec((1, 1, N_CMP, HEAD_DIM), lambda b, g, i: (b, g, 0, 0)),
            pl.BlockSpec((1, 1, HEAD_DIM, N_CMP), lambda b, g, i: (b, g, 0, 0)),
            pl.BlockSpec((1, TQ, QL), lambda b, g, i: (g, 0, 0)),
            pl.BlockSpec((1, TQ, QL), lambda b, g, i: (g, 0, 0)),
            pl.BlockSpec((1, Q_PAIR, N_CMP, QL), lambda b, g, i: (g, i, 0, 0)),
        ],
        out_specs=pl.BlockSpec((rows, qw), lambda b, g, i: (b * npair + i, g)),
        out_shape=jax.ShapeDtypeStruct((TOK, NSA_W), BF16),
        scratch_shapes=[
            pltpu.VMEM((NK, TQ, HEAD_DIM), BF16),
            pltpu.VMEM((NK, V_ROWS, TQ), BF16),
            pltpu.VMEM((NK, TQ, HEAD_DIM), BF16),
            pltpu.VMEM((NK, V_ROWS, TQ), BF16),
            pltpu.VMEM((Q_PAIR, N_SBLK, TQ), F32),
            pltpu.VMEM((Q_PAIR, NK, TQ, QL), F32),
            pltpu.VMEM((Q_PAIR, LANE, TQ), F32),
            pltpu.VMEM((Q_PAIR, N_CMP, TQ), F32),
        ],
        compiler_params=_cparams(("parallel", "parallel", "arbitrary")),
        name="nsa_attention",
    )(z, z, z, z, kc, vcT, t0, t1, tc)


def _sgu_tile(u_ref, v_ref, g_ref, b_ref, w_ref, bs_ref):
    r = lax.broadcasted_iota(jnp.int32, (GM_CHUNK, GM_CHUNK), 0)
    c = lax.broadcasted_iota(jnp.int32, (GM_CHUNK, GM_CHUNK), 1)
    tril = r >= c
    ws = [jnp.where(tril, w_ref[gi], 0.0).astype(BF16) for gi in range(GM_GROUPS)]
    chunks = []
    for ch in range(u_ref.shape[0] // GM_CHUNK):
        rows = slice(ch * GM_CHUNK, (ch + 1) * GM_CHUNK)
        v = v_ref[rows, :]
        mu = jnp.mean(v, axis=-1, keepdims=True)
        vc = v - mu
        var = jnp.mean(vc * vc, axis=-1, keepdims=True)
        vn = (vc * lax.rsqrt(var + EPS) * g_ref[...] + b_ref[...]).astype(BF16)
        groups = []
        for gi in range(GM_GROUPS):
            cols = slice(gi * GM_GW, (gi + 1) * GM_GW)
            mixed = jnp.dot(ws[gi], vn[:, cols], preferred_element_type=F32) + bs_ref[:, cols]
            groups.append((u_ref[rows, cols] * mixed).astype(BF16))
        chunks.append(jnp.concatenate(groups, axis=1))
    return jnp.concatenate(chunks, axis=0)


def _out_body(a_ref, u_ref, v_ref, m1_ref, m2_ref, x_ref, lg_ref, lb_ref, sw_ref, bs_ref,
              wn_ref, ws_ref, wo_ref, gn_ref, o_ref, h_ref):
    sg = _sgu_tile(u_ref, v_ref, lg_ref, lb_ref, sw_ref, bs_ref)
    pa = jnp.dot(a_ref[...], wn_ref[...], preferred_element_type=F32)
    ps = jnp.dot(sg, ws_ref[...], preferred_element_type=F32)
    merged = (m1_ref[...] * pa + m2_ref[...] * ps).astype(BF16)
    y = x_ref[...] + jnp.dot(merged, wo_ref[...], preferred_element_type=F32)
    o_ref[...] = y
    h_ref[...] = _rms_bf16(y, gn_ref[...])


def _merge_out(a, z, x, ln_g, ln_b, w_s, bs_exp, wn, ws, wo, l, next_gain):
    const = lambda i: (l, 0, 0)
    once = pl.Buffered(1)
    return pl.pallas_call(
        _out_body,
        grid=(TOK // OUT_TM,),
        in_specs=[
            pl.BlockSpec((OUT_TM, NSA_W), lambda i: (i, 0)),
            pl.BlockSpec((OUT_TM, GM_W), lambda i: (i, Z_U // GM_W)),
            pl.BlockSpec((OUT_TM, GM_W), lambda i: (i, Z_V // GM_W)),
            pl.BlockSpec((OUT_TM, D_MODEL), lambda i: (i, 0)),
            pl.BlockSpec((OUT_TM, D_MODEL), lambda i: (i, 1)),
            pl.BlockSpec((OUT_TM, D_MODEL), lambda i: (i, 0)),
            pl.BlockSpec((1, GM_W), lambda i: (0, 0)),
            pl.BlockSpec((1, GM_W), lambda i: (0, 0)),
            pl.BlockSpec((GM_GROUPS, GM_CHUNK, GM_CHUNK), lambda i: (0, 0, 0), pipeline_mode=once),
            pl.BlockSpec((GM_CHUNK, GM_W), lambda i: (0, 0), pipeline_mode=once),
            pl.BlockSpec((None, NSA_W, D_MODEL), const, pipeline_mode=once),
            pl.BlockSpec((None, GM_W, D_MODEL), const, pipeline_mode=once),
            pl.BlockSpec((None, D_MODEL, D_MODEL), const, pipeline_mode=once),
            pl.BlockSpec((1, D_MODEL), lambda i: (0, 0)),
        ],
        out_specs=[pl.BlockSpec((OUT_TM, D_MODEL), lambda i: (i, 0)),
                   pl.BlockSpec((OUT_TM, D_MODEL), lambda i: (i, 0))],
        out_shape=[jax.ShapeDtypeStruct((TOK, D_MODEL), F32),
                   jax.ShapeDtypeStruct((TOK, D_MODEL), BF16)],
        compiler_params=_cparams(("parallel",)),
        name="merge_out",
    )(a, z, z, z, z, x, ln_g, ln_b, w_s, bs_exp, wn, ws, wo, next_gain)


def _check_overlap_stencil():
    ci = np.arange(N_CMP - 1)[None, :] * CMP_STRIDE
    sj = np.arange(N_SBLK)[:, None] * SEL_BLOCK
    ov = np.clip(np.minimum(ci + CMP_BLOCK, sj + SEL_BLOCK) - np.maximum(ci, sj), 0, None) / CMP_BLOCK
    stencil = np.zeros((N_SBLK, N_CMP - 1))
    for j in range(N_SBLK):
        for c, w in ((4 * j - 1, 0.5), (4 * j, 1.0), (4 * j + 1, 1.0), (4 * j + 2, 1.0), (4 * j + 3, 0.5)):
            if 0 <= c < N_CMP - 1:
                stencil[j, c] = w
    assert np.array_equal(ov, stencil)


def _block_diag_ones():
    r = np.arange(LANE)
    return (r[:, None] // HEAD_DIM == r[None, :] // HEAD_DIM).astype(np.float32)


def _prep_w_in(w):
    L = w.shape[0]
    w = w.astype(BF16)
    kv = w[:, :, OFF_KV:OFF_NG].reshape(L, D_MODEL, 3, 2, N_KV, HEAD_DIM)
    kv = kv.transpose(0, 1, 2, 4, 3, 5).reshape(L, D_MODEL, 6 * KV_W)
    ng = w[:, :, OFF_NG:OFF_UV].reshape(L, D_MODEL, N_KV, HPG, 3).transpose(0, 1, 2, 4, 3)
    ng = ng.reshape(L, D_MODEL, N_KV, 3 * HPG)
    ng = jnp.pad(ng, ((0, 0), (0, 0), (0, 0), (0, GATE_GW - 3 * HPG))).reshape(L, D_MODEL, N_KV * GATE_GW)
    ng = jnp.pad(ng, ((0, 0), (0, 0), (0, SEC_W - N_KV * GATE_GW)))
    return jnp.concatenate([w[:, :, OFF_MG:], w[:, :, OFF_UV:OFF_MG], w[:, :, :OFF_KV], kv, ng], axis=2)


def _mixer(x, z, l, p, tables):
    t0, t1, tc = tables
    pos = jnp.stack([p["cmp_pos_k"][l], p["cmp_pos_v"][l]]).reshape(2, 1, CMP_BLOCK * HEAD_DIM)
    w1 = jnp.stack([p["cmp_k_w1"][l], p["cmp_v_w1"][l]]).astype(BF16)
    kc, vcT = _compress(z, pos, w1, p["cmp_k_w2"][l].astype(BF16), p["cmp_v_w2"][l].T.astype(BF16),
                        p["k_norm"][l, 0].reshape(1, HEAD_DIM))
    a = _attention(z, kc, vcT, t0, t1, tc)
    bs_exp = jnp.repeat(p["sgu_b"][l].T, GM_GW, axis=1)
    return _merge_out(a, z, x, p["sgu_norm_g"][l].reshape(1, GM_W), p["sgu_norm_b"][l].reshape(1, GM_W),
                      p["sgu_w"][l], bs_exp, p["wn"], p["ws"], p["wo"], l,
                      p["ffn2_norm"][l].reshape(1, D_MODEL))


def kernel(x, rel_bias, ffn1_norm, ffn1_w_gate, ffn1_w_up, ffn1_w_down, mix_norm, w_in, q_norm, k_norm, cmp_pos_k, cmp_pos_v, cmp_k_w1, cmp_k_w2, cmp_v_w1, cmp_v_w2, sgu_norm_g, sgu_norm_b, sgu_w, sgu_b, w_proj_nsa, w_proj_sgu, w_out, ffn2_norm, ffn2_w_gate, ffn2_w_up, ffn2_w_down):
    p = dict(cmp_pos_k=cmp_pos_k, cmp_pos_v=cmp_pos_v, cmp_k_w1=cmp_k_w1, cmp_k_w2=cmp_k_w2,
             cmp_v_w1=cmp_v_w1, cmp_v_w2=cmp_v_w2, k_norm=k_norm, sgu_norm_g=sgu_norm_g,
             sgu_norm_b=sgu_norm_b, sgu_w=sgu_w, sgu_b=sgu_b, wn=w_proj_nsa.astype(BF16),
             ws=w_proj_sgu.astype(BF16), wo=w_out.astype(BF16), ffn2_norm=ffn2_norm)
    near, cmpb = _bucket_tables()
    t0, t1, tc = _bias_tables(rel_bias, jnp.asarray(near), jnp.asarray(cmpb))
    tables = (t0, t1, tc)
    _check_overlap_stencil()
    bd = jnp.asarray(_block_diag_ones()).astype(BF16)
    w_in_r = _prep_w_in(w_in)
    f1 = (ffn1_w_gate, ffn1_w_up, ffn1_w_down)
    f2 = (ffn2_w_gate, ffn2_w_up, ffn2_w_down)

    y = x.reshape(TOK, D_MODEL)
    for l in range(DEPTH):
        y = _ffn(y, _rmsnorm(y, ffn1_norm[l].reshape(1, D_MODEL)), *f1, l)
        z = _proj_in(y, mix_norm[l].reshape(1, D_MODEL), w_in_r, l, bd,
                     jnp.tile(q_norm[l], PIN_TN // HEAD_DIM).reshape(1, PIN_TN),
                     jnp.tile(k_norm[l, 1], SEC_W // HEAD_DIM).reshape(1, SEC_W),
                     jnp.tile(k_norm[l, 2], SEC_W // HEAD_DIM).reshape(1, SEC_W))
        y, h = _mixer(y, z, l, p, tables)
        y = _ffn(y, h, *f2, l)
    return y.reshape(BATCH, SEQ, D_MODEL)
```

```python
import math

import numpy as np
import jax
import jax.numpy as jnp
from jax import lax
from jax.experimental import pallas as pl
from jax.experimental.pallas import tpu as pltpu

F32 = jnp.float32
BF16 = jnp.bfloat16

D_MODEL = 2048
BATCH = 4
SEQ = 2048
DEPTH = 2
TOK = BATCH * SEQ
HEAD_DIM = 64
N_HEADS = 16
N_KV = 4
HPG = 4
NSA_W = 1024
KV_W = 256
CMP_BLOCK = 32
CMP_STRIDE = 16
CMP_HIDDEN = 256
SEL_BLOCK = 64
N_SEL = 16
WINDOW = 512
GM_W = 1024
GM_CHUNK = 128
GM_GW = 128
GM_GROUPS = 8
N_BUCKETS = 32
MAX_DISTANCE = 128
D_FF = 5504
EPS = 1e-6
NEG = -1e30
FORCE = 1e4
OFF_KV = NSA_W
OFF_NG = OFF_KV + 6 * KV_W
OFF_UV = OFF_NG + 3 * N_HEADS
OFF_MG = OFF_UV + 2 * GM_W
LOG2E = 1.4426950408889634

LANE = 128
SUBLANE = 8
VMEM_LIMIT = 56 * 1024 * 1024

FFN_TF = 512
FFN_UP_TM = 1024
NORM_TM = 1024
FFN_DN_TM = 512
FFN_DN_TN = 512
PIN_TM = 1024
PIN_TN = 1024
PIN_RC = 256
SEC_W = PIN_TN // 2
Z_MG = 0
Z_U = Z_MG + 2 * D_MODEL
Z_V = Z_U + GM_W
Z_Q = Z_V + GM_W
Z_KV = Z_Q + NSA_W
Z_NG = Z_KV + 6 * KV_W
Z_W = Z_NG + SEC_W
KV_GW = 2 * HEAD_DIM
GATE_GW = 16
TQ = 128
NQ = SEQ // TQ
NK = SEQ // TQ
QL = HPG * TQ
N_CMP = 128
CMP_BAND = 16
MASKED = -1
N_SBLK = SEQ // SEL_BLOCK
V_ROWS = HEAD_DIM + 16
K_AUG = HEAD_DIM + 16
FAR_GROUP = 4
Q_PAIR = 4
OUT_TM = 256


def _cparams(sem):
    return pltpu.CompilerParams(dimension_semantics=sem, vmem_limit_bytes=VMEM_LIMIT)


def _cast_rows(src_ref, dst_ref, chunk):
    def body(r, carry):
        rows = pl.ds(pl.multiple_of(r * chunk, chunk), chunk)
        dst_ref[rows, :] = src_ref[rows, :].astype(BF16)
        return carry

    lax.fori_loop(0, src_ref.shape[0] // chunk, body, 0)


def _rms_bf16(x, gain):
    ms = jnp.mean(x * x, axis=-1, keepdims=True)
    return (x * lax.rsqrt(ms + EPS) * gain).astype(BF16)


def _norm_body(x_ref, g_ref, h_ref):
    h_ref[...] = _rms_bf16(x_ref[...], g_ref[...])


def _rmsnorm(x, gain):
    return pl.pallas_call(
        _norm_body,
        grid=(TOK // NORM_TM,),
        in_specs=[pl.BlockSpec((NORM_TM, D_MODEL), lambda i: (i, 0)),
                  pl.BlockSpec((1, D_MODEL), lambda i: (0, 0))],
        out_specs=pl.BlockSpec((NORM_TM, D_MODEL), lambda i: (i, 0)),
        out_shape=jax.ShapeDtypeStruct((TOK, D_MODEL), BF16),
        compiler_params=_cparams(("parallel",)),
        name="rmsnorm",
    )(x, gain)


def _ffn_up_body(h_ref, wg_ref, wu_ref, a_ref, wgb_ref, wub_ref):
    @pl.when(pl.program_id(1) == 0)
    def _():
        _cast_rows(wg_ref, wgb_ref, LANE)
        _cast_rows(wu_ref, wub_ref, LANE)

    h = h_ref[...]
    g = jnp.dot(h, wgb_ref[...], preferred_element_type=F32)
    u = jnp.dot(h, wub_ref[...], preferred_element_type=F32)
    a_ref[...] = (g * jax.nn.sigmoid(g) * u).astype(BF16)


def _ffn_down_body(a_ref, wd_ref, x_ref, o_ref, wdb_ref):
    @pl.when(pl.program_id(1) == 0)
    def _():
        _cast_rows(wd_ref, wdb_ref, LANE)

    o_ref[...] = x_ref[...] + 0.5 * jnp.dot(a_ref[...], wdb_ref[...], preferred_element_type=F32)


def _ffn(x, h, wg, wu, wd, l):
    nf = pl.cdiv(D_FF, FFN_TF)
    a = pl.pallas_call(
        _ffn_up_body,
        grid=(nf, TOK // FFN_UP_TM),
        in_specs=[
            pl.BlockSpec((FFN_UP_TM, D_MODEL), lambda f, m: (m, 0)),
            pl.BlockSpec((None, D_MODEL, FFN_TF), lambda f, m: (l, 0, f)),
            pl.BlockSpec((None, D_MODEL, FFN_TF), lambda f, m: (l, 0, f)),
        ],
        out_specs=pl.BlockSpec((FFN_UP_TM, FFN_TF), lambda f, m: (m, f)),
        out_shape=jax.ShapeDtypeStruct((TOK, D_FF), BF16),
        scratch_shapes=[pltpu.VMEM((D_MODEL, FFN_TF), BF16), pltpu.VMEM((D_MODEL, FFN_TF), BF16)],
        compiler_params=_cparams(("arbitrary", "arbitrary")),
        name="ffn_up",
    )(h, wg, wu)
    return pl.pallas_call(
        _ffn_down_body,
        grid=(D_MODEL // FFN_DN_TN, TOK // FFN_DN_TM),
        in_specs=[
            pl.BlockSpec((FFN_DN_TM, D_FF), lambda n, m: (m, 0)),
            pl.BlockSpec((None, D_FF, FFN_DN_TN), lambda n, m: (l, 0, n)),
            pl.BlockSpec((FFN_DN_TM, FFN_DN_TN), lambda n, m: (m, n)),
        ],
        out_specs=pl.BlockSpec((FFN_DN_TM, FFN_DN_TN), lambda n, m: (m, n)),
        out_shape=jax.ShapeDtypeStruct((TOK, D_MODEL), F32),
        scratch_shapes=[pltpu.VMEM((D_FF, FFN_DN_TN), BF16)],
        compiler_params=_cparams(("arbitrary", "arbitrary")),
        name="ffn_down",
    )(a, wd, x)


def _head_rms(y, bd):
    y2 = (y * y).astype(BF16)
    ss = jnp.concatenate(
        [jnp.dot(y2[:, k * LANE:(k + 1) * LANE], bd, preferred_element_type=F32)
         for k in range(y.shape[1] // LANE)], axis=1)
    return y * lax.rsqrt(ss * (1.0 / HEAD_DIM) + EPS)


def _pin_body(x_ref, g_ref, w_ref, bd_ref, qg_ref, k1g_ref, k2g_ref, o_ref, h_ref):
    j = pl.program_id(1)

    @pl.when(j == 0)
    def _():
        x = x_ref[...]
        ms = jnp.mean(x * x, axis=-1, keepdims=True)
        h_ref[...] = (x * lax.rsqrt(ms + EPS) * g_ref[...]).astype(BF16)

    t_u = Z_U // PIN_TN
    t_q = Z_Q // PIN_TN
    t_kv = Z_KV // PIN_TN
    half = SEC_W

    def emit(act):
        for r in range(PIN_TM // PIN_RC):
            rows = slice(r * PIN_RC, (r + 1) * PIN_RC)
            o_ref[rows, :] = act(jnp.dot(h_ref[rows, :], w_ref[...], preferred_element_type=F32))

    def halves(act_lo, act_hi):
        return lambda y: jnp.concatenate([act_lo(y[:, :half]), act_hi(y[:, half:])], axis=1)

    def kv_act(kg_ref):
        is_k = (lax.broadcasted_iota(jnp.int32, (1, half), 1) & (KV_GW - 1)) < HEAD_DIM
        return lambda y: jnp.where(is_k, _head_rms(y, bd_ref[...]) * kg_ref[...], y)

    @pl.when(j < t_u)
    def _():
        emit(jax.nn.sigmoid)

    @pl.when((j >= t_u) & (j < t_q))
    def _():
        emit(jax.nn.gelu)

    @pl.when((j >= t_q) & (j < t_kv))
    def _():
        emit(lambda y: _head_rms(y, bd_ref[...]) * (qg_ref[...] * (HEAD_DIM ** -0.5 * LOG2E)))

    @pl.when(j == t_kv)
    def _():
        emit(halves(lambda y: y, kv_act(k1g_ref)))

    @pl.when(j == t_kv + 1)
    def _():
        emit(halves(kv_act(k2g_ref), jax.nn.sigmoid))


def _proj_in(x, gain, w, l, bd, qg, k1g, k2g):
    return pl.pallas_call(
        _pin_body,
        grid=(TOK // PIN_TM, Z_W // PIN_TN),
        in_specs=[
            pl.BlockSpec((PIN_TM, D_MODEL), lambda i, j: (i, 0)),
            pl.BlockSpec((1, D_MODEL), lambda i, j: (0, 0)),
            pl.BlockSpec((None, D_MODEL, PIN_TN), lambda i, j: (l, 0, j)),
            pl.BlockSpec((LANE, LANE), lambda i, j: (0, 0)),
            pl.BlockSpec((1, PIN_TN), lambda i, j: (0, 0)),
            pl.BlockSpec((1, SEC_W), lambda i, j: (0, 0)),
            pl.BlockSpec((1, SEC_W), lambda i, j: (0, 0)),
        ],
        out_specs=pl.BlockSpec((PIN_TM, PIN_TN), lambda i, j: (i, j)),
        out_shape=jax.ShapeDtypeStruct((TOK, Z_W), F32),
        scratch_shapes=[pltpu.VMEM((PIN_TM, D_MODEL), BF16)],
        compiler_params=_cparams(("parallel", "arbitrary")),
        name="proj_in",
    )(x, gain, w, bd, qg, k1g, k2g)


def _cmp_body(z_ref, pos_ref, w1_ref, w2k_ref, w2vT_ref, kg_ref, kc_ref, vcT_ref):
    half = CMP_STRIDE * HEAD_DIM
    lane = lax.broadcasted_iota(jnp.int32, (N_CMP, LANE), 1)
    first = lane < HEAD_DIM
    xk, xv = [], []
    for j in range(CMP_STRIDE // 2):
        ev = z_ref[pl.ds(2 * j, N_CMP, stride=CMP_STRIDE), :]
        od = z_ref[pl.ds(2 * j + 1, N_CMP, stride=CMP_STRIDE), :]
        xk.append(jnp.where(first, ev, pltpu.roll(od, HEAD_DIM, 1)))
        xv.append(jnp.where(first, pltpu.roll(ev, HEAD_DIM, 1), od))

    def hidden(x, kind):
        xa = (x + pos_ref[kind, :, :half]).astype(BF16)
        xb = (x + pos_ref[kind, :, half:]).astype(BF16)
        a = jnp.dot(xa, w1_ref[kind, :half, :], preferred_element_type=F32)
        b = jnp.dot(xb, w1_ref[kind, half:, :], preferred_element_type=F32)
        hid = a + pltpu.roll(b, N_CMP - 1, 0)
        return (hid * jax.nn.sigmoid(hid)).astype(BF16)

    yk = jnp.dot(hidden(jnp.concatenate(xk, axis=1), 0), w2k_ref[...], preferred_element_type=F32)
    ms = jnp.mean(yk * yk, axis=-1, keepdims=True)
    kc_ref[0, 0] = (yk * lax.rsqrt(ms + EPS) * kg_ref[...]).astype(BF16)
    hv = hidden(jnp.concatenate(xv, axis=1), 1)
    vcT_ref[0, 0] = lax.dot_general(w2vT_ref[...], hv, (((1,), (1,)), ((), ())),
                                    preferred_element_type=F32).astype(BF16)


def _compress(z, pos, w1, w2k, w2vT, kg):
    return pl.pallas_call(
        _cmp_body,
        grid=(BATCH, N_KV),
        in_specs=[
            pl.BlockSpec((SEQ, KV_GW), lambda b, g: (b, Z_KV // KV_GW + g)),
            pl.BlockSpec((2, 1, CMP_BLOCK * HEAD_DIM), lambda b, g: (0, 0, 0)),
            pl.BlockSpec((2, CMP_BLOCK * HEAD_DIM, CMP_HIDDEN), lambda b, g: (0, 0, 0)),
            pl.BlockSpec((CMP_HIDDEN, HEAD_DIM), lambda b, g: (0, 0)),
            pl.BlockSpec((HEAD_DIM, CMP_HIDDEN), lambda b, g: (0, 0)),
            pl.BlockSpec((1, HEAD_DIM), lambda b, g: (0, 0)),
        ],
        out_specs=[
            pl.BlockSpec((1, 1, N_CMP, HEAD_DIM), lambda b, g: (b, g, 0, 0)),
            pl.BlockSpec((1, 1, HEAD_DIM, N_CMP), lambda b, g: (b, g, 0, 0)),
        ],
        out_shape=[
            jax.ShapeDtypeStruct((BATCH, N_KV, N_CMP, HEAD_DIM), BF16),
            jax.ShapeDtypeStruct((BATCH, N_KV, HEAD_DIM, N_CMP), BF16),
        ],
        compiler_params=_cparams(("parallel", "parallel")),
        name="compress",
    )(z, pos, w1, w2k, w2vT, kg)


def _np_bucket(dist):
    n = np.maximum(dist, 0)
    max_exact = N_BUCKETS // 2
    nf = np.maximum(n, 1).astype(np.float32)
    large = max_exact + (np.log(nf / np.float32(max_exact)) / np.float32(math.log(MAX_DISTANCE / max_exact))
                         * np.float32(N_BUCKETS - max_exact)).astype(np.int32)
    large = np.minimum(large, N_BUCKETS - 1)
    return np.where(n < max_exact, n, large).astype(np.int32)


def _bucket_tables():
    kj = np.arange(TQ)[:, None]
    qi = np.arange(TQ)[None, :]
    near = np.stack([np.where(qi >= kj, _np_bucket(qi - kj), MASKED), _np_bucket(TQ + qi - kj)])
    start = np.array([_cmp_band_start(i) for i in range(NQ)])[:, None, None]
    c = start + np.arange(CMP_BAND)[None, :, None]
    t = (np.arange(NQ)[:, None, None] * TQ + qi[None])
    dist_band = t - (c * CMP_STRIDE + CMP_BLOCK - 1)
    cmp_b = np.where(dist_band >= 0, _np_bucket(dist_band), MASKED)
    dist = t - (np.arange(N_CMP)[None, :, None] * CMP_STRIDE + CMP_BLOCK - 1)
    for i in range(NQ):
        s0 = _cmp_band_start(i)
        assert np.all(dist[i, :s0] >= 0) and np.all(_np_bucket(dist[i, :s0]) == N_BUCKETS - 1)
        assert np.all(dist[i, s0 + CMP_BAND:] < 0)
    return near.astype(np.int32), cmp_b.astype(np.int32)


def _cmp_band_start(i):
    return max((TQ // CMP_STRIDE) * (i - 1), 0)


def _bias_body(rb_ref, near_ref, cmpb_ref, t0_ref, t1_ref, tc_ref):
    g = pl.program_id(0)

    def lut(bk, h):
        val = jnp.full(bk.shape, NEG, F32)
        for b in range(N_BUCKETS):
            val = jnp.where(bk == b, rb_ref[b, h], val)
        return val

    for hh in range(HPG):
        h = g * HPG + hh
        sl = slice(hh * TQ, (hh + 1) * TQ)
        far = rb_ref[N_BUCKETS - 1, h]
        t0_ref[0, :, sl] = (lut(near_ref[0], h) - far) * LOG2E
        t1_ref[0, :, sl] = (lut(near_ref[1], h) - far) * LOG2E

        def body(i, carry):
            start = pl.multiple_of(jnp.maximum((TQ // CMP_STRIDE) * (i - 1), 0), SUBLANE)
            row = lax.broadcasted_iota(jnp.int32, (N_CMP, TQ), 0)
            tc_ref[0, i, :, sl] = jnp.where(row < start, far * LOG2E, NEG)
            tc_ref[0, i, pl.ds(start, CMP_BAND), sl] = lut(cmpb_ref[i], h) * LOG2E
            return carry

        lax.fori_loop(0, NQ, body, 0, unroll=4)


def _bias_tables(rel_bias, near, cmpb):
    return pl.pallas_call(
        _bias_body,
        grid=(N_KV,),
        in_specs=[
            pl.BlockSpec(memory_space=pltpu.SMEM),
            pl.BlockSpec((2, TQ, TQ), lambda g: (0, 0, 0)),
            pl.BlockSpec((NQ, CMP_BAND, TQ), lambda g: (0, 0, 0)),
        ],
        out_specs=[
            pl.BlockSpec((1, TQ, QL), lambda g: (g, 0, 0)),
            pl.BlockSpec((1, TQ, QL), lambda g: (g, 0, 0)),
            pl.BlockSpec((1, NQ, N_CMP, QL), lambda g: (g, 0, 0, 0)),
        ],
        out_shape=[
            jax.ShapeDtypeStruct((N_KV, TQ, QL), F32),
            jax.ShapeDtypeStruct((N_KV, TQ, QL), F32),
            jax.ShapeDtypeStruct((N_KV, NQ, N_CMP, QL), F32),
        ],
        compiler_params=_cparams(("arbitrary",)),
        name="bias_tables",
    )(rel_bias, near, cmpb)


def _attn_body(zq_ref, zs_ref, zw_ref, zg_ref, kc_ref, vcT_ref, t0_ref, t1_ref, tc_ref, o_ref,
               ks_ref, vsT_ref, kw_ref, vwT_ref, sel_ref, sfar_ref, gsc_ref, psum_ref):
    g = pl.program_id(1)
    ip = pl.program_id(2)

    @pl.when(ip == 0)
    def _():
        ones = jnp.ones((V_ROWS - HEAD_DIM, TQ), BF16)

        def fill(n, carry):
            for z_ref, k_ref, vT_ref in ((zs_ref, ks_ref, vsT_ref), (zw_ref, kw_ref, vwT_ref)):
                blk = z_ref[pl.ds(pl.multiple_of(n * TQ, TQ), TQ), :]
                k_ref[n, :, :HEAD_DIM] = blk[:, :HEAD_DIM].astype(BF16)
                vT_ref[n, :HEAD_DIM, :] = blk.T[HEAD_DIM:, :].astype(BF16)
                vT_ref[n, HEAD_DIM:, :] = ones
            kw_ref[n, :, HEAD_DIM:] = jnp.zeros((TQ, K_AUG - HEAD_DIM), BF16)
            return carry

        lax.fori_loop(0, NK, fill, 0, unroll=4)
        lane = lax.broadcasted_iota(jnp.int32, (TQ, K_AUG), 1)
        kw_ref[NK] = jnp.where(lane == HEAD_DIM, NEG, 0.0).astype(BF16)
        vwT_ref[NK] = jnp.zeros((V_ROWS, TQ), BF16)

    blk = lax.broadcasted_iota(jnp.int32, (N_SBLK, TQ), 0)
    ql = lax.broadcasted_iota(jnp.int32, (N_SBLK, TQ), 1)
    key1 = lax.broadcasted_iota(jnp.int32, (TQ, TQ), 0)
    q1 = lax.broadcasted_iota(jnp.int32, (TQ, TQ), 1)
    nlast = WINDOW // TQ

    def sel_mask(t, n):
        r0 = sel_ref[t, pl.ds(2 * n, 1), :]
        r1 = sel_ref[t, pl.ds(2 * n + 1, 1), :]
        return jnp.where(key1 < SEL_BLOCK, r0, r1) > 0.5

    def scores(qT, k_ref, n, bias, mask):
        sc = jnp.dot(k_ref[n], qT, preferred_element_type=F32)
        if bias is not None:
            sc = sc + bias
        if mask is None:
            return sc
        return jnp.concatenate(
            [jnp.where(mask, sc[:, h * TQ:(h + 1) * TQ], NEG) for h in range(HPG)], axis=1)

    def colmax8(sc):
        return jnp.max(sc.reshape(TQ // SUBLANE, SUBLANE, QL), axis=0)

    def probs(sc, mrow):
        return jnp.exp2(sc - mrow).astype(BF16)

    def pv(vT_ref, ns, ps):
        vv = vT_ref[ns[0]] if len(ns) == 1 else jnp.concatenate([vT_ref[n] for n in ns], axis=1)
        pp = ps[0] if len(ps) == 1 else jnp.concatenate(ps, axis=0)
        return jnp.dot(vv, pp, preferred_element_type=F32)

    def finish(acc):
        return acc[:HEAD_DIM] * (1.0 / acc[HEAD_DIM:HEAD_DIM + 1])

    def stage_q(t):
        i = Q_PAIR * ip + t
        rows = slice(t * TQ, (t + 1) * TQ)
        qt = zq_ref[rows, :].T
        qT = jnp.concatenate([qt[h * HEAD_DIM:(h + 1) * HEAD_DIM] for h in range(HPG)], axis=1).astype(BF16)
        gsc_ref[t] = zg_ref[rows, :].T
        one_row = (lax.broadcasted_iota(jnp.int32, (K_AUG - HEAD_DIM, QL), 0) == 0).astype(BF16)
        return dict(i=i, qT=qT, qTa=jnp.concatenate([qT, one_row], axis=0),
                    n1=jnp.maximum(i - 1, 0), i1=jnp.full((TQ, TQ), i, jnp.int32),
                    nfar=jnp.maximum(i - 1, 0))

    def stage_cmp_scores(t, tl):
        tl["s_c"] = jnp.dot(kc_ref[0, 0], tl["qT"], preferred_element_type=F32) + tc_ref[0, t]
        tl["m_c"] = jnp.maximum(jnp.max(colmax8(tl["s_c"]), axis=0, keepdims=True), 0.1 * NEG)

    def stage_cmp(t, tl):
        e = jnp.exp2(tl["s_c"] - tl["m_c"])
        l = jnp.sum(e, axis=0, keepdims=True)
        p = e * jnp.where(l > 0.0, 1.0 / l, 0.0)
        tl["ocT"] = jnp.dot(vcT_ref[0, 0], p.astype(BF16), preferred_element_type=F32)
        psum_ref[t] = p[:, 0:TQ] + p[:, TQ:2 * TQ] + p[:, 2 * TQ:3 * TQ] + p[:, 3 * TQ:4 * TQ]

    def stage_select(t, tl):
        i = tl["i"]
        ratio = SEL_BLOCK // CMP_STRIDE
        part = [psum_ref[t, pl.ds(r, N_SBLK, stride=ratio), :] for r in range(ratio)]
        prev_last = jnp.where(blk == 0, 0.0, pltpu.roll(part[3], 1, 0))
        imp = (part[0] + part[1] + part[2]) + 0.5 * (part[3] + prev_last)
        cur = jnp.full((N_SBLK, TQ), i, jnp.int32) * (TQ // SEL_BLOCK) + ql // SEL_BLOCK
        forced = (blk == 0) | (blk == cur) | (blk == cur - 1)
        score = jnp.where(blk <= cur, jnp.where(forced, FORCE, imp), NEG)
        sub = SUBLANE
        pieces = [score[k * sub:(k + 1) * sub, :] for k in range(N_SBLK // sub)]
        ranks = [jnp.zeros((sub, TQ), F32) for _ in pieces]
        for jj in range(N_SBLK):
            row = score[jj:jj + 1, :]
            for k, pc in enumerate(pieces):
                if k * sub > jj:
                    beats = row >= pc
                elif (k + 1) * sub - 1 < jj:
                    beats = row > pc
                else:
                    beats = (row > pc) | ((row == pc) & (blk[k * sub:(k + 1) * sub, :] > jj))
                ranks[k] = ranks[k] + jnp.where(beats, 1.0, 0.0)
        rank = jnp.concatenate(ranks, axis=0)
        sel_ref[t] = jnp.where((rank < N_SEL) & (blk <= cur), 1.0, 0.0)

    def stage_window_scores(t, tl):
        i, qT = tl["i"], tl["qTa"]
        n1, n2, n3, n4 = [jnp.where(i >= c, i - c, NK) for c in range(1, nlast + 1)]
        w_d = scores(qT, kw_ref, i, t0_ref[0], None)
        w_1 = scores(qT, kw_ref, n1, t1_ref[0], None)
        w_2 = scores(qT, kw_ref, n2, None, None)
        w_3 = scores(qT, kw_ref, n3, None, None)
        w_4 = scores(qT, kw_ref, n4, None, q1 < key1)
        mxw = jnp.maximum(jnp.maximum(jnp.maximum(colmax8(w_d), colmax8(w_1)),
                                      jnp.maximum(colmax8(w_2), colmax8(w_3))), colmax8(w_4))
        tl["w"] = (w_d, w_1, w_2, w_3, w_4)
        tl["wn"] = (i, n1, n2, n3, n4)
        tl["m_w"] = jnp.max(mxw, axis=0, keepdims=True)

    def stage_window_pv(t, tl):
        (w_d, w_1, w_2, w_3, w_4), (i, n1, n2, n3, n4), m_w = tl["w"], tl["wn"], tl["m_w"]
        tl["owT"] = finish(pv(vwT_ref, [i], [probs(w_d, m_w)])
                           + pv(vwT_ref, [n1, n2], [probs(w_1, m_w), probs(w_2, m_w)])
                           + pv(vwT_ref, [n3, n4], [probs(w_3, m_w), probs(w_4, m_w)]))

    def stage_near(t, tl):
        i, n1, i1, qT = tl["i"], tl["n1"], tl["i1"], tl["qT"]
        tl["s_d"] = scores(qT, ks_ref, i, t0_ref[0], None)
        tl["s_1"] = scores(qT, ks_ref, n1, t1_ref[0], sel_mask(t, n1) & (i1 >= 1))
        tl["mx"] = jnp.maximum(colmax8(tl["s_d"]), colmax8(tl["s_1"]))

    tiles = [stage_q(t) for t in range(Q_PAIR)]
    for stage in (stage_cmp_scores, stage_cmp, stage_window_scores, stage_select, stage_near, stage_window_pv):
        for t, tl in enumerate(tiles):
            stage(t, tl)
    assert Q_PAIR == FAR_GROUP
    every = {t: list(range(FAR_GROUP)) for t in range(Q_PAIR)}
    leftover = {t: list(range(t - 1)) for t in range(2, Q_PAIR)}

    def far_scores(tr, mxs, offs):
        out = list(mxs)
        for t, cs in offs.items():
            tl = tiles[t]
            for c in cs:
                n = FAR_GROUP * tr + c
                sc = scores(tl["qT"], ks_ref, n, None,
                            sel_mask(t, n) & (jnp.full((TQ, TQ), n, jnp.int32) < tl["nfar"]))
                sfar_ref[t, n] = sc
                out[t] = jnp.maximum(out[t], colmax8(sc))
        return tuple(out)

    mxs = lax.fori_loop(0, ip, lambda tr, c: far_scores(tr, c, every), tuple(tl["mx"] for tl in tiles))
    mxs = far_scores(ip, mxs, leftover)
    m_s = [jnp.max(mx, axis=0, keepdims=True) for mx in mxs]
    accs = tuple(pv(vsT_ref, [tl["i"], tl["n1"]], [probs(tl["s_d"], m_s[t]), probs(tl["s_1"], m_s[t])])
                 for t, tl in enumerate(tiles))

    def far_pv(tr, accs_, offs):
        out = list(accs_)
        for t, cs in offs.items():
            ns = [FAR_GROUP * tr + c for c in cs]
            out[t] = out[t] + pv(vsT_ref, ns, [probs(sfar_ref[t, n], m_s[t]) for n in ns])
        return tuple(out)

    accs = lax.fori_loop(0, ip, lambda tr, c: far_pv(tr, c, every), accs)
    accs = far_pv(ip, accs, leftover)

    for t, tl in enumerate(tiles):
        base = g * GATE_GW
        gate = [jnp.concatenate([gsc_ref[t, pl.ds(base + br * HPG + h, 1), :] for h in range(HPG)], axis=1)
                for br in range(3)]
        aT = gate[0] * tl["ocT"] + gate[1] * finish(accs[t]) + gate[2] * tl["owT"]
        a4 = jnp.concatenate([aT[:, h * TQ:(h + 1) * TQ] for h in range(HPG)], axis=0)
        o_ref[t * TQ:(t + 1) * TQ, :] = a4.T.astype(o_ref.dtype)


def _attention(z, kc, vcT, t0, t1, tc):
    qw = HPG * HEAD_DIM
    rows = Q_PAIR * TQ
    npair = NQ // Q_PAIR
    return pl.pallas_call(
        _attn_body,
        grid=(BATCH, N_KV, npair),
        in_specs=[
            pl.BlockSpec((rows, qw), lambda b, g, i: (b * npair + i, Z_Q // qw + g)),
            pl.BlockSpec((SEQ, KV_GW), lambda b, g, i: (b, (Z_KV + 2 * KV_W) // KV_GW + g)),
            pl.BlockSpec((SEQ, KV_GW), lambda b, g, i: (b, (Z_KV + 4 * KV_W) // KV_GW + g)),
            pl.BlockSpec((rows, LANE), lambda b, g, i: (b * npair + i, Z_NG // LANE)),
            pl.BlockSpec((1, 1, N_CMP, HEAD_DIM), lambda b, g, i: (b, g, 0, 0)),
            pl.BlockSpec((1, 1, HEAD_DIM, N_CMP), lambda b, g, i: (b, g, 0, 0)),
            pl.BlockSpec((1, TQ, QL), lambda b, g, i: (g, 0, 0)),
            pl.BlockSpec((1, TQ, QL), lambda b, g, i: (g, 0, 0)),
            pl.BlockSpec((1, Q_PAIR, N_CMP, QL), lambda b, g, i: (g, i, 0, 0)),
        ],
        out_specs=pl.BlockSpec((rows, qw), lambda b, g, i: (b * npair + i, g)),
        out_shape=jax.ShapeDtypeStruct((TOK, NSA_W), BF16),
        scratch_shapes=[
            pltpu.VMEM((NK, TQ, HEAD_DIM), BF16),
            pltpu.VMEM((NK, V_ROWS, TQ), BF16),
            pltpu.VMEM((NK + 1, TQ, K_AUG), BF16),
            pltpu.VMEM((NK + 1, V_ROWS, TQ), BF16),
            pltpu.VMEM((Q_PAIR, N_SBLK, TQ), F32),
            pltpu.VMEM((Q_PAIR, NK, TQ, QL), F32),
            pltpu.VMEM((Q_PAIR, LANE, TQ), F32),
            pltpu.VMEM((Q_PAIR, N_CMP, TQ), F32),
        ],
        compiler_params=_cparams(("parallel", "parallel", "arbitrary")),
        name="nsa_attention",
    )(z, z, z, z, kc, vcT, t0, t1, tc)


def _sgu_tile(u_ref, v_ref, g_ref, b_ref, w_ref, bs_ref):
    r = lax.broadcasted_iota(jnp.int32, (GM_CHUNK, GM_CHUNK), 0)
    c = lax.broadcasted_iota(jnp.int32, (GM_CHUNK, GM_CHUNK), 1)
    tril = r >= c
    ws = [jnp.where(tril, w_ref[gi], 0.0).astype(BF16) for gi in range(GM_GROUPS)]
    chunks = []
    for ch in range(u_ref.shape[0] // GM_CHUNK):
        rows = slice(ch * GM_CHUNK, (ch + 1) * GM_CHUNK)
        v = v_ref[rows, :]
        mu = jnp.mean(v, axis=-1, keepdims=True)
        vc = v - mu
        var = jnp.mean(vc * vc, axis=-1, keepdims=True)
        vn = (vc * lax.rsqrt(var + EPS) * g_ref[...] + b_ref[...]).astype(BF16)
        groups = []
        for gi in range(GM_GROUPS):
            cols = slice(gi * GM_GW, (gi + 1) * GM_GW)
            mixed = jnp.dot(ws[gi], vn[:, cols], preferred_element_type=F32) + bs_ref[:, cols]
            groups.append((u_ref[rows, cols] * mixed).astype(BF16))
        chunks.append(jnp.concatenate(groups, axis=1))
    return jnp.concatenate(chunks, axis=0)


def _out_body(a_ref, u_ref, v_ref, m1_ref, m2_ref, x_ref, lg_ref, lb_ref, sw_ref, bs_ref,
              wn_ref, ws_ref, wo_ref, gn_ref, o_ref, h_ref):
    sg = _sgu_tile(u_ref, v_ref, lg_ref, lb_ref, sw_ref, bs_ref)
    pa = jnp.dot(a_ref[...], wn_ref[...], preferred_element_type=F32)
    ps = jnp.dot(sg, ws_ref[...], preferred_element_type=F32)
    merged = (m1_ref[...] * pa + m2_ref[...] * ps).astype(BF16)
    y = x_ref[...] + jnp.dot(merged, wo_ref[...], preferred_element_type=F32)
    o_ref[...] = y
    h_ref[...] = _rms_bf16(y, gn_ref[...])


def _merge_out(a, z, x, ln_g, ln_b, w_s, bs_exp, wn, ws, wo, l, next_gain):
    const = lambda i: (l, 0, 0)
    once = pl.Buffered(1)
    return pl.pallas_call(
        _out_body,
        grid=(TOK // OUT_TM,),
        in_specs=[
            pl.BlockSpec((OUT_TM, NSA_W), lambda i: (i, 0)),
            pl.BlockSpec((OUT_TM, GM_W), lambda i: (i, Z_U // GM_W)),
            pl.BlockSpec((OUT_TM, GM_W), lambda i: (i, Z_V // GM_W)),
            pl.BlockSpec((OUT_TM, D_MODEL), lambda i: (i, 0)),
            pl.BlockSpec((OUT_TM, D_MODEL), lambda i: (i, 1)),
            pl.BlockSpec((OUT_TM, D_MODEL), lambda i: (i, 0)),
            pl.BlockSpec((1, GM_W), lambda i: (0, 0)),
            pl.BlockSpec((1, GM_W), lambda i: (0, 0)),
            pl.BlockSpec((GM_GROUPS, GM_CHUNK, GM_CHUNK), lambda i: (0, 0, 0), pipeline_mode=once),
            pl.BlockSpec((GM_CHUNK, GM_W), lambda i: (0, 0), pipeline_mode=once),
            pl.BlockSpec((None, NSA_W, D_MODEL), const, pipeline_mode=once),
            pl.BlockSpec((None, GM_W, D_MODEL), const, pipeline_mode=once),
            pl.BlockSpec((None, D_MODEL, D_MODEL), const, pipeline_mode=once),
            pl.BlockSpec((1, D_MODEL), lambda i: (0, 0)),
        ],
        out_specs=[pl.BlockSpec((OUT_TM, D_MODEL), lambda i: (i, 0)),
                   pl.BlockSpec((OUT_TM, D_MODEL), lambda i: (i, 0))],
        out_shape=[jax.ShapeDtypeStruct((TOK, D_MODEL), F32),
                   jax.ShapeDtypeStruct((TOK, D_MODEL), BF16)],
        compiler_params=_cparams(("parallel",)),
        name="merge_out",
    )(a, z, z, z, z, x, ln_g, ln_b, w_s, bs_exp, wn, ws, wo, next_gain)


def _check_overlap_stencil():
    ci = np.arange(N_CMP - 1)[None, :] * CMP_STRIDE
    sj = np.arange(N_SBLK)[:, None] * SEL_BLOCK
    ov = np.clip(np.minimum(ci + CMP_BLOCK, sj + SEL_BLOCK) - np.maximum(ci, sj), 0, None) / CMP_BLOCK
    stencil = np.zeros((N_SBLK, N_CMP - 1))
    for j in range(N_SBLK):
        for c, w in ((4 * j - 1, 0.5), (4 * j, 1.0), (4 * j + 1, 1.0), (4 * j + 2, 1.0), (4 * j + 3, 0.5)):
            if 0 <= c < N_CMP - 1:
                stencil[j, c] = w
    assert np.array_equal(ov, stencil)


def _block_diag_ones():
    r = np.arange(LANE)
    return (r[:, None] // HEAD_DIM == r[None, :] // HEAD_DIM).astype(np.float32)


def _prep_w_in(w):
    L = w.shape[0]
    w = w.astype(BF16)
    kv = w[:, :, OFF_KV:OFF_NG].reshape(L, D_MODEL, 3, 2, N_KV, HEAD_DIM)
    kv = kv.transpose(0, 1, 2, 4, 3, 5).reshape(L, D_MODEL, 6 * KV_W)
    ng = w[:, :, OFF_NG:OFF_UV].reshape(L, D_MODEL, N_KV, HPG, 3).transpose(0, 1, 2, 4, 3)
    ng = ng.reshape(L, D_MODEL, N_KV, 3 * HPG)
    ng = jnp.pad(ng, ((0, 0), (0, 0), (0, 0), (0, GATE_GW - 3 * HPG))).reshape(L, D_MODEL, N_KV * GATE_GW)
    ng = jnp.pad(ng, ((0, 0), (0, 0), (0, SEC_W - N_KV * GATE_GW)))
    return jnp.concatenate([w[:, :, OFF_MG:], w[:, :, OFF_UV:OFF_MG], w[:, :, :OFF_KV], kv, ng], axis=2)


def _mixer(x, z, l, p, tables):
    t0, t1, tc = tables
    pos = jnp.stack([p["cmp_pos_k"][l], p["cmp_pos_v"][l]]).reshape(2, 1, CMP_BLOCK * HEAD_DIM)
    w1 = jnp.stack([p["cmp_k_w1"][l], p["cmp_v_w1"][l]]).astype(BF16)
    kc, vcT = _compress(z, pos, w1, p["cmp_k_w2"][l].astype(BF16), p["cmp_v_w2"][l].T.astype(BF16),
                        p["k_norm"][l, 0].reshape(1, HEAD_DIM))
    a = _attention(z, kc, vcT, t0, t1, tc)
    bs_exp = jnp.repeat(p["sgu_b"][l].T, GM_GW, axis=1)
    return _merge_out(a, z, x, p["sgu_norm_g"][l].reshape(1, GM_W), p["sgu_norm_b"][l].reshape(1, GM_W),
                      p["sgu_w"][l], bs_exp, p["wn"], p["ws"], p["wo"], l,
                      p["ffn2_norm"][l].reshape(1, D_MODEL))


def kernel(x, rel_bias, ffn1_norm, ffn1_w_gate, ffn1_w_up, ffn1_w_down, mix_norm, w_in, q_norm, k_norm, cmp_pos_k, cmp_pos_v, cmp_k_w1, cmp_k_w2, cmp_v_w1, cmp_v_w2, sgu_norm_g, sgu_norm_b, sgu_w, sgu_b, w_proj_nsa, w_proj_sgu, w_out, ffn2_norm, ffn2_w_gate, ffn2_w_up, ffn2_w_down):
    p = dict(cmp_pos_k=cmp_pos_k, cmp_pos_v=cmp_pos_v, cmp_k_w1=cmp_k_w1, cmp_k_w2=cmp_k_w2,
             cmp_v_w1=cmp_v_w1, cmp_v_w2=cmp_v_w2, k_norm=k_norm, sgu_norm_g=sgu_norm_g,
             sgu_norm_b=sgu_norm_b, sgu_w=sgu_w, sgu_b=sgu_b, wn=w_proj_nsa.astype(BF16),
             ws=w_proj_sgu.astype(BF16), wo=w_out.astype(BF16), ffn2_norm=ffn2_norm)
    near, cmpb = _bucket_tables()
    t0, t1, tc = _bias_tables(rel_bias, jnp.asarray(near), jnp.asarray(cmpb))
    tables = (t0, t1, tc)
    _check_overlap_stencil()
    bd = jnp.asarray(_block_diag_ones()).astype(BF16)
    w_in_r = _prep_w_in(w_in)
    f1 = (ffn1_w_gate, ffn1_w_up, ffn1_w_down)
    f2 = (ffn2_w_gate, ffn2_w_up, ffn2_w_down)

    y = x.reshape(TOK, D_MODEL)
    for l in range(DEPTH):
        y = _ffn(y, _rmsnorm(y, ffn1_norm[l].reshape(1, D_MODEL)), *f1, l)
        z = _proj_in(y, mix_norm[l].reshape(1, D_MODEL), w_in_r, l, bd,
                     jnp.tile(q_norm[l], PIN_TN // HEAD_DIM).reshape(1, PIN_TN),
                     jnp.tile(k_norm[l, 1], SEC_W // HEAD_DIM).reshape(1, SEC_W),
                     jnp.tile(k_norm[l, 2], SEC_W // HEAD_DIM).reshape(1, SEC_W))
        y, h = _mixer(y, z, l, p, tables)
        y = _ffn(y, h, *f2, l)
    return y.reshape(BATCH, SEQ, D_MODEL)
```

```python
import math

import numpy as np
import jax
import jax.numpy as jnp
from jax import lax
from jax.experimental import pallas as pl
from jax.experimental.pallas import tpu as pltpu

F32 = jnp.float32
BF16 = jnp.bfloat16

D_MODEL = 2048
BATCH = 4
SEQ = 2048
DEPTH = 2
TOK = BATCH * SEQ
HEAD_DIM = 64
N_HEADS = 16
N_KV = 4
HPG = 4
NSA_W = 1024
KV_W = 256
CMP_BLOCK = 32
CMP_STRIDE = 16
CMP_HIDDEN = 256
SEL_BLOCK = 64
N_SEL = 16
WINDOW = 512
GM_W = 1024
GM_CHUNK = 128
GM_GW = 128
GM_GROUPS = 8
N_BUCKETS = 32
MAX_DISTANCE = 128
D_FF = 5504
EPS = 1e-6
NEG = -1e30
FORCE = 1e4
OFF_KV = NSA_W
OFF_NG = OFF_KV + 6 * KV_W
OFF_UV = OFF_NG + 3 * N_HEADS
OFF_MG = OFF_UV + 2 * GM_W
LOG2E = 1.4426950408889634

LANE = 128
SUBLANE = 8
VMEM_LIMIT = 56 * 1024 * 1024

FFN_TF = 512
FFN_UP_TM = 1024
NORM_TM = 1024
FFN_DN_TM = 512
FFN_DN_TN = 512
A_BUFS = 3
PIN_TM = 1024
PIN_TN = 1024
PIN_RC = 256
SEC_W = PIN_TN // 2
Z_MG = 0
Z_U = Z_MG + 2 * D_MODEL
Z_V = Z_U + GM_W
Z_Q = Z_V + GM_W
Z_KV = Z_Q + NSA_W
Z_NG = Z_KV + 6 * KV_W
Z_W = Z_NG + SEC_W
KV_GW = 2 * HEAD_DIM
GATE_GW = 16
TQ = 128
NQ = SEQ // TQ
NK = SEQ // TQ
QL = HPG * TQ
N_CMP = 128
CMP_BAND = 16
MASKED = -1
N_SBLK = SEQ // SEL_BLOCK
V_ROWS = HEAD_DIM + 16
FAR_GROUP = 4
Q_PAIR = 4
OUT_TM = 256


def _cparams(sem):
    return pltpu.CompilerParams(dimension_semantics=sem, vmem_limit_bytes=VMEM_LIMIT)


def _cast_rows(src_ref, dst_ref, chunk):
    def body(r, carry):
        rows = pl.ds(pl.multiple_of(r * chunk, chunk), chunk)
        dst_ref[rows, :] = src_ref[rows, :].astype(BF16)
        return carry

    lax.fori_loop(0, src_ref.shape[0] // chunk, body, 0)


def _rms_bf16(x, gain):
    ms = jnp.mean(x * x, axis=-1, keepdims=True)
    return (x * lax.rsqrt(ms + EPS) * gain).astype(BF16)


def _norm_body(x_ref, g_ref, h_ref):
    h_ref[...] = _rms_bf16(x_ref[...], g_ref[...])


def _rmsnorm(x, gain):
    return pl.pallas_call(
        _norm_body,
        grid=(TOK // NORM_TM,),
        in_specs=[pl.BlockSpec((NORM_TM, D_MODEL), lambda i: (i, 0)),
                  pl.BlockSpec((1, D_MODEL), lambda i: (0, 0))],
        out_specs=pl.BlockSpec((NORM_TM, D_MODEL), lambda i: (i, 0)),
        out_shape=jax.ShapeDtypeStruct((TOK, D_MODEL), BF16),
        compiler_params=_cparams(("parallel",)),
        name="rmsnorm",
    )(x, gain)


def _ffn_up_body(h_ref, wg_ref, wu_ref, a_ref, wgb_ref, wub_ref):
    @pl.when(pl.program_id(1) == 0)
    def _():
        _cast_rows(wg_ref, wgb_ref, LANE)
        _cast_rows(wu_ref, wub_ref, LANE)

    h = h_ref[...]
    g = jnp.dot(h, wgb_ref[...], preferred_element_type=F32)
    u = jnp.dot(h, wub_ref[...], preferred_element_type=F32)
    a_ref[...] = (g * jax.nn.sigmoid(g) * u).astype(BF16)


def _ffn_down_body(a_hbm, wd_ref, x_ref, o_ref, wdb_ref, abuf_ref, sem_ref):
    nm = pl.num_programs(1)
    m = pl.program_id(1)
    s = pl.program_id(0) * nm + m
    total = pl.num_programs(0) * nm

    def tile_copy(step):
        rows = pl.ds(pl.multiple_of(lax.rem(step, nm) * FFN_DN_TM, FFN_DN_TM), FFN_DN_TM)
        slot = lax.rem(step, A_BUFS)
        return pltpu.make_async_copy(a_hbm.at[rows, :], abuf_ref.at[slot], sem_ref.at[slot])

    @pl.when(s == 0)
    def _():
        for k in range(A_BUFS - 1):
            tile_copy(k).start()

    @pl.when(m == 0)
    def _():
        _cast_rows(wd_ref, wdb_ref, LANE)

    @pl.when(s + (A_BUFS - 1) < total)
    def _():
        tile_copy(s + (A_BUFS - 1)).start()

    tile_copy(s).wait()
    a = abuf_ref[lax.rem(s, A_BUFS)]
    o_ref[...] = x_ref[...] + 0.5 * jnp.dot(a, wdb_ref[...], preferred_element_type=F32)


def _ffn(x, h, wg, wu, wd, l):
    nf = pl.cdiv(D_FF, FFN_TF)
    a = pl.pallas_call(
        _ffn_up_body,
        grid=(nf, TOK // FFN_UP_TM),
        in_specs=[
            pl.BlockSpec((FFN_UP_TM, D_MODEL), lambda f, m: (m, 0)),
            pl.BlockSpec((None, D_MODEL, FFN_TF), lambda f, m: (l, 0, f)),
            pl.BlockSpec((None, D_MODEL, FFN_TF), lambda f, m: (l, 0, f)),
        ],
        out_specs=pl.BlockSpec((FFN_UP_TM, FFN_TF), lambda f, m: (m, f)),
        out_shape=jax.ShapeDtypeStruct((TOK, D_FF), BF16),
        scratch_shapes=[pltpu.VMEM((D_MODEL, FFN_TF), BF16), pltpu.VMEM((D_MODEL, FFN_TF), BF16)],
        compiler_params=_cparams(("arbitrary", "arbitrary")),
        name="ffn_up",
    )(h, wg, wu)
    return pl.pallas_call(
        _ffn_down_body,
        grid=(D_MODEL // FFN_DN_TN, TOK // FFN_DN_TM),
        in_specs=[
            pl.BlockSpec(memory_space=pl.ANY),
            pl.BlockSpec((None, D_FF, FFN_DN_TN), lambda n, m: (l, 0, n)),
            pl.BlockSpec((FFN_DN_TM, FFN_DN_TN), lambda n, m: (m, n)),
        ],
        out_specs=pl.BlockSpec((FFN_DN_TM, FFN_DN_TN), lambda n, m: (m, n)),
        out_shape=jax.ShapeDtypeStruct((TOK, D_MODEL), F32),
        scratch_shapes=[pltpu.VMEM((D_FF, FFN_DN_TN), BF16),
                        pltpu.VMEM((A_BUFS, FFN_DN_TM, D_FF), BF16),
                        pltpu.SemaphoreType.DMA((A_BUFS,))],
        compiler_params=_cparams(("arbitrary", "arbitrary")),
        name="ffn_down",
    )(a, wd, x)


def _head_rms(y, bd):
    y2 = (y * y).astype(BF16)
    ss = jnp.concatenate(
        [jnp.dot(y2[:, k * LANE:(k + 1) * LANE], bd, preferred_element_type=F32)
         for k in range(y.shape[1] // LANE)], axis=1)
    return y * lax.rsqrt(ss * (1.0 / HEAD_DIM) + EPS)


def _pin_body(x_ref, g_ref, w_ref, bd_ref, qg_ref, k1g_ref, k2g_ref, o_ref, h_ref):
    j = pl.program_id(1)

    @pl.when(j == 0)
    def _():
        x = x_ref[...]
        ms = jnp.mean(x * x, axis=-1, keepdims=True)
        h_ref[...] = (x * lax.rsqrt(ms + EPS) * g_ref[...]).astype(BF16)

    t_u = Z_U // PIN_TN
    t_q = Z_Q // PIN_TN
    t_kv = Z_KV // PIN_TN
    half = SEC_W

    def emit(act):
        for r in range(PIN_TM // PIN_RC):
            rows = slice(r * PIN_RC, (r + 1) * PIN_RC)
            o_ref[rows, :] = act(jnp.dot(h_ref[rows, :], w_ref[...], preferred_element_type=F32))

    def halves(act_lo, act_hi):
        return lambda y: jnp.concatenate([act_lo(y[:, :half]), act_hi(y[:, half:])], axis=1)

    def kv_act(kg_ref):
        is_k = (lax.broadcasted_iota(jnp.int32, (1, half), 1) & (KV_GW - 1)) < HEAD_DIM
        return lambda y: jnp.where(is_k, _head_rms(y, bd_ref[...]) * kg_ref[...], y)

    @pl.when(j < t_u)
    def _():
        emit(jax.nn.sigmoid)

    @pl.when((j >= t_u) & (j < t_q))
    def _():
        emit(jax.nn.gelu)

    @pl.when((j >= t_q) & (j < t_kv))
    def _():
        emit(lambda y: _head_rms(y, bd_ref[...]) * (qg_ref[...] * (HEAD_DIM ** -0.5 * LOG2E)))

    @pl.when(j == t_kv)
    def _():
        emit(halves(lambda y: y, kv_act(k1g_ref)))

    @pl.when(j == t_kv + 1)
    def _():
        emit(halves(kv_act(k2g_ref), jax.nn.sigmoid))


def _proj_in(x, gain, w, l, bd, qg, k1g, k2g):
    return pl.pallas_call(
        _pin_body,
        grid=(TOK // PIN_TM, Z_W // PIN_TN),
        in_specs=[
            pl.BlockSpec((PIN_TM, D_MODEL), lambda i, j: (i, 0)),
            pl.BlockSpec((1, D_MODEL), lambda i, j: (0, 0)),
            pl.BlockSpec((None, D_MODEL, PIN_TN), lambda i, j: (l, 0, j)),
            pl.BlockSpec((LANE, LANE), lambda i, j: (0, 0)),
            pl.BlockSpec((1, PIN_TN), lambda i, j: (0, 0)),
            pl.BlockSpec((1, SEC_W), lambda i, j: (0, 0)),
            pl.BlockSpec((1, SEC_W), lambda i, j: (0, 0)),
        ],
        out_specs=pl.BlockSpec((PIN_TM, PIN_TN), lambda i, j: (i, j)),
        out_shape=jax.ShapeDtypeStruct((TOK, Z_W), F32),
        scratch_shapes=[pltpu.VMEM((PIN_TM, D_MODEL), BF16)],
        compiler_params=_cparams(("parallel", "arbitrary")),
        name="proj_in",
    )(x, gain, w, bd, qg, k1g, k2g)


def _cmp_body(z_ref, pos_ref, w1_ref, w2k_ref, w2vT_ref, kg_ref, kc_ref, vcT_ref):
    half = CMP_STRIDE * HEAD_DIM
    lane = lax.broadcasted_iota(jnp.int32, (N_CMP, LANE), 1)
    first = lane < HEAD_DIM
    xk, xv = [], []
    for j in range(CMP_STRIDE // 2):
        ev = z_ref[pl.ds(2 * j, N_CMP, stride=CMP_STRIDE), :]
        od = z_ref[pl.ds(2 * j + 1, N_CMP, stride=CMP_STRIDE), :]
        xk.append(jnp.where(first, ev, pltpu.roll(od, HEAD_DIM, 1)))
        xv.append(jnp.where(first, pltpu.roll(ev, HEAD_DIM, 1), od))

    def hidden(x, kind):
        xa = (x + pos_ref[kind, :, :half]).astype(BF16)
        xb = (x + pos_ref[kind, :, half:]).astype(BF16)
        a = jnp.dot(xa, w1_ref[kind, :half, :], preferred_element_type=F32)
        b = jnp.dot(xb, w1_ref[kind, half:, :], preferred_element_type=F32)
        hid = a + pltpu.roll(b, N_CMP - 1, 0)
        return (hid * jax.nn.sigmoid(hid)).astype(BF16)

    yk = jnp.dot(hidden(jnp.concatenate(xk, axis=1), 0), w2k_ref[...], preferred_element_type=F32)
    ms = jnp.mean(yk * yk, axis=-1, keepdims=True)
    kc_ref[0, 0] = (yk * lax.rsqrt(ms + EPS) * kg_ref[...]).astype(BF16)
    hv = hidden(jnp.concatenate(xv, axis=1), 1)
    vcT_ref[0, 0] = lax.dot_general(w2vT_ref[...], hv, (((1,), (1,)), ((), ())),
                                    preferred_element_type=F32).astype(BF16)


def _compress(z, pos, w1, w2k, w2vT, kg):
    return pl.pallas_call(
        _cmp_body,
        grid=(BATCH, N_KV),
        in_specs=[
            pl.BlockSpec((SEQ, KV_GW), lambda b, g: (b, Z_KV // KV_GW + g)),
            pl.BlockSpec((2, 1, CMP_BLOCK * HEAD_DIM), lambda b, g: (0, 0, 0)),
            pl.BlockSpec((2, CMP_BLOCK * HEAD_DIM, CMP_HIDDEN), lambda b, g: (0, 0, 0)),
            pl.BlockSpec((CMP_HIDDEN, HEAD_DIM), lambda b, g: (0, 0)),
            pl.BlockSpec((HEAD_DIM, CMP_HIDDEN), lambda b, g: (0, 0)),
            pl.BlockSpec((1, HEAD_DIM), lambda b, g: (0, 0)),
        ],
        out_specs=[
            pl.BlockSpec((1, 1, N_CMP, HEAD_DIM), lambda b, g: (b, g, 0, 0)),
            pl.BlockSpec((1, 1, HEAD_DIM, N_CMP), lambda b, g: (b, g, 0, 0)),
        ],
        out_shape=[
            jax.ShapeDtypeStruct((BATCH, N_KV, N_CMP, HEAD_DIM), BF16),
            jax.ShapeDtypeStruct((BATCH, N_KV, HEAD_DIM, N_CMP), BF16),
        ],
        compiler_params=_cparams(("parallel", "parallel")),
        name="compress",
    )(z, pos, w1, w2k, w2vT, kg)


def _np_bucket(dist):
    n = np.maximum(dist, 0)
    max_exact = N_BUCKETS // 2
    nf = np.maximum(n, 1).astype(np.float32)
    large = max_exact + (np.log(nf / np.float32(max_exact)) / np.float32(math.log(MAX_DISTANCE / max_exact))
                         * np.float32(N_BUCKETS - max_exact)).astype(np.int32)
    large = np.minimum(large, N_BUCKETS - 1)
    return np.where(n < max_exact, n, large).astype(np.int32)


def _bucket_tables():
    kj = np.arange(TQ)[:, None]
    qi = np.arange(TQ)[None, :]
    near = np.stack([np.where(qi >= kj, _np_bucket(qi - kj), MASKED), _np_bucket(TQ + qi - kj)])
    start = np.array([_cmp_band_start(i) for i in range(NQ)])[:, None, None]
    c = start + np.arange(CMP_BAND)[None, :, None]
    t = (np.arange(NQ)[:, None, None] * TQ + qi[None])
    dist_band = t - (c * CMP_STRIDE + CMP_BLOCK - 1)
    cmp_b = np.where(dist_band >= 0, _np_bucket(dist_band), MASKED)
    dist = t - (np.arange(N_CMP)[None, :, None] * CMP_STRIDE + CMP_BLOCK - 1)
    for i in range(NQ):
        s0 = _cmp_band_start(i)
        assert np.all(dist[i, :s0] >= 0) and np.all(_np_bucket(dist[i, :s0]) == N_BUCKETS - 1)
        assert np.all(dist[i, s0 + CMP_BAND:] < 0)
    return near.astype(np.int32), cmp_b.astype(np.int32)


def _cmp_band_start(i):
    return max((TQ // CMP_STRIDE) * (i - 1), 0)


def _bias_body(rb_ref, near_ref, cmpb_ref, t0_ref, t1_ref, tc_ref):
    g = pl.program_id(0)

    def lut(bk, h):
        val = jnp.full(bk.shape, NEG, F32)
        for b in range(N_BUCKETS):
            val = jnp.where(bk == b, rb_ref[b, h], val)
        return val

    for hh in range(HPG):
        h = g * HPG + hh
        sl = slice(hh * TQ, (hh + 1) * TQ)
        far = rb_ref[N_BUCKETS - 1, h]
        t0_ref[0, :, sl] = (lut(near_ref[0], h) - far) * LOG2E
        t1_ref[0, :, sl] = (lut(near_ref[1], h) - far) * LOG2E

        def body(i, carry):
            start = pl.multiple_of(jnp.maximum((TQ // CMP_STRIDE) * (i - 1), 0), SUBLANE)
            row = lax.broadcasted_iota(jnp.int32, (N_CMP, TQ), 0)
            tc_ref[0, i, :, sl] = jnp.where(row < start, far * LOG2E, NEG)
            tc_ref[0, i, pl.ds(start, CMP_BAND), sl] = lut(cmpb_ref[i], h) * LOG2E
            return carry

        lax.fori_loop(0, NQ, body, 0, unroll=4)


def _bias_tables(rel_bias, near, cmpb):
    return pl.pallas_call(
        _bias_body,
        grid=(N_KV,),
        in_specs=[
            pl.BlockSpec(memory_space=pltpu.SMEM),
            pl.BlockSpec((2, TQ, TQ), lambda g: (0, 0, 0)),
            pl.BlockSpec((NQ, CMP_BAND, TQ), lambda g: (0, 0, 0)),
        ],
        out_specs=[
            pl.BlockSpec((1, TQ, QL), lambda g: (g, 0, 0)),
            pl.BlockSpec((1, TQ, QL), lambda g: (g, 0, 0)),
            pl.BlockSpec((1, NQ, N_CMP, QL), lambda g: (g, 0, 0, 0)),
        ],
        out_shape=[
            jax.ShapeDtypeStruct((N_KV, TQ, QL), F32),
            jax.ShapeDtypeStruct((N_KV, TQ, QL), F32),
            jax.ShapeDtypeStruct((N_KV, NQ, N_CMP, QL), F32),
        ],
        compiler_params=_cparams(("arbitrary",)),
        name="bias_tables",
    )(rel_bias, near, cmpb)


def _attn_body(zq_ref, zs_ref, zw_ref, zg_ref, kc_ref, vcT_ref, t0_ref, t1_ref, tc_ref, o_ref,
               ks_ref, vsT_ref, kw_ref, vwT_ref, sel_ref, sfar_ref, gsc_ref, psum_ref):
    g = pl.program_id(1)
    ip = pl.program_id(2)

    @pl.when(ip == 0)
    def _():
        ones = jnp.ones((V_ROWS - HEAD_DIM, TQ), BF16)

        def fill(n, carry):
            for z_ref, k_ref, vT_ref in ((zs_ref, ks_ref, vsT_ref), (zw_ref, kw_ref, vwT_ref)):
                blk = z_ref[pl.ds(pl.multiple_of(n * TQ, TQ), TQ), :]
                k_ref[n] = blk[:, :HEAD_DIM].astype(BF16)
                vT_ref[n, :HEAD_DIM, :] = blk.T[HEAD_DIM:, :].astype(BF16)
                vT_ref[n, HEAD_DIM:, :] = ones
            return carry

        lax.fori_loop(0, NK, fill, 0, unroll=4)

    blk = lax.broadcasted_iota(jnp.int32, (N_SBLK, TQ), 0)
    ql = lax.broadcasted_iota(jnp.int32, (N_SBLK, TQ), 1)
    key1 = lax.broadcasted_iota(jnp.int32, (TQ, TQ), 0)
    q1 = lax.broadcasted_iota(jnp.int32, (TQ, TQ), 1)
    nlast = WINDOW // TQ

    def sel_mask(t, n):
        r0 = sel_ref[t, pl.ds(2 * n, 1), :]
        r1 = sel_ref[t, pl.ds(2 * n + 1, 1), :]
        return jnp.where(key1 < SEL_BLOCK, r0, r1) > 0.5

    def scores(qT, k_ref, n, bias, mask):
        sc = jnp.dot(k_ref[n], qT, preferred_element_type=F32)
        if bias is not None:
            sc = sc + bias
        if mask is None:
            return sc
        return jnp.concatenate(
            [jnp.where(mask, sc[:, h * TQ:(h + 1) * TQ], NEG) for h in range(HPG)], axis=1)

    def colmax8(sc):
        return jnp.max(sc.reshape(TQ // SUBLANE, SUBLANE, QL), axis=0)

    def probs(sc, mrow):
        return jnp.exp2(sc - mrow).astype(BF16)

    def pv(vT_ref, ns, ps):
        vv = vT_ref[ns[0]] if len(ns) == 1 else jnp.concatenate([vT_ref[n] for n in ns], axis=1)
        pp = ps[0] if len(ps) == 1 else jnp.concatenate(ps, axis=0)
        return jnp.dot(vv, pp, preferred_element_type=F32)

    def finish(acc):
        return acc[:HEAD_DIM] * (1.0 / acc[HEAD_DIM:HEAD_DIM + 1])

    def stage_q(t):
        i = Q_PAIR * ip + t
        rows = slice(t * TQ, (t + 1) * TQ)
        qt = zq_ref[rows, :].T
        qT = jnp.concatenate([qt[h * HEAD_DIM:(h + 1) * HEAD_DIM] for h in range(HPG)], axis=1).astype(BF16)
        gsc_ref[t] = zg_ref[rows, :].T
        return dict(i=i, qT=qT, n1=jnp.maximum(i - 1, 0), i1=jnp.full((TQ, TQ), i, jnp.int32),
                    nfar=jnp.maximum(i - 1, 0))

    def stage_cmp_scores(t, tl):
        tl["s_c"] = jnp.dot(kc_ref[0, 0], tl["qT"], preferred_element_type=F32) + tc_ref[0, t]
        tl["m_c"] = jnp.maximum(jnp.max(colmax8(tl["s_c"]), axis=0, keepdims=True), 0.1 * NEG)

    def stage_cmp(t, tl):
        e = jnp.exp2(tl["s_c"] - tl["m_c"])
        l = jnp.sum(e, axis=0, keepdims=True)
        p = e * jnp.where(l > 0.0, 1.0 / l, 0.0)
        tl["ocT"] = jnp.dot(vcT_ref[0, 0], p.astype(BF16), preferred_element_type=F32)
        psum_ref[t] = p[:, 0:TQ] + p[:, TQ:2 * TQ] + p[:, 2 * TQ:3 * TQ] + p[:, 3 * TQ:4 * TQ]

    def stage_select(t, tl):
        i = tl["i"]
        ratio = SEL_BLOCK // CMP_STRIDE
        part = [psum_ref[t, pl.ds(r, N_SBLK, stride=ratio), :] for r in range(ratio)]
        prev_last = jnp.where(blk == 0, 0.0, pltpu.roll(part[3], 1, 0))
        imp = (part[0] + part[1] + part[2]) + 0.5 * (part[3] + prev_last)
        cur = jnp.full((N_SBLK, TQ), i, jnp.int32) * (TQ // SEL_BLOCK) + ql // SEL_BLOCK
        forced = (blk == 0) | (blk == cur) | (blk == cur - 1)
        score = jnp.where(blk <= cur, jnp.where(forced, FORCE, imp), NEG)
        sub = SUBLANE
        pieces = [score[k * sub:(k + 1) * sub, :] for k in range(N_SBLK // sub)]
        ranks = [jnp.zeros((sub, TQ), F32) for _ in pieces]
        for jj in range(N_SBLK):
            row = score[jj:jj + 1, :]
            for k, pc in enumerate(pieces):
                if k * sub > jj:
                    beats = row >= pc
                elif (k + 1) * sub - 1 < jj:
                    beats = row > pc
                else:
                    beats = (row > pc) | ((row == pc) & (blk[k * sub:(k + 1) * sub, :] > jj))
                ranks[k] = ranks[k] + jnp.where(beats, 1.0, 0.0)
        rank = jnp.concatenate(ranks, axis=0)
        sel_ref[t] = jnp.where((rank < N_SEL) & (blk <= cur), 1.0, 0.0)

    def stage_window_scores(t, tl):
        i, n1, i1, qT = tl["i"], tl["n1"], tl["i1"], tl["qT"]
        n2, n3, n4 = jnp.maximum(i - 2, 0), jnp.maximum(i - 3, 0), jnp.maximum(i - nlast, 0)
        w_d = scores(qT, kw_ref, i, t0_ref[0], None)
        w_1 = scores(qT, kw_ref, n1, t1_ref[0], i1 >= 1)
        w_2 = scores(qT, kw_ref, n2, None, i1 >= 2)
        w_3 = scores(qT, kw_ref, n3, None, i1 >= 3)
        w_4 = scores(qT, kw_ref, n4, None, (q1 < key1) & (i1 >= nlast))
        mxw = jnp.maximum(jnp.maximum(jnp.maximum(colmax8(w_d), colmax8(w_1)),
                                      jnp.maximum(colmax8(w_2), colmax8(w_3))), colmax8(w_4))
        tl["w"] = (w_d, w_1, w_2, w_3, w_4)
        tl["wn"] = (i, n1, n2, n3, n4)
        tl["m_w"] = jnp.max(mxw, axis=0, keepdims=True)

    def stage_window_pv(t, tl):
        (w_d, w_1, w_2, w_3, w_4), (i, n1, n2, n3, n4), m_w = tl["w"], tl["wn"], tl["m_w"]
        tl["owT"] = finish(pv(vwT_ref, [i], [probs(w_d, m_w)])
                           + pv(vwT_ref, [n1, n2], [probs(w_1, m_w), probs(w_2, m_w)])
                           + pv(vwT_ref, [n3, n4], [probs(w_3, m_w), probs(w_4, m_w)]))

    def stage_near(t, tl):
        i, n1, i1, qT = tl["i"], tl["n1"], tl["i1"], tl["qT"]
        tl["s_d"] = scores(qT, ks_ref, i, t0_ref[0], None)
        tl["s_1"] = scores(qT, ks_ref, n1, t1_ref[0], sel_mask(t, n1) & (i1 >= 1))
        tl["mx"] = jnp.maximum(colmax8(tl["s_d"]), colmax8(tl["s_1"]))

    tiles = [stage_q(t) for t in range(Q_PAIR)]
    for stage in (stage_cmp_scores, stage_cmp, stage_window_scores, stage_select, stage_near, stage_window_pv):
        for t, tl in enumerate(tiles):
            stage(t, tl)
    assert Q_PAIR == FAR_GROUP
    every = {t: list(range(FAR_GROUP)) for t in range(Q_PAIR)}
    leftover = {t: list(range(t - 1)) for t in range(2, Q_PAIR)}

    def far_scores(tr, mxs, offs):
        out = list(mxs)
        for t, cs in offs.items():
            tl = tiles[t]
            for c in cs:
                n = FAR_GROUP * tr + c
                sc = scores(tl["qT"], ks_ref, n, None,
                            sel_mask(t, n) & (jnp.full((TQ, TQ), n, jnp.int32) < tl["nfar"]))
                sfar_ref[t, n] = sc
                out[t] = jnp.maximum(out[t], colmax8(sc))
        return tuple(out)

    mxs = lax.fori_loop(0, ip, lambda tr, c: far_scores(tr, c, every), tuple(tl["mx"] for tl in tiles))
    mxs = far_scores(ip, mxs, leftover)
    m_s = [jnp.max(mx, axis=0, keepdims=True) for mx in mxs]
    accs = tuple(pv(vsT_ref, [tl["i"], tl["n1"]], [probs(tl["s_d"], m_s[t]), probs(tl["s_1"], m_s[t])])
                 for t, tl in enumerate(tiles))

    def far_pv(tr, accs_, offs):
        out = list(accs_)
        for t, cs in offs.items():
            ns = [FAR_GROUP * tr + c for c in cs]
            out[t] = out[t] + pv(vsT_ref, ns, [probs(sfar_ref[t, n], m_s[t]) for n in ns])
        return tuple(out)

    accs = lax.fori_loop(0, ip, lambda tr, c: far_pv(tr, c, every), accs)
    accs = far_pv(ip, accs, leftover)

    for t, tl in enumerate(tiles):
        base = g * GATE_GW
        gate = [jnp.concatenate([gsc_ref[t, pl.ds(base + br * HPG + h, 1), :] for h in range(HPG)], axis=1)
                for br in range(3)]
        aT = gate[0] * tl["ocT"] + gate[1] * finish(accs[t]) + gate[2] * tl["owT"]
        a4 = jnp.concatenate([aT[:, h * TQ:(h + 1) * TQ] for h in range(HPG)], axis=0)
        o_ref[t * TQ:(t + 1) * TQ, :] = a4.T.astype(o_ref.dtype)


def _attention(z, kc, vcT, t0, t1, tc):
    qw = HPG * HEAD_DIM
    rows = Q_PAIR * TQ
    npair = NQ // Q_PAIR
    return pl.pallas_call(
        _attn_body,
        grid=(BATCH, N_KV, npair),
        in_specs=[
            pl.BlockSpec((rows, qw), lambda b, g, i: (b * npair + i, Z_Q // qw + g)),
            pl.BlockSpec((SEQ, KV_GW), lambda b, g, i: (b, (Z_KV + 2 * KV_W) // KV_GW + g)),
            pl.BlockSpec((SEQ, KV_GW), lambda b, g, i: (b, (Z_KV + 4 * KV_W) // KV_GW + g)),
            pl.BlockSpec((rows, LANE), lambda b, g, i: (b * npair + i, Z_NG // LANE)),
            pl.BlockSpec((1, 1, N_CMP, HEAD_DIM), lambda b, g, i: (b, g, 0, 0)),
            pl.BlockSpec((1, 1, HEAD_DIM, N_CMP), lambda b, g, i: (b, g, 0, 0)),
            pl.BlockSpec((1, TQ, QL), lambda b, g, i: (g, 0, 0)),
            pl.BlockSpec((1, TQ, QL), lambda b, g, i: (g, 0, 0)),
            pl.BlockSpec((1, Q_PAIR, N_CMP, QL), lambda b, g, i: (g, i, 0, 0)),
        ],
        out_specs=pl.BlockSpec((rows, qw), lambda b, g, i: (b * npair + i, g)),
        out_shape=jax.ShapeDtypeStruct((TOK, NSA_W), BF16),
        scratch_shapes=[
            pltpu.VMEM((NK, TQ, HEAD_DIM), BF16),
            pltpu.VMEM((NK, V_ROWS, TQ), BF16),
            pltpu.VMEM((NK, TQ, HEAD_DIM), BF16),
            pltpu.VMEM((NK, V_ROWS, TQ), BF16),
            pltpu.VMEM((Q_PAIR, N_SBLK, TQ), F32),
            pltpu.VMEM((Q_PAIR, NK, TQ, QL), F32),
            pltpu.VMEM((Q_PAIR, LANE, TQ), F32),
            pltpu.VMEM((Q_PAIR, N_CMP, TQ), F32),
        ],
        compiler_params=_cparams(("parallel", "parallel", "arbitrary")),
        name="nsa_attention",
    )(z, z, z, z, kc, vcT, t0, t1, tc)


def _sgu_tile(u_ref, v_ref, g_ref, b_ref, w_ref, bs_ref):
    r = lax.broadcasted_iota(jnp.int32, (GM_CHUNK, GM_CHUNK), 0)
    c = lax.broadcasted_iota(jnp.int32, (GM_CHUNK, GM_CHUNK), 1)
    tril = r >= c
    ws = [jnp.where(tril, w_ref[gi], 0.0).astype(BF16) for gi in range(GM_GROUPS)]
    chunks = []
    for ch in range(u_ref.shape[0] // GM_CHUNK):
        rows = slice(ch * GM_CHUNK, (ch + 1) * GM_CHUNK)
        v = v_ref[rows, :]
        mu = jnp.mean(v, axis=-1, keepdims=True)
        vc = v - mu
        var = jnp.mean(vc * vc, axis=-1, keepdims=True)
        vn = (vc * lax.rsqrt(var + EPS) * g_ref[...] + b_ref[...]).astype(BF16)
        groups = []
        for gi in range(GM_GROUPS):
            cols = slice(gi * GM_GW, (gi + 1) * GM_GW)
            mixed = jnp.dot(ws[gi], vn[:, cols], preferred_element_type=F32) + bs_ref[:, cols]
            groups.append((u_ref[rows, cols] * mixed).astype(BF16))
        chunks.append(jnp.concatenate(groups, axis=1))
    return jnp.concatenate(chunks, axis=0)


def _out_body(a_ref, u_ref, v_ref, m1_ref, m2_ref, x_ref, lg_ref, lb_ref, sw_ref, bs_ref,
              wn_ref, ws_ref, wo_ref, gn_ref, o_ref, h_ref):
    sg = _sgu_tile(u_ref, v_ref, lg_ref, lb_ref, sw_ref, bs_ref)
    pa = jnp.dot(a_ref[...], wn_ref[...], preferred_element_type=F32)
    ps = jnp.dot(sg, ws_ref[...], preferred_element_type=F32)
    merged = (m1_ref[...] * pa + m2_ref[...] * ps).astype(BF16)
    y = x_ref[...] + jnp.dot(merged, wo_ref[...], preferred_element_type=F32)
    o_ref[...] = y
    h_ref[...] = _rms_bf16(y, gn_ref[...])


def _merge_out(a, z, x, ln_g, ln_b, w_s, bs_exp, wn, ws, wo, l, next_gain):
    const = lambda i: (l, 0, 0)
    once = pl.Buffered(1)
    return pl.pallas_call(
        _out_body,
        grid=(TOK // OUT_TM,),
        in_specs=[
            pl.BlockSpec((OUT_TM, NSA_W), lambda i: (i, 0)),
            pl.BlockSpec((OUT_TM, GM_W), lambda i: (i, Z_U // GM_W)),
            pl.BlockSpec((OUT_TM, GM_W), lambda i: (i, Z_V // GM_W)),
            pl.BlockSpec((OUT_TM, D_MODEL), lambda i: (i, 0)),
            pl.BlockSpec((OUT_TM, D_MODEL), lambda i: (i, 1)),
            pl.BlockSpec((OUT_TM, D_MODEL), lambda i: (i, 0)),
            pl.BlockSpec((1, GM_W), lambda i: (0, 0)),
            pl.BlockSpec((1, GM_W), lambda i: (0, 0)),
            pl.BlockSpec((GM_GROUPS, GM_CHUNK, GM_CHUNK), lambda i: (0, 0, 0), pipeline_mode=once),
            pl.BlockSpec((GM_CHUNK, GM_W), lambda i: (0, 0), pipeline_mode=once),
            pl.BlockSpec((None, NSA_W, D_MODEL), const, pipeline_mode=once),
            pl.BlockSpec((None, GM_W, D_MODEL), const, pipeline_mode=once),
            pl.BlockSpec((None, D_MODEL, D_MODEL), const, pipeline_mode=once),
            pl.BlockSpec((1, D_MODEL), lambda i: (0, 0)),
        ],
        out_specs=[pl.BlockSpec((OUT_TM, D_MODEL), lambda i: (i, 0)),
                   pl.BlockSpec((OUT_TM, D_MODEL), lambda i: (i, 0))],
        out_shape=[jax.ShapeDtypeStruct((TOK, D_MODEL), F32),
                   jax.ShapeDtypeStruct((TOK, D_MODEL), BF16)],
        compiler_params=_cparams(("parallel",)),
        name="merge_out",
    )(a, z, z, z, z, x, ln_g, ln_b, w_s, bs_exp, wn, ws, wo, next_gain)


def _check_overlap_stencil():
    ci = np.arange(N_CMP - 1)[None, :] * CMP_STRIDE
    sj = np.arange(N_SBLK)[:, None] * SEL_BLOCK
    ov = np.clip(np.minimum(ci + CMP_BLOCK, sj + SEL_BLOCK) - np.maximum(ci, sj), 0, None) / CMP_BLOCK
    stencil = np.zeros((N_SBLK, N_CMP - 1))
    for j in range(N_SBLK):
        for c, w in ((4 * j - 1, 0.5), (4 * j, 1.0), (4 * j + 1, 1.0), (4 * j + 2, 1.0), (4 * j + 3, 0.5)):
            if 0 <= c < N_CMP - 1:
                stencil[j, c] = w
    assert np.array_equal(ov, stencil)


def _block_diag_ones():
    r = np.arange(LANE)
    return (r[:, None] // HEAD_DIM == r[None, :] // HEAD_DIM).astype(np.float32)


def _prep_w_in(w):
    L = w.shape[0]
    w = w.astype(BF16)
    kv = w[:, :, OFF_KV:OFF_NG].reshape(L, D_MODEL, 3, 2, N_KV, HEAD_DIM)
    kv = kv.transpose(0, 1, 2, 4, 3, 5).reshape(L, D_MODEL, 6 * KV_W)
    ng = w[:, :, OFF_NG:OFF_UV].reshape(L, D_MODEL, N_KV, HPG, 3).transpose(0, 1, 2, 4, 3)
    ng = ng.reshape(L, D_MODEL, N_KV, 3 * HPG)
    ng = jnp.pad(ng, ((0, 0), (0, 0), (0, 0), (0, GATE_GW - 3 * HPG))).reshape(L, D_MODEL, N_KV * GATE_GW)
    ng = jnp.pad(ng, ((0, 0), (0, 0), (0, SEC_W - N_KV * GATE_GW)))
    return jnp.concatenate([w[:, :, OFF_MG:], w[:, :, OFF_UV:OFF_MG], w[:, :, :OFF_KV], kv, ng], axis=2)


def _mixer(x, z, l, p, tables):
    t0, t1, tc = tables
    pos = jnp.stack([p["cmp_pos_k"][l], p["cmp_pos_v"][l]]).reshape(2, 1, CMP_BLOCK * HEAD_DIM)
    w1 = jnp.stack([p["cmp_k_w1"][l], p["cmp_v_w1"][l]]).astype(BF16)
    kc, vcT = _compress(z, pos, w1, p["cmp_k_w2"][l].astype(BF16), p["cmp_v_w2"][l].T.astype(BF16),
                        p["k_norm"][l, 0].reshape(1, HEAD_DIM))
    a = _attention(z, kc, vcT, t0, t1, tc)
    bs_exp = jnp.repeat(p["sgu_b"][l].T, GM_GW, axis=1)
    return _merge_out(a, z, x, p["sgu_norm_g"][l].reshape(1, GM_W), p["sgu_norm_b"][l].reshape(1, GM_W),
                      p["sgu_w"][l], bs_exp, p["wn"], p["ws"], p["wo"], l,
                      p["ffn2_norm"][l].reshape(1, D_MODEL))


def kernel(x, rel_bias, ffn1_norm, ffn1_w_gate, ffn1_w_up, ffn1_w_down, mix_norm, w_in, q_norm, k_norm, cmp_pos_k, cmp_pos_v, cmp_k_w1, cmp_k_w2, cmp_v_w1, cmp_v_w2, sgu_norm_g, sgu_norm_b, sgu_w, sgu_b, w_proj_nsa, w_proj_sgu, w_out, ffn2_norm, ffn2_w_gate, ffn2_w_up, ffn2_w_down):
    p = dict(cmp_pos_k=cmp_pos_k, cmp_pos_v=cmp_pos_v, cmp_k_w1=cmp_k_w1, cmp_k_w2=cmp_k_w2,
             cmp_v_w1=cmp_v_w1, cmp_v_w2=cmp_v_w2, k_norm=k_norm, sgu_norm_g=sgu_norm_g,
             sgu_norm_b=sgu_norm_b, sgu_w=sgu_w, sgu_b=sgu_b, wn=w_proj_nsa.astype(BF16),
             ws=w_proj_sgu.astype(BF16), wo=w_out.astype(BF16), ffn2_norm=ffn2_norm)
    near, cmpb = _bucket_tables()
    t0, t1, tc = _bias_tables(rel_bias, jnp.asarray(near), jnp.asarray(cmpb))
    tables = (t0, t1, tc)
    _check_overlap_stencil()
    bd = jnp.asarray(_block_diag_ones()).astype(BF16)
    w_in_r = _prep_w_in(w_in)
    f1 = (ffn1_w_gate, ffn1_w_up, ffn1_w_down)
    f2 = (ffn2_w_gate, ffn2_w_up, ffn2_w_down)

    y = x.reshape(TOK, D_MODEL)
    for l in range(DEPTH):
        y = _ffn(y, _rmsnorm(y, ffn1_norm[l].reshape(1, D_MODEL)), *f1, l)
        z = _proj_in(y, mix_norm[l].reshape(1, D_MODEL), w_in_r, l, bd,
                     jnp.tile(q_norm[l], PIN_TN // HEAD_DIM).reshape(1, PIN_TN),
                     jnp.tile(k_norm[l, 1], SEC_W // HEAD_DIM).reshape(1, SEC_W),
                     jnp.tile(k_norm[l, 2], SEC_W // HEAD_DIM).reshape(1, SEC_W))
        y, h = _mixer(y, z, l, p, tables)
        y = _ffn(y, h, *f2, l)
    return y.reshape(BATCH, SEQ, D_MODEL)
```
